```python
import math
import jax, jax.numpy as jnp
from jax import lax
import numpy as np

D_MODEL = 1024
BATCH = 8
SEQ = 2048
DEPTH = 2

D_MIX = D_MODEL
DIFF_HEADS = 4
DIFF_WIDTH = D_MIX // 2
DIFF_V_DIM = DIFF_WIDTH // DIFF_HEADS
DIFF_QK_DIM = DIFF_V_DIM // 2
MLA_HEADS = 4
MLA_WIDTH = D_MIX // 4
MLA_V_DIM = MLA_WIDTH // MLA_HEADS
MLA_NOPE_DIM = 64
MLA_ROPE_DIM = 32
MLA_Q_RANK = 192
MLA_KV_RANK = 128
ROPE_BASE = 10000.0
POOL_WIDTH = D_MIX - DIFF_WIDTH - MLA_WIDTH
POOL_GROUPS = 4
POOL_GROUP_DIM = POOL_WIDTH // POOL_GROUPS
POOL_WINDOWS = (2, 4, 8, 16)
IN_WIDTHS = (DIFF_HEADS * 2 * DIFF_QK_DIM, DIFF_HEADS * 2 * DIFF_QK_DIM, DIFF_WIDTH,
             MLA_Q_RANK, MLA_KV_RANK, MLA_ROPE_DIM, POOL_WIDTH)
D_IN = sum(IN_WIDTHS)
IN_SPLITS = tuple(sum(IN_WIDTHS[:i + 1]) for i in range(len(IN_WIDTHS) - 1))
N_GROUPS = 4
EXPERTS_PER_GROUP = 8
N_EXPERTS = N_GROUPS * EXPERTS_PER_GROUP
TOP_K = 2
D_FF_EXPERT = 256
ROUTE_BLOCK = 256
Q_BLOCK = 128
RMS_EPS = 1e-6

kernel_name = 'hybrid_diffattn_mla_pool_hmoe_encoder'


def _rms_norm(x, g):
    xf = x.astype(jnp.float32)
    y = xf * lax.rsqrt(jnp.mean(xf * xf, axis=-1, keepdims=True) + RMS_EPS)
    return (y * g.astype(jnp.float32)).astype(x.dtype)


def _to_blocks(x):
    b, s = x.shape[0], x.shape[1]
    return jnp.moveaxis(x.reshape((b, s // Q_BLOCK, Q_BLOCK) + x.shape[2:]), 1, 0)


def _from_blocks(y):
    nb, b, qb = y.shape[0], y.shape[1], y.shape[2]
    return jnp.moveaxis(y, 0, 1).reshape((b, nb * qb) + y.shape[3:])


def _alibi_slopes(n_heads):
    return 2.0 ** (-8.0 * jnp.arange(1, n_heads + 1, dtype=jnp.float32) / n_heads)


def _diff_attention(q1, q2, k1, k2, v, lam):
    s = q1.shape[1]
    scale = q1.shape[-1] ** -0.5
    slopes = _alibi_slopes(q1.shape[2])
    k_pos = jnp.arange(s, dtype=jnp.int32)
    q_pos = k_pos.reshape(s // Q_BLOCK, Q_BLOCK)

    def block(args):
        q1b, q2b, qp = args
        dist = jnp.abs(qp[:, None] - k_pos[None, :]).astype(jnp.float32)
        bias = -slopes[:, None, None] * dist[None]
        s1 = jnp.einsum('bqhd,bkhd->bhqk', q1b, k1, preferred_element_type=jnp.float32) * scale + bias
        s2 = jnp.einsum('bqhd,bkhd->bhqk', q2b, k2, preferred_element_type=jnp.float32) * scale + bias
        a = jax.nn.softmax(s1, axis=-1) - lam * jax.nn.softmax(s2, axis=-1)
        return jnp.einsum('bhqk,bkhe->bqhe', a.astype(v.dtype), v)

    return _from_blocks(lax.map(block, (_to_blocks(q1), _to_blocks(q2), q_pos)))


def _rope_tables(s):
    inv = 1.0 / (ROPE_BASE ** (jnp.arange(0, MLA_ROPE_DIM, 2, dtype=jnp.float32) / MLA_ROPE_DIM))
    ang = jnp.arange(s, dtype=jnp.float32)[:, None] * inv[None, :]
    return jnp.cos(ang), jnp.sin(ang)


def _rotate(x, cos, sin):
    half = x.shape[-1] // 2
    x1 = x[..., :half].astype(jnp.float32)
    x2 = x[..., half:].astype(jnp.float32)
    return jnp.concatenate([x1 * cos - x2 * sin, x2 * cos + x1 * sin], axis=-1).astype(x.dtype)


def _mla_attention(q_nope, q_rope, k_nope, k_rope, v):
    scale = (q_nope.shape[-1] + q_rope.shape[-1]) ** -0.5

    def block(args):
        qn, qr = args
        sc = (jnp.einsum('bqhd,bkhd->bhqk', qn, k_nope, preferred_element_type=jnp.float32)
              + jnp.einsum('bqhr,bkr->bhqk', qr, k_rope, preferred_element_type=jnp.float32)) * scale
        p = jax.nn.softmax(sc, axis=-1)
        return jnp.einsum('bhqk,bkhe->bqhe', p.astype(v.dtype), v)

    return _from_blocks(lax.map(block, (_to_blocks(q_nope), _to_blocks(q_rope))))


def _multi_scale_pool(u, w, scale):
    b, s, c = u.shape
    uf = u.astype(jnp.float32).reshape(b, s, POOL_GROUPS, POOL_GROUP_DIM)
    cs = jnp.concatenate([jnp.zeros((b, 1, POOL_GROUPS, POOL_GROUP_DIM), jnp.float32),
                          jnp.cumsum(uf, axis=1)], axis=1)
    t = jnp.arange(s, dtype=jnp.int32)
    outs = []
    for g, win in enumerate(POOL_WINDOWS):
        lo = jnp.clip(t - win // 2, 0, s - 1)
        hi = jnp.clip(t + win // 2 - 1, 0, s - 1)
        csg = cs[:, :, g]
        win_sum = jnp.take(csg, hi + 1, axis=1) - jnp.take(csg, lo, axis=1)
        cnt = (hi - lo + 1).astype(jnp.float32)
        outs.append(win_sum / cnt[None, :, None] - uf[:, :, g])
    pooled = jnp.stack(outs, axis=2).astype(u.dtype)
    mixed = jnp.einsum('bsgc,gcd->bsgd', pooled, w)
    return mixed.reshape(b, s, c) * scale


def _hier_moe(h, wg, bg, we, be, w_gate, w_up, w_down):
    b, s, d = h.shape
    t = b * s
    hf = h.reshape(t, d)
    g_prob = jax.nn.softmax((hf @ wg).astype(jnp.float32) + bg.astype(jnp.float32), axis=-1)
    g_top, g_idx = lax.top_k(g_prob, 1)
    e_logits = ((hf @ we).astype(jnp.float32) + be.astype(jnp.float32)).reshape(t, N_GROUPS, EXPERTS_PER_GROUP)
    e_sel = e_logits[jnp.arange(t), g_idx[:, 0]]
    e_top, e_idx = lax.top_k(jax.nn.softmax(e_sel, axis=-1), TOP_K)
    gate = g_top * e_top / jnp.sum(e_top, axis=-1, keepdims=True)
    expert_id = (g_idx * EXPERTS_PER_GROUP + e_idx).reshape(-1).astype(jnp.int32)
    token_id = jnp.repeat(jnp.arange(t, dtype=jnp.int32), TOP_K)
    weight = gate.reshape(-1)
    n_assign = t * TOP_K
    n_blocks = n_assign // ROUTE_BLOCK + N_EXPERTS
    cap = n_blocks * ROUTE_BLOCK
    order = jnp.argsort(expert_id)
    sorted_eid = expert_id[order]
    counts = jnp.bincount(expert_id, length=N_EXPERTS)
    starts = jnp.cumsum(counts) - counts
    padded = (counts + ROUTE_BLOCK - 1) // ROUTE_BLOCK * ROUTE_BLOCK
    padded_ends = jnp.cumsum(padded)
    padded_starts = padded_ends - padded
    dest = padded_starts[sorted_eid] + jnp.arange(n_assign, dtype=jnp.int32) - starts[sorted_eid]
    slot_tok = jnp.full((cap,), t, jnp.int32).at[dest].set(token_id[order])
    slot_w = jnp.zeros((cap,), jnp.float32).at[dest].set(weight[order])
    block_start = jnp.arange(n_blocks, dtype=jnp.int32) * ROUTE_BLOCK
    block_eid = jnp.minimum(jnp.sum(block_start[:, None] >= padded_ends[None, :], axis=1), N_EXPERTS - 1)
    xs = jnp.concatenate([hf, jnp.zeros((1, d), hf.dtype)], axis=0)[slot_tok].reshape(n_blocks, ROUTE_BLOCK, d)

    def expert_block(args):
        xb, e = args
        return (jax.nn.silu(xb @ w_gate[e]) * (xb @ w_up[e])) @ w_down[e]

    ys = lax.map(expert_block, (xs, block_eid)).reshape(cap, d)
    y = jnp.zeros((t + 1, d), jnp.float32).at[slot_tok].add(ys.astype(jnp.float32) * slot_w[:, None])
    return y[:t].astype(h.dtype).reshape(b, s, d)


def setup_inputs(seed: int = 0) -> dict:
    key = jax.random.key(seed)
    ks = jax.random.split(key, 26)
    f32 = jnp.float32
    L = DEPTH

    def nrm(k, shape, fan_in):
        return jax.random.normal(k, shape, f32) * (fan_in ** -0.5)

    def gain(k, shape):
        return 1.0 + 0.02 * jax.random.normal(k, shape, f32)

    return {
        'x': jax.random.normal(ks[0], (BATCH, SEQ, D_MODEL), f32),
        'norm1_g': gain(ks[1], (L, D_MODEL)),
        'w_in': nrm(ks[2], (L, D_MODEL, D_IN), D_MODEL),
        'diff_q_norm_g': gain(ks[3], (L, DIFF_QK_DIM)),
        'diff_k_norm_g': gain(ks[4], (L, DIFF_QK_DIM)),
        'diff_lambda': 0.1 * jax.random.normal(ks[5], (L, 4, DIFF_QK_DIM), f32),
        'diff_sub_norm_g': gain(ks[6], (L, DIFF_V_DIM)),
        'mla_q_lat_norm_g': gain(ks[7], (L, MLA_Q_RANK)),
        'mla_kv_lat_norm_g': gain(ks[8], (L, MLA_KV_RANK)),
        'mla_w_uq': nrm(ks[9], (L, MLA_Q_RANK, MLA_HEADS * (MLA_NOPE_DIM + MLA_ROPE_DIM)), MLA_Q_RANK),
        'mla_w_ukv': nrm(ks[10], (L, MLA_KV_RANK, MLA_HEADS * (MLA_NOPE_DIM + MLA_V_DIM)), MLA_KV_RANK),
        'mla_q_nope_norm_g': gain(ks[11], (L, MLA_NOPE_DIM)),
        'mla_q_rope_norm_g': gain(ks[12], (L, MLA_ROPE_DIM)),
        'mla_k_nope_norm_g': gain(ks[13], (L, MLA_NOPE_DIM)),
        'mla_k_rope_norm_g': gain(ks[14], (L, MLA_ROPE_DIM)),
        'pool_w': nrm(ks[15], (L, POOL_GROUPS, POOL_GROUP_DIM, POOL_GROUP_DIM), POOL_GROUP_DIM),
        'pool_scale': gain(ks[16], (L, POOL_WIDTH)),
        'w_out': nrm(ks[17], (L, D_MIX, D_MODEL), D_MIX),
        'norm2_g': gain(ks[18], (L, D_MODEL)),
        'router_group_w': nrm(ks[19], (L, D_MODEL, N_GROUPS), D_MODEL),
        'router_group_b': 0.01 * jax.random.normal(ks[20], (L, N_GROUPS), f32),
        'router_expert_w': nrm(ks[21], (L, D_MODEL, N_EXPERTS), D_MODEL),
        'router_expert_b': 0.01 * jax.random.normal(ks[22], (L, N_EXPERTS), f32),
        'expert_w_gate': nrm(ks[23], (L, N_EXPERTS, D_MODEL, D_FF_EXPERT), D_MODEL),
        'expert_w_up': nrm(ks[24], (L, N_EXPERTS, D_MODEL, D_FF_EXPERT), D_MODEL),
        'expert_w_down': nrm(ks[25], (L, N_EXPERTS, D_FF_EXPERT, D_MODEL), D_FF_EXPERT),
    }


def reference(x, norm1_g, w_in, diff_q_norm_g, diff_k_norm_g, diff_lambda, diff_sub_norm_g,
              mla_q_lat_norm_g, mla_kv_lat_norm_g, mla_w_uq, mla_w_ukv,
              mla_q_nope_norm_g, mla_q_rope_norm_g, mla_k_nope_norm_g, mla_k_rope_norm_g,
              pool_w, pool_scale, w_out, norm2_g,
              router_group_w, router_group_b, router_expert_w, router_expert_b,
              expert_w_gate, expert_w_up, expert_w_down):
    b, s, _ = x.shape
    cos, sin = _rope_tables(s)
    for l in range(DEPTH):
        lam_init = 0.8 - 0.6 * math.exp(-0.3 * l)
        h = _rms_norm(x, norm1_g[l])
        proj = h @ w_in[l]
        dq, dk, dv, cq, ckv, kr, pu = jnp.split(proj, IN_SPLITS, axis=-1)

        dq = _rms_norm(dq.reshape(b, s, DIFF_HEADS, 2, DIFF_QK_DIM), diff_q_norm_g[l])
        dk = _rms_norm(dk.reshape(b, s, DIFF_HEADS, 2, DIFF_QK_DIM), diff_k_norm_g[l])
        dv = dv.reshape(b, s, DIFF_HEADS, DIFF_V_DIM)
        lv = diff_lambda[l].astype(jnp.float32)
        lam = jnp.exp(jnp.sum(lv[0] * lv[1])) - jnp.exp(jnp.sum(lv[2] * lv[3])) + lam_init
        o_diff = _diff_attention(dq[..., 0, :], dq[..., 1, :], dk[..., 0, :], dk[..., 1, :], dv, lam)
        o_diff = (_rms_norm(o_diff, diff_sub_norm_g[l]) * (1.0 - lam_init)).reshape(b, s, DIFF_WIDTH)

        q = (_rms_norm(cq, mla_q_lat_norm_g[l]) @ mla_w_uq[l]).reshape(b, s, MLA_HEADS, MLA_NOPE_DIM + MLA_ROPE_DIM)
        kv = (_rms_norm(ckv, mla_kv_lat_norm_g[l]) @ mla_w_ukv[l]).reshape(b, s, MLA_HEADS, MLA_NOPE_DIM + MLA_V_DIM)
        q_nope = _rms_norm(q[..., :MLA_NOPE_DIM], mla_q_nope_norm_g[l])
        q_rope = _rotate(_rms_norm(q[..., MLA_NOPE_DIM:], mla_q_rope_norm_g[l]),
                         cos[None, :, None, :], sin[None, :, None, :])
        k_nope = _rms_norm(kv[..., :MLA_NOPE_DIM], mla_k_nope_norm_g[l])
        k_rope = _rotate(_rms_norm(kr, mla_k_rope_norm_g[l]), cos[None], sin[None])
        o_mla = _mla_attention(q_nope, q_rope, k_nope, k_rope, kv[..., MLA_NOPE_DIM:]).reshape(b, s, MLA_WIDTH)

        o_pool = _multi_scale_pool(pu, pool_w[l], pool_scale[l])

        x = x + jnp.concatenate([o_diff, o_mla, o_pool], axis=-1) @ w_out[l]

        x = x + _hier_moe(_rms_norm(x, norm2_g[l]), router_group_w[l], router_group_b[l],
                          router_expert_w[l], router_expert_b[l],
                          expert_w_gate[l], expert_w_up[l], expert_w_down[l])
    return x
```

```python
import functools
import math

import jax
import jax.numpy as jnp
from jax import lax
from jax.experimental import pallas as pl
from jax.experimental.pallas import tpu as pltpu

F32 = jnp.float32
BF16 = jnp.bfloat16

D_MODEL = 1024
DEPTH = 2
DIFF_HEADS = 4
DIFF_QK = 64
DIFF_V = 128
DIFF_WIDTH = 512
MLA_HEADS = 4
MLA_NOPE = 64
MLA_ROPE = 32
MLA_V = 64
MLA_Q_RANK = 192
MLA_KV_RANK = 128
MLA_WIDTH = 256
ROPE_BASE = 10000.0
POOL_WIDTH = 256
POOL_GROUPS = 4
POOL_GROUP_DIM = 64
POOL_WINDOWS = (2, 4, 8, 16)
N_GROUPS = 4
EXPERTS_PER_GROUP = 8
N_EXPERTS = 32
D_FF = 256
ROUTE_BLOCK = 256
RMS_EPS = 1e-6

LANES = 128
HEAD_SLOT = 128
PROJ_WIDTH = 2176
POOL_PAD = 16
VMEM_LIMIT = 48 * 1024 * 1024

TM_PROJ = 512
TQ = 256
TM_OUT = 256
TM_COMB = 256

NT_DIMS = (((1,), (1,)), ((), ()))


def _cparams(n_axes):
    return pltpu.CompilerParams(dimension_semantics=("arbitrary",) * n_axes,
                                vmem_limit_bytes=VMEM_LIMIT)


def _full(shape):
    return pl.BlockSpec(shape, lambda *_: (0,) * len(shape))


def _proj_kernel(x_ref, g1_ref, win_ref, gq_ref, gk_ref, gckv_ref, gcqa_ref, gcqb_ref,
                 wuqa_ref, wuqb_ref, wkk_ref, wkv_ref, gkn_ref, qtab_ref, ktab_ref, eplace_ref,
                 dq1_ref, dq2_ref, dk_ref, dv_ref, qm_ref, km_ref, vm_ref, pu_ref):
    x = x_ref[...]
    xn = x * lax.rsqrt(jnp.mean(x * x, axis=-1, keepdims=True) + RMS_EPS) * g1_ref[...]
    proj = jnp.dot(xn.astype(BF16), win_ref[...], preferred_element_type=F32)

    tm = x.shape[0]
    lane = lax.broadcasted_iota(jnp.int32, (tm, LANES), 1)
    lo = lane < DIFF_QK

    def half_norm(c, g_row):
        sq = c * c
        s_lo = jnp.sum(jnp.where(lo, sq, 0.0), axis=-1, keepdims=True)
        s_hi = jnp.sum(jnp.where(lo, 0.0, sq), axis=-1, keepdims=True)
        r = jnp.where(lo, lax.rsqrt(s_lo / DIFF_QK + RMS_EPS), lax.rsqrt(s_hi / DIFF_QK + RMS_EPS))
        return c * r * g_row

    for h in range(DIFF_HEADS):
        sl = slice(h * HEAD_SLOT, (h + 1) * HEAD_SLOT)
        qn = half_norm(proj[:, sl], gq_ref[...])
        dq1_ref[:, sl] = jnp.where(lo, qn, 0.0).astype(BF16)
        dq2_ref[:, sl] = jnp.where(lo, 0.0, qn).astype(BF16)
        ksl = slice(512 + h * HEAD_SLOT, 512 + (h + 1) * HEAD_SLOT)
        dk_ref[:, sl] = half_norm(proj[:, ksl], gk_ref[...]).astype(BF16)
    dv_ref[...] = proj[:, 1024:1536].astype(BF16)
    pu_ref[...] = proj[:, 1536:1792]

    ckv = proj[:, 1792:1920]
    ckvn = ckv * lax.rsqrt(jnp.mean(ckv * ckv, axis=-1, keepdims=True) + RMS_EPS) * gckv_ref[...]
    ckvn = ckvn.astype(BF16)
    cqa = proj[:, 1920:2048]
    last = proj[:, 2048:2176]
    lsq = last * last
    ss_q = (jnp.sum(cqa * cqa, axis=-1, keepdims=True)
            + jnp.sum(jnp.where(lo, lsq, 0.0), axis=-1, keepdims=True))
    r_q = lax.rsqrt(ss_q / MLA_Q_RANK + RMS_EPS)
    q_raw = (jnp.dot((cqa * r_q * gcqa_ref[...]).astype(BF16), wuqa_ref[...], preferred_element_type=F32)
             + jnp.dot((last * r_q * gcqb_ref[...]).astype(BF16), wuqb_ref[...], preferred_element_type=F32))

    rope_lanes = (lane >= MLA_NOPE) & (lane < MLA_NOPE + MLA_ROPE)
    ss_kr = jnp.sum(jnp.where(rope_lanes, lsq, 0.0), axis=-1, keepdims=True)
    kr_terms = last * lax.rsqrt(ss_kr / MLA_ROPE + RMS_EPS) * ktab_ref[...]
    kr_placed = jnp.dot(kr_terms.astype(BF16), eplace_ref[...], preferred_element_type=F32)

    k_raw = jnp.dot(ckvn, wkk_ref[...], preferred_element_type=F32)
    vm_ref[...] = jnp.dot(ckvn, wkv_ref[...], preferred_element_type=F32).astype(BF16)
    qtab = qtab_ref[...]
    for h in range(MLA_HEADS):
        sl = slice(h * HEAD_SLOT, (h + 1) * HEAD_SLOT)
        c = q_raw[:, sl]
        sq = c * c
        s_n = jnp.sum(jnp.where(lo, sq, 0.0), axis=-1, keepdims=True)
        s_r = jnp.sum(jnp.where(rope_lanes, sq, 0.0), axis=-1, keepdims=True)
        r = jnp.where(lo, lax.rsqrt(s_n / MLA_NOPE + RMS_EPS), lax.rsqrt(s_r / MLA_ROPE + RMS_EPS))
        qm_ref[:, sl] = (c * r * qtab[:, sl]).astype(BF16)
        kc = k_raw[:, sl]
        r_k = lax.rsqrt(jnp.sum(kc * kc, axis=-1, keepdims=True) / MLA_NOPE + RMS_EPS)
        km_ref[:, sl] = (kc * r_k * gkn_ref[...] + kr_placed[:, sl]).astype(BF16)


def _proj_call(x2, p, seq):
    t = x2.shape[0]
    tm = TM_PROJ
    n_pos = seq // tm
    row = lambda i: (i, 0)
    pos = lambda i: (i % n_pos, 0)
    bf = lambda w: jax.ShapeDtypeStruct((t, w), BF16)
    in_specs = [
        pl.BlockSpec((tm, D_MODEL), row),
        _full((1, D_MODEL)), _full((D_MODEL, PROJ_WIDTH)),
        _full((1, LANES)), _full((1, LANES)), _full((1, LANES)), _full((1, LANES)), _full((1, LANES)),
        _full((LANES, 512)), _full((LANES, 512)), _full((LANES, 512)), _full((LANES, 512)),
        _full((1, LANES)),
        pl.BlockSpec((tm, 512), pos), pl.BlockSpec((tm, LANES), pos),
        _full((LANES, 512)),
    ]
    out_specs = [pl.BlockSpec((tm, 512), row)] * 7 + [pl.BlockSpec((tm, POOL_WIDTH), row)]
    out_shape = [bf(512)] * 7 + [jax.ShapeDtypeStruct((t, POOL_WIDTH), F32)]
    return pl.pallas_call(
        _proj_kernel, grid=(t // tm,), in_specs=in_specs, out_specs=out_specs, out_shape=out_shape,
        compiler_params=_cparams(1),
    )(x2, p["g1"], p["win"], p["gq"], p["gk"], p["gckv"], p["gcqa"], p["gcqb"],
      p["wuqa"], p["wuqb"], p["wkk"], p["wkv"], p["gkn"], p["qtab"], p["ktab"], p["eplace"])


def _diff_kernel(slope_ref, lam_ref, q1_ref, q2_ref, k_ref, v_ref, gsub_ref, o_ref, bias_ref):
    h = pl.program_id(0)
    qi = pl.program_id(1)
    b = pl.program_id(2)
    tq, seq = bias_ref.shape

    @pl.when(b == 0)
    def _():
        qpos = qi * tq + lax.broadcasted_iota(jnp.int32, (tq, seq), 0)
        kpos = lax.broadcasted_iota(jnp.int32, (tq, seq), 1)
        bias_ref[...] = -slope_ref[h] * jnp.abs(qpos - kpos).astype(F32)

    k = k_ref[...]
    bias = bias_ref[...]
    s1 = lax.dot_general(q1_ref[...], k, NT_DIMS, preferred_element_type=F32) + bias
    s2 = lax.dot_general(q2_ref[...], k, NT_DIMS, preferred_element_type=F32) + bias
    p1 = jnp.exp(s1 - jnp.max(s1, axis=-1, keepdims=True))
    p2 = jnp.exp(s2 - jnp.max(s2, axis=-1, keepdims=True))
    w1 = 1.0 / jnp.sum(p1, axis=-1, keepdims=True)
    w2 = lam_ref[0] / jnp.sum(p2, axis=-1, keepdims=True)
    a = p1 * w1 - p2 * w2
    o = jnp.dot(a.astype(BF16), v_ref[...], preferred_element_type=F32)
    r = lax.rsqrt(jnp.mean(o * o, axis=-1, keepdims=True) + RMS_EPS)
    o_ref[...] = (o * r * gsub_ref[...]).astype(BF16)


def _diff_call(slopes, lam, dq1, dq2, dk, dv, gsub, batch, seq):
    t = dq1.shape[0]
    nq = seq // TQ
    qmap = lambda h, qi, b, *_: (b * nq + qi, h)
    kmap = lambda h, qi, b, *_: (b, h)
    grid_spec = pltpu.PrefetchScalarGridSpec(
        num_scalar_prefetch=2, grid=(DIFF_HEADS, nq, batch),
        in_specs=[pl.BlockSpec((TQ, HEAD_SLOT), qmap), pl.BlockSpec((TQ, HEAD_SLOT), qmap),
                  pl.BlockSpec((seq, HEAD_SLOT), kmap), pl.BlockSpec((seq, HEAD_SLOT), kmap),
                  pl.BlockSpec((1, HEAD_SLOT), lambda *_: (0, 0))],
        out_specs=pl.BlockSpec((TQ, HEAD_SLOT), qmap),
        scratch_shapes=[pltpu.VMEM((TQ, seq), F32)])
    return pl.pallas_call(
        _diff_kernel, grid_spec=grid_spec,
        out_shape=jax.ShapeDtypeStruct((t, DIFF_WIDTH), BF16),
        compiler_params=_cparams(3),
    )(slopes, lam, dq1, dq2, dk, dv, gsub)


def _mla_kernel(q_ref, k_ref, v_ref, o_ref):
    acc = None
    for hh in range(2):
        sl = slice(hh * HEAD_SLOT, (hh + 1) * HEAD_SLOT)
        s = lax.dot_general(q_ref[:, sl], k_ref[:, sl], NT_DIMS, preferred_element_type=F32)
        p = jnp.exp(s - jnp.max(s, axis=-1, keepdims=True))
        w = 1.0 / jnp.sum(p, axis=-1, keepdims=True)
        o = jnp.dot(p.astype(BF16), v_ref[:, sl], preferred_element_type=F32) * w
        acc = o if acc is None else acc + o
    o_ref[...] = acc.astype(BF16)


def _mla_call(qm, km, vm, batch, seq):
    t = qm.shape[0]
    nq = seq // TQ
    qmap = lambda p, qi, b: (b * nq + qi, p)
    kmap = lambda p, qi, b: (b, p)
    return pl.pallas_call(
        _mla_kernel, grid=(MLA_HEADS // 2, nq, batch),
        in_specs=[pl.BlockSpec((TQ, 2 * HEAD_SLOT), qmap), pl.BlockSpec((seq, 2 * HEAD_SLOT), kmap),
                  pl.BlockSpec((seq, 2 * HEAD_SLOT), kmap)],
        out_specs=pl.BlockSpec((TQ, HEAD_SLOT), qmap),
        out_shape=jax.ShapeDtypeStruct((t, MLA_WIDTH), BF16),
        compiler_params=_cparams(3),
    )(qm, km, vm)


def _pool_kernel(u_ref, w_ref, scale_ref, o_ref):
    u = u_ref[...]
    seq, width = u.shape
    zpad = jnp.zeros((POOL_PAD, width), F32)
    ue = jnp.concatenate([zpad, u, zpad], axis=0)
    n = seq + 2 * POOL_PAD

    def down(a, k):
        return pltpu.roll(a, k, axis=0)

    def up(a, k):
        return pltpu.roll(a, n - k, axis=0)

    a2 = ue + down(ue, 1)
    a4 = down(a2, 1) + up(a2, 1)
    a8 = down(a4, 2) + up(a4, 2)
    a16 = down(a8, 4) + up(a8, 4)
    core = slice(POOL_PAD, POOL_PAD + seq)
    lane = lax.broadcasted_iota(jnp.int32, (seq, width), 1)
    tpos = lax.broadcasted_iota(jnp.int32, (seq, width), 0)
    grp = lane // POOL_GROUP_DIM
    win_sum = jnp.where(grp == 0, a2[core], jnp.where(grp == 1, a4[core], jnp.where(grp == 2, a8[core], a16[core])))
    half = jnp.where(grp == 0, 1, jnp.where(grp == 1, 2, jnp.where(grp == 2, 4, 8)))
    lo_i = jnp.maximum(tpos - half, 0)
    hi_i = jnp.minimum(tpos + half - 1, seq - 1)
    cnt = (hi_i - lo_i + 1).astype(F32)
    pooled = win_sum / cnt - u
    mixed = jnp.dot(pooled.astype(BF16), w_ref[...], preferred_element_type=F32)
    o_ref[...] = (mixed * scale_ref[...]).astype(BF16)


def _pool_call(pu, w_bd, scale, batch, seq):
    t = pu.shape[0]
    return pl.pallas_call(
        _pool_kernel, grid=(batch,),
        in_specs=[pl.BlockSpec((seq, POOL_WIDTH), lambda b: (b, 0)),
                  _full((POOL_WIDTH, POOL_WIDTH)), _full((1, POOL_WIDTH))],
        out_specs=pl.BlockSpec((seq, POOL_WIDTH), lambda b: (b, 0)),
        out_shape=jax.ShapeDtypeStruct((t, POOL_WIDTH), BF16),
        compiler_params=_cparams(1),
    )(pu, w_bd, scale)


def _outproj_kernel(x_ref, od_ref, om_ref, op_ref, wo_ref, g2_ref, wrh_ref, wrl_ref, br_ref,
                    xo_ref, h2_ref, ri_ref, rg_ref, cnt_ref, carry_ref):
    i = pl.program_id(0)

    @pl.when(i == 0)
    def _():
        carry_ref[...] = jnp.zeros_like(carry_ref)

    xn = (x_ref[...]
          + jnp.dot(od_ref[...], wo_ref[0:DIFF_WIDTH, :], preferred_element_type=F32)
          + jnp.dot(om_ref[...], wo_ref[DIFF_WIDTH:DIFF_WIDTH + MLA_WIDTH, :], preferred_element_type=F32)
          + jnp.dot(op_ref[...], wo_ref[DIFF_WIDTH + MLA_WIDTH:, :], preferred_element_type=F32))
    xo_ref[...] = xn
    h2 = xn * lax.rsqrt(jnp.mean(xn * xn, axis=-1, keepdims=True) + RMS_EPS) * g2_ref[...]
    h2_ref[...] = h2

    h_hi = h2.astype(BF16)
    h_lo = (h2 - h_hi.astype(F32)).astype(BF16)
    logits = (jnp.dot(h_hi, wrh_ref[...], preferred_element_type=F32)
              + jnp.dot(h_lo, wrh_ref[...], preferred_element_type=F32)
              + jnp.dot(h_hi, wrl_ref[...], preferred_element_type=F32)
              + br_ref[...])
    tm = logits.shape[0]
    lane = lax.broadcasted_iota(jnp.int32, (tm, LANES), 1)
    lane_f = lane.astype(F32)
    neg = jnp.float32(-jnp.inf)
    big = jnp.float32(1e9)

    gmask = lane < N_GROUPS
    gl = jnp.where(gmask, logits, neg)
    gmax = jnp.max(gl, axis=-1, keepdims=True)
    gsum = jnp.sum(jnp.where(gmask, jnp.exp(gl - gmax), 0.0), axis=-1, keepdims=True)
    g_top = 1.0 / gsum
    g_idx = jnp.min(jnp.where(gl == gmax, lane_f, big), axis=-1, keepdims=True)

    e_lo = N_GROUPS + EXPERTS_PER_GROUP * g_idx
    emask = (lane_f >= e_lo) & (lane_f < e_lo + EXPERTS_PER_GROUP)
    el = jnp.where(emask, logits, neg)
    emax = jnp.max(el, axis=-1, keepdims=True)
    eexp = jnp.where(emask, jnp.exp(el - emax), 0.0)
    prob = eexp / jnp.sum(eexp, axis=-1, keepdims=True)
    pm = jnp.where(emask, prob, -1.0)
    p1 = jnp.max(pm, axis=-1, keepdims=True)
    i1 = jnp.min(jnp.where(pm == p1, lane_f, big), axis=-1, keepdims=True)
    pm2 = jnp.where(lane_f == i1, -1.0, pm)
    p2 = jnp.max(pm2, axis=-1, keepdims=True)
    i2 = jnp.min(jnp.where(pm2 == p2, lane_f, big), axis=-1, keepdims=True)
    denom = p1 + p2
    gate1 = g_top * p1 / denom
    gate2 = g_top * p2 / denom

    sel1 = lane_f == i1
    sel2 = lane_f == i2
    onehot = jnp.where(sel1 | sel2, 1.0, 0.0)
    rr = lax.broadcasted_iota(jnp.int32, (tm, tm), 0)
    cc = lax.broadcasted_iota(jnp.int32, (tm, tm), 1)
    ltri = jnp.where(cc < rr, 1.0, 0.0).astype(BF16)
    prefix = jnp.dot(ltri, onehot.astype(BF16), preferred_element_type=F32) + carry_ref[...]
    rank1 = jnp.sum(jnp.where(sel1, prefix, 0.0), axis=-1, keepdims=True)
    rank2 = jnp.sum(jnp.where(sel2, prefix, 0.0), axis=-1, keepdims=True)
    carry_ref[...] = carry_ref[...] + jnp.sum(onehot, axis=0, keepdims=True)
    cnt_ref[...] = carry_ref[...]

    info = jnp.where(lane == 0, i1 - N_GROUPS,
                     jnp.where(lane == 1, i2 - N_GROUPS,
                               jnp.where(lane == 2, rank1, jnp.where(lane == 3, rank2, 0.0))))
    ri_ref[...] = info.astype(jnp.int32)
    rg_ref[...] = jnp.where(lane == 0, gate1, jnp.where(lane == 1, gate2, 0.0))


def _outproj_call(x2, od, om, op, p):
    t = x2.shape[0]
    tm = TM_OUT
    row = lambda i: (i, 0)
    return pl.pallas_call(
        _outproj_kernel, grid=(t // tm,),
        in_specs=[pl.BlockSpec((tm, D_MODEL), row), pl.BlockSpec((tm, DIFF_WIDTH), row),
                  pl.BlockSpec((tm, MLA_WIDTH), row), pl.BlockSpec((tm, POOL_WIDTH), row),
                  _full((D_MODEL, D_MODEL)), _full((1, D_MODEL)),
                  _full((D_MODEL, LANES)), _full((D_MODEL, LANES)), _full((1, LANES))],
        out_specs=[pl.BlockSpec((tm, D_MODEL), row), pl.BlockSpec((tm, D_MODEL), row),
                   pl.BlockSpec((tm, LANES), row), pl.BlockSpec((tm, LANES), row),
                   _full((1, LANES))],
        out_shape=[jax.ShapeDtypeStruct((t, D_MODEL), F32), jax.ShapeDtypeStruct((t, D_MODEL), F32),
                   jax.ShapeDtypeStruct((t, LANES), jnp.int32), jax.ShapeDtypeStruct((t, LANES), F32),
                   jax.ShapeDtypeStruct((1, LANES), F32)],
        scratch_shapes=[pltpu.VMEM((1, LANES), F32)],
        compiler_params=_cparams(1),
    )(x2, od, om, op, p["wo"], p["g2"], p["wrh"], p["wrl"], p["br"])


def _expert_kernel(be_ref, nu_ref, idx0_ref, idxn_ref, h_hbm, wg_ref, wu_ref, wd_ref, ys_ref, buf, sem):
    i = pl.program_id(0)
    n_used = nu_ref[0]

    def issue(idx_ref, slot):
        def body(r, c):
            tok = idx_ref[0, 0, r]
            pltpu.make_async_copy(h_hbm.at[pl.ds(tok, 1), :], buf.at[slot, pl.ds(r, 1), :],
                                  sem.at[slot]).start()
            return c
        lax.fori_loop(0, ROUTE_BLOCK, body, 0, unroll=8)

    @pl.when(i == 0)
    def _():
        issue(idx0_ref, 0)

    @pl.when(i + 1 < n_used)
    def _():
        issue(idxn_ref, (i + 1) % 2)

    slot = i % 2

    @pl.when(i < n_used)
    def _():
        pltpu.make_async_copy(h_hbm.at[pl.ds(0, ROUTE_BLOCK), :], buf.at[slot], sem.at[slot]).wait()
        xb = buf[slot].astype(BF16)
        g = jnp.dot(xb, wg_ref[0].astype(BF16), preferred_element_type=F32)
        u = jnp.dot(xb, wu_ref[0].astype(BF16), preferred_element_type=F32)
        hmid = g * (1.0 / (1.0 + jnp.exp(-g))) * u
        ys_ref[...] = jnp.dot(hmid.astype(BF16), wd_ref[0].astype(BF16), preferred_element_type=F32)

    @pl.when(i >= n_used)
    def _():
        ys_ref[...] = jnp.zeros_like(ys_ref)


def _expert_call(block_eid, n_used, slot_tok3, h2, wg, wu, wd):
    n_blocks = slot_tok3.shape[0]
    grid_spec = pltpu.PrefetchScalarGridSpec(
        num_scalar_prefetch=2, grid=(n_blocks,),
        in_specs=[
            pl.BlockSpec((1, 1, ROUTE_BLOCK), lambda i, be, nu: (0, 0, 0), memory_space=pltpu.SMEM),
            pl.BlockSpec((1, 1, ROUTE_BLOCK), lambda i, be, nu: (jnp.minimum(i + 1, n_blocks - 1), 0, 0),
                         memory_space=pltpu.SMEM),
            pl.BlockSpec(memory_space=pl.ANY),
            pl.BlockSpec((1, D_MODEL, D_FF), lambda i, be, nu: (be[i], 0, 0)),
            pl.BlockSpec((1, D_MODEL, D_FF), lambda i, be, nu: (be[i], 0, 0)),
            pl.BlockSpec((1, D_FF, D_MODEL), lambda i, be, nu: (be[i], 0, 0)),
        ],
        out_specs=pl.BlockSpec((ROUTE_BLOCK, D_MODEL), lambda i, be, nu: (i, 0)),
        scratch_shapes=[pltpu.VMEM((2, ROUTE_BLOCK, D_MODEL), F32), pltpu.SemaphoreType.DMA((2,))])
    return pl.pallas_call(
        _expert_kernel, grid_spec=grid_spec,
        out_shape=jax.ShapeDtypeStruct((n_blocks * ROUTE_BLOCK, D_MODEL), F32),
        compiler_params=_cparams(1),
    )(block_eid, n_used, slot_tok3, slot_tok3, h2, wg, wu, wd)


def _combine_kernel(idx0_ref, idxn_ref, ys_hbm, x_ref, rg_ref, o_ref, buf, sem):
    i = pl.program_id(0)
    n = pl.num_programs(0)
    tm = x_ref.shape[0]

    def issue(idx_ref, slot):
        def body(r, c):
            for kk in range(2):
                d = idx_ref[0, 0, 2 * r + kk]
                pltpu.make_async_copy(ys_hbm.at[pl.ds(d, 1), :], buf.at[slot, kk, pl.ds(r, 1), :],
                                      sem.at[slot]).start()
            return c
        lax.fori_loop(0, tm, body, 0, unroll=4)

    @pl.when(i == 0)
    def _():
        issue(idx0_ref, 0)

    @pl.when(i + 1 < n)
    def _():
        issue(idxn_ref, (i + 1) % 2)

    slot = i % 2
    for kk in range(2):
        pltpu.make_async_copy(ys_hbm.at[pl.ds(0, tm), :], buf.at[slot, kk], sem.at[slot]).wait()
    rg = rg_ref[...]
    o_ref[...] = x_ref[...] + rg[:, 0:1] * buf[slot, 0] + rg[:, 1:2] * buf[slot, 1]


def _combine_call(dest3, ys, x2, rg):
    t = x2.shape[0]
    tm = TM_COMB
    n = t // tm
    row = lambda i: (i, 0)
    return pl.pallas_call(
        _combine_kernel, grid=(n,),
        in_specs=[
            pl.BlockSpec((1, 1, 2 * tm), lambda i: (0, 0, 0), memory_space=pltpu.SMEM),
            pl.BlockSpec((1, 1, 2 * tm), lambda i: (jnp.minimum(i + 1, n - 1), 0, 0), memory_space=pltpu.SMEM),
            pl.BlockSpec(memory_space=pl.ANY),
            pl.BlockSpec((tm, D_MODEL), row), pl.BlockSpec((tm, LANES), row)],
        out_specs=pl.BlockSpec((tm, D_MODEL), row),
        out_shape=jax.ShapeDtypeStruct((t, D_MODEL), F32),
        scratch_shapes=[pltpu.VMEM((2, 2, tm, D_MODEL), F32), pltpu.SemaphoreType.DMA((2,))],
        compiler_params=_cparams(1),
    )(dest3, dest3, ys, x2, rg)


def _swap_halves(a):
    half = a.shape[-1] // 2
    return jnp.concatenate([a[..., half:], a[..., :half]], axis=-1)


def _layer_params(l, seq, w):
    p = {}
    row = lambda v: v.reshape(1, -1).astype(F32)
    w_in = w["w_in"][l]
    kr_cols = w_in[:, 1856:1888]
    p["win"] = jnp.concatenate(
        [w_in[:, 0:1536], w_in[:, 1888:2144], w_in[:, 1728:1856], w_in[:, 1536:1728],
         kr_cols, _swap_halves(kr_cols)], axis=1).astype(BF16)
    p["g1"] = row(w["norm1_g"][l])
    p["gq"] = row(jnp.tile(w["diff_q_norm_g"][l], 2) * (DIFF_QK ** -0.5))
    p["gk"] = row(jnp.tile(w["diff_k_norm_g"][l], 2))
    p["gckv"] = row(w["mla_kv_lat_norm_g"][l])
    gcq = w["mla_q_lat_norm_g"][l]
    p["gcqa"] = row(gcq[:LANES])
    p["gcqb"] = row(jnp.concatenate([gcq[LANES:], jnp.zeros((2 * LANES - MLA_Q_RANK,), F32)]))

    wuq = w["mla_w_uq"][l].reshape(MLA_Q_RANK, MLA_HEADS, MLA_NOPE + MLA_ROPE)
    rope_w = wuq[:, :, MLA_NOPE:]
    wuq = jnp.concatenate([wuq[:, :, :MLA_NOPE], rope_w, _swap_halves(rope_w)], axis=-1)
    wuq = wuq.reshape(MLA_Q_RANK, MLA_HEADS * HEAD_SLOT)
    wuq = jnp.concatenate([wuq, jnp.zeros((2 * LANES - MLA_Q_RANK, wuq.shape[1]), F32)], axis=0).astype(BF16)
    p["wuqa"] = wuq[:LANES]
    p["wuqb"] = wuq[LANES:]

    wukv = w["mla_w_ukv"][l].reshape(MLA_KV_RANK, MLA_HEADS, MLA_NOPE + MLA_V)
    zk = jnp.zeros((MLA_KV_RANK, MLA_HEADS, HEAD_SLOT - MLA_NOPE), F32)
    p["wkk"] = jnp.concatenate([wukv[:, :, :MLA_NOPE], zk], axis=-1).reshape(MLA_KV_RANK, -1).astype(BF16)
    vcols = wukv[:, :, MLA_NOPE:]
    zv = jnp.zeros_like(vcols)
    even = (jnp.arange(MLA_HEADS) % 2 == 0)[None, :, None]
    wkv = jnp.concatenate([jnp.where(even, vcols, zv), jnp.where(even, zv, vcols)], axis=-1)
    p["wkv"] = wkv.reshape(MLA_KV_RANK, -1).astype(BF16)
    p["gkn"] = row(jnp.concatenate([w["mla_k_nope_norm_g"][l], jnp.zeros((HEAD_SLOT - MLA_NOPE,), F32)]))

    inv = 1.0 / (ROPE_BASE ** (jnp.arange(0, MLA_ROPE, 2, dtype=F32) / MLA_ROPE))
    ang = jnp.arange(seq, dtype=F32)[:, None] * inv[None, :]
    cosf = jnp.concatenate([jnp.cos(ang), jnp.cos(ang)], axis=-1)
    sinf = jnp.concatenate([-jnp.sin(ang), jnp.sin(ang)], axis=-1)
    scale = (MLA_NOPE + MLA_ROPE) ** -0.5
    gqr = w["mla_q_rope_norm_g"][l]
    q_head = jnp.concatenate([jnp.broadcast_to(w["mla_q_nope_norm_g"][l][None, :], (seq, MLA_NOPE)),
                              gqr[None, :] * cosf, _swap_halves(gqr)[None, :] * sinf], axis=-1) * scale
    p["qtab"] = jnp.tile(q_head, (1, MLA_HEADS))
    gkr = w["mla_k_rope_norm_g"][l]
    p["ktab"] = jnp.concatenate([jnp.zeros((seq, MLA_NOPE), F32), gkr[None, :] * cosf,
                                 _swap_halves(gkr)[None, :] * sinf], axis=-1)
    src = jnp.arange(LANES)
    dst = jnp.arange(MLA_HEADS * HEAD_SLOT)
    src_j = jnp.where(src >= MLA_NOPE, (src - MLA_NOPE) % MLA_ROPE, -1)
    dst_l = dst % HEAD_SLOT
    dst_j = jnp.where(dst_l >= MLA_NOPE, (dst_l - MLA_NOPE) % MLA_ROPE, -2)
    p["eplace"] = (src_j[:, None] == dst_j[None, :]).astype(BF16)

    pw = w["pool_w"][l]
    bd = jnp.zeros((POOL_WIDTH, POOL_WIDTH), F32)
    for g in range(POOL_GROUPS):
        s0 = g * POOL_GROUP_DIM
        bd = bd.at[s0:s0 + POOL_GROUP_DIM, s0:s0 + POOL_GROUP_DIM].set(pw[g])
    p["pool_w"] = bd.astype(BF16)
    p["pool_scale"] = row(w["pool_scale"][l])

    lam_init = 0.8 - 0.6 * math.exp(-0.3 * l)
    lv = w["diff_lambda"][l].astype(F32)
    p["lam"] = (jnp.exp(jnp.sum(lv[0] * lv[1])) - jnp.exp(jnp.sum(lv[2] * lv[3])) + lam_init).reshape(1)
    p["gsub"] = row(w["diff_sub_norm_g"][l] * (1.0 - lam_init))

    p["wo"] = w["w_out"][l].astype(BF16)
    p["g2"] = row(w["norm2_g"][l])
    wr = jnp.concatenate([w["router_group_w"][l], w["router_expert_w"][l],
                          jnp.zeros((D_MODEL, LANES - N_GROUPS - N_EXPERTS), F32)], axis=1)
    wr_hi = wr.astype(BF16)
    p["wrh"] = wr_hi
    p["wrl"] = (wr - wr_hi.astype(F32)).astype(BF16)
    p["br"] = row(jnp.concatenate([w["router_group_b"][l], w["router_expert_b"][l],
                                   jnp.zeros((LANES - N_GROUPS - N_EXPERTS,), F32)]))
    return p


def kernel(x, norm1_g, w_in, diff_q_norm_g, diff_k_norm_g, diff_lambda, diff_sub_norm_g, mla_q_lat_norm_g, mla_kv_lat_norm_g, mla_w_uq, mla_w_ukv, mla_q_nope_norm_g, mla_q_rope_norm_g, mla_k_nope_norm_g, mla_k_rope_norm_g, pool_w, pool_scale, w_out, norm2_g, router_group_w, router_group_b, router_expert_w, router_expert_b, expert_w_gate, expert_w_up, expert_w_down):
    w = dict(norm1_g=norm1_g, w_in=w_in, diff_q_norm_g=diff_q_norm_g, diff_k_norm_g=diff_k_norm_g,
             diff_lambda=diff_lambda, diff_sub_norm_g=diff_sub_norm_g, mla_q_lat_norm_g=mla_q_lat_norm_g,
             mla_kv_lat_norm_g=mla_kv_lat_norm_g, mla_w_uq=mla_w_uq, mla_w_ukv=mla_w_ukv,
             mla_q_nope_norm_g=mla_q_nope_norm_g, mla_q_rope_norm_g=mla_q_rope_norm_g,
             mla_k_nope_norm_g=mla_k_nope_norm_g, mla_k_rope_norm_g=mla_k_rope_norm_g,
             pool_w=pool_w, pool_scale=pool_scale, w_out=w_out, norm2_g=norm2_g,
             router_group_w=router_group_w, router_group_b=router_group_b,
             router_expert_w=router_expert_w, router_expert_b=router_expert_b)
    batch, seq, d = x.shape
    t = batch * seq
    n_assign = 2 * t
    n_blocks = n_assign // ROUTE_BLOCK + N_EXPERTS
    slopes = 2.0 ** (-8.0 * jnp.arange(1, DIFF_HEADS + 1, dtype=F32) / DIFF_HEADS)
    token_of_assign = jnp.repeat(jnp.arange(t, dtype=jnp.int32), 2)

    x2 = x.reshape(t, d)
    for l in range(DEPTH):
        p = _layer_params(l, seq, w)
        dq1, dq2, dk, dv, qm, km, vm, pu = _proj_call(x2, p, seq)
        o_diff = _diff_call(slopes, p["lam"], dq1, dq2, dk, dv, p["gsub"], batch, seq)
        o_mla = _mla_call(qm, km, vm, batch, seq)
        o_pool = _pool_call(pu, p["pool_w"], p["pool_scale"], batch, seq)
        x2, h2, route_i, route_g, counts = _outproj_call(x2, o_diff, o_mla, o_pool, p)

        cnt = counts[0, N_GROUPS:N_GROUPS + N_EXPERTS].astype(jnp.int32)
        padded = (cnt + ROUTE_BLOCK - 1) // ROUTE_BLOCK * ROUTE_BLOCK
        padded_ends = jnp.cumsum(padded)
        padded_starts = padded_ends - padded
        dest = padded_starts[route_i[:, 0:2]] + route_i[:, 2:4]
        block_start = jnp.arange(n_blocks, dtype=jnp.int32) * ROUTE_BLOCK
        block_eid = jnp.minimum(jnp.sum(block_start[:, None] >= padded_ends[None, :], axis=1),
                                N_EXPERTS - 1).astype(jnp.int32)
        n_used = (padded_ends[-1] // ROUTE_BLOCK).astype(jnp.int32).reshape(1)
        slot_tok = jnp.zeros((n_blocks * ROUTE_BLOCK,), jnp.int32).at[dest.reshape(-1)].set(token_of_assign)

        ys = _expert_call(block_eid, n_used, slot_tok.reshape(n_blocks, 1, ROUTE_BLOCK), h2,
                          expert_w_gate[l], expert_w_up[l], expert_w_down[l])
        x2 = _combine_call(dest.reshape(t // TM_COMB, 1, 2 * TM_COMB), ys, x2, route_g)
    return x2.reshape(batch, seq, d)
```

```python
import functools
import math

import jax
import jax.numpy as jnp
from jax import lax
from jax.experimental import pallas as pl
from jax.experimental.pallas import tpu as pltpu

F32 = jnp.float32
BF16 = jnp.bfloat16

D_MODEL = 1024
DEPTH = 2
DIFF_HEADS = 4
DIFF_QK = 64
DIFF_V = 128
DIFF_WIDTH = 512
MLA_HEADS = 4
MLA_NOPE = 64
MLA_ROPE = 32
MLA_V = 64
MLA_Q_RANK = 192
MLA_KV_RANK = 128
MLA_WIDTH = 256
ROPE_BASE = 10000.0
POOL_WIDTH = 256
POOL_GROUPS = 4
POOL_GROUP_DIM = 64
POOL_WINDOWS = (2, 4, 8, 16)
N_GROUPS = 4
EXPERTS_PER_GROUP = 8
N_EXPERTS = 32
D_FF = 256
ROUTE_BLOCK = 256
RMS_EPS = 1e-6

LANES = 128
HEAD_SLOT = 128
PROJ_WIDTH = 2176
POOL_PAD = 16
VMEM_LIMIT = 48 * 1024 * 1024

TM_PROJ = 512
TQ = 256
TM_OUT = 256
TM_COMB = 256

NT_DIMS = (((1,), (1,)), ((), ()))


def _cparams(n_axes):
    return pltpu.CompilerParams(dimension_semantics=("arbitrary",) * n_axes,
                                vmem_limit_bytes=VMEM_LIMIT)


def _full(shape):
    return pl.BlockSpec(shape, lambda *_: (0,) * len(shape))


def _proj_kernel(x_ref, g1_ref, win_ref, gq_ref, gk_ref, gckv_ref, gcqa_ref, gcqb_ref,
                 wuqa_ref, wuqb_ref, wkk_ref, wkv_ref, gkn_ref, qtab_ref, ktab_ref, eplace_ref,
                 dq1_ref, dq2_ref, dk_ref, dv_ref, qm_ref, km_ref, vm_ref, pu_ref):
    x = x_ref[...]
    xn = x * lax.rsqrt(jnp.mean(x * x, axis=-1, keepdims=True) + RMS_EPS) * g1_ref[...]
    proj = jnp.dot(xn.astype(BF16), win_ref[...], preferred_element_type=F32)

    tm = x.shape[0]
    lane = lax.broadcasted_iota(jnp.int32, (tm, LANES), 1)
    lo = lane < DIFF_QK

    def half_norm(c, g_row):
        sq = c * c
        s_lo = jnp.sum(jnp.where(lo, sq, 0.0), axis=-1, keepdims=True)
        s_hi = jnp.sum(jnp.where(lo, 0.0, sq), axis=-1, keepdims=True)
        r = jnp.where(lo, lax.rsqrt(s_lo / DIFF_QK + RMS_EPS), lax.rsqrt(s_hi / DIFF_QK + RMS_EPS))
        return c * r * g_row

    for h in range(DIFF_HEADS):
        sl = slice(h * HEAD_SLOT, (h + 1) * HEAD_SLOT)
        qn = half_norm(proj[:, sl], gq_ref[...])
        dq1_ref[:, sl] = jnp.where(lo, qn, 0.0).astype(BF16)
        dq2_ref[:, sl] = jnp.where(lo, 0.0, qn).astype(BF16)
        ksl = slice(512 + h * HEAD_SLOT, 512 + (h + 1) * HEAD_SLOT)
        dk_ref[:, sl] = half_norm(proj[:, ksl], gk_ref[...]).astype(BF16)
    dv_ref[...] = proj[:, 1024:1536].astype(BF16)
    pu_ref[...] = proj[:, 1536:1792]

    ckv = proj[:, 1792:1920]
    ckvn = ckv * lax.rsqrt(jnp.mean(ckv * ckv, axis=-1, keepdims=True) + RMS_EPS) * gckv_ref[...]
    ckvn = ckvn.astype(BF16)
    cqa = proj[:, 1920:2048]
    last = proj[:, 2048:2176]
    lsq = last * last
    ss_q = (jnp.sum(cqa * cqa, axis=-1, keepdims=True)
            + jnp.sum(jnp.where(lo, lsq, 0.0), axis=-1, keepdims=True))
    r_q = lax.rsqrt(ss_q / MLA_Q_RANK + RMS_EPS)
    q_raw = (jnp.dot((cqa * r_q * gcqa_ref[...]).astype(BF16), wuqa_ref[...], preferred_element_type=F32)
             + jnp.dot((last * r_q * gcqb_ref[...]).astype(BF16), wuqb_ref[...], preferred_element_type=F32))

    rope_lanes = (lane >= MLA_NOPE) & (lane < MLA_NOPE + MLA_ROPE)
    ss_kr = jnp.sum(jnp.where(rope_lanes, lsq, 0.0), axis=-1, keepdims=True)
    kr_terms = last * lax.rsqrt(ss_kr / MLA_ROPE + RMS_EPS) * ktab_ref[...]
    kr_placed = jnp.dot(kr_terms.astype(BF16), eplace_ref[...], preferred_element_type=F32)

    k_raw = jnp.dot(ckvn, wkk_ref[...], preferred_element_type=F32)
    vm_ref[...] = jnp.dot(ckvn, wkv_ref[...], preferred_element_type=F32).astype(BF16)
    qtab = qtab_ref[...]
    for h in range(MLA_HEADS):
        sl = slice(h * HEAD_SLOT, (h + 1) * HEAD_SLOT)
        c = q_raw[:, sl]
        sq = c * c
        s_n = jnp.sum(jnp.where(lo, sq, 0.0), axis=-1, keepdims=True)
        s_r = jnp.sum(jnp.where(rope_lanes, sq, 0.0), axis=-1, keepdims=True)
        r = jnp.where(lo, lax.rsqrt(s_n / MLA_NOPE + RMS_EPS), lax.rsqrt(s_r / MLA_ROPE + RMS_EPS))
        qm_ref[:, sl] = (c * r * qtab[:, sl]).astype(BF16)
        kc = k_raw[:, sl]
        r_k = lax.rsqrt(jnp.sum(kc * kc, axis=-1, keepdims=True) / MLA_NOPE + RMS_EPS)
        km_ref[:, sl] = (kc * r_k * gkn_ref[...] + kr_placed[:, sl]).astype(BF16)


def _proj_call(x2, p, seq):
    t = x2.shape[0]
    tm = TM_PROJ
    n_pos = seq // tm
    row = lambda i: (i, 0)
    pos = lambda i: (i % n_pos, 0)
    bf = lambda w: jax.ShapeDtypeStruct((t, w), BF16)
    in_specs = [
        pl.BlockSpec((tm, D_MODEL), row),
        _full((1, D_MODEL)), _full((D_MODEL, PROJ_WIDTH)),
        _full((1, LANES)), _full((1, LANES)), _full((1, LANES)), _full((1, LANES)), _full((1, LANES)),
        _full((LANES, 512)), _full((LANES, 512)), _full((LANES, 512)), _full((LANES, 512)),
        _full((1, LANES)),
        pl.BlockSpec((tm, 512), pos), pl.BlockSpec((tm, LANES), pos),
        _full((LANES, 512)),
    ]
    out_specs = [pl.BlockSpec((tm, 512), row)] * 7 + [pl.BlockSpec((tm, POOL_WIDTH), row)]
    out_shape = [bf(512)] * 7 + [jax.ShapeDtypeStruct((t, POOL_WIDTH), F32)]
    return pl.pallas_call(
        _proj_kernel, grid=(t // tm,), in_specs=in_specs, out_specs=out_specs, out_shape=out_shape,
        compiler_params=_cparams(1),
    )(x2, p["g1"], p["win"], p["gq"], p["gk"], p["gckv"], p["gcqa"], p["gcqb"],
      p["wuqa"], p["wuqb"], p["wkk"], p["wkv"], p["gkn"], p["qtab"], p["ktab"], p["eplace"])


def _unflatten(n, sizes):
    n = jnp.minimum(n, math.prod(sizes) - 1)
    coords = []
    for size in reversed(sizes):
        coords.append(n % size)
        n = n // size
    return tuple(reversed(coords))


def _two_stage(n, stage, bufs):
    (s0, m0), (s1, m1) = bufs

    @pl.when(n == 0)
    def _():
        s1[...] = jnp.zeros_like(s1)
        m1[...] = jnp.zeros_like(m1)

    @pl.when(n % 2 == 0)
    def _():
        stage((s0, m0), (s1, m1))

    @pl.when(n % 2 == 1)
    def _():
        stage((s1, m1), (s0, m0))


def _two_stage_scratch(seq):
    pair = [pltpu.VMEM((2 * TQ, seq), F32), pltpu.VMEM((2 * TQ, LANES), F32)]
    return pair + pair


def _softmax_pv_tile(s_prev, m_rows, c, tq, lsum, acc, v_tiles, exp_fn):
    n_half = tq // LANES
    ps = [exp_fn(s_prev[:, (c * n_half + j) * LANES:(c * n_half + j + 1) * LANES] - m_rows)
          for j in range(n_half)]
    for ch in ps:
        lsum = ch if lsum is None else lsum + ch
    pb = jnp.concatenate(ps, axis=1).astype(BF16)
    for g, vt in enumerate(v_tiles):
        pv = jnp.dot(pb[g * tq:(g + 1) * tq], vt, preferred_element_type=F32)
        acc[g] = pv if acc[g] is None else acc[g] + pv
    return lsum


def _running_max(mx, sc):
    for j in range(sc.shape[1] // LANES):
        chunk = sc[:, j * LANES:(j + 1) * LANES]
        mx = chunk if mx is None else jnp.maximum(mx, chunk)
    return mx


def _diff_kernel(lam_ref, q1_ref, q2_ref, qx_ref, k_ref, kx_ref, bd_ref, v_ref, gsub_ref, o_ref,
                 s0_ref, m0_ref, s1_ref, m1_ref, *, sizes):
    n = pl.program_id(0)
    tq = q1_ref.shape[0]
    n_kt = k_ref.shape[0] // tq
    qi_cur = _unflatten(n, sizes)[2]
    qi_prev = _unflatten(jnp.maximum(n - 1, 0), sizes)[2]

    def stage(cur, prev):
        s_cur, m_cur = cur
        s_prev, m_prev = prev
        q1, q2 = q1_ref[...], q2_ref[...]
        qx_left = qx_ref[0, 0]
        qx_right = -qx_left
        bd = bd_ref[0]
        bd2 = jnp.concatenate([bd, bd], axis=0)
        mx = None
        m_rows = m_prev[...]
        lsum = None
        acc = [None, None]
        for c in range(n_kt):
            start = pl.multiple_of(((qi_prev + c) % n_kt) * tq, tq)
            vt = v_ref[pl.ds(start, tq), :]
            lsum = _softmax_pv_tile(s_prev, m_rows, c, tq, lsum, acc, [vt, vt], jnp.exp)

            tile = (qi_cur + c) % n_kt
            start = pl.multiple_of(tile * tq, tq)
            if c == 0:
                qx = jnp.zeros_like(qx_left)
            else:
                qx = jnp.where(qi_cur + c >= n_kt, qx_left, qx_right)
            qq = jnp.concatenate([jnp.concatenate([q1, qx], axis=1),
                                  jnp.concatenate([q2, qx], axis=1)], axis=0)
            kk = jnp.concatenate([k_ref[pl.ds(start, tq), :], kx_ref[pl.ds(start, tq), :]], axis=1)
            sc = lax.dot_general(qq, kk, NT_DIMS, preferred_element_type=F32)
            if c == 0:
                sc = sc + bd2
            s_cur[:, c * tq:(c + 1) * tq] = sc
            mx = _running_max(mx, sc)
        m_cur[...] = jnp.broadcast_to(jnp.max(mx, axis=-1, keepdims=True), m_cur.shape)
        l = jnp.sum(lsum, axis=-1, keepdims=True)
        o = acc[0] * (1.0 / l[0:tq]) - acc[1] * (lam_ref[0] / l[tq:2 * tq])
        r = lax.rsqrt(jnp.mean(o * o, axis=-1, keepdims=True) + RMS_EPS)
        o_ref[...] = (o * r * gsub_ref[...]).astype(BF16)

    _two_stage(n, stage, ((s0_ref, m0_ref), (s1_ref, m1_ref)))


def _alibi_tables(seq):
    nq = seq // TQ
    slopes = 2.0 ** (-8.0 * jnp.arange(1, DIFF_HEADS + 1, dtype=F32) / DIFF_HEADS)
    pos = jnp.arange(seq, dtype=jnp.int32)
    hi = (pos // 256).astype(F32)
    lo = (pos % 256).astype(F32)
    s4 = slopes[:, None]
    ones = jnp.ones((DIFF_HEADS, seq), F32)
    q_left = jnp.stack([-s4 * 256.0 * hi[None], -s4 * lo[None], s4 * 256.0 * ones, s4 * ones], axis=-1)
    qx = jnp.concatenate([q_left, jnp.zeros((DIFF_HEADS, seq, HEAD_SLOT - 4), F32)], axis=-1)
    qx = qx.reshape(DIFF_HEADS, nq, TQ, HEAD_SLOT).astype(BF16)
    k_cols = jnp.stack([jnp.ones((seq,), F32), jnp.ones((seq,), F32), hi, lo], axis=-1)
    kx = jnp.concatenate([k_cols, jnp.zeros((seq, HEAD_SLOT - 4), F32)], axis=-1).astype(BF16)
    loc = jnp.arange(TQ, dtype=jnp.int32)
    bd = -slopes[:, None, None] * jnp.abs(loc[:, None] - loc[None, :]).astype(F32)[None]
    return qx, kx, bd


def _diff_call(lam, dq1, dq2, dk, dv, gsub, tabs, batch, seq):
    t = dq1.shape[0]
    nq = seq // TQ
    qx, kx, bd = tabs
    sizes = (batch, DIFF_HEADS, nq)
    cur = lambda n: _unflatten(n, sizes)
    prev = lambda n: _unflatten(jnp.maximum(n - 1, 0), sizes)

    def qmap(n, *_):
        b, h, qi = cur(n)
        return (b * nq + qi, h)

    def kmap(n, *_):
        b, h, qi = cur(n)
        return (b, h)

    def vmap(n, *_):
        b, h, qi = prev(n)
        return (b, h)

    def omap(n, *_):
        b, h, qi = prev(n)
        return (b * nq + qi, h)

    grid_spec = pltpu.PrefetchScalarGridSpec(
        num_scalar_prefetch=1, grid=(math.prod(sizes) + 1,),
        in_specs=[pl.BlockSpec((TQ, HEAD_SLOT), qmap), pl.BlockSpec((TQ, HEAD_SLOT), qmap),
                  pl.BlockSpec((1, 1, TQ, HEAD_SLOT), lambda n, *_: cur(n)[1:] + (0, 0)),
                  pl.BlockSpec((seq, HEAD_SLOT), kmap),
                  pl.BlockSpec((seq, HEAD_SLOT), lambda *_: (0, 0)),
                  pl.BlockSpec((1, TQ, TQ), lambda n, *_: (cur(n)[1], 0, 0)),
                  pl.BlockSpec((seq, HEAD_SLOT), vmap),
                  pl.BlockSpec((1, HEAD_SLOT), lambda *_: (0, 0))],
        out_specs=pl.BlockSpec((TQ, HEAD_SLOT), omap),
        scratch_shapes=_two_stage_scratch(seq))
    return pl.pallas_call(
        functools.partial(_diff_kernel, sizes=sizes), grid_spec=grid_spec,
        out_shape=jax.ShapeDtypeStruct((t, DIFF_WIDTH), BF16),
        compiler_params=_cparams(1),
    )(lam, dq1, dq2, qx, dk, kx, bd, dv, gsub)


def _mla_kernel(q_ref, k_ref, v_ref, o_ref, s0_ref, m0_ref, s1_ref, m1_ref):
    n = pl.program_id(0)
    tq = q_ref.shape[0]
    n_kt = k_ref.shape[0] // tq

    def stage(cur, prev):
        s_cur, m_cur = cur
        s_prev, m_prev = prev
        mx = [None, None]
        m_rows = m_prev[...]
        lsum = None
        acc = [None, None]
        for c in range(n_kt):
            rows = slice(c * tq, (c + 1) * tq)
            v_tiles = [v_ref[rows, hh * HEAD_SLOT:(hh + 1) * HEAD_SLOT] for hh in range(2)]
            lsum = _softmax_pv_tile(s_prev, m_rows, c, tq, lsum, acc, v_tiles, jnp.exp2)
            for hh in range(2):
                sl = slice(hh * HEAD_SLOT, (hh + 1) * HEAD_SLOT)
                sc = lax.dot_general(q_ref[:, sl], k_ref[rows, sl], NT_DIMS, preferred_element_type=F32)
                s_cur[hh * tq:(hh + 1) * tq, rows] = sc
                mx[hh] = _running_max(mx[hh], sc)
        mx = jnp.concatenate(mx, axis=0)
        m_cur[...] = jnp.broadcast_to(jnp.max(mx, axis=-1, keepdims=True), m_cur.shape)
        l = jnp.sum(lsum, axis=-1, keepdims=True)
        o = acc[0] * (1.0 / l[0:tq]) + acc[1] * (1.0 / l[tq:2 * tq])
        o_ref[...] = o.astype(BF16)

    _two_stage(n, stage, ((s0_ref, m0_ref), (s1_ref, m1_ref)))


def _mla_call(qm, km, vm, batch, seq):
    t = qm.shape[0]
    nq = seq // TQ
    sizes = (batch, MLA_HEADS // 2, nq)
    cur = lambda n: _unflatten(n, sizes)
    prev = lambda n: _unflatten(jnp.maximum(n - 1, 0), sizes)

    def qmap(n):
        b, p, qi = cur(n)
        return (b * nq + qi, p)

    def kmap(n):
        b, p, qi = cur(n)
        return (b, p)

    def vmap(n):
        b, p, qi = prev(n)
        return (b, p)

    def omap(n):
        b, p, qi = prev(n)
        return (b * nq + qi, p)

    return pl.pallas_call(
        _mla_kernel, grid=(math.prod(sizes) + 1,),
        in_specs=[pl.BlockSpec((TQ, 2 * HEAD_SLOT), qmap), pl.BlockSpec((seq, 2 * HEAD_SLOT), kmap),
                  pl.BlockSpec((seq, 2 * HEAD_SLOT), vmap)],
        out_specs=pl.BlockSpec((TQ, HEAD_SLOT), omap),
        out_shape=jax.ShapeDtypeStruct((t, MLA_WIDTH), BF16),
        scratch_shapes=_two_stage_scratch(seq),
        compiler_params=_cparams(1),
    )(qm, km, vm)


def _pool_kernel(u_ref, w_ref, scale_ref, o_ref):
    u = u_ref[...]
    seq, width = u.shape
    zpad = jnp.zeros((POOL_PAD, width), F32)
    ue = jnp.concatenate([zpad, u, zpad], axis=0)
    n = seq + 2 * POOL_PAD

    def down(a, k):
        return pltpu.roll(a, k, axis=0)

    def up(a, k):
        return pltpu.roll(a, n - k, axis=0)

    a2 = ue + down(ue, 1)
    a4 = down(a2, 1) + up(a2, 1)
    a8 = down(a4, 2) + up(a4, 2)
    a16 = down(a8, 4) + up(a8, 4)
    core = slice(POOL_PAD, POOL_PAD + seq)
    lane = lax.broadcasted_iota(jnp.int32, (seq, width), 1)
    tpos = lax.broadcasted_iota(jnp.int32, (seq, width), 0)
    grp = lane // POOL_GROUP_DIM
    win_sum = jnp.where(grp == 0, a2[core], jnp.where(grp == 1, a4[core], jnp.where(grp == 2, a8[core], a16[core])))
    half = jnp.where(grp == 0, 1, jnp.where(grp == 1, 2, jnp.where(grp == 2, 4, 8)))
    lo_i = jnp.maximum(tpos - half, 0)
    hi_i = jnp.minimum(tpos + half - 1, seq - 1)
    cnt = (hi_i - lo_i + 1).astype(F32)
    pooled = win_sum / cnt - u
    mixed = jnp.dot(pooled.astype(BF16), w_ref[...], preferred_element_type=F32)
    o_ref[...] = (mixed * scale_ref[...]).astype(BF16)


def _pool_call(pu, w_bd, scale, batch, seq):
    t = pu.shape[0]
    return pl.pallas_call(
        _pool_kernel, grid=(batch,),
        in_specs=[pl.BlockSpec((seq, POOL_WIDTH), lambda b: (b, 0)),
                  _full((POOL_WIDTH, POOL_WIDTH)), _full((1, POOL_WIDTH))],
        out_specs=pl.BlockSpec((seq, POOL_WIDTH), lambda b: (b, 0)),
        out_shape=jax.ShapeDtypeStruct((t, POOL_WIDTH), BF16),
        compiler_params=_cparams(1),
    )(pu, w_bd, scale)


def _outproj_kernel(x_ref, od_ref, om_ref, op_ref, wo_ref, g2_ref, wrh_ref, wrl_ref, br_ref,
                    xo_ref, h2_ref, ri_ref, rg_ref, cnt_ref, carry_ref):
    i = pl.program_id(0)

    @pl.when(i == 0)
    def _():
        carry_ref[...] = jnp.zeros_like(carry_ref)

    xn = (x_ref[...]
          + jnp.dot(od_ref[...], wo_ref[0:DIFF_WIDTH, :], preferred_element_type=F32)
          + jnp.dot(om_ref[...], wo_ref[DIFF_WIDTH:DIFF_WIDTH + MLA_WIDTH, :], preferred_element_type=F32)
          + jnp.dot(op_ref[...], wo_ref[DIFF_WIDTH + MLA_WIDTH:, :], preferred_element_type=F32))
    xo_ref[...] = xn
    h2 = xn * lax.rsqrt(jnp.mean(xn * xn, axis=-1, keepdims=True) + RMS_EPS) * g2_ref[...]
    h2_ref[...] = h2

    h_hi = h2.astype(BF16)
    h_lo = (h2 - h_hi.astype(F32)).astype(BF16)
    logits = (jnp.dot(h_hi, wrh_ref[...], preferred_element_type=F32)
              + jnp.dot(h_lo, wrh_ref[...], preferred_element_type=F32)
              + jnp.dot(h_hi, wrl_ref[...], preferred_element_type=F32)
              + br_ref[...])
    tm = logits.shape[0]
    lane = lax.broadcasted_iota(jnp.int32, (tm, LANES), 1)
    lane_f = lane.astype(F32)
    neg = jnp.float32(-jnp.inf)
    big = jnp.float32(1e9)

    gmask = lane < N_GROUPS
    gl = jnp.where(gmask, logits, neg)
    gmax = jnp.max(gl, axis=-1, keepdims=True)
    gsum = jnp.sum(jnp.where(gmask, jnp.exp(gl - gmax), 0.0), axis=-1, keepdims=True)
    g_top = 1.0 / gsum
    g_idx = jnp.min(jnp.where(gl == gmax, lane_f, big), axis=-1, keepdims=True)

    e_lo = N_GROUPS + EXPERTS_PER_GROUP * g_idx
    emask = (lane_f >= e_lo) & (lane_f < e_lo + EXPERTS_PER_GROUP)
    el = jnp.where(emask, logits, neg)
    emax = jnp.max(el, axis=-1, keepdims=True)
    eexp = jnp.where(emask, jnp.exp(el - emax), 0.0)
    prob = eexp / jnp.sum(eexp, axis=-1, keepdims=True)
    pm = jnp.where(emask, prob, -1.0)
    p1 = jnp.max(pm, axis=-1, keepdims=True)
    i1 = jnp.min(jnp.where(pm == p1, lane_f, big), axis=-1, keepdims=True)
    pm2 = jnp.where(lane_f == i1, -1.0, pm)
    p2 = jnp.max(pm2, axis=-1, keepdims=True)
    i2 = jnp.min(jnp.where(pm2 == p2, lane_f, big), axis=-1, keepdims=True)
    denom = p1 + p2
    gate1 = g_top * p1 / denom
    gate2 = g_top * p2 / denom

    sel1 = lane_f == i1
    sel2 = lane_f == i2
    onehot = jnp.where(sel1 | sel2, 1.0, 0.0)
    rr = lax.broadcasted_iota(jnp.int32, (tm, tm), 0)
    cc = lax.broadcasted_iota(jnp.int32, (tm, tm), 1)
    ltri = jnp.where(cc < rr, 1.0, 0.0).astype(BF16)
    prefix = jnp.dot(ltri, onehot.astype(BF16), preferred_element_type=F32) + carry_ref[...]
    rank1 = jnp.sum(jnp.where(sel1, prefix, 0.0), axis=-1, keepdims=True)
    rank2 = jnp.sum(jnp.where(sel2, prefix, 0.0), axis=-1, keepdims=True)
    carry_ref[...] = carry_ref[...] + jnp.sum(onehot, axis=0, keepdims=True)
    cnt_ref[...] = carry_ref[...]

    info = jnp.where(lane == 0, i1 - N_GROUPS,
                     jnp.where(lane == 1, i2 - N_GROUPS,
                               jnp.where(lane == 2, rank1, jnp.where(lane == 3, rank2, 0.0))))
    ri_ref[...] = info.astype(jnp.int32)
    rg_ref[...] = jnp.where(lane == 0, gate1, jnp.where(lane == 1, gate2, 0.0))


def _outproj_call(x2, od, om, op, p):
    t = x2.shape[0]
    tm = TM_OUT
    row = lambda i: (i, 0)
    return pl.pallas_call(
        _outproj_kernel, grid=(t // tm,),
        in_specs=[pl.BlockSpec((tm, D_MODEL), row), pl.BlockSpec((tm, DIFF_WIDTH), row),
                  pl.BlockSpec((tm, MLA_WIDTH), row), pl.BlockSpec((tm, POOL_WIDTH), row),
                  _full((D_MODEL, D_MODEL)), _full((1, D_MODEL)),
                  _full((D_MODEL, LANES)), _full((D_MODEL, LANES)), _full((1, LANES))],
        out_specs=[pl.BlockSpec((tm, D_MODEL), row), pl.BlockSpec((tm, D_MODEL), row),
                   pl.BlockSpec((tm, LANES), row), pl.BlockSpec((tm, LANES), row),
                   _full((1, LANES))],
        out_shape=[jax.ShapeDtypeStruct((t, D_MODEL), F32), jax.ShapeDtypeStruct((t, D_MODEL), F32),
                   jax.ShapeDtypeStruct((t, LANES), jnp.int32), jax.ShapeDtypeStruct((t, LANES), F32),
                   jax.ShapeDtypeStruct((1, LANES), F32)],
        scratch_shapes=[pltpu.VMEM((1, LANES), F32)],
        compiler_params=_cparams(1),
    )(x2, od, om, op, p["wo"], p["g2"], p["wrh"], p["wrl"], p["br"])


def _expert_kernel(be_ref, nu_ref, idx0_ref, idxn_ref, h_hbm, wg_ref, wu_ref, wd_ref, ys_ref, buf, sem):
    i = pl.program_id(0)
    n_used = nu_ref[0]

    def issue(idx_ref, slot):
        def body(r, c):
            tok = idx_ref[0, 0, r]
            pltpu.make_async_copy(h_hbm.at[pl.ds(tok, 1), :], buf.at[slot, pl.ds(r, 1), :],
                                  sem.at[slot]).start()
            return c
        lax.fori_loop(0, ROUTE_BLOCK, body, 0, unroll=8)

    @pl.when(i == 0)
    def _():
        issue(idx0_ref, 0)

    @pl.when(i + 1 < n_used)
    def _():
        issue(idxn_ref, (i + 1) % 2)

    slot = i % 2

    @pl.when(i < n_used)
    def _():
        pltpu.make_async_copy(h_hbm.at[pl.ds(0, ROUTE_BLOCK), :], buf.at[slot], sem.at[slot]).wait()
        xb = buf[slot].astype(BF16)
        g = jnp.dot(xb, wg_ref[0, 0].astype(BF16), preferred_element_type=F32)
        u = jnp.dot(xb, wu_ref[0, 0].astype(BF16), preferred_element_type=F32)
        hmid = g * (1.0 / (1.0 + jnp.exp(-g))) * u
        ys_ref[...] = jnp.dot(hmid.astype(BF16), wd_ref[0, 0].astype(BF16), preferred_element_type=F32)

    @pl.when(i >= n_used)
    def _():
        ys_ref[...] = jnp.zeros_like(ys_ref)


def _expert_call(block_eid, n_used, slot_tok3, h2, wg, wu, wd, layer):
    n_blocks = slot_tok3.shape[0]
    wmap = lambda i, be, nu: (layer, be[i], 0, 0)
    grid_spec = pltpu.PrefetchScalarGridSpec(
        num_scalar_prefetch=2, grid=(n_blocks,),
        in_specs=[
            pl.BlockSpec((1, 1, ROUTE_BLOCK), lambda i, be, nu: (0, 0, 0), memory_space=pltpu.SMEM),
            pl.BlockSpec((1, 1, ROUTE_BLOCK), lambda i, be, nu: (jnp.minimum(i + 1, n_blocks - 1), 0, 0),
                         memory_space=pltpu.SMEM),
            pl.BlockSpec(memory_space=pl.ANY),
            pl.BlockSpec((1, 1, D_MODEL, D_FF), wmap),
            pl.BlockSpec((1, 1, D_MODEL, D_FF), wmap),
            pl.BlockSpec((1, 1, D_FF, D_MODEL), wmap),
        ],
        out_specs=pl.BlockSpec((ROUTE_BLOCK, D_MODEL), lambda i, be, nu: (i, 0)),
        scratch_shapes=[pltpu.VMEM((2, ROUTE_BLOCK, D_MODEL), F32), pltpu.SemaphoreType.DMA((2,))])
    return pl.pallas_call(
        _expert_kernel, grid_spec=grid_spec,
        out_shape=jax.ShapeDtypeStruct((n_blocks * ROUTE_BLOCK, D_MODEL), F32),
        compiler_params=_cparams(1),
    )(block_eid, n_used, slot_tok3, slot_tok3, h2, wg, wu, wd)


def _combine_kernel(idx0_ref, idxn_ref, ys_hbm, x_ref, rg_ref, o_ref, buf, sem):
    i = pl.program_id(0)
    n = pl.num_programs(0)
    tm = x_ref.shape[0]

    def issue(idx_ref, slot):
        def body(r, c):
            for kk in range(2):
                d = idx_ref[0, 0, 2 * r + kk]
                pltpu.make_async_copy(ys_hbm.at[pl.ds(d, 1), :], buf.at[slot, kk, pl.ds(r, 1), :],
                                      sem.at[slot]).start()
            return c
        lax.fori_loop(0, tm, body, 0, unroll=4)

    @pl.when(i == 0)
    def _():
        issue(idx0_ref, 0)

    @pl.when(i + 1 < n)
    def _():
        issue(idxn_ref, (i + 1) % 2)

    slot = i % 2
    for kk in range(2):
        pltpu.make_async_copy(ys_hbm.at[pl.ds(0, tm), :], buf.at[slot, kk], sem.at[slot]).wait()
    rg = rg_ref[...]
    o_ref[...] = x_ref[...] + rg[:, 0:1] * buf[slot, 0] + rg[:, 1:2] * buf[slot, 1]


def _combine_call(dest3, ys, x2, rg):
    t = x2.shape[0]
    tm = TM_COMB
    n = t // tm
    row = lambda i: (i, 0)
    return pl.pallas_call(
        _combine_kernel, grid=(n,),
        in_specs=[
            pl.BlockSpec((1, 1, 2 * tm), lambda i: (0, 0, 0), memory_space=pltpu.SMEM),
            pl.BlockSpec((1, 1, 2 * tm), lambda i: (jnp.minimum(i + 1, n - 1), 0, 0), memory_space=pltpu.SMEM),
            pl.BlockSpec(memory_space=pl.ANY),
            pl.BlockSpec((tm, D_MODEL), row), pl.BlockSpec((tm, LANES), row)],
        out_specs=pl.BlockSpec((tm, D_MODEL), row),
        out_shape=jax.ShapeDtypeStruct((t, D_MODEL), F32),
        scratch_shapes=[pltpu.VMEM((2, 2, tm, D_MODEL), F32), pltpu.SemaphoreType.DMA((2,))],
        compiler_params=_cparams(1),
    )(dest3, dest3, ys, x2, rg)


def _swap_halves(a):
    half = a.shape[-1] // 2
    return jnp.concatenate([a[..., half:], a[..., :half]], axis=-1)


def _layer_params(l, seq, w):
    p = {}
    row = lambda v: v.reshape(1, -1).astype(F32)
    w_in = w["w_in"][l]
    kr_cols = w_in[:, 1856:1888]
    p["win"] = jnp.concatenate(
        [w_in[:, 0:1536], w_in[:, 1888:2144], w_in[:, 1728:1856], w_in[:, 1536:1728],
         kr_cols, _swap_halves(kr_cols)], axis=1).astype(BF16)
    p["g1"] = row(w["norm1_g"][l])
    p["gq"] = row(jnp.tile(w["diff_q_norm_g"][l], 2) * (DIFF_QK ** -0.5))
    p["gk"] = row(jnp.tile(w["diff_k_norm_g"][l], 2))
    p["gckv"] = row(w["mla_kv_lat_norm_g"][l])
    gcq = w["mla_q_lat_norm_g"][l]
    p["gcqa"] = row(gcq[:LANES])
    p["gcqb"] = row(jnp.concatenate([gcq[LANES:], jnp.zeros((2 * LANES - MLA_Q_RANK,), F32)]))

    wuq = w["mla_w_uq"][l].reshape(MLA_Q_RANK, MLA_HEADS, MLA_NOPE + MLA_ROPE)
    rope_w = wuq[:, :, MLA_NOPE:]
    wuq = jnp.concatenate([wuq[:, :, :MLA_NOPE], rope_w, _swap_halves(rope_w)], axis=-1)
    wuq = wuq.reshape(MLA_Q_RANK, MLA_HEADS * HEAD_SLOT)
    wuq = jnp.concatenate([wuq, jnp.zeros((2 * LANES - MLA_Q_RANK, wuq.shape[1]), F32)], axis=0).astype(BF16)
    p["wuqa"] = wuq[:LANES]
    p["wuqb"] = wuq[LANES:]

    wukv = w["mla_w_ukv"][l].reshape(MLA_KV_RANK, MLA_HEADS, MLA_NOPE + MLA_V)
    zk = jnp.zeros((MLA_KV_RANK, MLA_HEADS, HEAD_SLOT - MLA_NOPE), F32)
    p["wkk"] = jnp.concatenate([wukv[:, :, :MLA_NOPE], zk], axis=-1).reshape(MLA_KV_RANK, -1).astype(BF16)
    vcols = wukv[:, :, MLA_NOPE:]
    zv = jnp.zeros_like(vcols)
    even = (jnp.arange(MLA_HEADS) % 2 == 0)[None, :, None]
    wkv = jnp.concatenate([jnp.where(even, vcols, zv), jnp.where(even, zv, vcols)], axis=-1)
    p["wkv"] = wkv.reshape(MLA_KV_RANK, -1).astype(BF16)
    p["gkn"] = row(jnp.concatenate([w["mla_k_nope_norm_g"][l], jnp.zeros((HEAD_SLOT - MLA_NOPE,), F32)]))

    inv = 1.0 / (ROPE_BASE ** (jnp.arange(0, MLA_ROPE, 2, dtype=F32) / MLA_ROPE))
    ang = jnp.arange(seq, dtype=F32)[:, None] * inv[None, :]
    cosf = jnp.concatenate([jnp.cos(ang), jnp.cos(ang)], axis=-1)
    sinf = jnp.concatenate([-jnp.sin(ang), jnp.sin(ang)], axis=-1)
    scale = (MLA_NOPE + MLA_ROPE) ** -0.5 * math.log2(math.e)
    gqr = w["mla_q_rope_norm_g"][l]
    q_head = jnp.concatenate([jnp.broadcast_to(w["mla_q_nope_norm_g"][l][None, :], (seq, MLA_NOPE)),
                              gqr[None, :] * cosf, _swap_halves(gqr)[None, :] * sinf], axis=-1) * scale
    p["qtab"] = jnp.tile(q_head, (1, MLA_HEADS))
    gkr = w["mla_k_rope_norm_g"][l]
    p["ktab"] = jnp.concatenate([jnp.zeros((seq, MLA_NOPE), F32), gkr[None, :] * cosf,
                                 _swap_halves(gkr)[None, :] * sinf], axis=-1)
    src = jnp.arange(LANES)
    dst = jnp.arange(MLA_HEADS * HEAD_SLOT)
    src_j = jnp.where(src >= MLA_NOPE, (src - MLA_NOPE) % MLA_ROPE, -1)
    dst_l = dst % HEAD_SLOT
    dst_j = jnp.where(dst_l >= MLA_NOPE, (dst_l - MLA_NOPE) % MLA_ROPE, -2)
    p["eplace"] = (src_j[:, None] == dst_j[None, :]).astype(BF16)

    pw = w["pool_w"][l]
    bd = jnp.zeros((POOL_WIDTH, POOL_WIDTH), F32)
    for g in range(POOL_GROUPS):
        s0 = g * POOL_GROUP_DIM
        bd = bd.at[s0:s0 + POOL_GROUP_DIM, s0:s0 + POOL_GROUP_DIM].set(pw[g])
    p["pool_w"] = bd.astype(BF16)
    p["pool_scale"] = row(w["pool_scale"][l])

    lam_init = 0.8 - 0.6 * math.exp(-0.3 * l)
    lv = w["diff_lambda"][l].astype(F32)
    p["lam"] = (jnp.exp(jnp.sum(lv[0] * lv[1])) - jnp.exp(jnp.sum(lv[2] * lv[3])) + lam_init).reshape(1)
    p["gsub"] = row(w["diff_sub_norm_g"][l] * (1.0 - lam_init))

    p["wo"] = w["w_out"][l].astype(BF16)
    p["g2"] = row(w["norm2_g"][l])
    wr = jnp.concatenate([w["router_group_w"][l], w["router_expert_w"][l],
                          jnp.zeros((D_MODEL, LANES - N_GROUPS - N_EXPERTS), F32)], axis=1)
    wr_hi = wr.astype(BF16)
    p["wrh"] = wr_hi
    p["wrl"] = (wr - wr_hi.astype(F32)).astype(BF16)
    p["br"] = row(jnp.concatenate([w["router_group_b"][l], w["router_expert_b"][l],
                                   jnp.zeros((LANES - N_GROUPS - N_EXPERTS,), F32)]))
    return p


def kernel(x, norm1_g, w_in, diff_q_norm_g, diff_k_norm_g, diff_lambda, diff_sub_norm_g, mla_q_lat_norm_g, mla_kv_lat_norm_g, mla_w_uq, mla_w_ukv, mla_q_nope_norm_g, mla_q_rope_norm_g, mla_k_nope_norm_g, mla_k_rope_norm_g, pool_w, pool_scale, w_out, norm2_g, router_group_w, router_group_b, router_expert_w, router_expert_b, expert_w_gate, expert_w_up, expert_w_down):
    w = dict(norm1_g=norm1_g, w_in=w_in, diff_q_norm_g=diff_q_norm_g, diff_k_norm_g=diff_k_norm_g,
             diff_lambda=diff_lambda, diff_sub_norm_g=diff_sub_norm_g, mla_q_lat_norm_g=mla_q_lat_norm_g,
             mla_kv_lat_norm_g=mla_kv_lat_norm_g, mla_w_uq=mla_w_uq, mla_w_ukv=mla_w_ukv,
             mla_q_nope_norm_g=mla_q_nope_norm_g, mla_q_rope_norm_g=mla_q_rope_norm_g,
             mla_k_nope_norm_g=mla_k_nope_norm_g, mla_k_rope_norm_g=mla_k_rope_norm_g,
             pool_w=pool_w, pool_scale=pool_scale, w_out=w_out, norm2_g=norm2_g,
             router_group_w=router_group_w, router_group_b=router_group_b,
             router_expert_w=router_expert_w, router_expert_b=router_expert_b)
    batch, seq, d = x.shape
    t = batch * seq
    n_assign = 2 * t
    n_blocks = n_assign // ROUTE_BLOCK + N_EXPERTS
    alibi = _alibi_tables(seq)
    token_of_assign = jnp.repeat(jnp.arange(t, dtype=jnp.int32), 2)

    x2 = x.reshape(t, d)
    for l in range(DEPTH):
        p = _layer_params(l, seq, w)
        dq1, dq2, dk, dv, qm, km, vm, pu = _proj_call(x2, p, seq)
        o_diff = _diff_call(p["lam"], dq1, dq2, dk, dv, p["gsub"], alibi, batch, seq)
        o_mla = _mla_call(qm, km, vm, batch, seq)
        o_pool = _pool_call(pu, p["pool_w"], p["pool_scale"], batch, seq)
        x2, h2, route_i, route_g, counts = _outproj_call(x2, o_diff, o_mla, o_pool, p)

        cnt = counts[0, N_GROUPS:N_GROUPS + N_EXPERTS].astype(jnp.int32)
        padded = (cnt + ROUTE_BLOCK - 1) // ROUTE_BLOCK * ROUTE_BLOCK
        padded_ends = jnp.cumsum(padded)
        padded_starts = padded_ends - padded
        dest = padded_starts[route_i[:, 0:2]] + route_i[:, 2:4]
        block_start = jnp.arange(n_blocks, dtype=jnp.int32) * ROUTE_BLOCK
        block_eid = jnp.minimum(jnp.sum(block_start[:, None] >= padded_ends[None, :], axis=1),
                                N_EXPERTS - 1).astype(jnp.int32)
        n_used = (padded_ends[-1] // ROUTE_BLOCK).astype(jnp.int32).reshape(1)
        slot_tok = jnp.zeros((n_blocks * ROUTE_BLOCK,), jnp.int32).at[dest.reshape(-1)].set(token_of_assign)

        ys = _expert_call(block_eid, n_used, slot_tok.reshape(n_blocks, 1, ROUTE_BLOCK), h2,
                          expert_w_gate, expert_w_up, expert_w_down, l)
        x2 = _combine_call(dest.reshape(t // TM_COMB, 1, 2 * TM_COMB), ys, x2, route_g)
    return x2.reshape(batch, seq, d)
```

```python
import functools
import math

import jax
import jax.numpy as jnp
from jax import lax
from jax.experimental import pallas as pl
from jax.experimental.pallas import tpu as pltpu
from jax.experimental.pallas import tpu_sc as plsc

F32 = jnp.float32
BF16 = jnp.bfloat16

D_MODEL = 1024
DEPTH = 2
DIFF_HEADS = 4
DIFF_QK = 64
DIFF_V = 128
DIFF_WIDTH = 512
MLA_HEADS = 4
MLA_NOPE = 64
MLA_ROPE = 32
MLA_V = 64
MLA_Q_RANK = 192
MLA_KV_RANK = 128
MLA_WIDTH = 256
ROPE_BASE = 10000.0
POOL_WIDTH = 256
POOL_GROUPS = 4
POOL_GROUP_DIM = 64
POOL_WINDOWS = (2, 4, 8, 16)
N_GROUPS = 4
EXPERTS_PER_GROUP = 8
N_EXPERTS = 32
D_FF = 256
ROUTE_BLOCK = 256
RMS_EPS = 1e-6

LANES = 128
HEAD_SLOT = 128
PROJ_WIDTH = 2176
POOL_PAD = 16
VMEM_LIMIT = 48 * 1024 * 1024
SC_CORES = 2
SC_SUBCORES = 16
SC_GATHER_ROWS = 64

TM_PROJ = 512
TQ = 256
TM_OUT = 256
TM_COMB = 256

NT_DIMS = (((1,), (1,)), ((), ()))


def _cparams(n_axes):
    return pltpu.CompilerParams(dimension_semantics=("arbitrary",) * n_axes,
                                vmem_limit_bytes=VMEM_LIMIT)


def _full(shape):
    return pl.BlockSpec(shape, lambda *_: (0,) * len(shape))


def _proj_kernel(x_ref, g1_ref, win_ref, gq_ref, gk_ref, gckv_ref, gcqa_ref, gcqb_ref,
                 wuqa_ref, wuqb_ref, wkk_ref, wkv_ref, gkn_ref, qtab_ref, ktab_ref, eplace_ref,
                 dq1_ref, dq2_ref, dk_ref, dv_ref, qm_ref, km_ref, vm_ref, pu_ref):
    x = x_ref[...]
    xn = x * lax.rsqrt(jnp.mean(x * x, axis=-1, keepdims=True) + RMS_EPS) * g1_ref[...]
    proj = jnp.dot(xn.astype(BF16), win_ref[...], preferred_element_type=F32)

    tm = x.shape[0]
    lane = lax.broadcasted_iota(jnp.int32, (tm, LANES), 1)
    lo = lane < DIFF_QK

    def half_norm(c, g_row):
        sq = c * c
        s_lo = jnp.sum(jnp.where(lo, sq, 0.0), axis=-1, keepdims=True)
        s_hi = jnp.sum(jnp.where(lo, 0.0, sq), axis=-1, keepdims=True)
        r = jnp.where(lo, lax.rsqrt(s_lo / DIFF_QK + RMS_EPS), lax.rsqrt(s_hi / DIFF_QK + RMS_EPS))
        return c * r * g_row

    for h in range(DIFF_HEADS):
        sl = slice(h * HEAD_SLOT, (h + 1) * HEAD_SLOT)
        qn = half_norm(proj[:, sl], gq_ref[...])
        dq1_ref[:, sl] = jnp.where(lo, qn, 0.0).astype(BF16)
        dq2_ref[:, sl] = jnp.where(lo, 0.0, qn).astype(BF16)
        ksl = slice(512 + h * HEAD_SLOT, 512 + (h + 1) * HEAD_SLOT)
        dk_ref[:, sl] = half_norm(proj[:, ksl], gk_ref[...]).astype(BF16)
    dv_ref[...] = proj[:, 1024:1536].astype(BF16)
    pu_ref[...] = proj[:, 1536:1792]

    ckv = proj[:, 1792:1920]
    ckvn = ckv * lax.rsqrt(jnp.mean(ckv * ckv, axis=-1, keepdims=True) + RMS_EPS) * gckv_ref[...]
    ckvn = ckvn.astype(BF16)
    cqa = proj[:, 1920:2048]
    last = proj[:, 2048:2176]
    lsq = last * last
    ss_q = (jnp.sum(cqa * cqa, axis=-1, keepdims=True)
            + jnp.sum(jnp.where(lo, lsq, 0.0), axis=-1, keepdims=True))
    r_q = lax.rsqrt(ss_q / MLA_Q_RANK + RMS_EPS)
    q_raw = (jnp.dot((cqa * r_q * gcqa_ref[...]).astype(BF16), wuqa_ref[...], preferred_element_type=F32)
             + jnp.dot((last * r_q * gcqb_ref[...]).astype(BF16), wuqb_ref[...], preferred_element_type=F32))

    rope_lanes = (lane >= MLA_NOPE) & (lane < MLA_NOPE + MLA_ROPE)
    ss_kr = jnp.sum(jnp.where(rope_lanes, lsq, 0.0), axis=-1, keepdims=True)
    kr_terms = last * lax.rsqrt(ss_kr / MLA_ROPE + RMS_EPS) * ktab_ref[...]
    kr_placed = jnp.dot(kr_terms.astype(BF16), eplace_ref[...], preferred_element_type=F32)

    k_raw = jnp.dot(ckvn, wkk_ref[...], preferred_element_type=F32)
    vm_ref[...] = jnp.dot(ckvn, wkv_ref[...], preferred_element_type=F32).astype(BF16)
    qtab = qtab_ref[...]
    for h in range(MLA_HEADS):
        sl = slice(h * HEAD_SLOT, (h + 1) * HEAD_SLOT)
        c = q_raw[:, sl]
        sq = c * c
        s_n = jnp.sum(jnp.where(lo, sq, 0.0), axis=-1, keepdims=True)
        s_r = jnp.sum(jnp.where(rope_lanes, sq, 0.0), axis=-1, keepdims=True)
        r = jnp.where(lo, lax.rsqrt(s_n / MLA_NOPE + RMS_EPS), lax.rsqrt(s_r / MLA_ROPE + RMS_EPS))
        qm_ref[:, sl] = (c * r * qtab[:, sl]).astype(BF16)
        kc = k_raw[:, sl]
        r_k = lax.rsqrt(jnp.sum(kc * kc, axis=-1, keepdims=True) / MLA_NOPE + RMS_EPS)
        km_ref[:, sl] = (kc * r_k * gkn_ref[...] + kr_placed[:, sl]).astype(BF16)


def _proj_call(x2, p, seq):
    t = x2.shape[0]
    tm = TM_PROJ
    n_pos = seq // tm
    row = lambda i: (i, 0)
    pos = lambda i: (i % n_pos, 0)
    bf = lambda w: jax.ShapeDtypeStruct((t, w), BF16)
    in_specs = [
        pl.BlockSpec((tm, D_MODEL), row),
        _full((1, D_MODEL)), _full((D_MODEL, PROJ_WIDTH)),
        _full((1, LANES)), _full((1, LANES)), _full((1, LANES)), _full((1, LANES)), _full((1, LANES)),
        _full((LANES, 512)), _full((LANES, 512)), _full((LANES, 512)), _full((LANES, 512)),
        _full((1, LANES)),
        pl.BlockSpec((tm, 512), pos), pl.BlockSpec((tm, LANES), pos),
        _full((LANES, 512)),
    ]
    out_specs = [pl.BlockSpec((tm, 512), row)] * 7 + [pl.BlockSpec((tm, POOL_WIDTH), row)]
    out_shape = [bf(512)] * 7 + [jax.ShapeDtypeStruct((t, POOL_WIDTH), F32)]
    return pl.pallas_call(
        _proj_kernel, grid=(t // tm,), in_specs=in_specs, out_specs=out_specs, out_shape=out_shape,
        compiler_params=_cparams(1),
    )(x2, p["g1"], p["win"], p["gq"], p["gk"], p["gckv"], p["gcqa"], p["gcqb"],
      p["wuqa"], p["wuqb"], p["wkk"], p["wkv"], p["gkn"], p["qtab"], p["ktab"], p["eplace"])


def _unflatten(n, sizes):
    n = jnp.minimum(n, math.prod(sizes) - 1)
    coords = []
    for size in reversed(sizes):
        coords.append(n % size)
        n = n // size
    return tuple(reversed(coords))


def _two_stage(n, stage, bufs):
    (s0, m0), (s1, m1) = bufs

    @pl.when(n == 0)
    def _():
        s1[...] = jnp.zeros_like(s1)
        m1[...] = jnp.zeros_like(m1)

    @pl.when(n % 2 == 0)
    def _():
        stage((s0, m0), (s1, m1))

    @pl.when(n % 2 == 1)
    def _():
        stage((s1, m1), (s0, m0))


def _two_stage_scratch(seq):
    pair = [pltpu.VMEM((2 * TQ, seq), F32), pltpu.VMEM((2 * TQ, LANES), F32)]
    return pair + pair


def _softmax_pv_tile(s_prev, m_rows, c, tq, lsum, acc, v_tiles, exp_fn):
    n_half = tq // LANES
    ps = [exp_fn(s_prev[:, (c * n_half + j) * LANES:(c * n_half + j + 1) * LANES] - m_rows)
          for j in range(n_half)]
    for ch in ps:
        lsum = ch if lsum is None else lsum + ch
    pb = jnp.concatenate(ps, axis=1).astype(BF16)
    for g, vt in enumerate(v_tiles):
        pv = jnp.dot(pb[g * tq:(g + 1) * tq], vt, preferred_element_type=F32)
        acc[g] = pv if acc[g] is None else acc[g] + pv
    return lsum


def _running_max(mx, sc):
    for j in range(sc.shape[1] // LANES):
        chunk = sc[:, j * LANES:(j + 1) * LANES]
        mx = chunk if mx is None else jnp.maximum(mx, chunk)
    return mx


def _diff_kernel(lam_ref, q1_ref, q2_ref, qx_ref, k_ref, kx_ref, bd_ref, v_ref, gsub_ref, o_ref,
                 s0_ref, m0_ref, s1_ref, m1_ref, *, sizes):
    n = pl.program_id(0)
    tq = q1_ref.shape[0]
    n_kt = k_ref.shape[0] // tq
    qi_cur = _unflatten(n, sizes)[2]
    qi_prev = _unflatten(jnp.maximum(n - 1, 0), sizes)[2]

    def stage(cur, prev):
        s_cur, m_cur = cur
        s_prev, m_prev = prev
        q1, q2 = q1_ref[...], q2_ref[...]
        qx_left = qx_ref[0, 0]
        qx_right = -qx_left
        bd = bd_ref[0]
        bd2 = jnp.concatenate([bd, bd], axis=0)
        mx = None
        m_rows = m_prev[...]
        lsum = None
        acc = [None, None]
        for c in range(n_kt):
            start = pl.multiple_of(((qi_prev + c) % n_kt) * tq, tq)
            vt = v_ref[pl.ds(start, tq), :]
            lsum = _softmax_pv_tile(s_prev, m_rows, c, tq, lsum, acc, [vt, vt], jnp.exp)

            tile = (qi_cur + c) % n_kt
            start = pl.multiple_of(tile * tq, tq)
            if c == 0:
                qx = jnp.zeros_like(qx_left)
            else:
                qx = jnp.where(qi_cur + c >= n_kt, qx_left, qx_right)
            qq = jnp.concatenate([jnp.concatenate([q1, qx], axis=1),
                                  jnp.concatenate([q2, qx], axis=1)], axis=0)
            kk = jnp.concatenate([k_ref[pl.ds(start, tq), :], kx_ref[pl.ds(start, tq), :]], axis=1)
            sc = lax.dot_general(qq, kk, NT_DIMS, preferred_element_type=F32)
            if c == 0:
                sc = sc + bd2
            s_cur[:, c * tq:(c + 1) * tq] = sc
            mx = _running_max(mx, sc)
        m_cur[...] = jnp.broadcast_to(jnp.max(mx, axis=-1, keepdims=True), m_cur.shape)
        l = jnp.sum(lsum, axis=-1, keepdims=True)
        o = acc[0] * (1.0 / l[0:tq]) - acc[1] * (lam_ref[0] / l[tq:2 * tq])
        r = lax.rsqrt(jnp.mean(o * o, axis=-1, keepdims=True) + RMS_EPS)
        o_ref[...] = (o * r * gsub_ref[...]).astype(BF16)

    _two_stage(n, stage, ((s0_ref, m0_ref), (s1_ref, m1_ref)))


def _alibi_tables(seq):
    nq = seq // TQ
    slopes = 2.0 ** (-8.0 * jnp.arange(1, DIFF_HEADS + 1, dtype=F32) / DIFF_HEADS)
    pos = jnp.arange(seq, dtype=jnp.int32)
    hi = (pos // 256).astype(F32)
    lo = (pos % 256).astype(F32)
    s4 = slopes[:, None]
    ones = jnp.ones((DIFF_HEADS, seq), F32)
    q_left = jnp.stack([-s4 * 256.0 * hi[None], -s4 * lo[None], s4 * 256.0 * ones, s4 * ones], axis=-1)
    qx = jnp.concatenate([q_left, jnp.zeros((DIFF_HEADS, seq, HEAD_SLOT - 4), F32)], axis=-1)
    qx = qx.reshape(DIFF_HEADS, nq, TQ, HEAD_SLOT).astype(BF16)
    k_cols = jnp.stack([jnp.ones((seq,), F32), jnp.ones((seq,), F32), hi, lo], axis=-1)
    kx = jnp.concatenate([k_cols, jnp.zeros((seq, HEAD_SLOT - 4), F32)], axis=-1).astype(BF16)
    loc = jnp.arange(TQ, dtype=jnp.int32)
    bd = -slopes[:, None, None] * jnp.abs(loc[:, None] - loc[None, :]).astype(F32)[None]
    return qx, kx, bd


def _diff_call(lam, dq1, dq2, dk, dv, gsub, tabs, batch, seq):
    t = dq1.shape[0]
    nq = seq // TQ
    qx, kx, bd = tabs
    sizes = (batch, DIFF_HEADS, nq)
    cur = lambda n: _unflatten(n, sizes)
    prev = lambda n: _unflatten(jnp.maximum(n - 1, 0), sizes)

    def qmap(n, *_):
        b, h, qi = cur(n)
        return (b * nq + qi, h)

    def kmap(n, *_):
        b, h, qi = cur(n)
        return (b, h)

    def vmap(n, *_):
        b, h, qi = prev(n)
        return (b, h)

    def omap(n, *_):
        b, h, qi = prev(n)
        return (b * nq + qi, h)

    grid_spec = pltpu.PrefetchScalarGridSpec(
        num_scalar_prefetch=1, grid=(math.prod(sizes) + 1,),
        in_specs=[pl.BlockSpec((TQ, HEAD_SLOT), qmap), pl.BlockSpec((TQ, HEAD_SLOT), qmap),
                  pl.BlockSpec((1, 1, TQ, HEAD_SLOT), lambda n, *_: cur(n)[1:] + (0, 0)),
                  pl.BlockSpec((seq, HEAD_SLOT), kmap),
                  pl.BlockSpec((seq, HEAD_SLOT), lambda *_: (0, 0)),
                  pl.BlockSpec((1, TQ, TQ), lambda n, *_: (cur(n)[1], 0, 0)),
                  pl.BlockSpec((seq, HEAD_SLOT), vmap),
                  pl.BlockSpec((1, HEAD_SLOT), lambda *_: (0, 0))],
        out_specs=pl.BlockSpec((TQ, HEAD_SLOT), omap),
        scratch_shapes=_two_stage_scratch(seq))
    return pl.pallas_call(
        functools.partial(_diff_kernel, sizes=sizes), grid_spec=grid_spec,
        out_shape=jax.ShapeDtypeStruct((t, DIFF_WIDTH), BF16),
        compiler_params=_cparams(1),
    )(lam, dq1, dq2, qx, dk, kx, bd, dv, gsub)


def _mla_kernel(q_ref, k_ref, v_ref, o_ref, s0_ref, m0_ref, s1_ref, m1_ref):
    n = pl.program_id(0)
    tq = q_ref.shape[0]
    n_kt = k_ref.shape[0] // tq

    def stage(cur, prev):
        s_cur, m_cur = cur
        s_prev, m_prev = prev
        mx = [None, None]
        m_rows = m_prev[...]
        lsum = None
        acc = [None, None]
        for c in range(n_kt):
            rows = slice(c * tq, (c + 1) * tq)
            v_tiles = [v_ref[rows, hh * HEAD_SLOT:(hh + 1) * HEAD_SLOT] for hh in range(2)]
            lsum = _softmax_pv_tile(s_prev, m_rows, c, tq, lsum, acc, v_tiles, jnp.exp2)
            for hh in range(2):
                sl = slice(hh * HEAD_SLOT, (hh + 1) * HEAD_SLOT)
                sc = lax.dot_general(q_ref[:, sl], k_ref[rows, sl], NT_DIMS, preferred_element_type=F32)
                s_cur[hh * tq:(hh + 1) * tq, rows] = sc
                mx[hh] = _running_max(mx[hh], sc)
        mx = jnp.concatenate(mx, axis=0)
        m_cur[...] = jnp.broadcast_to(jnp.max(mx, axis=-1, keepdims=True), m_cur.shape)
        l = jnp.sum(lsum, axis=-1, keepdims=True)
        o = acc[0] * (1.0 / l[0:tq]) + acc[1] * (1.0 / l[tq:2 * tq])
        o_ref[...] = o.astype(BF16)

    _two_stage(n, stage, ((s0_ref, m0_ref), (s1_ref, m1_ref)))


def _mla_call(qm, km, vm, batch, seq):
    t = qm.shape[0]
    nq = seq // TQ
    sizes = (batch, MLA_HEADS // 2, nq)
    cur = lambda n: _unflatten(n, sizes)
    prev = lambda n: _unflatten(jnp.maximum(n - 1, 0), sizes)

    def qmap(n):
        b, p, qi = cur(n)
        return (b * nq + qi, p)

    def kmap(n):
        b, p, qi = cur(n)
        return (b, p)

    def vmap(n):
        b, p, qi = prev(n)
        return (b, p)

    def omap(n):
        b, p, qi = prev(n)
        return (b * nq + qi, p)

    return pl.pallas_call(
        _mla_kernel, grid=(math.prod(sizes) + 1,),
        in_specs=[pl.BlockSpec((TQ, 2 * HEAD_SLOT), qmap), pl.BlockSpec((seq, 2 * HEAD_SLOT), kmap),
                  pl.BlockSpec((seq, 2 * HEAD_SLOT), vmap)],
        out_specs=pl.BlockSpec((TQ, HEAD_SLOT), omap),
        out_shape=jax.ShapeDtypeStruct((t, MLA_WIDTH), BF16),
        scratch_shapes=_two_stage_scratch(seq),
        compiler_params=_cparams(1),
    )(qm, km, vm)


def _pool_kernel(u_ref, w_ref, scale_ref, o_ref):
    u = u_ref[...]
    seq, width = u.shape
    zpad = jnp.zeros((POOL_PAD, width), F32)
    ue = jnp.concatenate([zpad, u, zpad], axis=0)
    n = seq + 2 * POOL_PAD

    def down(a, k):
        return pltpu.roll(a, k, axis=0)

    def up(a, k):
        return pltpu.roll(a, n - k, axis=0)

    a2 = ue + down(ue, 1)
    a4 = down(a2, 1) + up(a2, 1)
    a8 = down(a4, 2) + up(a4, 2)
    a16 = down(a8, 4) + up(a8, 4)
    core = slice(POOL_PAD, POOL_PAD + seq)
    lane = lax.broadcasted_iota(jnp.int32, (seq, width), 1)
    tpos = lax.broadcasted_iota(jnp.int32, (seq, width), 0)
    grp = lane // POOL_GROUP_DIM
    win_sum = jnp.where(grp == 0, a2[core], jnp.where(grp == 1, a4[core], jnp.where(grp == 2, a8[core], a16[core])))
    half = jnp.where(grp == 0, 1, jnp.where(grp == 1, 2, jnp.where(grp == 2, 4, 8)))
    lo_i = jnp.maximum(tpos - half, 0)
    hi_i = jnp.minimum(tpos + half - 1, seq - 1)
    cnt = (hi_i - lo_i + 1).astype(F32)
    pooled = win_sum / cnt - u
    mixed = jnp.dot(pooled.astype(BF16), w_ref[...], preferred_element_type=F32)
    o_ref[...] = (mixed * scale_ref[...]).astype(BF16)


def _pool_call(pu, w_bd, scale, batch, seq):
    t = pu.shape[0]
    return pl.pallas_call(
        _pool_kernel, grid=(batch,),
        in_specs=[pl.BlockSpec((seq, POOL_WIDTH), lambda b: (b, 0)),
                  _full((POOL_WIDTH, POOL_WIDTH)), _full((1, POOL_WIDTH))],
        out_specs=pl.BlockSpec((seq, POOL_WIDTH), lambda b: (b, 0)),
        out_shape=jax.ShapeDtypeStruct((t, POOL_WIDTH), BF16),
        compiler_params=_cparams(1),
    )(pu, w_bd, scale)


def _outproj_kernel(x_ref, od_ref, om_ref, op_ref, wo_ref, g2_ref, wrh_ref, wrl_ref, br_ref,
                    xo_ref, h2_ref, ri_ref, rg_ref, cnt_ref, carry_ref):
    i = pl.program_id(0)

    @pl.when(i == 0)
    def _():
        carry_ref[...] = jnp.zeros_like(carry_ref)

    xn = (x_ref[...]
          + jnp.dot(od_ref[...], wo_ref[0:DIFF_WIDTH, :], preferred_element_type=F32)
          + jnp.dot(om_ref[...], wo_ref[DIFF_WIDTH:DIFF_WIDTH + MLA_WIDTH, :], preferred_element_type=F32)
          + jnp.dot(op_ref[...], wo_ref[DIFF_WIDTH + MLA_WIDTH:, :], preferred_element_type=F32))
    xo_ref[...] = xn
    h2 = xn * lax.rsqrt(jnp.mean(xn * xn, axis=-1, keepdims=True) + RMS_EPS) * g2_ref[...]
    h2_ref[...] = h2

    h_hi = h2.astype(BF16)
    h_lo = (h2 - h_hi.astype(F32)).astype(BF16)
    logits = (jnp.dot(h_hi, wrh_ref[...], preferred_element_type=F32)
              + jnp.dot(h_lo, wrh_ref[...], preferred_element_type=F32)
              + jnp.dot(h_hi, wrl_ref[...], preferred_element_type=F32)
              + br_ref[...])
    tm = logits.shape[0]
    lane = lax.broadcasted_iota(jnp.int32, (tm, LANES), 1)
    lane_f = lane.astype(F32)
    neg = jnp.float32(-jnp.inf)
    big = jnp.float32(1e9)

    gmask = lane < N_GROUPS
    gl = jnp.where(gmask, logits, neg)
    gmax = jnp.max(gl, axis=-1, keepdims=True)
    gsum = jnp.sum(jnp.where(gmask, jnp.exp(gl - gmax), 0.0), axis=-1, keepdims=True)
    g_top = 1.0 / gsum
    g_idx = jnp.min(jnp.where(gl == gmax, lane_f, big), axis=-1, keepdims=True)

    e_lo = N_GROUPS + EXPERTS_PER_GROUP * g_idx
    emask = (lane_f >= e_lo) & (lane_f < e_lo + EXPERTS_PER_GROUP)
    el = jnp.where(emask, logits, neg)
    emax = jnp.max(el, axis=-1, keepdims=True)
    eexp = jnp.where(emask, jnp.exp(el - emax), 0.0)
    prob = eexp / jnp.sum(eexp, axis=-1, keepdims=True)
    pm = jnp.where(emask, prob, -1.0)
    p1 = jnp.max(pm, axis=-1, keepdims=True)
    i1 = jnp.min(jnp.where(pm == p1, lane_f, big), axis=-1, keepdims=True)
    pm2 = jnp.where(lane_f == i1, -1.0, pm)
    p2 = jnp.max(pm2, axis=-1, keepdims=True)
    i2 = jnp.min(jnp.where(pm2 == p2, lane_f, big), axis=-1, keepdims=True)
    denom = p1 + p2
    gate1 = g_top * p1 / denom
    gate2 = g_top * p2 / denom

    sel1 = lane_f == i1
    sel2 = lane_f == i2
    onehot = jnp.where(sel1 | sel2, 1.0, 0.0)
    rr = lax.broadcasted_iota(jnp.int32, (tm, tm), 0)
    cc = lax.broadcasted_iota(jnp.int32, (tm, tm), 1)
    ltri = jnp.where(cc < rr, 1.0, 0.0).astype(BF16)
    prefix = jnp.dot(ltri, onehot.astype(BF16), preferred_element_type=F32) + carry_ref[...]
    rank1 = jnp.sum(jnp.where(sel1, prefix, 0.0), axis=-1, keepdims=True)
    rank2 = jnp.sum(jnp.where(sel2, prefix, 0.0), axis=-1, keepdims=True)
    carry_ref[...] = carry_ref[...] + jnp.sum(onehot, axis=0, keepdims=True)
    cnt_ref[...] = carry_ref[...]

    info = jnp.where(lane == 0, i1 - N_GROUPS,
                     jnp.where(lane == 1, i2 - N_GROUPS,
                               jnp.where(lane == 2, rank1, jnp.where(lane == 3, rank2, 0.0))))
    ri_ref[...] = info.astype(jnp.int32)
    rg_ref[...] = jnp.where(lane == 0, gate1, jnp.where(lane == 1, gate2, 0.0))


def _outproj_call(x2, od, om, op, p):
    t = x2.shape[0]
    tm = TM_OUT
    row = lambda i: (i, 0)
    return pl.pallas_call(
        _outproj_kernel, grid=(t // tm,),
        in_specs=[pl.BlockSpec((tm, D_MODEL), row), pl.BlockSpec((tm, DIFF_WIDTH), row),
                  pl.BlockSpec((tm, MLA_WIDTH), row), pl.BlockSpec((tm, POOL_WIDTH), row),
                  _full((D_MODEL, D_MODEL)), _full((1, D_MODEL)),
                  _full((D_MODEL, LANES)), _full((D_MODEL, LANES)), _full((1, LANES))],
        out_specs=[pl.BlockSpec((tm, D_MODEL), row), pl.BlockSpec((tm, D_MODEL), row),
                   pl.BlockSpec((tm, LANES), row), pl.BlockSpec((tm, LANES), row),
                   _full((1, LANES))],
        out_shape=[jax.ShapeDtypeStruct((t, D_MODEL), F32), jax.ShapeDtypeStruct((t, D_MODEL), F32),
                   jax.ShapeDtypeStruct((t, LANES), jnp.int32), jax.ShapeDtypeStruct((t, LANES), F32),
                   jax.ShapeDtypeStruct((1, LANES), F32)],
        scratch_shapes=[pltpu.VMEM((1, LANES), F32)],
        compiler_params=_cparams(1),
    )(x2, od, om, op, p["wo"], p["g2"], p["wrh"], p["wrl"], p["br"])


def _sc_gather_rows(table, idx):
    n = idx.shape[0]
    d = table.shape[1]
    n_workers = SC_CORES * SC_SUBCORES
    per_worker = n // n_workers
    n_chunks = per_worker // SC_GATHER_ROWS
    assert per_worker * n_workers == n and n_chunks * SC_GATHER_ROWS == per_worker
    mesh = plsc.VectorSubcoreMesh(core_axis_name="c", subcore_axis_name="s",
                                  num_cores=SC_CORES, num_subcores=SC_SUBCORES)

    def body(table_hbm, idx_hbm, out_hbm, idx_v, rows_v, sem):
        wid = lax.axis_index("s") * SC_CORES + lax.axis_index("c")
        base = wid * per_worker

        @pl.loop(0, n_chunks)
        def _(ci):
            off = base + ci * SC_GATHER_ROWS
            pltpu.sync_copy(idx_hbm.at[pl.ds(off, SC_GATHER_ROWS)], idx_v)
            pltpu.async_copy(table_hbm.at[idx_v], rows_v, sem).wait()
            pltpu.sync_copy(rows_v, out_hbm.at[pl.ds(off, SC_GATHER_ROWS)])

    return pl.kernel(
        body, out_type=jax.ShapeDtypeStruct((n, d), table.dtype), mesh=mesh,
        scratch_types=[pltpu.VMEM((SC_GATHER_ROWS,), jnp.int32),
                       pltpu.VMEM((SC_GATHER_ROWS, d), table.dtype),
                       pltpu.SemaphoreType.DMA],
    )(table, idx)


def _expert_kernel(be_ref, nu_ref, xs_ref, wg_ref, wu_ref, wd_ref, ys_ref):
    i = pl.program_id(0)
    n_used = nu_ref[0]

    @pl.when(i < n_used)
    def _():
        xb = xs_ref[...].astype(BF16)
        g = jnp.dot(xb, wg_ref[0, 0].astype(BF16), preferred_element_type=F32)
        u = jnp.dot(xb, wu_ref[0, 0].astype(BF16), preferred_element_type=F32)
        hmid = g * (1.0 / (1.0 + jnp.exp(-g))) * u
        ys_ref[...] = jnp.dot(hmid.astype(BF16), wd_ref[0, 0].astype(BF16), preferred_element_type=F32)

    @pl.when(i >= n_used)
    def _():
        ys_ref[...] = jnp.zeros_like(ys_ref)


def _expert_call(block_eid, n_used, xs, wg, wu, wd, layer):
    n_blocks = xs.shape[0] // ROUTE_BLOCK
    wmap = lambda i, be, nu: (layer, be[i], 0, 0)
    row = lambda i, be, nu: (i, 0)
    grid_spec = pltpu.PrefetchScalarGridSpec(
        num_scalar_prefetch=2, grid=(n_blocks,),
        in_specs=[
            pl.BlockSpec((ROUTE_BLOCK, D_MODEL), row),
            pl.BlockSpec((1, 1, D_MODEL, D_FF), wmap),
            pl.BlockSpec((1, 1, D_MODEL, D_FF), wmap),
            pl.BlockSpec((1, 1, D_FF, D_MODEL), wmap),
        ],
        out_specs=pl.BlockSpec((ROUTE_BLOCK, D_MODEL), row))
    return pl.pallas_call(
        _expert_kernel, grid_spec=grid_spec,
        out_shape=jax.ShapeDtypeStruct(xs.shape, F32),
        compiler_params=_cparams(1),
    )(block_eid, n_used, xs, wg, wu, wd)


def _combine_kernel(idx0_ref, idxn_ref, ys_hbm, x_ref, rg_ref, o_ref, buf, sem):
    i = pl.program_id(0)
    n = pl.num_programs(0)
    tm = x_ref.shape[0]

    def issue(idx_ref, slot):
        def body(r, c):
            for kk in range(2):
                d = idx_ref[0, 0, 2 * r + kk]
                pltpu.make_async_copy(ys_hbm.at[pl.ds(d, 1), :], buf.at[slot, kk, pl.ds(r, 1), :],
                                      sem.at[slot]).start()
            return c
        lax.fori_loop(0, tm, body, 0, unroll=4)

    @pl.when(i == 0)
    def _():
        issue(idx0_ref, 0)

    @pl.when(i + 1 < n)
    def _():
        issue(idxn_ref, (i + 1) % 2)

    slot = i % 2
    for kk in range(2):
        pltpu.make_async_copy(ys_hbm.at[pl.ds(0, tm), :], buf.at[slot, kk], sem.at[slot]).wait()
    rg = rg_ref[...]
    o_ref[...] = x_ref[...] + rg[:, 0:1] * buf[slot, 0] + rg[:, 1:2] * buf[slot, 1]


def _combine_call(dest3, ys, x2, rg):
    t = x2.shape[0]
    tm = TM_COMB
    n = t // tm
    row = lambda i: (i, 0)
    return pl.pallas_call(
        _combine_kernel, grid=(n,),
        in_specs=[
            pl.BlockSpec((1, 1, 2 * tm), lambda i: (0, 0, 0), memory_space=pltpu.SMEM),
            pl.BlockSpec((1, 1, 2 * tm), lambda i: (jnp.minimum(i + 1, n - 1), 0, 0), memory_space=pltpu.SMEM),
            pl.BlockSpec(memory_space=pl.ANY),
            pl.BlockSpec((tm, D_MODEL), row), pl.BlockSpec((tm, LANES), row)],
        out_specs=pl.BlockSpec((tm, D_MODEL), row),
        out_shape=jax.ShapeDtypeStruct((t, D_MODEL), F32),
        scratch_shapes=[pltpu.VMEM((2, 2, tm, D_MODEL), F32), pltpu.SemaphoreType.DMA((2,))],
        compiler_params=_cparams(1),
    )(dest3, dest3, ys, x2, rg)


def _swap_halves(a):
    half = a.shape[-1] // 2
    return jnp.concatenate([a[..., half:], a[..., :half]], axis=-1)


def _layer_params(l, seq, w):
    p = {}
    row = lambda v: v.reshape(1, -1).astype(F32)
    w_in = w["w_in"][l]
    kr_cols = w_in[:, 1856:1888]
    p["win"] = jnp.concatenate(
        [w_in[:, 0:1536], w_in[:, 1888:2144], w_in[:, 1728:1856], w_in[:, 1536:1728],
         kr_cols, _swap_halves(kr_cols)], axis=1).astype(BF16)
    p["g1"] = row(w["norm1_g"][l])
    p["gq"] = row(jnp.tile(w["diff_q_norm_g"][l], 2) * (DIFF_QK ** -0.5))
    p["gk"] = row(jnp.tile(w["diff_k_norm_g"][l], 2))
    p["gckv"] = row(w["mla_kv_lat_norm_g"][l])
    gcq = w["mla_q_lat_norm_g"][l]
    p["gcqa"] = row(gcq[:LANES])
    p["gcqb"] = row(jnp.concatenate([gcq[LANES:], jnp.zeros((2 * LANES - MLA_Q_RANK,), F32)]))

    wuq = w["mla_w_uq"][l].reshape(MLA_Q_RANK, MLA_HEADS, MLA_NOPE + MLA_ROPE)
    rope_w = wuq[:, :, MLA_NOPE:]
    wuq = jnp.concatenate([wuq[:, :, :MLA_NOPE], rope_w, _swap_halves(rope_w)], axis=-1)
    wuq = wuq.reshape(MLA_Q_RANK, MLA_HEADS * HEAD_SLOT)
    wuq = jnp.concatenate([wuq, jnp.zeros((2 * LANES - MLA_Q_RANK, wuq.shape[1]), F32)], axis=0).astype(BF16)
    p["wuqa"] = wuq[:LANES]
    p["wuqb"] = wuq[LANES:]

    wukv = w["mla_w_ukv"][l].reshape(MLA_KV_RANK, MLA_HEADS, MLA_NOPE + MLA_V)
    zk = jnp.zeros((MLA_KV_RANK, MLA_HEADS, HEAD_SLOT - MLA_NOPE), F32)
    p["wkk"] = jnp.concatenate([wukv[:, :, :MLA_NOPE], zk], axis=-1).reshape(MLA_KV_RANK, -1).astype(BF16)
    vcols = wukv[:, :, MLA_NOPE:]
    zv = jnp.zeros_like(vcols)
    even = (jnp.arange(MLA_HEADS) % 2 == 0)[None, :, None]
    wkv = jnp.concatenate([jnp.where(even, vcols, zv), jnp.where(even, zv, vcols)], axis=-1)
    p["wkv"] = wkv.reshape(MLA_KV_RANK, -1).astype(BF16)
    p["gkn"] = row(jnp.concatenate([w["mla_k_nope_norm_g"][l], jnp.zeros((HEAD_SLOT - MLA_NOPE,), F32)]))

    inv = 1.0 / (ROPE_BASE ** (jnp.arange(0, MLA_ROPE, 2, dtype=F32) / MLA_ROPE))
    ang = jnp.arange(seq, dtype=F32)[:, None] * inv[None, :]
    cosf = jnp.concatenate([jnp.cos(ang), jnp.cos(ang)], axis=-1)
    sinf = jnp.concatenate([-jnp.sin(ang), jnp.sin(ang)], axis=-1)
    scale = (MLA_NOPE + MLA_ROPE) ** -0.5 * math.log2(math.e)
    gqr = w["mla_q_rope_norm_g"][l]
    q_head = jnp.concatenate([jnp.broadcast_to(w["mla_q_nope_norm_g"][l][None, :], (seq, MLA_NOPE)),
                              gqr[None, :] * cosf, _swap_halves(gqr)[None, :] * sinf], axis=-1) * scale
    p["qtab"] = jnp.tile(q_head, (1, MLA_HEADS))
    gkr = w["mla_k_rope_norm_g"][l]
    p["ktab"] = jnp.concatenate([jnp.zeros((seq, MLA_NOPE), F32), gkr[None, :] * cosf,
                                 _swap_halves(gkr)[None, :] * sinf], axis=-1)
    src = jnp.arange(LANES)
    dst = jnp.arange(MLA_HEADS * HEAD_SLOT)
    src_j = jnp.where(src >= MLA_NOPE, (src - MLA_NOPE) % MLA_ROPE, -1)
    dst_l = dst % HEAD_SLOT
    dst_j = jnp.where(dst_l >= MLA_NOPE, (dst_l - MLA_NOPE) % MLA_ROPE, -2)
    p["eplace"] = (src_j[:, None] == dst_j[None, :]).astype(BF16)

    pw = w["pool_w"][l]
    bd = jnp.zeros((POOL_WIDTH, POOL_WIDTH), F32)
    for g in range(POOL_GROUPS):
        s0 = g * POOL_GROUP_DIM
        bd = bd.at[s0:s0 + POOL_GROUP_DIM, s0:s0 + POOL_GROUP_DIM].set(pw[g])
    p["pool_w"] = bd.astype(BF16)
    p["pool_scale"] = row(w["pool_scale"][l])

    lam_init = 0.8 - 0.6 * math.exp(-0.3 * l)
    lv = w["diff_lambda"][l].astype(F32)
    p["lam"] = (jnp.exp(jnp.sum(lv[0] * lv[1])) - jnp.exp(jnp.sum(lv[2] * lv[3])) + lam_init).reshape(1)
    p["gsub"] = row(w["diff_sub_norm_g"][l] * (1.0 - lam_init))

    p["wo"] = w["w_out"][l].astype(BF16)
    p["g2"] = row(w["norm2_g"][l])
    wr = jnp.concatenate([w["router_group_w"][l], w["router_expert_w"][l],
                          jnp.zeros((D_MODEL, LANES - N_GROUPS - N_EXPERTS), F32)], axis=1)
    wr_hi = wr.astype(BF16)
    p["wrh"] = wr_hi
    p["wrl"] = (wr - wr_hi.astype(F32)).astype(BF16)
    p["br"] = row(jnp.concatenate([w["router_group_b"][l], w["router_expert_b"][l],
                                   jnp.zeros((LANES - N_GROUPS - N_EXPERTS,), F32)]))
    return p


def kernel(x, norm1_g, w_in, diff_q_norm_g, diff_k_norm_g, diff_lambda, diff_sub_norm_g, mla_q_lat_norm_g, mla_kv_lat_norm_g, mla_w_uq, mla_w_ukv, mla_q_nope_norm_g, mla_q_rope_norm_g, mla_k_nope_norm_g, mla_k_rope_norm_g, pool_w, pool_scale, w_out, norm2_g, router_group_w, router_group_b, router_expert_w, router_expert_b, expert_w_gate, expert_w_up, expert_w_down):
    w = dict(norm1_g=norm1_g, w_in=w_in, diff_q_norm_g=diff_q_norm_g, diff_k_norm_g=diff_k_norm_g,
             diff_lambda=diff_lambda, diff_sub_norm_g=diff_sub_norm_g, mla_q_lat_norm_g=mla_q_lat_norm_g,
             mla_kv_lat_norm_g=mla_kv_lat_norm_g, mla_w_uq=mla_w_uq, mla_w_ukv=mla_w_ukv,
             mla_q_nope_norm_g=mla_q_nope_norm_g, mla_q_rope_norm_g=mla_q_rope_norm_g,
             mla_k_nope_norm_g=mla_k_nope_norm_g, mla_k_rope_norm_g=mla_k_rope_norm_g,
             pool_w=pool_w, pool_scale=pool_scale, w_out=w_out, norm2_g=norm2_g,
             router_group_w=router_group_w, router_group_b=router_group_b,
             router_expert_w=router_expert_w, router_expert_b=router_expert_b)
    batch, seq, d = x.shape
    t = batch * seq
    n_assign = 2 * t
    n_blocks = n_assign // ROUTE_BLOCK + N_EXPERTS
    alibi = _alibi_tables(seq)
    token_of_assign = jnp.repeat(jnp.arange(t, dtype=jnp.int32), 2)

    x2 = x.reshape(t, d)
    for l in range(DEPTH):
        p = _layer_params(l, seq, w)
        dq1, dq2, dk, dv, qm, km, vm, pu = _proj_call(x2, p, seq)
        o_diff = _diff_call(p["lam"], dq1, dq2, dk, dv, p["gsub"], alibi, batch, seq)
        o_mla = _mla_call(qm, km, vm, batch, seq)
        o_pool = _pool_call(pu, p["pool_w"], p["pool_scale"], batch, seq)
        x2, h2, route_i, route_g, counts = _outproj_call(x2, o_diff, o_mla, o_pool, p)

        cnt = counts[0, N_GROUPS:N_GROUPS + N_EXPERTS].astype(jnp.int32)
        padded = (cnt + ROUTE_BLOCK - 1) // ROUTE_BLOCK * ROUTE_BLOCK
        padded_ends = jnp.cumsum(padded)
        padded_starts = padded_ends - padded
        dest = padded_starts[route_i[:, 0:2]] + route_i[:, 2:4]
        block_start = jnp.arange(n_blocks, dtype=jnp.int32) * ROUTE_BLOCK
        block_eid = jnp.minimum(jnp.sum(block_start[:, None] >= padded_ends[None, :], axis=1),
                                N_EXPERTS - 1).astype(jnp.int32)
        n_used = (padded_ends[-1] // ROUTE_BLOCK).astype(jnp.int32).reshape(1)
        slot_tok = jnp.zeros((n_blocks * ROUTE_BLOCK,), jnp.int32).at[dest.reshape(-1)].set(token_of_assign)

        xs = _sc_gather_rows(h2, slot_tok)
        ys = _expert_call(block_eid, n_used, xs, expert_w_gate, expert_w_up, expert_w_down, l)
        x2 = _combine_call(dest.reshape(t // TM_COMB, 1, 2 * TM_COMB), ys, x2, route_g)
    return x2.reshape(batch, seq, d)
```

```python
import functools
import math

import jax
import jax.numpy as jnp
from jax import lax
from jax.experimental import pallas as pl
from jax.experimental.pallas import tpu as pltpu
from jax.experimental.pallas import tpu_sc as plsc

F32 = jnp.float32
BF16 = jnp.bfloat16

D_MODEL = 1024
DEPTH = 2
DIFF_HEADS = 4
DIFF_QK = 64
DIFF_V = 128
DIFF_WIDTH = 512
MLA_HEADS = 4
MLA_NOPE = 64
MLA_ROPE = 32
MLA_V = 64
MLA_Q_RANK = 192
MLA_KV_RANK = 128
MLA_WIDTH = 256
ROPE_BASE = 10000.0
POOL_WIDTH = 256
POOL_GROUPS = 4
POOL_GROUP_DIM = 64
POOL_WINDOWS = (2, 4, 8, 16)
N_GROUPS = 4
EXPERTS_PER_GROUP = 8
N_EXPERTS = 32
D_FF = 256
ROUTE_BLOCK = 256
RMS_EPS = 1e-6

LANES = 128
HEAD_SLOT = 128
PROJ_WIDTH = 2176
POOL_PAD = 16
VMEM_LIMIT = 48 * 1024 * 1024
SC_CORES = 2
SC_SUBCORES = 16
SC_GATHER_ROWS = 64

TM_PROJ = 512
TQ = 256
TM_OUT = 256
TM_COMB = 256

NT_DIMS = (((1,), (1,)), ((), ()))


def _cparams(n_axes):
    return pltpu.CompilerParams(dimension_semantics=("arbitrary",) * n_axes,
                                vmem_limit_bytes=VMEM_LIMIT)


def _full(shape):
    return pl.BlockSpec(shape, lambda *_: (0,) * len(shape))


def _proj_kernel(x_ref, g1_ref, win_ref, gq_ref, gk_ref, gckv_ref, gcqa_ref, gcqb_ref,
                 wuqa_ref, wuqb_ref, wkk_ref, wkv_ref, gkn_ref, qtab_ref, ktab_ref, eplace_ref,
                 dq1_ref, dq2_ref, dk_ref, dv_ref, qm_ref, km_ref, vm_ref, pu_ref):
    x = x_ref[...]
    xn = x * lax.rsqrt(jnp.mean(x * x, axis=-1, keepdims=True) + RMS_EPS) * g1_ref[...]
    proj = jnp.dot(xn.astype(BF16), win_ref[...], preferred_element_type=F32)

    tm = x.shape[0]
    lane = lax.broadcasted_iota(jnp.int32, (tm, LANES), 1)
    lo = lane < DIFF_QK

    def half_norm(c, g_row):
        sq = c * c
        s_lo = jnp.sum(jnp.where(lo, sq, 0.0), axis=-1, keepdims=True)
        s_hi = jnp.sum(jnp.where(lo, 0.0, sq), axis=-1, keepdims=True)
        r = jnp.where(lo, lax.rsqrt(s_lo / DIFF_QK + RMS_EPS), lax.rsqrt(s_hi / DIFF_QK + RMS_EPS))
        return c * r * g_row

    for h in range(DIFF_HEADS):
        sl = slice(h * HEAD_SLOT, (h + 1) * HEAD_SLOT)
        qn = half_norm(proj[:, sl], gq_ref[...])
        dq1_ref[:, sl] = jnp.where(lo, qn, 0.0).astype(BF16)
        dq2_ref[:, sl] = jnp.where(lo, 0.0, qn).astype(BF16)
        ksl = slice(512 + h * HEAD_SLOT, 512 + (h + 1) * HEAD_SLOT)
        dk_ref[:, sl] = half_norm(proj[:, ksl], gk_ref[...]).astype(BF16)
    dv_ref[...] = proj[:, 1024:1536].astype(BF16)
    pu_ref[...] = proj[:, 1536:1792]

    ckv = proj[:, 1792:1920]
    ckvn = ckv * lax.rsqrt(jnp.mean(ckv * ckv, axis=-1, keepdims=True) + RMS_EPS) * gckv_ref[...]
    ckvn = ckvn.astype(BF16)
    cqa = proj[:, 1920:2048]
    last = proj[:, 2048:2176]
    lsq = last * last
    ss_q = (jnp.sum(cqa * cqa, axis=-1, keepdims=True)
            + jnp.sum(jnp.where(lo, lsq, 0.0), axis=-1, keepdims=True))
    r_q = lax.rsqrt(ss_q / MLA_Q_RANK + RMS_EPS)
    q_raw = (jnp.dot((cqa * r_q * gcqa_ref[...]).astype(BF16), wuqa_ref[...], preferred_element_type=F32)
             + jnp.dot((last * r_q * gcqb_ref[...]).astype(BF16), wuqb_ref[...], preferred_element_type=F32))

    rope_lanes = (lane >= MLA_NOPE) & (lane < MLA_NOPE + MLA_ROPE)
    ss_kr = jnp.sum(jnp.where(rope_lanes, lsq, 0.0), axis=-1, keepdims=True)
    kr_terms = last * lax.rsqrt(ss_kr / MLA_ROPE + RMS_EPS) * ktab_ref[...]
    kr_placed = jnp.dot(kr_terms.astype(BF16), eplace_ref[...], preferred_element_type=F32)

    k_raw = jnp.dot(ckvn, wkk_ref[...], preferred_element_type=F32)
    vm_ref[...] = jnp.dot(ckvn, wkv_ref[...], preferred_element_type=F32).astype(BF16)
    qtab = qtab_ref[...]
    for h in range(MLA_HEADS):
        sl = slice(h * HEAD_SLOT, (h + 1) * HEAD_SLOT)
        c = q_raw[:, sl]
        sq = c * c
        s_n = jnp.sum(jnp.where(lo, sq, 0.0), axis=-1, keepdims=True)
        s_r = jnp.sum(jnp.where(rope_lanes, sq, 0.0), axis=-1, keepdims=True)
        r = jnp.where(lo, lax.rsqrt(s_n / MLA_NOPE + RMS_EPS), lax.rsqrt(s_r / MLA_ROPE + RMS_EPS))
        qm_ref[:, sl] = (c * r * qtab[:, sl]).astype(BF16)
        kc = k_raw[:, sl]
        r_k = lax.rsqrt(jnp.sum(kc * kc, axis=-1, keepdims=True) / MLA_NOPE + RMS_EPS)
        km_ref[:, sl] = (kc * r_k * gkn_ref[...] + kr_placed[:, sl]).astype(BF16)


def _proj_call(x2, p, seq):
    t = x2.shape[0]
    tm = TM_PROJ
    n_pos = seq // tm
    row = lambda i: (i, 0)
    pos = lambda i: (i % n_pos, 0)
    bf = lambda w: jax.ShapeDtypeStruct((t, w), BF16)
    in_specs = [
        pl.BlockSpec((tm, D_MODEL), row),
        _full((1, D_MODEL)), _full((D_MODEL, PROJ_WIDTH)),
        _full((1, LANES)), _full((1, LANES)), _full((1, LANES)), _full((1, LANES)), _full((1, LANES)),
        _full((LANES, 512)), _full((LANES, 512)), _full((LANES, 512)), _full((LANES, 512)),
        _full((1, LANES)),
        pl.BlockSpec((tm, 512), pos), pl.BlockSpec((tm, LANES), pos),
        _full((LANES, 512)),
    ]
    out_specs = [pl.BlockSpec((tm, 512), row)] * 7 + [pl.BlockSpec((tm, POOL_WIDTH), row)]
    out_shape = [bf(512)] * 7 + [jax.ShapeDtypeStruct((t, POOL_WIDTH), F32)]
    return pl.pallas_call(
        _proj_kernel, grid=(t // tm,), in_specs=in_specs, out_specs=out_specs, out_shape=out_shape,
        compiler_params=_cparams(1),
    )(x2, p["g1"], p["win"], p["gq"], p["gk"], p["gckv"], p["gcqa"], p["gcqb"],
      p["wuqa"], p["wuqb"], p["wkk"], p["wkv"], p["gkn"], p["qtab"], p["ktab"], p["eplace"])


def _unflatten(n, sizes):
    n = jnp.minimum(n, math.prod(sizes) - 1)
    coords = []
    for size in reversed(sizes):
        coords.append(n % size)
        n = n // size
    return tuple(reversed(coords))


def _two_stage(n, stage, bufs):
    (s0, m0), (s1, m1) = bufs

    @pl.when(n == 0)
    def _():
        s1[...] = jnp.zeros_like(s1)
        m1[...] = jnp.zeros_like(m1)

    @pl.when(n % 2 == 0)
    def _():
        stage((s0, m0), (s1, m1))

    @pl.when(n % 2 == 1)
    def _():
        stage((s1, m1), (s0, m0))


def _two_stage_scratch(seq):
    pair = [pltpu.VMEM((2 * TQ, seq), F32), pltpu.VMEM((2 * TQ, LANES), F32)]
    return pair + pair


def _softmax_pv_tile(s_prev, m_rows, c, tq, lsum, acc, v_tiles, exp_fn):
    n_half = tq // LANES
    ps = [exp_fn(s_prev[:, (c * n_half + j) * LANES:(c * n_half + j + 1) * LANES] - m_rows)
          for j in range(n_half)]
    for ch in ps:
        lsum = ch if lsum is None else lsum + ch
    pb = jnp.concatenate(ps, axis=1).astype(BF16)
    for g, vt in enumerate(v_tiles):
        pv = jnp.dot(pb[g * tq:(g + 1) * tq], vt, preferred_element_type=F32)
        acc[g] = pv if acc[g] is None else acc[g] + pv
    return lsum


def _running_max(mx, sc):
    for j in range(sc.shape[1] // LANES):
        chunk = sc[:, j * LANES:(j + 1) * LANES]
        mx = chunk if mx is None else jnp.maximum(mx, chunk)
    return mx


def _diff_kernel(lam_ref, q1_ref, q2_ref, qx_ref, k_ref, kx_ref, bd_ref, v_ref, gsub_ref, o_ref,
                 s0_ref, m0_ref, s1_ref, m1_ref, *, sizes):
    n = pl.program_id(0)
    tq = q1_ref.shape[0]
    n_kt = k_ref.shape[0] // tq
    qi_cur = _unflatten(n, sizes)[2]
    qi_prev = _unflatten(jnp.maximum(n - 1, 0), sizes)[2]

    def stage(cur, prev):
        s_cur, m_cur = cur
        s_prev, m_prev = prev
        q1, q2 = q1_ref[...], q2_ref[...]
        qx_left = qx_ref[0, 0]
        qx_right = -qx_left
        bd = bd_ref[0]
        bd2 = jnp.concatenate([bd, bd], axis=0)
        mx = None
        m_rows = m_prev[...]
        lsum = None
        acc = [None, None]
        for c in range(n_kt):
            start = pl.multiple_of(((qi_prev + c) % n_kt) * tq, tq)
            vt = v_ref[pl.ds(start, tq), :]
            lsum = _softmax_pv_tile(s_prev, m_rows, c, tq, lsum, acc, [vt, vt], jnp.exp)

            tile = (qi_cur + c) % n_kt
            start = pl.multiple_of(tile * tq, tq)
            if c == 0:
                qx = jnp.zeros_like(qx_left)
            else:
                qx = jnp.where(qi_cur + c >= n_kt, qx_left, qx_right)
            qq = jnp.concatenate([jnp.concatenate([q1, qx], axis=1),
                                  jnp.concatenate([q2, qx], axis=1)], axis=0)
            kk = jnp.concatenate([k_ref[pl.ds(start, tq), :], kx_ref[pl.ds(start, tq), :]], axis=1)
            sc = lax.dot_general(qq, kk, NT_DIMS, preferred_element_type=F32)
            if c == 0:
                sc = sc + bd2
            s_cur[:, c * tq:(c + 1) * tq] = sc
            mx = _running_max(mx, sc)
        m_cur[...] = jnp.broadcast_to(jnp.max(mx, axis=-1, keepdims=True), m_cur.shape)
        l = jnp.sum(lsum, axis=-1, keepdims=True)
        o = acc[0] * (1.0 / l[0:tq]) - acc[1] * (lam_ref[0] / l[tq:2 * tq])
        r = lax.rsqrt(jnp.mean(o * o, axis=-1, keepdims=True) + RMS_EPS)
        o_ref[...] = (o * r * gsub_ref[...]).astype(BF16)

    _two_stage(n, stage, ((s0_ref, m0_ref), (s1_ref, m1_ref)))


def _alibi_tables(seq):
    nq = seq // TQ
    slopes = 2.0 ** (-8.0 * jnp.arange(1, DIFF_HEADS + 1, dtype=F32) / DIFF_HEADS)
    pos = jnp.arange(seq, dtype=jnp.int32)
    hi = (pos // 256).astype(F32)
    lo = (pos % 256).astype(F32)
    s4 = slopes[:, None]
    ones = jnp.ones((DIFF_HEADS, seq), F32)
    q_left = jnp.stack([-s4 * 256.0 * hi[None], -s4 * lo[None], s4 * 256.0 * ones, s4 * ones], axis=-1)
    qx = jnp.concatenate([q_left, jnp.zeros((DIFF_HEADS, seq, HEAD_SLOT - 4), F32)], axis=-1)
    qx = qx.reshape(DIFF_HEADS, nq, TQ, HEAD_SLOT).astype(BF16)
    k_cols = jnp.stack([jnp.ones((seq,), F32), jnp.ones((seq,), F32), hi, lo], axis=-1)
    kx = jnp.concatenate([k_cols, jnp.zeros((seq, HEAD_SLOT - 4), F32)], axis=-1).astype(BF16)
    loc = jnp.arange(TQ, dtype=jnp.int32)
    bd = -slopes[:, None, None] * jnp.abs(loc[:, None] - loc[None, :]).astype(F32)[None]
    return qx, kx, bd


def _diff_call(lam, dq1, dq2, dk, dv, gsub, tabs, batch, seq):
    t = dq1.shape[0]
    nq = seq // TQ
    qx, kx, bd = tabs
    sizes = (batch, DIFF_HEADS, nq)
    cur = lambda n: _unflatten(n, sizes)
    prev = lambda n: _unflatten(jnp.maximum(n - 1, 0), sizes)

    def qmap(n, *_):
        b, h, qi = cur(n)
        return (b * nq + qi, h)

    def kmap(n, *_):
        b, h, qi = cur(n)
        return (b, h)

    def vmap(n, *_):
        b, h, qi = prev(n)
        return (b, h)

    def omap(n, *_):
        b, h, qi = prev(n)
        return (b * nq + qi, h)

    grid_spec = pltpu.PrefetchScalarGridSpec(
        num_scalar_prefetch=1, grid=(math.prod(sizes) + 1,),
        in_specs=[pl.BlockSpec((TQ, HEAD_SLOT), qmap), pl.BlockSpec((TQ, HEAD_SLOT), qmap),
                  pl.BlockSpec((1, 1, TQ, HEAD_SLOT), lambda n, *_: cur(n)[1:] + (0, 0)),
                  pl.BlockSpec((seq, HEAD_SLOT), kmap),
                  pl.BlockSpec((seq, HEAD_SLOT), lambda *_: (0, 0)),
                  pl.BlockSpec((1, TQ, TQ), lambda n, *_: (cur(n)[1], 0, 0)),
                  pl.BlockSpec((seq, HEAD_SLOT), vmap),
                  pl.BlockSpec((1, HEAD_SLOT), lambda *_: (0, 0))],
        out_specs=pl.BlockSpec((TQ, HEAD_SLOT), omap),
        scratch_shapes=_two_stage_scratch(seq))
    return pl.pallas_call(
        functools.partial(_diff_kernel, sizes=sizes), grid_spec=grid_spec,
        out_shape=jax.ShapeDtypeStruct((t, DIFF_WIDTH), BF16),
        compiler_params=_cparams(1),
    )(lam, dq1, dq2, qx, dk, kx, bd, dv, gsub)


def _mla_kernel(q_ref, k_ref, v_ref, o_ref, s0_ref, m0_ref, s1_ref, m1_ref):
    n = pl.program_id(0)
    tq = q_ref.shape[0]
    n_kt = k_ref.shape[0] // tq

    def stage(cur, prev):
        s_cur, m_cur = cur
        s_prev, m_prev = prev
        mx = [None, None]
        m_rows = m_prev[...]
        lsum = None
        acc = [None, None]
        for c in range(n_kt):
            rows = slice(c * tq, (c + 1) * tq)
            v_tiles = [v_ref[rows, hh * HEAD_SLOT:(hh + 1) * HEAD_SLOT] for hh in range(2)]
            lsum = _softmax_pv_tile(s_prev, m_rows, c, tq, lsum, acc, v_tiles, jnp.exp2)
            for hh in range(2):
                sl = slice(hh * HEAD_SLOT, (hh + 1) * HEAD_SLOT)
                sc = lax.dot_general(q_ref[:, sl], k_ref[rows, sl], NT_DIMS, preferred_element_type=F32)
                s_cur[hh * tq:(hh + 1) * tq, rows] = sc
                mx[hh] = _running_max(mx[hh], sc)
        mx = jnp.concatenate(mx, axis=0)
        m_cur[...] = jnp.broadcast_to(jnp.max(mx, axis=-1, keepdims=True), m_cur.shape)
        l = jnp.sum(lsum, axis=-1, keepdims=True)
        o = acc[0] * (1.0 / l[0:tq]) + acc[1] * (1.0 / l[tq:2 * tq])
        o_ref[...] = o.astype(BF16)

    _two_stage(n, stage, ((s0_ref, m0_ref), (s1_ref, m1_ref)))


def _mla_call(qm, km, vm, batch, seq):
    t = qm.shape[0]
    nq = seq // TQ
    sizes = (batch, MLA_HEADS // 2, nq)
    cur = lambda n: _unflatten(n, sizes)
    prev = lambda n: _unflatten(jnp.maximum(n - 1, 0), sizes)

    def qmap(n):
        b, p, qi = cur(n)
        return (b * nq + qi, p)

    def kmap(n):
        b, p, qi = cur(n)
        return (b, p)

    def vmap(n):
        b, p, qi = prev(n)
        return (b, p)

    def omap(n):
        b, p, qi = prev(n)
        return (b * nq + qi, p)

    return pl.pallas_call(
        _mla_kernel, grid=(math.prod(sizes) + 1,),
        in_specs=[pl.BlockSpec((TQ, 2 * HEAD_SLOT), qmap), pl.BlockSpec((seq, 2 * HEAD_SLOT), kmap),
                  pl.BlockSpec((seq, 2 * HEAD_SLOT), vmap)],
        out_specs=pl.BlockSpec((TQ, HEAD_SLOT), omap),
        out_shape=jax.ShapeDtypeStruct((t, MLA_WIDTH), BF16),
        scratch_shapes=_two_stage_scratch(seq),
        compiler_params=_cparams(1),
    )(qm, km, vm)


def _pool_kernel(u_ref, w_ref, scale_ref, o_ref):
    u = u_ref[...]
    seq, width = u.shape
    zpad = jnp.zeros((POOL_PAD, width), F32)
    ue = jnp.concatenate([zpad, u, zpad], axis=0)
    n = seq + 2 * POOL_PAD

    def down(a, k):
        return pltpu.roll(a, k, axis=0)

    def up(a, k):
        return pltpu.roll(a, n - k, axis=0)

    a2 = ue + down(ue, 1)
    a4 = down(a2, 1) + up(a2, 1)
    a8 = down(a4, 2) + up(a4, 2)
    a16 = down(a8, 4) + up(a8, 4)
    core = slice(POOL_PAD, POOL_PAD + seq)
    lane = lax.broadcasted_iota(jnp.int32, (seq, width), 1)
    tpos = lax.broadcasted_iota(jnp.int32, (seq, width), 0)
    grp = lane // POOL_GROUP_DIM
    win_sum = jnp.where(grp == 0, a2[core], jnp.where(grp == 1, a4[core], jnp.where(grp == 2, a8[core], a16[core])))
    half = jnp.where(grp == 0, 1, jnp.where(grp == 1, 2, jnp.where(grp == 2, 4, 8)))
    lo_i = jnp.maximum(tpos - half, 0)
    hi_i = jnp.minimum(tpos + half - 1, seq - 1)
    cnt = (hi_i - lo_i + 1).astype(F32)
    pooled = win_sum / cnt - u
    mixed = jnp.dot(pooled.astype(BF16), w_ref[...], preferred_element_type=F32)
    o_ref[...] = (mixed * scale_ref[...]).astype(BF16)


def _pool_call(pu, w_bd, scale, batch, seq):
    t = pu.shape[0]
    return pl.pallas_call(
        _pool_kernel, grid=(batch,),
        in_specs=[pl.BlockSpec((seq, POOL_WIDTH), lambda b: (b, 0)),
                  _full((POOL_WIDTH, POOL_WIDTH)), _full((1, POOL_WIDTH))],
        out_specs=pl.BlockSpec((seq, POOL_WIDTH), lambda b: (b, 0)),
        out_shape=jax.ShapeDtypeStruct((t, POOL_WIDTH), BF16),
        compiler_params=_cparams(1),
    )(pu, w_bd, scale)


def _outproj_kernel(x_ref, od_ref, om_ref, op_ref, wo_ref, g2_ref, wrh_ref, wrl_ref, br_ref,
                    xo_ref, h2_ref, ri_ref, rg_ref, cnt_ref, carry_ref):
    i = pl.program_id(0)

    @pl.when(i == 0)
    def _():
        carry_ref[...] = jnp.zeros_like(carry_ref)

    xn = (x_ref[...]
          + jnp.dot(od_ref[...], wo_ref[0:DIFF_WIDTH, :], preferred_element_type=F32)
          + jnp.dot(om_ref[...], wo_ref[DIFF_WIDTH:DIFF_WIDTH + MLA_WIDTH, :], preferred_element_type=F32)
          + jnp.dot(op_ref[...], wo_ref[DIFF_WIDTH + MLA_WIDTH:, :], preferred_element_type=F32))
    xo_ref[...] = xn
    h2 = xn * lax.rsqrt(jnp.mean(xn * xn, axis=-1, keepdims=True) + RMS_EPS) * g2_ref[...]
    for j in range(D_MODEL // LANES):
        h2_ref[:, j, :] = h2[:, j * LANES:(j + 1) * LANES]

    h_hi = h2.astype(BF16)
    h_lo = (h2 - h_hi.astype(F32)).astype(BF16)
    logits = (jnp.dot(h_hi, wrh_ref[...], preferred_element_type=F32)
              + jnp.dot(h_lo, wrh_ref[...], preferred_element_type=F32)
              + jnp.dot(h_hi, wrl_ref[...], preferred_element_type=F32)
              + br_ref[...])
    tm = logits.shape[0]
    lane = lax.broadcasted_iota(jnp.int32, (tm, LANES), 1)
    lane_f = lane.astype(F32)
    neg = jnp.float32(-jnp.inf)
    big = jnp.float32(1e9)

    gmask = lane < N_GROUPS
    gl = jnp.where(gmask, logits, neg)
    gmax = jnp.max(gl, axis=-1, keepdims=True)
    gsum = jnp.sum(jnp.where(gmask, jnp.exp(gl - gmax), 0.0), axis=-1, keepdims=True)
    g_top = 1.0 / gsum
    g_idx = jnp.min(jnp.where(gl == gmax, lane_f, big), axis=-1, keepdims=True)

    e_lo = N_GROUPS + EXPERTS_PER_GROUP * g_idx
    emask = (lane_f >= e_lo) & (lane_f < e_lo + EXPERTS_PER_GROUP)
    el = jnp.where(emask, logits, neg)
    emax = jnp.max(el, axis=-1, keepdims=True)
    eexp = jnp.where(emask, jnp.exp(el - emax), 0.0)
    prob = eexp / jnp.sum(eexp, axis=-1, keepdims=True)
    pm = jnp.where(emask, prob, -1.0)
    p1 = jnp.max(pm, axis=-1, keepdims=True)
    i1 = jnp.min(jnp.where(pm == p1, lane_f, big), axis=-1, keepdims=True)
    pm2 = jnp.where(lane_f == i1, -1.0, pm)
    p2 = jnp.max(pm2, axis=-1, keepdims=True)
    i2 = jnp.min(jnp.where(pm2 == p2, lane_f, big), axis=-1, keepdims=True)
    denom = p1 + p2
    gate1 = g_top * p1 / denom
    gate2 = g_top * p2 / denom

    sel1 = lane_f == i1
    sel2 = lane_f == i2
    onehot = jnp.where(sel1 | sel2, 1.0, 0.0)
    rr = lax.broadcasted_iota(jnp.int32, (tm, tm), 0)
    cc = lax.broadcasted_iota(jnp.int32, (tm, tm), 1)
    ltri = jnp.where(cc < rr, 1.0, 0.0).astype(BF16)
    prefix = jnp.dot(ltri, onehot.astype(BF16), preferred_element_type=F32) + carry_ref[...]
    rank1 = jnp.sum(jnp.where(sel1, prefix, 0.0), axis=-1, keepdims=True)
    rank2 = jnp.sum(jnp.where(sel2, prefix, 0.0), axis=-1, keepdims=True)
    carry_ref[...] = carry_ref[...] + jnp.sum(onehot, axis=0, keepdims=True)
    cnt_ref[...] = carry_ref[...]

    info = jnp.where(lane == 0, i1 - N_GROUPS,
                     jnp.where(lane == 1, i2 - N_GROUPS,
                               jnp.where(lane == 2, rank1, jnp.where(lane == 3, rank2, 0.0))))
    ri_ref[...] = info.astype(jnp.int32)
    rg_ref[...] = jnp.where(lane == 0, gate1, jnp.where(lane == 1, gate2, 0.0))


def _outproj_call(x2, od, om, op, p):
    t = x2.shape[0]
    tm = TM_OUT
    row = lambda i: (i, 0)
    return pl.pallas_call(
        _outproj_kernel, grid=(t // tm,),
        in_specs=[pl.BlockSpec((tm, D_MODEL), row), pl.BlockSpec((tm, DIFF_WIDTH), row),
                  pl.BlockSpec((tm, MLA_WIDTH), row), pl.BlockSpec((tm, POOL_WIDTH), row),
                  _full((D_MODEL, D_MODEL)), _full((1, D_MODEL)),
                  _full((D_MODEL, LANES)), _full((D_MODEL, LANES)), _full((1, LANES))],
        out_specs=[pl.BlockSpec((tm, D_MODEL), row),
                   pl.BlockSpec((tm, D_MODEL // LANES, LANES), lambda i: (i, 0, 0)),
                   pl.BlockSpec((tm, LANES), row), pl.BlockSpec((tm, LANES), row),
                   _full((1, LANES))],
        out_shape=[jax.ShapeDtypeStruct((t, D_MODEL), F32),
                   jax.ShapeDtypeStruct((t, D_MODEL // LANES, LANES), F32),
                   jax.ShapeDtypeStruct((t, LANES), jnp.int32), jax.ShapeDtypeStruct((t, LANES), F32),
                   jax.ShapeDtypeStruct((1, LANES), F32)],
        scratch_shapes=[pltpu.VMEM((1, LANES), F32)],
        compiler_params=_cparams(1),
    )(x2, od, om, op, p["wo"], p["g2"], p["wrh"], p["wrl"], p["br"])


def _sc_gather_rows(table, idx):
    n = idx.shape[0]
    row_shape = table.shape[1:]
    n_workers = SC_CORES * SC_SUBCORES
    per_worker = n // n_workers
    n_chunks = per_worker // SC_GATHER_ROWS
    assert per_worker * n_workers == n and n_chunks * SC_GATHER_ROWS == per_worker
    mesh = plsc.VectorSubcoreMesh(core_axis_name="c", subcore_axis_name="s",
                                  num_cores=SC_CORES, num_subcores=SC_SUBCORES)

    def body(table_hbm, idx_hbm, out_hbm, idx_v, rows_v, sem):
        wid = lax.axis_index("s") * SC_CORES + lax.axis_index("c")
        base = wid * per_worker

        @pl.loop(0, n_chunks)
        def _(ci):
            off = base + ci * SC_GATHER_ROWS
            pltpu.sync_copy(idx_hbm.at[pl.ds(off, SC_GATHER_ROWS)], idx_v)
            pltpu.async_copy(table_hbm.at[idx_v], rows_v, sem).wait()
            pltpu.sync_copy(rows_v, out_hbm.at[pl.ds(off, SC_GATHER_ROWS)])

    return pl.kernel(
        body, out_type=jax.ShapeDtypeStruct((n,) + row_shape, table.dtype), mesh=mesh,
        scratch_types=[pltpu.VMEM((SC_GATHER_ROWS,), jnp.int32),
                       pltpu.VMEM((SC_GATHER_ROWS,) + row_shape, table.dtype),
                       pltpu.SemaphoreType.DMA],
    )(table, idx)


def _expert_kernel(be_ref, nu_ref, xs_ref, wg_ref, wu_ref, wd_ref, ys_ref):
    i = pl.program_id(0)
    n_used = nu_ref[0]

    @pl.when(i < n_used)
    def _():
        xb = jnp.concatenate([xs_ref[:, j, :] for j in range(D_MODEL // LANES)], axis=1).astype(BF16)
        g = jnp.dot(xb, wg_ref[0, 0].astype(BF16), preferred_element_type=F32)
        u = jnp.dot(xb, wu_ref[0, 0].astype(BF16), preferred_element_type=F32)
        hmid = g * (1.0 / (1.0 + jnp.exp(-g))) * u
        ys_ref[...] = jnp.dot(hmid.astype(BF16), wd_ref[0, 0].astype(BF16), preferred_element_type=F32)

    @pl.when(i >= n_used)
    def _():
        ys_ref[...] = jnp.zeros_like(ys_ref)


def _expert_call(block_eid, n_used, xs, wg, wu, wd, layer):
    n_blocks = xs.shape[0] // ROUTE_BLOCK
    wmap = lambda i, be, nu: (layer, be[i], 0, 0)
    row = lambda i, be, nu: (i, 0)
    grid_spec = pltpu.PrefetchScalarGridSpec(
        num_scalar_prefetch=2, grid=(n_blocks,),
        in_specs=[
            pl.BlockSpec((ROUTE_BLOCK, D_MODEL // LANES, LANES), lambda i, be, nu: (i, 0, 0)),
            pl.BlockSpec((1, 1, D_MODEL, D_FF), wmap),
            pl.BlockSpec((1, 1, D_MODEL, D_FF), wmap),
            pl.BlockSpec((1, 1, D_FF, D_MODEL), wmap),
        ],
        out_specs=pl.BlockSpec((ROUTE_BLOCK, D_MODEL), row))
    return pl.pallas_call(
        _expert_kernel, grid_spec=grid_spec,
        out_shape=jax.ShapeDtypeStruct((xs.shape[0], D_MODEL), F32),
        compiler_params=_cparams(1),
    )(block_eid, n_used, xs, wg, wu, wd)


def _combine_kernel(idx0_ref, idxn_ref, ys_hbm, x_ref, rg_ref, o_ref, buf, sem):
    i = pl.program_id(0)
    n = pl.num_programs(0)
    tm = x_ref.shape[0]

    def issue(idx_ref, slot):
        def body(r, c):
            for kk in range(2):
                d = idx_ref[0, 0, 2 * r + kk]
                pltpu.make_async_copy(ys_hbm.at[pl.ds(d, 1), :], buf.at[slot, kk, pl.ds(r, 1), :],
                                      sem.at[slot]).start()
            return c
        lax.fori_loop(0, tm, body, 0, unroll=4)

    @pl.when(i == 0)
    def _():
        issue(idx0_ref, 0)

    @pl.when(i + 1 < n)
    def _():
        issue(idxn_ref, (i + 1) % 2)

    slot = i % 2
    for kk in range(2):
        pltpu.make_async_copy(ys_hbm.at[pl.ds(0, tm), :], buf.at[slot, kk], sem.at[slot]).wait()
    rg = rg_ref[...]
    o_ref[...] = x_ref[...] + rg[:, 0:1] * buf[slot, 0] + rg[:, 1:2] * buf[slot, 1]


def _combine_call(dest3, ys, x2, rg):
    t = x2.shape[0]
    tm = TM_COMB
    n = t // tm
    row = lambda i: (i, 0)
    return pl.pallas_call(
        _combine_kernel, grid=(n,),
        in_specs=[
            pl.BlockSpec((1, 1, 2 * tm), lambda i: (0, 0, 0), memory_space=pltpu.SMEM),
            pl.BlockSpec((1, 1, 2 * tm), lambda i: (jnp.minimum(i + 1, n - 1), 0, 0), memory_space=pltpu.SMEM),
            pl.BlockSpec(memory_space=pl.ANY),
            pl.BlockSpec((tm, D_MODEL), row), pl.BlockSpec((tm, LANES), row)],
        out_specs=pl.BlockSpec((tm, D_MODEL), row),
        out_shape=jax.ShapeDtypeStruct((t, D_MODEL), F32),
        scratch_shapes=[pltpu.VMEM((2, 2, tm, D_MODEL), F32), pltpu.SemaphoreType.DMA((2,))],
        compiler_params=_cparams(1),
    )(dest3, dest3, ys, x2, rg)


def _swap_halves(a):
    half = a.shape[-1] // 2
    return jnp.concatenate([a[..., half:], a[..., :half]], axis=-1)


def _layer_params(l, seq, w):
    p = {}
    row = lambda v: v.reshape(1, -1).astype(F32)
    w_in = w["w_in"][l]
    kr_cols = w_in[:, 1856:1888]
    p["win"] = jnp.concatenate(
        [w_in[:, 0:1536], w_in[:, 1888:2144], w_in[:, 1728:1856], w_in[:, 1536:1728],
         kr_cols, _swap_halves(kr_cols)], axis=1).astype(BF16)
    p["g1"] = row(w["norm1_g"][l])
    p["gq"] = row(jnp.tile(w["diff_q_norm_g"][l], 2) * (DIFF_QK ** -0.5))
    p["gk"] = row(jnp.tile(w["diff_k_norm_g"][l], 2))
    p["gckv"] = row(w["mla_kv_lat_norm_g"][l])
    gcq = w["mla_q_lat_norm_g"][l]
    p["gcqa"] = row(gcq[:LANES])
    p["gcqb"] = row(jnp.concatenate([gcq[LANES:], jnp.zeros((2 * LANES - MLA_Q_RANK,), F32)]))

    wuq = w["mla_w_uq"][l].reshape(MLA_Q_RANK, MLA_HEADS, MLA_NOPE + MLA_ROPE)
    rope_w = wuq[:, :, MLA_NOPE:]
    wuq = jnp.concatenate([wuq[:, :, :MLA_NOPE], rope_w, _swap_halves(rope_w)], axis=-1)
    wuq = wuq.reshape(MLA_Q_RANK, MLA_HEADS * HEAD_SLOT)
    wuq = jnp.concatenate([wuq, jnp.zeros((2 * LANES - MLA_Q_RANK, wuq.shape[1]), F32)], axis=0).astype(BF16)
    p["wuqa"] = wuq[:LANES]
    p["wuqb"] = wuq[LANES:]

    wukv = w["mla_w_ukv"][l].reshape(MLA_KV_RANK, MLA_HEADS, MLA_NOPE + MLA_V)
    zk = jnp.zeros((MLA_KV_RANK, MLA_HEADS, HEAD_SLOT - MLA_NOPE), F32)
    p["wkk"] = jnp.concatenate([wukv[:, :, :MLA_NOPE], zk], axis=-1).reshape(MLA_KV_RANK, -1).astype(BF16)
    vcols = wukv[:, :, MLA_NOPE:]
    zv = jnp.zeros_like(vcols)
    even = (jnp.arange(MLA_HEADS) % 2 == 0)[None, :, None]
    wkv = jnp.concatenate([jnp.where(even, vcols, zv), jnp.where(even, zv, vcols)], axis=-1)
    p["wkv"] = wkv.reshape(MLA_KV_RANK, -1).astype(BF16)
    p["gkn"] = row(jnp.concatenate([w["mla_k_nope_norm_g"][l], jnp.zeros((HEAD_SLOT - MLA_NOPE,), F32)]))

    inv = 1.0 / (ROPE_BASE ** (jnp.arange(0, MLA_ROPE, 2, dtype=F32) / MLA_ROPE))
    ang = jnp.arange(seq, dtype=F32)[:, None] * inv[None, :]
    cosf = jnp.concatenate([jnp.cos(ang), jnp.cos(ang)], axis=-1)
    sinf = jnp.concatenate([-jnp.sin(ang), jnp.sin(ang)], axis=-1)
    scale = (MLA_NOPE + MLA_ROPE) ** -0.5 * math.log2(math.e)
    gqr = w["mla_q_rope_norm_g"][l]
    q_head = jnp.concatenate([jnp.broadcast_to(w["mla_q_nope_norm_g"][l][None, :], (seq, MLA_NOPE)),
                              gqr[None, :] * cosf, _swap_halves(gqr)[None, :] * sinf], axis=-1) * scale
    p["qtab"] = jnp.tile(q_head, (1, MLA_HEADS))
    gkr = w["mla_k_rope_norm_g"][l]
    p["ktab"] = jnp.concatenate([jnp.zeros((seq, MLA_NOPE), F32), gkr[None, :] * cosf,
                                 _swap_halves(gkr)[None, :] * sinf], axis=-1)
    src = jnp.arange(LANES)
    dst = jnp.arange(MLA_HEADS * HEAD_SLOT)
    src_j = jnp.where(src >= MLA_NOPE, (src - MLA_NOPE) % MLA_ROPE, -1)
    dst_l = dst % HEAD_SLOT
    dst_j = jnp.where(dst_l >= MLA_NOPE, (dst_l - MLA_NOPE) % MLA_ROPE, -2)
    p["eplace"] = (src_j[:, None] == dst_j[None, :]).astype(BF16)

    pw = w["pool_w"][l]
    bd = jnp.zeros((POOL_WIDTH, POOL_WIDTH), F32)
    for g in range(POOL_GROUPS):
        s0 = g * POOL_GROUP_DIM
        bd = bd.at[s0:s0 + POOL_GROUP_DIM, s0:s0 + POOL_GROUP_DIM].set(pw[g])
    p["pool_w"] = bd.astype(BF16)
    p["pool_scale"] = row(w["pool_scale"][l])

    lam_init = 0.8 - 0.6 * math.exp(-0.3 * l)
    lv = w["diff_lambda"][l].astype(F32)
    p["lam"] = (jnp.exp(jnp.sum(lv[0] * lv[1])) - jnp.exp(jnp.sum(lv[2] * lv[3])) + lam_init).reshape(1)
    p["gsub"] = row(w["diff_sub_norm_g"][l] * (1.0 - lam_init))

    p["wo"] = w["w_out"][l].astype(BF16)
    p["g2"] = row(w["norm2_g"][l])
    wr = jnp.concatenate([w["router_group_w"][l], w["router_expert_w"][l],
                          jnp.zeros((D_MODEL, LANES - N_GROUPS - N_EXPERTS), F32)], axis=1)
    wr_hi = wr.astype(BF16)
    p["wrh"] = wr_hi
    p["wrl"] = (wr - wr_hi.astype(F32)).astype(BF16)
    p["br"] = row(jnp.concatenate([w["router_group_b"][l], w["router_expert_b"][l],
                                   jnp.zeros((LANES - N_GROUPS - N_EXPERTS,), F32)]))
    return p


def kernel(x, norm1_g, w_in, diff_q_norm_g, diff_k_norm_g, diff_lambda, diff_sub_norm_g, mla_q_lat_norm_g, mla_kv_lat_norm_g, mla_w_uq, mla_w_ukv, mla_q_nope_norm_g, mla_q_rope_norm_g, mla_k_nope_norm_g, mla_k_rope_norm_g, pool_w, pool_scale, w_out, norm2_g, router_group_w, router_group_b, router_expert_w, router_expert_b, expert_w_gate, expert_w_up, expert_w_down):
    w = dict(norm1_g=norm1_g, w_in=w_in, diff_q_norm_g=diff_q_norm_g, diff_k_norm_g=diff_k_norm_g,
             diff_lambda=diff_lambda, diff_sub_norm_g=diff_sub_norm_g, mla_q_lat_norm_g=mla_q_lat_norm_g,
             mla_kv_lat_norm_g=mla_kv_lat_norm_g, mla_w_uq=mla_w_uq, mla_w_ukv=mla_w_ukv,
             mla_q_nope_norm_g=mla_q_nope_norm_g, mla_q_rope_norm_g=mla_q_rope_norm_g,
             mla_k_nope_norm_g=mla_k_nope_norm_g, mla_k_rope_norm_g=mla_k_rope_norm_g,
             pool_w=pool_w, pool_scale=pool_scale, w_out=w_out, norm2_g=norm2_g,
             router_group_w=router_group_w, router_group_b=router_group_b,
             router_expert_w=router_expert_w, router_expert_b=router_expert_b)
    batch, seq, d = x.shape
    t = batch * seq
    n_assign = 2 * t
    n_blocks = n_assign // ROUTE_BLOCK + N_EXPERTS
    alibi = _alibi_tables(seq)
    token_of_assign = jnp.repeat(jnp.arange(t, dtype=jnp.int32), 2)

    x2 = x.reshape(t, d)
    for l in range(DEPTH):
        p = _layer_params(l, seq, w)
        dq1, dq2, dk, dv, qm, km, vm, pu = _proj_call(x2, p, seq)
        o_diff = _diff_call(p["lam"], dq1, dq2, dk, dv, p["gsub"], alibi, batch, seq)
        o_mla = _mla_call(qm, km, vm, batch, seq)
        o_pool = _pool_call(pu, p["pool_w"], p["pool_scale"], batch, seq)
        x2, h2, route_i, route_g, counts = _outproj_call(x2, o_diff, o_mla, o_pool, p)

        cnt = counts[0, N_GROUPS:N_GROUPS + N_EXPERTS].astype(jnp.int32)
        padded = (cnt + ROUTE_BLOCK - 1) // ROUTE_BLOCK * ROUTE_BLOCK
        padded_ends = jnp.cumsum(padded)
        padded_starts = padded_ends - padded
        dest = padded_starts[route_i[:, 0:2]] + route_i[:, 2:4]
        block_start = jnp.arange(n_blocks, dtype=jnp.int32) * ROUTE_BLOCK
        block_eid = jnp.minimum(jnp.sum(block_start[:, None] >= padded_ends[None, :], axis=1),
                                N_EXPERTS - 1).astype(jnp.int32)
        n_used = (padded_ends[-1] // ROUTE_BLOCK).astype(jnp.int32).reshape(1)
        slot_tok = jnp.zeros((n_blocks * ROUTE_BLOCK,), jnp.int32).at[dest.reshape(-1)].set(token_of_assign)

        xs = _sc_gather_rows(h2, slot_tok)
        ys = _expert_call(block_eid, n_used, xs, expert_w_gate, expert_w_up, expert_w_down, l)
        x2 = _combine_call(dest.reshape(t // TM_COMB, 1, 2 * TM_COMB), ys, x2, route_g)
    return x2.reshape(batch, seq, d)
```

```python
import functools
import math

import jax
import jax.numpy as jnp
from jax import lax
from jax.experimental import pallas as pl
from jax.experimental.pallas import tpu as pltpu

F32 = jnp.float32
BF16 = jnp.bfloat16

D_MODEL = 1024
DEPTH = 2
DIFF_HEADS = 4
DIFF_QK = 64
DIFF_V = 128
DIFF_WIDTH = 512
MLA_HEADS = 4
MLA_NOPE = 64
MLA_ROPE = 32
MLA_V = 64
MLA_Q_RANK = 192
MLA_KV_RANK = 128
MLA_WIDTH = 256
ROPE_BASE = 10000.0
POOL_WIDTH = 256
POOL_GROUPS = 4
POOL_GROUP_DIM = 64
POOL_WINDOWS = (2, 4, 8, 16)
N_GROUPS = 4
EXPERTS_PER_GROUP = 8
N_EXPERTS = 32
D_FF = 256
ROUTE_BLOCK = 256
RMS_EPS = 1e-6

LANES = 128
HEAD_SLOT = 128
PROJ_WIDTH = 2176
POOL_PAD = 16
VMEM_LIMIT = 48 * 1024 * 1024

TM_PROJ = 512
TQ = 256
TM_OUT = 256
TM_COMB = 256

NT_DIMS = (((1,), (1,)), ((), ()))


def _cparams(n_axes):
    return pltpu.CompilerParams(dimension_semantics=("arbitrary",) * n_axes,
                                vmem_limit_bytes=VMEM_LIMIT)


def _full(shape):
    return pl.BlockSpec(shape, lambda *_: (0,) * len(shape))


def _proj_kernel(x_ref, g1_ref, win_ref, gq_ref, gk_ref, gckv_ref, gcqa_ref, gcqb_ref,
                 wuqa_ref, wuqb_ref, wkk_ref, wkv_ref, gkn_ref, qtab_ref, ktab_ref, eplace_ref,
                 dq1_ref, dq2_ref, dk_ref, dv_ref, qm_ref, km_ref, vm_ref, pu_ref):
    x = x_ref[...]
    xn = x * lax.rsqrt(jnp.mean(x * x, axis=-1, keepdims=True) + RMS_EPS) * g1_ref[...]
    proj = jnp.dot(xn.astype(BF16), win_ref[...], preferred_element_type=F32)

    tm = x.shape[0]
    lane = lax.broadcasted_iota(jnp.int32, (tm, LANES), 1)
    lo = lane < DIFF_QK

    def half_norm(c, g_row):
        sq = c * c
        s_lo = jnp.sum(jnp.where(lo, sq, 0.0), axis=-1, keepdims=True)
        s_hi = jnp.sum(jnp.where(lo, 0.0, sq), axis=-1, keepdims=True)
        r = jnp.where(lo, lax.rsqrt(s_lo / DIFF_QK + RMS_EPS), lax.rsqrt(s_hi / DIFF_QK + RMS_EPS))
        return c * r * g_row

    for h in range(DIFF_HEADS):
        sl = slice(h * HEAD_SLOT, (h + 1) * HEAD_SLOT)
        qn = half_norm(proj[:, sl], gq_ref[...])
        dq1_ref[:, sl] = jnp.where(lo, qn, 0.0).astype(BF16)
        dq2_ref[:, sl] = jnp.where(lo, 0.0, qn).astype(BF16)
        ksl = slice(512 + h * HEAD_SLOT, 512 + (h + 1) * HEAD_SLOT)
        dk_ref[:, sl] = half_norm(proj[:, ksl], gk_ref[...]).astype(BF16)
    dv_ref[...] = proj[:, 1024:1536].astype(BF16)
    pu_ref[...] = proj[:, 1536:1792]

    ckv = proj[:, 1792:1920]
    ckvn = ckv * lax.rsqrt(jnp.mean(ckv * ckv, axis=-1, keepdims=True) + RMS_EPS) * gckv_ref[...]
    ckvn = ckvn.astype(BF16)
    cqa = proj[:, 1920:2048]
    last = proj[:, 2048:2176]
    lsq = last * last
    ss_q = (jnp.sum(cqa * cqa, axis=-1, keepdims=True)
            + jnp.sum(jnp.where(lo, lsq, 0.0), axis=-1, keepdims=True))
    r_q = lax.rsqrt(ss_q / MLA_Q_RANK + RMS_EPS)
    q_raw = (jnp.dot((cqa * r_q * gcqa_ref[...]).astype(BF16), wuqa_ref[...], preferred_element_type=F32)
             + jnp.dot((last * r_q * gcqb_ref[...]).astype(BF16), wuqb_ref[...], preferred_element_type=F32))

    rope_lanes = (lane >= MLA_NOPE) & (lane < MLA_NOPE + MLA_ROPE)
    ss_kr = jnp.sum(jnp.where(rope_lanes, lsq, 0.0), axis=-1, keepdims=True)
    kr_terms = last * lax.rsqrt(ss_kr / MLA_ROPE + RMS_EPS) * ktab_ref[...]
    kr_placed = jnp.dot(kr_terms.astype(BF16), eplace_ref[...], preferred_element_type=F32)

    k_raw = jnp.dot(ckvn, wkk_ref[...], preferred_element_type=F32)
    vm_ref[...] = jnp.dot(ckvn, wkv_ref[...], preferred_element_type=F32).astype(BF16)
    qtab = qtab_ref[...]
    for h in range(MLA_HEADS):
        sl = slice(h * HEAD_SLOT, (h + 1) * HEAD_SLOT)
        c = q_raw[:, sl]
        sq = c * c
        s_n = jnp.sum(jnp.where(lo, sq, 0.0), axis=-1, keepdims=True)
        s_r = jnp.sum(jnp.where(rope_lanes, sq, 0.0), axis=-1, keepdims=True)
        r = jnp.where(lo, lax.rsqrt(s_n / MLA_NOPE + RMS_EPS), lax.rsqrt(s_r / MLA_ROPE + RMS_EPS))
        qm_ref[:, sl] = (c * r * qtab[:, sl]).astype(BF16)
        kc = k_raw[:, sl]
        r_k = lax.rsqrt(jnp.sum(kc * kc, axis=-1, keepdims=True) / MLA_NOPE + RMS_EPS)
        km_ref[:, sl] = (kc * r_k * gkn_ref[...] + kr_placed[:, sl]).astype(BF16)


def _proj_call(x2, p, seq):
    t = x2.shape[0]
    tm = TM_PROJ
    n_pos = seq // tm
    row = lambda i: (i, 0)
    pos = lambda i: (i % n_pos, 0)
    bf = lambda w: jax.ShapeDtypeStruct((t, w), BF16)
    in_specs = [
        pl.BlockSpec((tm, D_MODEL), row),
        _full((1, D_MODEL)), _full((D_MODEL, PROJ_WIDTH)),
        _full((1, LANES)), _full((1, LANES)), _full((1, LANES)), _full((1, LANES)), _full((1, LANES)),
        _full((LANES, 512)), _full((LANES, 512)), _full((LANES, 512)), _full((LANES, 512)),
        _full((1, LANES)),
        pl.BlockSpec((tm, 512), pos), pl.BlockSpec((tm, LANES), pos),
        _full((LANES, 512)),
    ]
    out_specs = [pl.BlockSpec((tm, 512), row)] * 7 + [pl.BlockSpec((tm, POOL_WIDTH), row)]
    out_shape = [bf(512)] * 7 + [jax.ShapeDtypeStruct((t, POOL_WIDTH), F32)]
    return pl.pallas_call(
        _proj_kernel, grid=(t // tm,), in_specs=in_specs, out_specs=out_specs, out_shape=out_shape,
        compiler_params=_cparams(1),
    )(x2, p["g1"], p["win"], p["gq"], p["gk"], p["gckv"], p["gcqa"], p["gcqb"],
      p["wuqa"], p["wuqb"], p["wkk"], p["wkv"], p["gkn"], p["qtab"], p["ktab"], p["eplace"])


def _unflatten(n, sizes):
    n = jnp.minimum(n, math.prod(sizes) - 1)
    coords = []
    for size in reversed(sizes):
        coords.append(n % size)
        n = n // size
    return tuple(reversed(coords))


def _two_stage(n, stage, bufs):
    (s0, m0), (s1, m1) = bufs

    @pl.when(n == 0)
    def _():
        s1[...] = jnp.zeros_like(s1)
        m1[...] = jnp.zeros_like(m1)

    @pl.when(n % 2 == 0)
    def _():
        stage((s0, m0), (s1, m1))

    @pl.when(n % 2 == 1)
    def _():
        stage((s1, m1), (s0, m0))


def _two_stage_scratch(seq):
    pair = [pltpu.VMEM((2 * TQ, seq), F32), pltpu.VMEM((2 * TQ, LANES), F32)]
    return pair + pair


def _softmax_pv_tile(s_prev, m_rows, c, tq, lsum, acc, v_tiles, exp_fn):
    n_half = tq // LANES
    ps = [exp_fn(s_prev[:, (c * n_half + j) * LANES:(c * n_half + j + 1) * LANES] - m_rows)
          for j in range(n_half)]
    for ch in ps:
        lsum = ch if lsum is None else lsum + ch
    pb = jnp.concatenate(ps, axis=1).astype(BF16)
    for g, vt in enumerate(v_tiles):
        pv = jnp.dot(pb[g * tq:(g + 1) * tq], vt, preferred_element_type=F32)
        acc[g] = pv if acc[g] is None else acc[g] + pv
    return lsum


def _running_max(mx, sc):
    for j in range(sc.shape[1] // LANES):
        chunk = sc[:, j * LANES:(j + 1) * LANES]
        mx = chunk if mx is None else jnp.maximum(mx, chunk)
    return mx


def _diff_kernel(lam_ref, q1_ref, q2_ref, qx_ref, k_ref, kx_ref, bd_ref, v_ref, gsub_ref, o_ref,
                 s0_ref, m0_ref, s1_ref, m1_ref, *, sizes):
    n = pl.program_id(0)
    tq = q1_ref.shape[0]
    n_kt = k_ref.shape[0] // tq
    qi_cur = _unflatten(n, sizes)[2]
    qi_prev = _unflatten(jnp.maximum(n - 1, 0), sizes)[2]

    def stage(cur, prev):
        s_cur, m_cur = cur
        s_prev, m_prev = prev
        q1, q2 = q1_ref[...], q2_ref[...]
        qx_left = qx_ref[0, 0]
        qx_right = -qx_left
        bd = bd_ref[0]
        bd2 = jnp.concatenate([bd, bd], axis=0)
        mx = None
        m_rows = m_prev[...]
        lsum = None
        acc = [None, None]
        for c in range(n_kt):
            start = pl.multiple_of(((qi_prev + c) % n_kt) * tq, tq)
            vt = v_ref[pl.ds(start, tq), :]
            lsum = _softmax_pv_tile(s_prev, m_rows, c, tq, lsum, acc, [vt, vt], jnp.exp)

            tile = (qi_cur + c) % n_kt
            start = pl.multiple_of(tile * tq, tq)
            if c == 0:
                qx = jnp.zeros_like(qx_left)
            else:
                qx = jnp.where(qi_cur + c >= n_kt, qx_left, qx_right)
            qq = jnp.concatenate([jnp.concatenate([q1, qx], axis=1),
                                  jnp.concatenate([q2, qx], axis=1)], axis=0)
            kk = jnp.concatenate([k_ref[pl.ds(start, tq), :], kx_ref[pl.ds(start, tq), :]], axis=1)
            sc = lax.dot_general(qq, kk, NT_DIMS, preferred_element_type=F32)
            if c == 0:
                sc = sc + bd2
            s_cur[:, c * tq:(c + 1) * tq] = sc
            mx = _running_max(mx, sc)
        m_cur[...] = jnp.broadcast_to(jnp.max(mx, axis=-1, keepdims=True), m_cur.shape)
        l = jnp.sum(lsum, axis=-1, keepdims=True)
        o = acc[0] * (1.0 / l[0:tq]) - acc[1] * (lam_ref[0] / l[tq:2 * tq])
        r = lax.rsqrt(jnp.mean(o * o, axis=-1, keepdims=True) + RMS_EPS)
        o_ref[...] = (o * r * gsub_ref[...]).astype(BF16)

    _two_stage(n, stage, ((s0_ref, m0_ref), (s1_ref, m1_ref)))


def _alibi_tables(seq):
    nq = seq // TQ
    slopes = 2.0 ** (-8.0 * jnp.arange(1, DIFF_HEADS + 1, dtype=F32) / DIFF_HEADS)
    pos = jnp.arange(seq, dtype=jnp.int32)
    hi = (pos // 256).astype(F32)
    lo = (pos % 256).astype(F32)
    s4 = slopes[:, None]
    ones = jnp.ones((DIFF_HEADS, seq), F32)
    q_left = jnp.stack([-s4 * 256.0 * hi[None], -s4 * lo[None], s4 * 256.0 * ones, s4 * ones], axis=-1)
    qx = jnp.concatenate([q_left, jnp.zeros((DIFF_HEADS, seq, HEAD_SLOT - 4), F32)], axis=-1)
    qx = qx.reshape(DIFF_HEADS, nq, TQ, HEAD_SLOT).astype(BF16)
    k_cols = jnp.stack([jnp.ones((seq,), F32), jnp.ones((seq,), F32), hi, lo], axis=-1)
    kx = jnp.concatenate([k_cols, jnp.zeros((seq, HEAD_SLOT - 4), F32)], axis=-1).astype(BF16)
    loc = jnp.arange(TQ, dtype=jnp.int32)
    bd = -slopes[:, None, None] * jnp.abs(loc[:, None] - loc[None, :]).astype(F32)[None]
    return qx, kx, bd


def _diff_call(lam, dq1, dq2, dk, dv, gsub, tabs, batch, seq):
    t = dq1.shape[0]
    nq = seq // TQ
    qx, kx, bd = tabs
    sizes = (batch, DIFF_HEADS, nq)
    cur = lambda n: _unflatten(n, sizes)
    prev = lambda n: _unflatten(jnp.maximum(n - 1, 0), sizes)

    def qmap(n, *_):
        b, h, qi = cur(n)
        return (b * nq + qi, h)

    def kmap(n, *_):
        b, h, qi = cur(n)
        return (b, h)

    def vmap(n, *_):
        b, h, qi = prev(n)
        return (b, h)

    def omap(n, *_):
        b, h, qi = prev(n)
        return (b * nq + qi, h)

    grid_spec = pltpu.PrefetchScalarGridSpec(
        num_scalar_prefetch=1, grid=(math.prod(sizes) + 1,),
        in_specs=[pl.BlockSpec((TQ, HEAD_SLOT), qmap), pl.BlockSpec((TQ, HEAD_SLOT), qmap),
                  pl.BlockSpec((1, 1, TQ, HEAD_SLOT), lambda n, *_: cur(n)[1:] + (0, 0)),
                  pl.BlockSpec((seq, HEAD_SLOT), kmap),
                  pl.BlockSpec((seq, HEAD_SLOT), lambda *_: (0, 0)),
                  pl.BlockSpec((1, TQ, TQ), lambda n, *_: (cur(n)[1], 0, 0)),
                  pl.BlockSpec((seq, HEAD_SLOT), vmap),
                  pl.BlockSpec((1, HEAD_SLOT), lambda *_: (0, 0))],
        out_specs=pl.BlockSpec((TQ, HEAD_SLOT), omap),
        scratch_shapes=_two_stage_scratch(seq))
    return pl.pallas_call(
        functools.partial(_diff_kernel, sizes=sizes), grid_spec=grid_spec,
        out_shape=jax.ShapeDtypeStruct((t, DIFF_WIDTH), BF16),
        compiler_params=_cparams(1),
    )(lam, dq1, dq2, qx, dk, kx, bd, dv, gsub)


def _mla_kernel(q_ref, k_ref, v_ref, o_ref, s0_ref, m0_ref, s1_ref, m1_ref):
    n = pl.program_id(0)
    tq = q_ref.shape[0]
    n_kt = k_ref.shape[0] // tq

    def stage(cur, prev):
        s_cur, m_cur = cur
        s_prev, m_prev = prev
        mx = [None, None]
        m_rows = m_prev[...]
        lsum = None
        acc = [None, None]
        for c in range(n_kt):
            rows = slice(c * tq, (c + 1) * tq)
            v_tiles = [v_ref[rows, hh * HEAD_SLOT:(hh + 1) * HEAD_SLOT] for hh in range(2)]
            lsum = _softmax_pv_tile(s_prev, m_rows, c, tq, lsum, acc, v_tiles, jnp.exp2)
            for hh in range(2):
                sl = slice(hh * HEAD_SLOT, (hh + 1) * HEAD_SLOT)
                sc = lax.dot_general(q_ref[:, sl], k_ref[rows, sl], NT_DIMS, preferred_element_type=F32)
                s_cur[hh * tq:(hh + 1) * tq, rows] = sc
                mx[hh] = _running_max(mx[hh], sc)
        mx = jnp.concatenate(mx, axis=0)
        m_cur[...] = jnp.broadcast_to(jnp.max(mx, axis=-1, keepdims=True), m_cur.shape)
        l = jnp.sum(lsum, axis=-1, keepdims=True)
        o = acc[0] * (1.0 / l[0:tq]) + acc[1] * (1.0 / l[tq:2 * tq])
        o_ref[...] = o.astype(BF16)

    _two_stage(n, stage, ((s0_ref, m0_ref), (s1_ref, m1_ref)))


def _mla_call(qm, km, vm, batch, seq):
    t = qm.shape[0]
    nq = seq // TQ
    sizes = (batch, MLA_HEADS // 2, nq)
    cur = lambda n: _unflatten(n, sizes)
    prev = lambda n: _unflatten(jnp.maximum(n - 1, 0), sizes)

    def qmap(n):
        b, p, qi = cur(n)
        return (b * nq + qi, p)

    def kmap(n):
        b, p, qi = cur(n)
        return (b, p)

    def vmap(n):
        b, p, qi = prev(n)
        return (b, p)

    def omap(n):
        b, p, qi = prev(n)
        return (b * nq + qi, p)

    return pl.pallas_call(
        _mla_kernel, grid=(math.prod(sizes) + 1,),
        in_specs=[pl.BlockSpec((TQ, 2 * HEAD_SLOT), qmap), pl.BlockSpec((seq, 2 * HEAD_SLOT), kmap),
                  pl.BlockSpec((seq, 2 * HEAD_SLOT), vmap)],
        out_specs=pl.BlockSpec((TQ, HEAD_SLOT), omap),
        out_shape=jax.ShapeDtypeStruct((t, MLA_WIDTH), BF16),
        scratch_shapes=_two_stage_scratch(seq),
        compiler_params=_cparams(1),
    )(qm, km, vm)


def _pool_kernel(u_ref, w_ref, scale_ref, o_ref):
    u = u_ref[...]
    seq, width = u.shape
    zpad = jnp.zeros((POOL_PAD, width), F32)
    ue = jnp.concatenate([zpad, u, zpad], axis=0)
    n = seq + 2 * POOL_PAD

    def down(a, k):
        return pltpu.roll(a, k, axis=0)

    def up(a, k):
        return pltpu.roll(a, n - k, axis=0)

    a2 = ue + down(ue, 1)
    a4 = down(a2, 1) + up(a2, 1)
    a8 = down(a4, 2) + up(a4, 2)
    a16 = down(a8, 4) + up(a8, 4)
    core = slice(POOL_PAD, POOL_PAD + seq)
    lane = lax.broadcasted_iota(jnp.int32, (seq, width), 1)
    tpos = lax.broadcasted_iota(jnp.int32, (seq, width), 0)
    grp = lane // POOL_GROUP_DIM
    win_sum = jnp.where(grp == 0, a2[core], jnp.where(grp == 1, a4[core], jnp.where(grp == 2, a8[core], a16[core])))
    half = jnp.where(grp == 0, 1, jnp.where(grp == 1, 2, jnp.where(grp == 2, 4, 8)))
    lo_i = jnp.maximum(tpos - half, 0)
    hi_i = jnp.minimum(tpos + half - 1, seq - 1)
    cnt = (hi_i - lo_i + 1).astype(F32)
    pooled = win_sum / cnt - u
    mixed = jnp.dot(pooled.astype(BF16), w_ref[...], preferred_element_type=F32)
    o_ref[...] = (mixed * scale_ref[...]).astype(BF16)


def _pool_call(pu, w_bd, scale, batch, seq):
    t = pu.shape[0]
    return pl.pallas_call(
        _pool_kernel, grid=(batch,),
        in_specs=[pl.BlockSpec((seq, POOL_WIDTH), lambda b: (b, 0)),
                  _full((POOL_WIDTH, POOL_WIDTH)), _full((1, POOL_WIDTH))],
        out_specs=pl.BlockSpec((seq, POOL_WIDTH), lambda b: (b, 0)),
        out_shape=jax.ShapeDtypeStruct((t, POOL_WIDTH), BF16),
        compiler_params=_cparams(1),
    )(pu, w_bd, scale)


def _outproj_kernel(x_ref, od_ref, om_ref, op_ref, wo_ref, g2_ref, wrh_ref, wrl_ref, br_ref,
                    xo_ref, h2_ref, ri_ref, rg_ref, cnt_ref, carry_ref):
    i = pl.program_id(0)

    @pl.when(i == 0)
    def _():
        carry_ref[...] = jnp.zeros_like(carry_ref)

    xn = (x_ref[...]
          + jnp.dot(od_ref[...], wo_ref[0:DIFF_WIDTH, :], preferred_element_type=F32)
          + jnp.dot(om_ref[...], wo_ref[DIFF_WIDTH:DIFF_WIDTH + MLA_WIDTH, :], preferred_element_type=F32)
          + jnp.dot(op_ref[...], wo_ref[DIFF_WIDTH + MLA_WIDTH:, :], preferred_element_type=F32))
    xo_ref[...] = xn
    h2 = xn * lax.rsqrt(jnp.mean(xn * xn, axis=-1, keepdims=True) + RMS_EPS) * g2_ref[...]
    n_chunk = D_MODEL // LANES
    for j in range(n_chunk):
        h2_ref[pl.ds(j, xn.shape[0], stride=n_chunk), :] = h2[:, j * LANES:(j + 1) * LANES]

    h_hi = h2.astype(BF16)
    h_lo = (h2 - h_hi.astype(F32)).astype(BF16)
    logits = (jnp.dot(h_hi, wrh_ref[...], preferred_element_type=F32)
              + jnp.dot(h_lo, wrh_ref[...], preferred_element_type=F32)
              + jnp.dot(h_hi, wrl_ref[...], preferred_element_type=F32)
              + br_ref[...])
    tm = logits.shape[0]
    lane = lax.broadcasted_iota(jnp.int32, (tm, LANES), 1)
    lane_f = lane.astype(F32)
    neg = jnp.float32(-jnp.inf)
    big = jnp.float32(1e9)

    gmask = lane < N_GROUPS
    gl = jnp.where(gmask, logits, neg)
    gmax = jnp.max(gl, axis=-1, keepdims=True)
    gsum = jnp.sum(jnp.where(gmask, jnp.exp(gl - gmax), 0.0), axis=-1, keepdims=True)
    g_top = 1.0 / gsum
    g_idx = jnp.min(jnp.where(gl == gmax, lane_f, big), axis=-1, keepdims=True)

    e_lo = N_GROUPS + EXPERTS_PER_GROUP * g_idx
    emask = (lane_f >= e_lo) & (lane_f < e_lo + EXPERTS_PER_GROUP)
    el = jnp.where(emask, logits, neg)
    emax = jnp.max(el, axis=-1, keepdims=True)
    eexp = jnp.where(emask, jnp.exp(el - emax), 0.0)
    prob = eexp / jnp.sum(eexp, axis=-1, keepdims=True)
    pm = jnp.where(emask, prob, -1.0)
    p1 = jnp.max(pm, axis=-1, keepdims=True)
    i1 = jnp.min(jnp.where(pm == p1, lane_f, big), axis=-1, keepdims=True)
    pm2 = jnp.where(lane_f == i1, -1.0, pm)
    p2 = jnp.max(pm2, axis=-1, keepdims=True)
    i2 = jnp.min(jnp.where(pm2 == p2, lane_f, big), axis=-1, keepdims=True)
    denom = p1 + p2
    gate1 = g_top * p1 / denom
    gate2 = g_top * p2 / denom

    sel1 = lane_f == i1
    sel2 = lane_f == i2
    onehot = jnp.where(sel1 | sel2, 1.0, 0.0)
    rr = lax.broadcasted_iota(jnp.int32, (tm, tm), 0)
    cc = lax.broadcasted_iota(jnp.int32, (tm, tm), 1)
    ltri = jnp.where(cc < rr, 1.0, 0.0).astype(BF16)
    prefix = jnp.dot(ltri, onehot.astype(BF16), preferred_element_type=F32) + carry_ref[...]
    rank1 = jnp.sum(jnp.where(sel1, prefix, 0.0), axis=-1, keepdims=True)
    rank2 = jnp.sum(jnp.where(sel2, prefix, 0.0), axis=-1, keepdims=True)
    carry_ref[...] = carry_ref[...] + jnp.sum(onehot, axis=0, keepdims=True)
    cnt_ref[...] = carry_ref[...]

    info = jnp.where(lane == 0, i1 - N_GROUPS,
                     jnp.where(lane == 1, i2 - N_GROUPS,
                               jnp.where(lane == 2, rank1, jnp.where(lane == 3, rank2, 0.0))))
    ri_ref[...] = info.astype(jnp.int32)
    rg_ref[...] = jnp.where(lane == 0, gate1, jnp.where(lane == 1, gate2, 0.0))


def _outproj_call(x2, od, om, op, p):
    t = x2.shape[0]
    tm = TM_OUT
    row = lambda i: (i, 0)
    return pl.pallas_call(
        _outproj_kernel, grid=(t // tm,),
        in_specs=[pl.BlockSpec((tm, D_MODEL), row), pl.BlockSpec((tm, DIFF_WIDTH), row),
                  pl.BlockSpec((tm, MLA_WIDTH), row), pl.BlockSpec((tm, POOL_WIDTH), row),
                  _full((D_MODEL, D_MODEL)), _full((1, D_MODEL)),
                  _full((D_MODEL, LANES)), _full((D_MODEL, LANES)), _full((1, LANES))],
        out_specs=[pl.BlockSpec((tm, D_MODEL), row),
                   pl.BlockSpec((tm * D_MODEL // LANES, LANES), row),
                   pl.BlockSpec((tm, LANES), row), pl.BlockSpec((tm, LANES), row),
                   _full((1, LANES))],
        out_shape=[jax.ShapeDtypeStruct((t, D_MODEL), F32),
                   jax.ShapeDtypeStruct((t * D_MODEL // LANES, LANES), F32),
                   jax.ShapeDtypeStruct((t, LANES), jnp.int32), jax.ShapeDtypeStruct((t, LANES), F32),
                   jax.ShapeDtypeStruct((1, LANES), F32)],
        scratch_shapes=[pltpu.VMEM((1, LANES), F32)],
        compiler_params=_cparams(1),
    )(x2, od, om, op, p["wo"], p["g2"], p["wrh"], p["wrl"], p["br"])


def _expert_kernel(be_ref, idx0_ref, idxn_ref, h_hbm, wg_ref, wu_ref, wd_ref, ys_ref,
                   buf0, buf1, sem):
    i = pl.program_id(0)
    n = pl.num_programs(0)
    n_chunk = D_MODEL // LANES
    quarter = ROUTE_BLOCK // 4

    def issue(idx_ref, buf, sem_slot, rows):
        for r in rows:
            src = pl.multiple_of(idx_ref[0, 0, r] * n_chunk, n_chunk)
            pltpu.make_async_copy(h_hbm.at[pl.ds(src, n_chunk), :],
                                  buf.at[pl.ds(r * n_chunk, n_chunk), :], sem_slot).start()

    def wait_block(buf, sem_slot):
        pltpu.make_async_copy(h_hbm.at[pl.ds(0, ROUTE_BLOCK * n_chunk), :], buf, sem_slot).wait()

    @pl.when(i == 0)
    def _():
        issue(idx0_ref, buf0, sem.at[0], range(ROUTE_BLOCK))

    def step(buf, sem_cur, buf_next, sem_next):
        wait_block(buf, sem_cur)
        xb = jnp.concatenate([buf[pl.ds(j, ROUTE_BLOCK, stride=n_chunk), :] for j in range(n_chunk)],
                             axis=1).astype(BF16)
        issue(idxn_ref, buf_next, sem_next, range(0, quarter))
        g = jnp.dot(xb, wg_ref[0, 0].astype(BF16), preferred_element_type=F32)
        issue(idxn_ref, buf_next, sem_next, range(quarter, 2 * quarter))
        u = jnp.dot(xb, wu_ref[0, 0].astype(BF16), preferred_element_type=F32)
        issue(idxn_ref, buf_next, sem_next, range(2 * quarter, 3 * quarter))
        hmid = g * (1.0 / (1.0 + jnp.exp(-g))) * u
        y = jnp.dot(hmid.astype(BF16), wd_ref[0, 0].astype(BF16), preferred_element_type=F32)
        for j in range(n_chunk):
            ys_ref[pl.ds(j, ROUTE_BLOCK, stride=n_chunk), :] = y[:, j * LANES:(j + 1) * LANES]
        issue(idxn_ref, buf_next, sem_next, range(3 * quarter, ROUTE_BLOCK))

    @pl.when(i % 2 == 0)
    def _():
        step(buf0, sem.at[0], buf1, sem.at[1])

    @pl.when(i % 2 == 1)
    def _():
        step(buf1, sem.at[1], buf0, sem.at[0])

    @pl.when(i == n - 1)
    def _():
        @pl.when(i % 2 == 0)
        def _():
            wait_block(buf1, sem.at[1])

        @pl.when(i % 2 == 1)
        def _():
            wait_block(buf0, sem.at[0])


def _expert_call(block_eid, slot_tok3, h3, wg, wu, wd, layer):
    n_blocks = slot_tok3.shape[0]
    wmap = lambda i, be: (layer, be[i], 0, 0)
    tile = (ROUTE_BLOCK * D_MODEL // LANES, LANES)
    grid_spec = pltpu.PrefetchScalarGridSpec(
        num_scalar_prefetch=1, grid=(n_blocks,),
        in_specs=[
            pl.BlockSpec((1, 1, ROUTE_BLOCK), lambda i, be: (0, 0, 0), memory_space=pltpu.SMEM),
            pl.BlockSpec((1, 1, ROUTE_BLOCK), lambda i, be: (jnp.minimum(i + 1, n_blocks - 1), 0, 0),
                         memory_space=pltpu.SMEM),
            pl.BlockSpec(memory_space=pl.ANY),
            pl.BlockSpec((1, 1, D_MODEL, D_FF), wmap),
            pl.BlockSpec((1, 1, D_MODEL, D_FF), wmap),
            pl.BlockSpec((1, 1, D_FF, D_MODEL), wmap),
        ],
        out_specs=pl.BlockSpec(tile, lambda i, be: (i, 0)),
        scratch_shapes=[pltpu.VMEM(tile, F32), pltpu.VMEM(tile, F32), pltpu.SemaphoreType.DMA((2,))])
    return pl.pallas_call(
        _expert_kernel, grid_spec=grid_spec,
        out_shape=jax.ShapeDtypeStruct((n_blocks * tile[0], LANES), F32),
        compiler_params=_cparams(1),
    )(block_eid, slot_tok3, slot_tok3, h3, wg, wu, wd)


def _combine_kernel(idx0_ref, idxn_ref, ys_hbm, x_ref, rg_ref, o_ref, buf0, buf1, sem):
    i = pl.program_id(0)
    n = pl.num_programs(0)
    tm = x_ref.shape[0]
    n_chunk = D_MODEL // LANES
    half = tm // 2

    def issue(idx_ref, buf, sem_slot, rows):
        for r in rows:
            for kk in range(2):
                src = pl.multiple_of(idx_ref[0, 0, 2 * r + kk] * n_chunk, n_chunk)
                pltpu.make_async_copy(ys_hbm.at[pl.ds(src, n_chunk), :],
                                      buf.at[pl.ds((kk * tm + r) * n_chunk, n_chunk), :], sem_slot).start()

    def wait_tile(buf, sem_slot):
        pltpu.make_async_copy(ys_hbm.at[pl.ds(0, 2 * tm * n_chunk), :], buf, sem_slot).wait()

    @pl.when(i == 0)
    def _():
        issue(idx0_ref, buf0, sem.at[0], range(tm))

    def step(buf, sem_cur, buf_next, sem_next):
        wait_tile(buf, sem_cur)
        rg = rg_ref[...]
        g0 = rg[:, 0:1]
        g1 = rg[:, 1:2]
        issue(idxn_ref, buf_next, sem_next, range(0, half))
        for j in range(n_chunk):
            cols = slice(j * LANES, (j + 1) * LANES)
            y0 = buf[pl.ds(j, tm, stride=n_chunk), :]
            y1 = buf[pl.ds(tm * n_chunk + j, tm, stride=n_chunk), :]
            o_ref[:, cols] = x_ref[:, cols] + g0 * y0 + g1 * y1
        issue(idxn_ref, buf_next, sem_next, range(half, tm))

    @pl.when(i % 2 == 0)
    def _():
        step(buf0, sem.at[0], buf1, sem.at[1])

    @pl.when(i % 2 == 1)
    def _():
        step(buf1, sem.at[1], buf0, sem.at[0])

    @pl.when(i == n - 1)
    def _():
        @pl.when(i % 2 == 0)
        def _():
            wait_tile(buf1, sem.at[1])

        @pl.when(i % 2 == 1)
        def _():
            wait_tile(buf0, sem.at[0])


def _combine_call(dest3, ys, x2, rg):
    t = x2.shape[0]
    tm = TM_COMB
    n = t // tm
    row = lambda i: (i, 0)
    tile = (2 * tm * D_MODEL // LANES, LANES)
    return pl.pallas_call(
        _combine_kernel, grid=(n,),
        in_specs=[
            pl.BlockSpec((1, 1, 2 * tm), lambda i: (0, 0, 0), memory_space=pltpu.SMEM),
            pl.BlockSpec((1, 1, 2 * tm), lambda i: (jnp.minimum(i + 1, n - 1), 0, 0), memory_space=pltpu.SMEM),
            pl.BlockSpec(memory_space=pl.ANY),
            pl.BlockSpec((tm, D_MODEL), row), pl.BlockSpec((tm, LANES), row)],
        out_specs=pl.BlockSpec((tm, D_MODEL), row),
        out_shape=jax.ShapeDtypeStruct((t, D_MODEL), F32),
        scratch_shapes=[pltpu.VMEM(tile, F32), pltpu.VMEM(tile, F32), pltpu.SemaphoreType.DMA((2,))],
        compiler_params=_cparams(1),
    )(dest3, dest3, ys, x2, rg)


def _swap_halves(a):
    half = a.shape[-1] // 2
    return jnp.concatenate([a[..., half:], a[..., :half]], axis=-1)


def _layer_params(l, seq, w):
    p = {}
    row = lambda v: v.reshape(1, -1).astype(F32)
    w_in = w["w_in"][l]
    kr_cols = w_in[:, 1856:1888]
    p["win"] = jnp.concatenate(
        [w_in[:, 0:1536], w_in[:, 1888:2144], w_in[:, 1728:1856], w_in[:, 1536:1728],
         kr_cols, _swap_halves(kr_cols)], axis=1).astype(BF16)
    p["g1"] = row(w["norm1_g"][l])
    p["gq"] = row(jnp.tile(w["diff_q_norm_g"][l], 2) * (DIFF_QK ** -0.5))
    p["gk"] = row(jnp.tile(w["diff_k_norm_g"][l], 2))
    p["gckv"] = row(w["mla_kv_lat_norm_g"][l])
    gcq = w["mla_q_lat_norm_g"][l]
    p["gcqa"] = row(gcq[:LANES])
    p["gcqb"] = row(jnp.concatenate([gcq[LANES:], jnp.zeros((2 * LANES - MLA_Q_RANK,), F32)]))

    wuq = w["mla_w_uq"][l].reshape(MLA_Q_RANK, MLA_HEADS, MLA_NOPE + MLA_ROPE)
    rope_w = wuq[:, :, MLA_NOPE:]
    wuq = jnp.concatenate([wuq[:, :, :MLA_NOPE], rope_w, _swap_halves(rope_w)], axis=-1)
    wuq = wuq.reshape(MLA_Q_RANK, MLA_HEADS * HEAD_SLOT)
    wuq = jnp.concatenate([wuq, jnp.zeros((2 * LANES - MLA_Q_RANK, wuq.shape[1]), F32)], axis=0).astype(BF16)
    p["wuqa"] = wuq[:LANES]
    p["wuqb"] = wuq[LANES:]

    wukv = w["mla_w_ukv"][l].reshape(MLA_KV_RANK, MLA_HEADS, MLA_NOPE + MLA_V)
    zk = jnp.zeros((MLA_KV_RANK, MLA_HEADS, HEAD_SLOT - MLA_NOPE), F32)
    p["wkk"] = jnp.concatenate([wukv[:, :, :MLA_NOPE], zk], axis=-1).reshape(MLA_KV_RANK, -1).astype(BF16)
    vcols = wukv[:, :, MLA_NOPE:]
    zv = jnp.zeros_like(vcols)
    even = (jnp.arange(MLA_HEADS) % 2 == 0)[None, :, None]
    wkv = jnp.concatenate([jnp.where(even, vcols, zv), jnp.where(even, zv, vcols)], axis=-1)
    p["wkv"] = wkv.reshape(MLA_KV_RANK, -1).astype(BF16)
    p["gkn"] = row(jnp.concatenate([w["mla_k_nope_norm_g"][l], jnp.zeros((HEAD_SLOT - MLA_NOPE,), F32)]))

    inv = 1.0 / (ROPE_BASE ** (jnp.arange(0, MLA_ROPE, 2, dtype=F32) / MLA_ROPE))
    ang = jnp.arange(seq, dtype=F32)[:, None] * inv[None, :]
    cosf = jnp.concatenate([jnp.cos(ang), jnp.cos(ang)], axis=-1)
    sinf = jnp.concatenate([-jnp.sin(ang), jnp.sin(ang)], axis=-1)
    scale = (MLA_NOPE + MLA_ROPE) ** -0.5 * math.log2(math.e)
    gqr = w["mla_q_rope_norm_g"][l]
    q_head = jnp.concatenate([jnp.broadcast_to(w["mla_q_nope_norm_g"][l][None, :], (seq, MLA_NOPE)),
                              gqr[None, :] * cosf, _swap_halves(gqr)[None, :] * sinf], axis=-1) * scale
    p["qtab"] = jnp.tile(q_head, (1, MLA_HEADS))
    gkr = w["mla_k_rope_norm_g"][l]
    p["ktab"] = jnp.concatenate([jnp.zeros((seq, MLA_NOPE), F32), gkr[None, :] * cosf,
                                 _swap_halves(gkr)[None, :] * sinf], axis=-1)
    src = jnp.arange(LANES)
    dst = jnp.arange(MLA_HEADS * HEAD_SLOT)
    src_j = jnp.where(src >= MLA_NOPE, (src - MLA_NOPE) % MLA_ROPE, -1)
    dst_l = dst % HEAD_SLOT
    dst_j = jnp.where(dst_l >= MLA_NOPE, (dst_l - MLA_NOPE) % MLA_ROPE, -2)
    p["eplace"] = (src_j[:, None] == dst_j[None, :]).astype(BF16)

    pw = w["pool_w"][l]
    bd = jnp.zeros((POOL_WIDTH, POOL_WIDTH), F32)
    for g in range(POOL_GROUPS):
        s0 = g * POOL_GROUP_DIM
        bd = bd.at[s0:s0 + POOL_GROUP_DIM, s0:s0 + POOL_GROUP_DIM].set(pw[g])
    p["pool_w"] = bd.astype(BF16)
    p["pool_scale"] = row(w["pool_scale"][l])

    lam_init = 0.8 - 0.6 * math.exp(-0.3 * l)
    lv = w["diff_lambda"][l].astype(F32)
    p["lam"] = (jnp.exp(jnp.sum(lv[0] * lv[1])) - jnp.exp(jnp.sum(lv[2] * lv[3])) + lam_init).reshape(1)
    p["gsub"] = row(w["diff_sub_norm_g"][l] * (1.0 - lam_init))

    p["wo"] = w["w_out"][l].astype(BF16)
    p["g2"] = row(w["norm2_g"][l])
    wr = jnp.concatenate([w["router_group_w"][l], w["router_expert_w"][l],
                          jnp.zeros((D_MODEL, LANES - N_GROUPS - N_EXPERTS), F32)], axis=1)
    wr_hi = wr.astype(BF16)
    p["wrh"] = wr_hi
    p["wrl"] = (wr - wr_hi.astype(F32)).astype(BF16)
    p["br"] = row(jnp.concatenate([w["router_group_b"][l], w["router_expert_b"][l],
                                   jnp.zeros((LANES - N_GROUPS - N_EXPERTS,), F32)]))
    return p


def kernel(x, norm1_g, w_in, diff_q_norm_g, diff_k_norm_g, diff_lambda, diff_sub_norm_g, mla_q_lat_norm_g, mla_kv_lat_norm_g, mla_w_uq, mla_w_ukv, mla_q_nope_norm_g, mla_q_rope_norm_g, mla_k_nope_norm_g, mla_k_rope_norm_g, pool_w, pool_scale, w_out, norm2_g, router_group_w, router_group_b, router_expert_w, router_expert_b, expert_w_gate, expert_w_up, expert_w_down):
    w = dict(norm1_g=norm1_g, w_in=w_in, diff_q_norm_g=diff_q_norm_g, diff_k_norm_g=diff_k_norm_g,
             diff_lambda=diff_lambda, diff_sub_norm_g=diff_sub_norm_g, mla_q_lat_norm_g=mla_q_lat_norm_g,
             mla_kv_lat_norm_g=mla_kv_lat_norm_g, mla_w_uq=mla_w_uq, mla_w_ukv=mla_w_ukv,
             mla_q_nope_norm_g=mla_q_nope_norm_g, mla_q_rope_norm_g=mla_q_rope_norm_g,
             mla_k_nope_norm_g=mla_k_nope_norm_g, mla_k_rope_norm_g=mla_k_rope_norm_g,
             pool_w=pool_w, pool_scale=pool_scale, w_out=w_out, norm2_g=norm2_g,
             router_group_w=router_group_w, router_group_b=router_group_b,
             router_expert_w=router_expert_w, router_expert_b=router_expert_b)
    batch, seq, d = x.shape
    t = batch * seq
    n_assign = 2 * t
    n_blocks = n_assign // ROUTE_BLOCK + N_EXPERTS
    alibi = _alibi_tables(seq)
    token_of_assign = jnp.repeat(jnp.arange(t, dtype=jnp.int32), 2)

    x2 = x.reshape(t, d)
    for l in range(DEPTH):
        p = _layer_params(l, seq, w)
        dq1, dq2, dk, dv, qm, km, vm, pu = _proj_call(x2, p, seq)
        o_diff = _diff_call(p["lam"], dq1, dq2, dk, dv, p["gsub"], alibi, batch, seq)
        o_mla = _mla_call(qm, km, vm, batch, seq)
        o_pool = _pool_call(pu, p["pool_w"], p["pool_scale"], batch, seq)
        x2, h2, route_i, route_g, counts = _outproj_call(x2, o_diff, o_mla, o_pool, p)

        cnt = counts[0, N_GROUPS:N_GROUPS + N_EXPERTS].astype(jnp.int32)
        padded = (cnt + ROUTE_BLOCK - 1) // ROUTE_BLOCK * ROUTE_BLOCK
        padded_ends = jnp.cumsum(padded)
        padded_starts = padded_ends - padded
        dest = padded_starts[route_i[:, 0:2]] + route_i[:, 2:4]
        block_start = jnp.arange(n_blocks, dtype=jnp.int32) * ROUTE_BLOCK
        block_eid = jnp.minimum(jnp.sum(block_start[:, None] >= padded_ends[None, :], axis=1),
                                N_EXPERTS - 1).astype(jnp.int32)
        n_used = (padded_ends[-1] // ROUTE_BLOCK).astype(jnp.int32).reshape(1)
        slot_tok = jnp.zeros((n_blocks * ROUTE_BLOCK,), jnp.int32).at[dest.reshape(-1)].set(token_of_assign)

        ys = _expert_call(block_eid, slot_tok.reshape(n_blocks, 1, ROUTE_BLOCK), h2,
                          expert_w_gate, expert_w_up, expert_w_down, l)
        x2 = _combine_call(dest.reshape(t // TM_COMB, 1, 2 * TM_COMB), ys, x2, route_g)
    return x2.reshape(batch, seq, d)
```

```python
import functools
import math

import jax
import jax.numpy as jnp
from jax import lax
from jax.experimental import pallas as pl
from jax.experimental.pallas import tpu as pltpu

F32 = jnp.float32
BF16 = jnp.bfloat16

D_MODEL = 1024
DEPTH = 2
DIFF_HEADS = 4
DIFF_QK = 64
DIFF_V = 128
DIFF_WIDTH = 512
MLA_HEADS = 4
MLA_NOPE = 64
MLA_ROPE = 32
MLA_V = 64
MLA_Q_RANK = 192
MLA_KV_RANK = 128
MLA_WIDTH = 256
ROPE_BASE = 10000.0
POOL_WIDTH = 256
POOL_GROUPS = 4
POOL_GROUP_DIM = 64
POOL_WINDOWS = (2, 4, 8, 16)
N_GROUPS = 4
EXPERTS_PER_GROUP = 8
N_EXPERTS = 32
D_FF = 256
ROUTE_BLOCK = 256
RMS_EPS = 1e-6

LANES = 128
HEAD_SLOT = 128
PROJ_WIDTH = 2176
POOL_PAD = 16
VMEM_LIMIT = 48 * 1024 * 1024

TM_PROJ = 512
TQ = 256
TM_OUT = 256
TM_COMB = 256

NT_DIMS = (((1,), (1,)), ((), ()))


def _cparams(n_axes):
    return pltpu.CompilerParams(dimension_semantics=("arbitrary",) * n_axes,
                                vmem_limit_bytes=VMEM_LIMIT)


def _full(shape):
    return pl.BlockSpec(shape, lambda *_: (0,) * len(shape))


def _proj_kernel(x_ref, g1_ref, win_ref, gq_ref, gk_ref, gckv_ref, gcqa_ref, gcqb_ref,
                 wuqa_ref, wuqb_ref, wkk_ref, wkv_ref, gkn_ref, qtab_ref, ktab_ref, eplace_ref,
                 dq1_ref, dq2_ref, dk_ref, dv_ref, qm_ref, km_ref, vm_ref, pu_ref):
    x = x_ref[...]
    xn = x * lax.rsqrt(jnp.mean(x * x, axis=-1, keepdims=True) + RMS_EPS) * g1_ref[...]
    proj = jnp.dot(xn.astype(BF16), win_ref[...], preferred_element_type=F32)

    tm = x.shape[0]
    lane = lax.broadcasted_iota(jnp.int32, (tm, LANES), 1)
    lo = lane < DIFF_QK

    def half_norm(c, g_row):
        sq = c * c
        s_lo = jnp.sum(jnp.where(lo, sq, 0.0), axis=-1, keepdims=True)
        s_hi = jnp.sum(jnp.where(lo, 0.0, sq), axis=-1, keepdims=True)
        r = jnp.where(lo, lax.rsqrt(s_lo / DIFF_QK + RMS_EPS), lax.rsqrt(s_hi / DIFF_QK + RMS_EPS))
        return c * r * g_row

    for h in range(DIFF_HEADS):
        sl = slice(h * HEAD_SLOT, (h + 1) * HEAD_SLOT)
        qn = half_norm(proj[:, sl], gq_ref[...])
        dq1_ref[:, sl] = jnp.where(lo, qn, 0.0).astype(BF16)
        dq2_ref[:, sl] = jnp.where(lo, 0.0, qn).astype(BF16)
        ksl = slice(512 + h * HEAD_SLOT, 512 + (h + 1) * HEAD_SLOT)
        dk_ref[:, sl] = half_norm(proj[:, ksl], gk_ref[...]).astype(BF16)
    dv_ref[...] = proj[:, 1024:1536].astype(BF16)
    pu_ref[...] = proj[:, 1536:1792]

    ckv = proj[:, 1792:1920]
    ckvn = ckv * lax.rsqrt(jnp.mean(ckv * ckv, axis=-1, keepdims=True) + RMS_EPS) * gckv_ref[...]
    ckvn = ckvn.astype(BF16)
    cqa = proj[:, 1920:2048]
    last = proj[:, 2048:2176]
    lsq = last * last
    ss_q = (jnp.sum(cqa * cqa, axis=-1, keepdims=True)
            + jnp.sum(jnp.where(lo, lsq, 0.0), axis=-1, keepdims=True))
    r_q = lax.rsqrt(ss_q / MLA_Q_RANK + RMS_EPS)
    q_raw = (jnp.dot((cqa * r_q * gcqa_ref[...]).astype(BF16), wuqa_ref[...], preferred_element_type=F32)
             + jnp.dot((last * r_q * gcqb_ref[...]).astype(BF16), wuqb_ref[...], preferred_element_type=F32))

    rope_lanes = (lane >= MLA_NOPE) & (lane < MLA_NOPE + MLA_ROPE)
    ss_kr = jnp.sum(jnp.where(rope_lanes, lsq, 0.0), axis=-1, keepdims=True)
    kr_terms = last * lax.rsqrt(ss_kr / MLA_ROPE + RMS_EPS) * ktab_ref[...]
    kr_placed = jnp.dot(kr_terms.astype(BF16), eplace_ref[...], preferred_element_type=F32)

    k_raw = jnp.dot(ckvn, wkk_ref[...], preferred_element_type=F32)
    vm_ref[...] = jnp.dot(ckvn, wkv_ref[...], preferred_element_type=F32).astype(BF16)
    qtab = qtab_ref[...]
    for h in range(MLA_HEADS):
        sl = slice(h * HEAD_SLOT, (h + 1) * HEAD_SLOT)
        c = q_raw[:, sl]
        sq = c * c
        s_n = jnp.sum(jnp.where(lo, sq, 0.0), axis=-1, keepdims=True)
        s_r = jnp.sum(jnp.where(rope_lanes, sq, 0.0), axis=-1, keepdims=True)
        r = jnp.where(lo, lax.rsqrt(s_n / MLA_NOPE + RMS_EPS), lax.rsqrt(s_r / MLA_ROPE + RMS_EPS))
        qm_ref[:, sl] = (c * r * qtab[:, sl]).astype(BF16)
        kc = k_raw[:, sl]
        r_k = lax.rsqrt(jnp.sum(kc * kc, axis=-1, keepdims=True) / MLA_NOPE + RMS_EPS)
        km_ref[:, sl] = (kc * r_k * gkn_ref[...] + kr_placed[:, sl]).astype(BF16)


def _proj_call(x2, p, seq):
    t = x2.shape[0]
    tm = TM_PROJ
    n_pos = seq // tm
    row = lambda i: (i, 0)
    pos = lambda i: (i % n_pos, 0)
    bf = lambda w: jax.ShapeDtypeStruct((t, w), BF16)
    in_specs = [
        pl.BlockSpec((tm, D_MODEL), row),
        _full((1, D_MODEL)), _full((D_MODEL, PROJ_WIDTH)),
        _full((1, LANES)), _full((1, LANES)), _full((1, LANES)), _full((1, LANES)), _full((1, LANES)),
        _full((LANES, 512)), _full((LANES, 512)), _full((LANES, 512)), _full((LANES, 512)),
        _full((1, LANES)),
        pl.BlockSpec((tm, 512), pos), pl.BlockSpec((tm, LANES), pos),
        _full((LANES, 512)),
    ]
    out_specs = [pl.BlockSpec((tm, 512), row)] * 7 + [pl.BlockSpec((tm, POOL_WIDTH), row)]
    out_shape = [bf(512)] * 7 + [jax.ShapeDtypeStruct((t, POOL_WIDTH), F32)]
    return pl.pallas_call(
        _proj_kernel, grid=(t // tm,), in_specs=in_specs, out_specs=out_specs, out_shape=out_shape,
        compiler_params=_cparams(1),
    )(x2, p["g1"], p["win"], p["gq"], p["gk"], p["gckv"], p["gcqa"], p["gcqb"],
      p["wuqa"], p["wuqb"], p["wkk"], p["wkv"], p["gkn"], p["qtab"], p["ktab"], p["eplace"])


def _unflatten(n, sizes):
    n = jnp.minimum(n, math.prod(sizes) - 1)
    coords = []
    for size in reversed(sizes):
        coords.append(n % size)
        n = n // size
    return tuple(reversed(coords))


def _two_stage(n, stage, bufs):
    (s0, m0), (s1, m1) = bufs

    @pl.when(n == 0)
    def _():
        s1[...] = jnp.zeros_like(s1)
        m1[...] = jnp.zeros_like(m1)

    @pl.when(n % 2 == 0)
    def _():
        stage((s0, m0), (s1, m1))

    @pl.when(n % 2 == 1)
    def _():
        stage((s1, m1), (s0, m0))


def _two_stage_scratch(seq):
    pair = [pltpu.VMEM((2 * TQ, seq), F32), pltpu.VMEM((2 * TQ, LANES), F32)]
    return pair + pair


def _softmax_pv_tile(s_prev, m_rows, c, tq, lsum, acc, v_tiles, exp_fn):
    n_half = tq // LANES
    ps = [exp_fn(s_prev[:, (c * n_half + j) * LANES:(c * n_half + j + 1) * LANES] - m_rows)
          for j in range(n_half)]
    for ch in ps:
        lsum = ch if lsum is None else lsum + ch
    pb = jnp.concatenate(ps, axis=1).astype(BF16)
    for g, vt in enumerate(v_tiles):
        pv = jnp.dot(pb[g * tq:(g + 1) * tq], vt, preferred_element_type=F32)
        acc[g] = pv if acc[g] is None else acc[g] + pv
    return lsum


def _running_max(mx, sc):
    for j in range(sc.shape[1] // LANES):
        chunk = sc[:, j * LANES:(j + 1) * LANES]
        mx = chunk if mx is None else jnp.maximum(mx, chunk)
    return mx


def _diff_kernel(lam_ref, q1_ref, q2_ref, qx_ref, k_ref, kx_ref, bd_ref, v_ref, gsub_ref, o_ref,
                 s0_ref, m0_ref, s1_ref, m1_ref, *, sizes):
    n = pl.program_id(0)
    tq = q1_ref.shape[0]
    n_kt = k_ref.shape[0] // tq
    qi_cur = _unflatten(n, sizes)[2]
    qi_prev = _unflatten(jnp.maximum(n - 1, 0), sizes)[2]

    def stage(cur, prev):
        s_cur, m_cur = cur
        s_prev, m_prev = prev
        q1, q2 = q1_ref[...], q2_ref[...]
        qx_left = qx_ref[0, 0]
        qx_right = -qx_left
        bd = bd_ref[0]
        bd2 = jnp.concatenate([bd, bd], axis=0)
        mx = None
        m_rows = m_prev[...]
        lsum = None
        acc = [None, None]
        for c in range(n_kt):
            start = pl.multiple_of(((qi_prev + c) % n_kt) * tq, tq)
            vt = v_ref[pl.ds(start, tq), :]
            lsum = _softmax_pv_tile(s_prev, m_rows, c, tq, lsum, acc, [vt, vt], jnp.exp)

            tile = (qi_cur + c) % n_kt
            start = pl.multiple_of(tile * tq, tq)
            if c == 0:
                qx = jnp.zeros_like(qx_left)
            else:
                qx = jnp.where(qi_cur + c >= n_kt, qx_left, qx_right)
            qq = jnp.concatenate([jnp.concatenate([q1, qx], axis=1),
                                  jnp.concatenate([q2, qx], axis=1)], axis=0)
            kk = jnp.concatenate([k_ref[pl.ds(start, tq), :], kx_ref[pl.ds(start, tq), :]], axis=1)
            sc = lax.dot_general(qq, kk, NT_DIMS, preferred_element_type=F32)
            if c == 0:
                sc = sc + bd2
            s_cur[:, c * tq:(c + 1) * tq] = sc
            mx = _running_max(mx, sc)
        m_cur[...] = jnp.broadcast_to(jnp.max(mx, axis=-1, keepdims=True), m_cur.shape)
        l = jnp.sum(lsum, axis=-1, keepdims=True)
        o = acc[0] * (1.0 / l[0:tq]) - acc[1] * (lam_ref[0] / l[tq:2 * tq])
        r = lax.rsqrt(jnp.mean(o * o, axis=-1, keepdims=True) + RMS_EPS)
        o_ref[...] = (o * r * gsub_ref[...]).astype(BF16)

    _two_stage(n, stage, ((s0_ref, m0_ref), (s1_ref, m1_ref)))


def _alibi_tables(seq):
    nq = seq // TQ
    slopes = 2.0 ** (-8.0 * jnp.arange(1, DIFF_HEADS + 1, dtype=F32) / DIFF_HEADS)
    pos = jnp.arange(seq, dtype=jnp.int32)
    hi = (pos // 256).astype(F32)
    lo = (pos % 256).astype(F32)
    s4 = slopes[:, None]
    ones = jnp.ones((DIFF_HEADS, seq), F32)
    q_left = jnp.stack([-s4 * 256.0 * hi[None], -s4 * lo[None], s4 * 256.0 * ones, s4 * ones], axis=-1)
    qx = jnp.concatenate([q_left, jnp.zeros((DIFF_HEADS, seq, HEAD_SLOT - 4), F32)], axis=-1)
    qx = qx.reshape(DIFF_HEADS, nq, TQ, HEAD_SLOT).astype(BF16)
    k_cols = jnp.stack([jnp.ones((seq,), F32), jnp.ones((seq,), F32), hi, lo], axis=-1)
    kx = jnp.concatenate([k_cols, jnp.zeros((seq, HEAD_SLOT - 4), F32)], axis=-1).astype(BF16)
    loc = jnp.arange(TQ, dtype=jnp.int32)
    bd = -slopes[:, None, None] * jnp.abs(loc[:, None] - loc[None, :]).astype(F32)[None]
    return qx, kx, bd


def _diff_call(lam, dq1, dq2, dk, dv, gsub, tabs, batch, seq):
    t = dq1.shape[0]
    nq = seq // TQ
    qx, kx, bd = tabs
    sizes = (batch, DIFF_HEADS, nq)
    cur = lambda n: _unflatten(n, sizes)
    prev = lambda n: _unflatten(jnp.maximum(n - 1, 0), sizes)

    def qmap(n, *_):
        b, h, qi = cur(n)
        return (b * nq + qi, h)

    def kmap(n, *_):
        b, h, qi = cur(n)
        return (b, h)

    def vmap(n, *_):
        b, h, qi = prev(n)
        return (b, h)

    def omap(n, *_):
        b, h, qi = prev(n)
        return (b * nq + qi, h)

    grid_spec = pltpu.PrefetchScalarGridSpec(
        num_scalar_prefetch=1, grid=(math.prod(sizes) + 1,),
        in_specs=[pl.BlockSpec((TQ, HEAD_SLOT), qmap), pl.BlockSpec((TQ, HEAD_SLOT), qmap),
                  pl.BlockSpec((1, 1, TQ, HEAD_SLOT), lambda n, *_: cur(n)[1:] + (0, 0)),
                  pl.BlockSpec((seq, HEAD_SLOT), kmap),
                  pl.BlockSpec((seq, HEAD_SLOT), lambda *_: (0, 0)),
                  pl.BlockSpec((1, TQ, TQ), lambda n, *_: (cur(n)[1], 0, 0)),
                  pl.BlockSpec((seq, HEAD_SLOT), vmap),
                  pl.BlockSpec((1, HEAD_SLOT), lambda *_: (0, 0))],
        out_specs=pl.BlockSpec((TQ, HEAD_SLOT), omap),
        scratch_shapes=_two_stage_scratch(seq))
    return pl.pallas_call(
        functools.partial(_diff_kernel, sizes=sizes), grid_spec=grid_spec,
        out_shape=jax.ShapeDtypeStruct((t, DIFF_WIDTH), BF16),
        compiler_params=_cparams(1),
    )(lam, dq1, dq2, qx, dk, kx, bd, dv, gsub)


def _mla_kernel(q_ref, k_ref, v_ref, o_ref, s0_ref, m0_ref, s1_ref, m1_ref):
    n = pl.program_id(0)
    tq = q_ref.shape[0]
    n_kt = k_ref.shape[0] // tq

    def stage(cur, prev):
        s_cur, m_cur = cur
        s_prev, m_prev = prev
        mx = [None, None]
        m_rows = m_prev[...]
        lsum = None
        acc = [None, None]
        for c in range(n_kt):
            rows = slice(c * tq, (c + 1) * tq)
            v_tiles = [v_ref[rows, hh * HEAD_SLOT:(hh + 1) * HEAD_SLOT] for hh in range(2)]
            lsum = _softmax_pv_tile(s_prev, m_rows, c, tq, lsum, acc, v_tiles, jnp.exp2)
            for hh in range(2):
                sl = slice(hh * HEAD_SLOT, (hh + 1) * HEAD_SLOT)
                sc = lax.dot_general(q_ref[:, sl], k_ref[rows, sl], NT_DIMS, preferred_element_type=F32)
                s_cur[hh * tq:(hh + 1) * tq, rows] = sc
                mx[hh] = _running_max(mx[hh], sc)
        mx = jnp.concatenate(mx, axis=0)
        m_cur[...] = jnp.broadcast_to(jnp.max(mx, axis=-1, keepdims=True), m_cur.shape)
        l = jnp.sum(lsum, axis=-1, keepdims=True)
        o = acc[0] * (1.0 / l[0:tq]) + acc[1] * (1.0 / l[tq:2 * tq])
        o_ref[...] = o.astype(BF16)

    _two_stage(n, stage, ((s0_ref, m0_ref), (s1_ref, m1_ref)))


def _mla_call(qm, km, vm, batch, seq):
    t = qm.shape[0]
    nq = seq // TQ
    sizes = (batch, MLA_HEADS // 2, nq)
    cur = lambda n: _unflatten(n, sizes)
    prev = lambda n: _unflatten(jnp.maximum(n - 1, 0), sizes)

    def qmap(n):
        b, p, qi = cur(n)
        return (b * nq + qi, p)

    def kmap(n):
        b, p, qi = cur(n)
        return (b, p)

    def vmap(n):
        b, p, qi = prev(n)
        return (b, p)

    def omap(n):
        b, p, qi = prev(n)
        return (b * nq + qi, p)

    return pl.pallas_call(
        _mla_kernel, grid=(math.prod(sizes) + 1,),
        in_specs=[pl.BlockSpec((TQ, 2 * HEAD_SLOT), qmap), pl.BlockSpec((seq, 2 * HEAD_SLOT), kmap),
                  pl.BlockSpec((seq, 2 * HEAD_SLOT), vmap)],
        out_specs=pl.BlockSpec((TQ, HEAD_SLOT), omap),
        out_shape=jax.ShapeDtypeStruct((t, MLA_WIDTH), BF16),
        scratch_shapes=_two_stage_scratch(seq),
        compiler_params=_cparams(1),
    )(qm, km, vm)


def _pool_kernel(u_ref, w_ref, scale_ref, o_ref):
    u = u_ref[...]
    seq, width = u.shape
    zpad = jnp.zeros((POOL_PAD, width), F32)
    ue = jnp.concatenate([zpad, u, zpad], axis=0)
    n = seq + 2 * POOL_PAD

    def down(a, k):
        return pltpu.roll(a, k, axis=0)

    def up(a, k):
        return pltpu.roll(a, n - k, axis=0)

    a2 = ue + down(ue, 1)
    a4 = down(a2, 1) + up(a2, 1)
    a8 = down(a4, 2) + up(a4, 2)
    a16 = down(a8, 4) + up(a8, 4)
    core = slice(POOL_PAD, POOL_PAD + seq)
    lane = lax.broadcasted_iota(jnp.int32, (seq, width), 1)
    tpos = lax.broadcasted_iota(jnp.int32, (seq, width), 0)
    grp = lane // POOL_GROUP_DIM
    win_sum = jnp.where(grp == 0, a2[core], jnp.where(grp == 1, a4[core], jnp.where(grp == 2, a8[core], a16[core])))
    half = jnp.where(grp == 0, 1, jnp.where(grp == 1, 2, jnp.where(grp == 2, 4, 8)))
    lo_i = jnp.maximum(tpos - half, 0)
    hi_i = jnp.minimum(tpos + half - 1, seq - 1)
    cnt = (hi_i - lo_i + 1).astype(F32)
    pooled = win_sum / cnt - u
    mixed = jnp.dot(pooled.astype(BF16), w_ref[...], preferred_element_type=F32)
    o_ref[...] = (mixed * scale_ref[...]).astype(BF16)


def _pool_call(pu, w_bd, scale, batch, seq):
    t = pu.shape[0]
    return pl.pallas_call(
        _pool_kernel, grid=(batch,),
        in_specs=[pl.BlockSpec((seq, POOL_WIDTH), lambda b: (b, 0)),
                  _full((POOL_WIDTH, POOL_WIDTH)), _full((1, POOL_WIDTH))],
        out_specs=pl.BlockSpec((seq, POOL_WIDTH), lambda b: (b, 0)),
        out_shape=jax.ShapeDtypeStruct((t, POOL_WIDTH), BF16),
        compiler_params=_cparams(1),
    )(pu, w_bd, scale)


def _outproj_kernel(x_ref, od_ref, om_ref, op_ref, wo_ref, g2_ref, wrh_ref, wrl_ref, br_ref,
                    xo_ref, h2_ref, ri_ref, rg_ref, cnt_ref, carry_ref):
    i = pl.program_id(0)

    @pl.when(i == 0)
    def _():
        carry_ref[...] = jnp.zeros_like(carry_ref)

    xn = (x_ref[...]
          + jnp.dot(od_ref[...], wo_ref[0:DIFF_WIDTH, :], preferred_element_type=F32)
          + jnp.dot(om_ref[...], wo_ref[DIFF_WIDTH:DIFF_WIDTH + MLA_WIDTH, :], preferred_element_type=F32)
          + jnp.dot(op_ref[...], wo_ref[DIFF_WIDTH + MLA_WIDTH:, :], preferred_element_type=F32))
    xo_ref[...] = xn
    h2 = xn * lax.rsqrt(jnp.mean(xn * xn, axis=-1, keepdims=True) + RMS_EPS) * g2_ref[...]
    n_chunk = D_MODEL // LANES
    for j in range(n_chunk):
        h2_ref[pl.ds(j, xn.shape[0], stride=n_chunk), :] = h2[:, j * LANES:(j + 1) * LANES]

    h_hi = h2.astype(BF16)
    h_lo = (h2 - h_hi.astype(F32)).astype(BF16)
    logits = (jnp.dot(h_hi, wrh_ref[...], preferred_element_type=F32)
              + jnp.dot(h_lo, wrh_ref[...], preferred_element_type=F32)
              + jnp.dot(h_hi, wrl_ref[...], preferred_element_type=F32)
              + br_ref[...])
    tm = logits.shape[0]
    lane = lax.broadcasted_iota(jnp.int32, (tm, LANES), 1)
    lane_f = lane.astype(F32)
    neg = jnp.float32(-jnp.inf)
    big = jnp.float32(1e9)

    gmask = lane < N_GROUPS
    gl = jnp.where(gmask, logits, neg)
    gmax = jnp.max(gl, axis=-1, keepdims=True)
    gsum = jnp.sum(jnp.where(gmask, jnp.exp(gl - gmax), 0.0), axis=-1, keepdims=True)
    g_top = 1.0 / gsum
    g_idx = jnp.min(jnp.where(gl == gmax, lane_f, big), axis=-1, keepdims=True)

    e_lo = N_GROUPS + EXPERTS_PER_GROUP * g_idx
    emask = (lane_f >= e_lo) & (lane_f < e_lo + EXPERTS_PER_GROUP)
    el = jnp.where(emask, logits, neg)
    emax = jnp.max(el, axis=-1, keepdims=True)
    eexp = jnp.where(emask, jnp.exp(el - emax), 0.0)
    prob = eexp / jnp.sum(eexp, axis=-1, keepdims=True)
    pm = jnp.where(emask, prob, -1.0)
    p1 = jnp.max(pm, axis=-1, keepdims=True)
    i1 = jnp.min(jnp.where(pm == p1, lane_f, big), axis=-1, keepdims=True)
    pm2 = jnp.where(lane_f == i1, -1.0, pm)
    p2 = jnp.max(pm2, axis=-1, keepdims=True)
    i2 = jnp.min(jnp.where(pm2 == p2, lane_f, big), axis=-1, keepdims=True)
    denom = p1 + p2
    gate1 = g_top * p1 / denom
    gate2 = g_top * p2 / denom

    sel1 = lane_f == i1
    sel2 = lane_f == i2
    onehot = jnp.where(sel1 | sel2, 1.0, 0.0)
    rr = lax.broadcasted_iota(jnp.int32, (tm, tm), 0)
    cc = lax.broadcasted_iota(jnp.int32, (tm, tm), 1)
    ltri = jnp.where(cc < rr, 1.0, 0.0).astype(BF16)
    prefix = jnp.dot(ltri, onehot.astype(BF16), preferred_element_type=F32) + carry_ref[...]
    rank1 = jnp.sum(jnp.where(sel1, prefix, 0.0), axis=-1, keepdims=True)
    rank2 = jnp.sum(jnp.where(sel2, prefix, 0.0), axis=-1, keepdims=True)
    carry_ref[...] = carry_ref[...] + jnp.sum(onehot, axis=0, keepdims=True)
    cnt_ref[...] = carry_ref[...]

    info = jnp.where(lane == 0, i1 - N_GROUPS,
                     jnp.where(lane == 1, i2 - N_GROUPS,
                               jnp.where(lane == 2, rank1, jnp.where(lane == 3, rank2, 0.0))))
    ri_ref[...] = info.astype(jnp.int32)
    rg_ref[...] = jnp.where(lane == 0, gate1, jnp.where(lane == 1, gate2, 0.0))


def _outproj_call(x2, od, om, op, p):
    t = x2.shape[0]
    tm = TM_OUT
    row = lambda i: (i, 0)
    return pl.pallas_call(
        _outproj_kernel, grid=(t // tm,),
        in_specs=[pl.BlockSpec((tm, D_MODEL), row), pl.BlockSpec((tm, DIFF_WIDTH), row),
                  pl.BlockSpec((tm, MLA_WIDTH), row), pl.BlockSpec((tm, POOL_WIDTH), row),
                  _full((D_MODEL, D_MODEL)), _full((1, D_MODEL)),
                  _full((D_MODEL, LANES)), _full((D_MODEL, LANES)), _full((1, LANES))],
        out_specs=[pl.BlockSpec((tm, D_MODEL), row),
                   pl.BlockSpec((tm * D_MODEL // LANES, LANES), row),
                   pl.BlockSpec((tm, LANES), row), pl.BlockSpec((tm, LANES), row),
                   _full((1, LANES))],
        out_shape=[jax.ShapeDtypeStruct((t, D_MODEL), F32),
                   jax.ShapeDtypeStruct((t * D_MODEL // LANES, LANES), F32),
                   jax.ShapeDtypeStruct((t, LANES), jnp.int32), jax.ShapeDtypeStruct((t, LANES), F32),
                   jax.ShapeDtypeStruct((1, LANES), F32)],
        scratch_shapes=[pltpu.VMEM((1, LANES), F32)],
        compiler_params=_cparams(1),
    )(x2, od, om, op, p["wo"], p["g2"], p["wrh"], p["wrl"], p["br"])


def _expert_kernel(be_ref, idx0_ref, idxn_ref, h_hbm, wg_ref, wu_ref, wd_ref, ys_ref,
                   buf0, buf1, sem):
    i = pl.program_id(0)
    n = pl.num_programs(0)
    n_chunk = D_MODEL // LANES

    def issue(idx_ref, buf, sem_slot, rows):
        for r in rows:
            src = pl.multiple_of(idx_ref[0, 0, r] * n_chunk, n_chunk)
            pltpu.make_async_copy(h_hbm.at[pl.ds(src, n_chunk), :],
                                  buf.at[pl.ds(r * n_chunk, n_chunk), :], sem_slot).start(priority=r % 2)

    def wait_block(buf, sem_slot):
        pltpu.make_async_copy(h_hbm.at[pl.ds(0, ROUTE_BLOCK * n_chunk), :], buf, sem_slot).wait()

    @pl.when(i == 0)
    def _():
        issue(idx0_ref, buf0, sem.at[0], range(ROUTE_BLOCK))

    def step(buf, sem_cur, buf_next, sem_next):
        issue(idxn_ref, buf_next, sem_next, range(ROUTE_BLOCK))
        wait_block(buf, sem_cur)
        xb = jnp.concatenate([buf[pl.ds(j, ROUTE_BLOCK, stride=n_chunk), :] for j in range(n_chunk)],
                             axis=1).astype(BF16)
        g = jnp.dot(xb, wg_ref[0, 0].astype(BF16), preferred_element_type=F32)
        u = jnp.dot(xb, wu_ref[0, 0].astype(BF16), preferred_element_type=F32)
        hmid = g * (1.0 / (1.0 + jnp.exp(-g))) * u
        y = jnp.dot(hmid.astype(BF16), wd_ref[0, 0].astype(BF16), preferred_element_type=F32)
        for j in range(n_chunk):
            ys_ref[pl.ds(j, ROUTE_BLOCK, stride=n_chunk), :] = y[:, j * LANES:(j + 1) * LANES]

    @pl.when(i % 2 == 0)
    def _():
        step(buf0, sem.at[0], buf1, sem.at[1])

    @pl.when(i % 2 == 1)
    def _():
        step(buf1, sem.at[1], buf0, sem.at[0])

    @pl.when(i == n - 1)
    def _():
        @pl.when(i % 2 == 0)
        def _():
            wait_block(buf1, sem.at[1])

        @pl.when(i % 2 == 1)
        def _():
            wait_block(buf0, sem.at[0])


def _expert_call(block_eid, slot_tok3, h3, wg, wu, wd, layer):
    n_blocks = slot_tok3.shape[0]
    wmap = lambda i, be: (layer, be[i], 0, 0)
    tile = (ROUTE_BLOCK * D_MODEL // LANES, LANES)
    grid_spec = pltpu.PrefetchScalarGridSpec(
        num_scalar_prefetch=1, grid=(n_blocks,),
        in_specs=[
            pl.BlockSpec((1, 1, ROUTE_BLOCK), lambda i, be: (0, 0, 0), memory_space=pltpu.SMEM),
            pl.BlockSpec((1, 1, ROUTE_BLOCK), lambda i, be: (jnp.minimum(i + 1, n_blocks - 1), 0, 0),
                         memory_space=pltpu.SMEM),
            pl.BlockSpec(memory_space=pl.ANY),
            pl.BlockSpec((1, 1, D_MODEL, D_FF), wmap),
            pl.BlockSpec((1, 1, D_MODEL, D_FF), wmap),
            pl.BlockSpec((1, 1, D_FF, D_MODEL), wmap),
        ],
        out_specs=pl.BlockSpec(tile, lambda i, be: (i, 0)),
        scratch_shapes=[pltpu.VMEM(tile, F32), pltpu.VMEM(tile, F32), pltpu.SemaphoreType.DMA((2,))])
    return pl.pallas_call(
        _expert_kernel, grid_spec=grid_spec,
        out_shape=jax.ShapeDtypeStruct((n_blocks * tile[0], LANES), F32),
        compiler_params=_cparams(1),
    )(block_eid, slot_tok3, slot_tok3, h3, wg, wu, wd)


def _combine_kernel(idx0_ref, idxn_ref, ys_hbm, x_ref, rg_ref, o_ref, buf0, buf1, sem):
    i = pl.program_id(0)
    n = pl.num_programs(0)
    tm = x_ref.shape[0]
    n_chunk = D_MODEL // LANES

    def issue(idx_ref, buf, sem_slot, rows):
        for r in rows:
            for kk in range(2):
                src = pl.multiple_of(idx_ref[0, 0, 2 * r + kk] * n_chunk, n_chunk)
                pltpu.make_async_copy(ys_hbm.at[pl.ds(src, n_chunk), :],
                                      buf.at[pl.ds((kk * tm + r) * n_chunk, n_chunk), :],
                                      sem_slot).start(priority=kk)

    def wait_tile(buf, sem_slot):
        pltpu.make_async_copy(ys_hbm.at[pl.ds(0, 2 * tm * n_chunk), :], buf, sem_slot).wait()

    @pl.when(i == 0)
    def _():
        issue(idx0_ref, buf0, sem.at[0], range(tm))

    def step(buf, sem_cur, buf_next, sem_next):
        issue(idxn_ref, buf_next, sem_next, range(tm))
        wait_tile(buf, sem_cur)
        rg = rg_ref[...]
        g0 = rg[:, 0:1]
        g1 = rg[:, 1:2]
        for j in range(n_chunk):
            cols = slice(j * LANES, (j + 1) * LANES)
            y0 = buf[pl.ds(j, tm, stride=n_chunk), :]
            y1 = buf[pl.ds(tm * n_chunk + j, tm, stride=n_chunk), :]
            o_ref[:, cols] = x_ref[:, cols] + g0 * y0 + g1 * y1

    @pl.when(i % 2 == 0)
    def _():
        step(buf0, sem.at[0], buf1, sem.at[1])

    @pl.when(i % 2 == 1)
    def _():
        step(buf1, sem.at[1], buf0, sem.at[0])

    @pl.when(i == n - 1)
    def _():
        @pl.when(i % 2 == 0)
        def _():
            wait_tile(buf1, sem.at[1])

        @pl.when(i % 2 == 1)
        def _():
            wait_tile(buf0, sem.at[0])


def _combine_call(dest3, ys, x2, rg):
    t = x2.shape[0]
    tm = TM_COMB
    n = t // tm
    row = lambda i: (i, 0)
    tile = (2 * tm * D_MODEL // LANES, LANES)
    return pl.pallas_call(
        _combine_kernel, grid=(n,),
        in_specs=[
            pl.BlockSpec((1, 1, 2 * tm), lambda i: (0, 0, 0), memory_space=pltpu.SMEM),
            pl.BlockSpec((1, 1, 2 * tm), lambda i: (jnp.minimum(i + 1, n - 1), 0, 0), memory_space=pltpu.SMEM),
            pl.BlockSpec(memory_space=pl.ANY),
            pl.BlockSpec((tm, D_MODEL), row), pl.BlockSpec((tm, LANES), row)],
        out_specs=pl.BlockSpec((tm, D_MODEL), row),
        out_shape=jax.ShapeDtypeStruct((t, D_MODEL), F32),
        scratch_shapes=[pltpu.VMEM(tile, F32), pltpu.VMEM(tile, F32), pltpu.SemaphoreType.DMA((2,))],
        compiler_params=_cparams(1),
    )(dest3, dest3, ys, x2, rg)


def _swap_halves(a):
    half = a.shape[-1] // 2
    return jnp.concatenate([a[..., half:], a[..., :half]], axis=-1)


def _layer_params(l, seq, w):
    p = {}
    row = lambda v: v.reshape(1, -1).astype(F32)
    w_in = w["w_in"][l]
    kr_cols = w_in[:, 1856:1888]
    p["win"] = jnp.concatenate(
        [w_in[:, 0:1536], w_in[:, 1888:2144], w_in[:, 1728:1856], w_in[:, 1536:1728],
         kr_cols, _swap_halves(kr_cols)], axis=1).astype(BF16)
    p["g1"] = row(w["norm1_g"][l])
    p["gq"] = row(jnp.tile(w["diff_q_norm_g"][l], 2) * (DIFF_QK ** -0.5))
    p["gk"] = row(jnp.tile(w["diff_k_norm_g"][l], 2))
    p["gckv"] = row(w["mla_kv_lat_norm_g"][l])
    gcq = w["mla_q_lat_norm_g"][l]
    p["gcqa"] = row(gcq[:LANES])
    p["gcqb"] = row(jnp.concatenate([gcq[LANES:], jnp.zeros((2 * LANES - MLA_Q_RANK,), F32)]))

    wuq = w["mla_w_uq"][l].reshape(MLA_Q_RANK, MLA_HEADS, MLA_NOPE + MLA_ROPE)
    rope_w = wuq[:, :, MLA_NOPE:]
    wuq = jnp.concatenate([wuq[:, :, :MLA_NOPE], rope_w, _swap_halves(rope_w)], axis=-1)
    wuq = wuq.reshape(MLA_Q_RANK, MLA_HEADS * HEAD_SLOT)
    wuq = jnp.concatenate([wuq, jnp.zeros((2 * LANES - MLA_Q_RANK, wuq.shape[1]), F32)], axis=0).astype(BF16)
    p["wuqa"] = wuq[:LANES]
    p["wuqb"] = wuq[LANES:]

    wukv = w["mla_w_ukv"][l].reshape(MLA_KV_RANK, MLA_HEADS, MLA_NOPE + MLA_V)
    zk = jnp.zeros((MLA_KV_RANK, MLA_HEADS, HEAD_SLOT - MLA_NOPE), F32)
    p["wkk"] = jnp.concatenate([wukv[:, :, :MLA_NOPE], zk], axis=-1).reshape(MLA_KV_RANK, -1).astype(BF16)
    vcols = wukv[:, :, MLA_NOPE:]
    zv = jnp.zeros_like(vcols)
    even = (jnp.arange(MLA_HEADS) % 2 == 0)[None, :, None]
    wkv = jnp.concatenate([jnp.where(even, vcols, zv), jnp.where(even, zv, vcols)], axis=-1)
    p["wkv"] = wkv.reshape(MLA_KV_RANK, -1).astype(BF16)
    p["gkn"] = row(jnp.concatenate([w["mla_k_nope_norm_g"][l], jnp.zeros((HEAD_SLOT - MLA_NOPE,), F32)]))

    inv = 1.0 / (ROPE_BASE ** (jnp.arange(0, MLA_ROPE, 2, dtype=F32) / MLA_ROPE))
    ang = jnp.arange(seq, dtype=F32)[:, None] * inv[None, :]
    cosf = jnp.concatenate([jnp.cos(ang), jnp.cos(ang)], axis=-1)
    sinf = jnp.concatenate([-jnp.sin(ang), jnp.sin(ang)], axis=-1)
    scale = (MLA_NOPE + MLA_ROPE) ** -0.5 * math.log2(math.e)
    gqr = w["mla_q_rope_norm_g"][l]
    q_head = jnp.concatenate([jnp.broadcast_to(w["mla_q_nope_norm_g"][l][None, :], (seq, MLA_NOPE)),
                              gqr[None, :] * cosf, _swap_halves(gqr)[None, :] * sinf], axis=-1) * scale
    p["qtab"] = jnp.tile(q_head, (1, MLA_HEADS))
    gkr = w["mla_k_rope_norm_g"][l]
    p["ktab"] = jnp.concatenate([jnp.zeros((seq, MLA_NOPE), F32), gkr[None, :] * cosf,
                                 _swap_halves(gkr)[None, :] * sinf], axis=-1)
    src = jnp.arange(LANES)
    dst = jnp.arange(MLA_HEADS * HEAD_SLOT)
    src_j = jnp.where(src >= MLA_NOPE, (src - MLA_NOPE) % MLA_ROPE, -1)
    dst_l = dst % HEAD_SLOT
    dst_j = jnp.where(dst_l >= MLA_NOPE, (dst_l - MLA_NOPE) % MLA_ROPE, -2)
    p["eplace"] = (src_j[:, None] == dst_j[None, :]).astype(BF16)

    pw = w["pool_w"][l]
    bd = jnp.zeros((POOL_WIDTH, POOL_WIDTH), F32)
    for g in range(POOL_GROUPS):
        s0 = g * POOL_GROUP_DIM
        bd = bd.at[s0:s0 + POOL_GROUP_DIM, s0:s0 + POOL_GROUP_DIM].set(pw[g])
    p["pool_w"] = bd.astype(BF16)
    p["pool_scale"] = row(w["pool_scale"][l])

    lam_init = 0.8 - 0.6 * math.exp(-0.3 * l)
    lv = w["diff_lambda"][l].astype(F32)
    p["lam"] = (jnp.exp(jnp.sum(lv[0] * lv[1])) - jnp.exp(jnp.sum(lv[2] * lv[3])) + lam_init).reshape(1)
    p["gsub"] = row(w["diff_sub_norm_g"][l] * (1.0 - lam_init))

    p["wo"] = w["w_out"][l].astype(BF16)
    p["g2"] = row(w["norm2_g"][l])
    wr = jnp.concatenate([w["router_group_w"][l], w["router_expert_w"][l],
                          jnp.zeros((D_MODEL, LANES - N_GROUPS - N_EXPERTS), F32)], axis=1)
    wr_hi = wr.astype(BF16)
    p["wrh"] = wr_hi
    p["wrl"] = (wr - wr_hi.astype(F32)).astype(BF16)
    p["br"] = row(jnp.concatenate([w["router_group_b"][l], w["router_expert_b"][l],
                                   jnp.zeros((LANES - N_GROUPS - N_EXPERTS,), F32)]))
    return p


def kernel(x, norm1_g, w_in, diff_q_norm_g, diff_k_norm_g, diff_lambda, diff_sub_norm_g, mla_q_lat_norm_g, mla_kv_lat_norm_g, mla_w_uq, mla_w_ukv, mla_q_nope_norm_g, mla_q_rope_norm_g, mla_k_nope_norm_g, mla_k_rope_norm_g, pool_w, pool_scale, w_out, norm2_g, router_group_w, router_group_b, router_expert_w, router_expert_b, expert_w_gate, expert_w_up, expert_w_down):
    w = dict(norm1_g=norm1_g, w_in=w_in, diff_q_norm_g=diff_q_norm_g, diff_k_norm_g=diff_k_norm_g,
             diff_lambda=diff_lambda, diff_sub_norm_g=diff_sub_norm_g, mla_q_lat_norm_g=mla_q_lat_norm_g,
             mla_kv_lat_norm_g=mla_kv_lat_norm_g, mla_w_uq=mla_w_uq, mla_w_ukv=mla_w_ukv,
             mla_q_nope_norm_g=mla_q_nope_norm_g, mla_q_rope_norm_g=mla_q_rope_norm_g,
             mla_k_nope_norm_g=mla_k_nope_norm_g, mla_k_rope_norm_g=mla_k_rope_norm_g,
             pool_w=pool_w, pool_scale=pool_scale, w_out=w_out, norm2_g=norm2_g,
             router_group_w=router_group_w, router_group_b=router_group_b,
             router_expert_w=router_expert_w, router_expert_b=router_expert_b)
    batch, seq, d = x.shape
    t = batch * seq
    n_assign = 2 * t
    n_blocks = n_assign // ROUTE_BLOCK + N_EXPERTS
    alibi = _alibi_tables(seq)
    token_of_assign = jnp.repeat(jnp.arange(t, dtype=jnp.int32), 2)

    x2 = x.reshape(t, d)
    for l in range(DEPTH):
        p = _layer_params(l, seq, w)
        dq1, dq2, dk, dv, qm, km, vm, pu = _proj_call(x2, p, seq)
        o_diff = _diff_call(p["lam"], dq1, dq2, dk, dv, p["gsub"], alibi, batch, seq)
        o_mla = _mla_call(qm, km, vm, batch, seq)
        o_pool = _pool_call(pu, p["pool_w"], p["pool_scale"], batch, seq)
        x2, h2, route_i, route_g, counts = _outproj_call(x2, o_diff, o_mla, o_pool, p)

        cnt = counts[0, N_GROUPS:N_GROUPS + N_EXPERTS].astype(jnp.int32)
        padded = (cnt + ROUTE_BLOCK - 1) // ROUTE_BLOCK * ROUTE_BLOCK
        padded_ends = jnp.cumsum(padded)
        padded_starts = padded_ends - padded
        dest = padded_starts[route_i[:, 0:2]] + route_i[:, 2:4]
        block_start = jnp.arange(n_blocks, dtype=jnp.int32) * ROUTE_BLOCK
        block_eid = jnp.minimum(jnp.sum(block_start[:, None] >= padded_ends[None, :], axis=1),
                                N_EXPERTS - 1).astype(jnp.int32)
        n_used = (padded_ends[-1] // ROUTE_BLOCK).astype(jnp.int32).reshape(1)
        slot_tok = jnp.zeros((n_blocks * ROUTE_BLOCK,), jnp.int32).at[dest.reshape(-1)].set(token_of_assign)

        ys = _expert_call(block_eid, slot_tok.reshape(n_blocks, 1, ROUTE_BLOCK), h2,
                          expert_w_gate, expert_w_up, expert_w_down, l)
        x2 = _combine_call(dest.reshape(t // TM_COMB, 1, 2 * TM_COMB), ys, x2, route_g)
    return x2.reshape(batch, seq, d)
```

```python
import functools
import math

import jax
import jax.numpy as jnp
from jax import lax
from jax.experimental import pallas as pl
from jax.experimental.pallas import tpu as pltpu

F32 = jnp.float32
BF16 = jnp.bfloat16

D_MODEL = 1024
DEPTH = 2
DIFF_HEADS = 4
DIFF_QK = 64
DIFF_V = 128
DIFF_WIDTH = 512
MLA_HEADS = 4
MLA_NOPE = 64
MLA_ROPE = 32
MLA_V = 64
MLA_Q_RANK = 192
MLA_KV_RANK = 128
MLA_WIDTH = 256
ROPE_BASE = 10000.0
POOL_WIDTH = 256
POOL_GROUPS = 4
POOL_GROUP_DIM = 64
POOL_WINDOWS = (2, 4, 8, 16)
N_GROUPS = 4
EXPERTS_PER_GROUP = 8
N_EXPERTS = 32
D_FF = 256
ROUTE_BLOCK = 256
RMS_EPS = 1e-6

LANES = 128
HEAD_SLOT = 128
PROJ_WIDTH = 2176
POOL_PAD = 16
VMEM_LIMIT = 48 * 1024 * 1024

TM_PROJ = 512
TQ = 256
TM_OUT = 256
TM_COMB = 256

NT_DIMS = (((1,), (1,)), ((), ()))


def _cparams(n_axes):
    return pltpu.CompilerParams(dimension_semantics=("arbitrary",) * n_axes,
                                vmem_limit_bytes=VMEM_LIMIT)


def _full(shape):
    return pl.BlockSpec(shape, lambda *_: (0,) * len(shape))


def _proj_kernel(x_ref, g1_ref, win_ref, gq_ref, gk_ref, gckv_ref, gcqa_ref, gcqb_ref,
                 wuqa_ref, wuqb_ref, wkk_ref, wkv_ref, gkn_ref, qtab_ref, ktab_ref, eplace_ref,
                 dq1_ref, dq2_ref, dk_ref, dv_ref, qm_ref, km_ref, vm_ref, pu_ref):
    x = x_ref[...]
    xn = x * lax.rsqrt(jnp.mean(x * x, axis=-1, keepdims=True) + RMS_EPS) * g1_ref[...]
    proj = jnp.dot(xn.astype(BF16), win_ref[...], preferred_element_type=F32)

    tm = x.shape[0]
    lane = lax.broadcasted_iota(jnp.int32, (tm, LANES), 1)
    lo = lane < DIFF_QK

    def half_norm(c, g_row):
        sq = c * c
        s_lo = jnp.sum(jnp.where(lo, sq, 0.0), axis=-1, keepdims=True)
        s_hi = jnp.sum(jnp.where(lo, 0.0, sq), axis=-1, keepdims=True)
        r = jnp.where(lo, lax.rsqrt(s_lo / DIFF_QK + RMS_EPS), lax.rsqrt(s_hi / DIFF_QK + RMS_EPS))
        return c * r * g_row

    for h in range(DIFF_HEADS):
        sl = slice(h * HEAD_SLOT, (h + 1) * HEAD_SLOT)
        qn = half_norm(proj[:, sl], gq_ref[...])
        dq1_ref[:, sl] = jnp.where(lo, qn, 0.0).astype(BF16)
        dq2_ref[:, sl] = jnp.where(lo, 0.0, qn).astype(BF16)
        ksl = slice(512 + h * HEAD_SLOT, 512 + (h + 1) * HEAD_SLOT)
        dk_ref[:, sl] = half_norm(proj[:, ksl], gk_ref[...]).astype(BF16)
    dv_ref[...] = proj[:, 1024:1536].astype(BF16)
    pu_ref[...] = proj[:, 1536:1792]

    ckv = proj[:, 1792:1920]
    ckvn = ckv * lax.rsqrt(jnp.mean(ckv * ckv, axis=-1, keepdims=True) + RMS_EPS) * gckv_ref[...]
    ckvn = ckvn.astype(BF16)
    cqa = proj[:, 1920:2048]
    last = proj[:, 2048:2176]
    lsq = last * last
    ss_q = (jnp.sum(cqa * cqa, axis=-1, keepdims=True)
            + jnp.sum(jnp.where(lo, lsq, 0.0), axis=-1, keepdims=True))
    r_q = lax.rsqrt(ss_q / MLA_Q_RANK + RMS_EPS)
    q_raw = (jnp.dot((cqa * r_q * gcqa_ref[...]).astype(BF16), wuqa_ref[...], preferred_element_type=F32)
             + jnp.dot((last * r_q * gcqb_ref[...]).astype(BF16), wuqb_ref[...], preferred_element_type=F32))

    rope_lanes = (lane >= MLA_NOPE) & (lane < MLA_NOPE + MLA_ROPE)
    ss_kr = jnp.sum(jnp.where(rope_lanes, lsq, 0.0), axis=-1, keepdims=True)
    kr_terms = last * lax.rsqrt(ss_kr / MLA_ROPE + RMS_EPS) * ktab_ref[...]
    kr_placed = jnp.dot(kr_terms.astype(BF16), eplace_ref[...], preferred_element_type=F32)

    k_raw = jnp.dot(ckvn, wkk_ref[...], preferred_element_type=F32)
    vm_ref[...] = jnp.dot(ckvn, wkv_ref[...], preferred_element_type=F32).astype(BF16)
    qtab = qtab_ref[...]
    for h in range(MLA_HEADS):
        sl = slice(h * HEAD_SLOT, (h + 1) * HEAD_SLOT)
        c = q_raw[:, sl]
        sq = c * c
        s_n = jnp.sum(jnp.where(lo, sq, 0.0), axis=-1, keepdims=True)
        s_r = jnp.sum(jnp.where(rope_lanes, sq, 0.0), axis=-1, keepdims=True)
        r = jnp.where(lo, lax.rsqrt(s_n / MLA_NOPE + RMS_EPS), lax.rsqrt(s_r / MLA_ROPE + RMS_EPS))
        qm_ref[:, sl] = (c * r * qtab[:, sl]).astype(BF16)
        kc = k_raw[:, sl]
        r_k = lax.rsqrt(jnp.sum(kc * kc, axis=-1, keepdims=True) / MLA_NOPE + RMS_EPS)
        km_ref[:, sl] = (kc * r_k * gkn_ref[...] + kr_placed[:, sl]).astype(BF16)


def _proj_call(x2, p, seq):
    t = x2.shape[0]
    tm = TM_PROJ
    n_pos = seq // tm
    row = lambda i: (i, 0)
    pos = lambda i: (i % n_pos, 0)
    bf = lambda w: jax.ShapeDtypeStruct((t, w), BF16)
    in_specs = [
        pl.BlockSpec((tm, D_MODEL), row),
        _full((1, D_MODEL)), _full((D_MODEL, PROJ_WIDTH)),
        _full((1, LANES)), _full((1, LANES)), _full((1, LANES)), _full((1, LANES)), _full((1, LANES)),
        _full((LANES, 512)), _full((LANES, 512)), _full((LANES, 512)), _full((LANES, 512)),
        _full((1, LANES)),
        pl.BlockSpec((tm, 512), pos), pl.BlockSpec((tm, LANES), pos),
        _full((LANES, 512)),
    ]
    out_specs = [pl.BlockSpec((tm, 512), row)] * 7 + [pl.BlockSpec((tm, POOL_WIDTH), row)]
    out_shape = [bf(512)] * 7 + [jax.ShapeDtypeStruct((t, POOL_WIDTH), F32)]
    return pl.pallas_call(
        _proj_kernel, grid=(t // tm,), in_specs=in_specs, out_specs=out_specs, out_shape=out_shape,
        compiler_params=_cparams(1),
    )(x2, p["g1"], p["win"], p["gq"], p["gk"], p["gckv"], p["gcqa"], p["gcqb"],
      p["wuqa"], p["wuqb"], p["wkk"], p["wkv"], p["gkn"], p["qtab"], p["ktab"], p["eplace"])


def _unflatten(n, sizes):
    n = jnp.minimum(n, math.prod(sizes) - 1)
    coords = []
    for size in reversed(sizes):
        coords.append(n % size)
        n = n // size
    return tuple(reversed(coords))


def _two_stage(n, stage, bufs):
    (s0, m0), (s1, m1) = bufs

    @pl.when(n == 0)
    def _():
        s1[...] = jnp.zeros_like(s1)
        m1[...] = jnp.zeros_like(m1)

    @pl.when(n % 2 == 0)
    def _():
        stage((s0, m0), (s1, m1))

    @pl.when(n % 2 == 1)
    def _():
        stage((s1, m1), (s0, m0))


def _two_stage_scratch(seq):
    pair = [pltpu.VMEM((2 * TQ, seq), F32), pltpu.VMEM((2 * TQ, LANES), F32)]
    return pair + pair


def _softmax_pv_tile(s_prev, m_rows, c, tq, lsum, acc, v_tiles, exp_fn):
    n_half = tq // LANES
    ps = [exp_fn(s_prev[:, (c * n_half + j) * LANES:(c * n_half + j + 1) * LANES] - m_rows)
          for j in range(n_half)]
    for ch in ps:
        lsum = ch if lsum is None else lsum + ch
    pb = jnp.concatenate(ps, axis=1).astype(BF16)
    for g, vt in enumerate(v_tiles):
        pv = jnp.dot(pb[g * tq:(g + 1) * tq], vt, preferred_element_type=F32)
        acc[g] = pv if acc[g] is None else acc[g] + pv
    return lsum


def _running_max(mx, sc):
    for j in range(sc.shape[1] // LANES):
        chunk = sc[:, j * LANES:(j + 1) * LANES]
        mx = chunk if mx is None else jnp.maximum(mx, chunk)
    return mx


def _diff_kernel(lam_ref, q1_ref, q2_ref, qx_ref, k_ref, kx_ref, bd_ref, v_ref, gsub_ref, o_ref,
                 s0_ref, m0_ref, s1_ref, m1_ref, *, sizes):
    n = pl.program_id(0)
    tq = q1_ref.shape[0]
    n_kt = k_ref.shape[0] // tq
    qi_cur = _unflatten(n, sizes)[2]
    qi_prev = _unflatten(jnp.maximum(n - 1, 0), sizes)[2]

    def stage(cur, prev):
        s_cur, m_cur = cur
        s_prev, m_prev = prev
        q1, q2 = q1_ref[...], q2_ref[...]
        qx_left = qx_ref[0, 0]
        qx_right = -qx_left
        bd = bd_ref[0]
        bd2 = jnp.concatenate([bd, bd], axis=0)
        mx = None
        m_rows = m_prev[...]
        lsum = None
        acc = [None, None]
        for c in range(n_kt):
            start = pl.multiple_of(((qi_prev + c) % n_kt) * tq, tq)
            vt = v_ref[pl.ds(start, tq), :]
            lsum = _softmax_pv_tile(s_prev, m_rows, c, tq, lsum, acc, [vt, vt], jnp.exp)

            tile = (qi_cur + c) % n_kt
            start = pl.multiple_of(tile * tq, tq)
            if c == 0:
                qx = jnp.zeros_like(qx_left)
            else:
                qx = jnp.where(qi_cur + c >= n_kt, qx_left, qx_right)
            qq = jnp.concatenate([jnp.concatenate([q1, qx], axis=1),
                                  jnp.concatenate([q2, qx], axis=1)], axis=0)
            kk = jnp.concatenate([k_ref[pl.ds(start, tq), :], kx_ref[pl.ds(start, tq), :]], axis=1)
            sc = lax.dot_general(qq, kk, NT_DIMS, preferred_element_type=F32)
            if c == 0:
                sc = sc + bd2
            s_cur[:, c * tq:(c + 1) * tq] = sc
            mx = _running_max(mx, sc)
        m_cur[...] = jnp.broadcast_to(jnp.max(mx, axis=-1, keepdims=True), m_cur.shape)
        l = jnp.sum(lsum, axis=-1, keepdims=True)
        o = acc[0] * (1.0 / l[0:tq]) - acc[1] * (lam_ref[0] / l[tq:2 * tq])
        r = lax.rsqrt(jnp.mean(o * o, axis=-1, keepdims=True) + RMS_EPS)
        o_ref[...] = (o * r * gsub_ref[...]).astype(BF16)

    _two_stage(n, stage, ((s0_ref, m0_ref), (s1_ref, m1_ref)))


def _alibi_tables(seq):
    nq = seq // TQ
    slopes = 2.0 ** (-8.0 * jnp.arange(1, DIFF_HEADS + 1, dtype=F32) / DIFF_HEADS)
    pos = jnp.arange(seq, dtype=jnp.int32)
    hi = (pos // 256).astype(F32)
    lo = (pos % 256).astype(F32)
    s4 = slopes[:, None]
    ones = jnp.ones((DIFF_HEADS, seq), F32)
    q_left = jnp.stack([-s4 * 256.0 * hi[None], -s4 * lo[None], s4 * 256.0 * ones, s4 * ones], axis=-1)
    qx = jnp.concatenate([q_left, jnp.zeros((DIFF_HEADS, seq, HEAD_SLOT - 4), F32)], axis=-1)
    qx = qx.reshape(DIFF_HEADS, nq, TQ, HEAD_SLOT).astype(BF16)
    k_cols = jnp.stack([jnp.ones((seq,), F32), jnp.ones((seq,), F32), hi, lo], axis=-1)
    kx = jnp.concatenate([k_cols, jnp.zeros((seq, HEAD_SLOT - 4), F32)], axis=-1).astype(BF16)
    loc = jnp.arange(TQ, dtype=jnp.int32)
    bd = -slopes[:, None, None] * jnp.abs(loc[:, None] - loc[None, :]).astype(F32)[None]
    return qx, kx, bd


def _diff_call(lam, dq1, dq2, dk, dv, gsub, tabs, batch, seq):
    t = dq1.shape[0]
    nq = seq // TQ
    qx, kx, bd = tabs
    sizes = (batch, DIFF_HEADS, nq)
    cur = lambda n: _unflatten(n, sizes)
    prev = lambda n: _unflatten(jnp.maximum(n - 1, 0), sizes)

    def qmap(n, *_):
        b, h, qi = cur(n)
        return (b * nq + qi, h)

    def kmap(n, *_):
        b, h, qi = cur(n)
        return (b, h)

    def vmap(n, *_):
        b, h, qi = prev(n)
        return (b, h)

    def omap(n, *_):
        b, h, qi = prev(n)
        return (b * nq + qi, h)

    grid_spec = pltpu.PrefetchScalarGridSpec(
        num_scalar_prefetch=1, grid=(math.prod(sizes) + 1,),
        in_specs=[pl.BlockSpec((TQ, HEAD_SLOT), qmap), pl.BlockSpec((TQ, HEAD_SLOT), qmap),
                  pl.BlockSpec((1, 1, TQ, HEAD_SLOT), lambda n, *_: cur(n)[1:] + (0, 0)),
                  pl.BlockSpec((seq, HEAD_SLOT), kmap),
                  pl.BlockSpec((seq, HEAD_SLOT), lambda *_: (0, 0)),
                  pl.BlockSpec((1, TQ, TQ), lambda n, *_: (cur(n)[1], 0, 0)),
                  pl.BlockSpec((seq, HEAD_SLOT), vmap),
                  pl.BlockSpec((1, HEAD_SLOT), lambda *_: (0, 0))],
        out_specs=pl.BlockSpec((TQ, HEAD_SLOT), omap),
        scratch_shapes=_two_stage_scratch(seq))
    return pl.pallas_call(
        functools.partial(_diff_kernel, sizes=sizes), grid_spec=grid_spec,
        out_shape=jax.ShapeDtypeStruct((t, DIFF_WIDTH), BF16),
        compiler_params=_cparams(1),
    )(lam, dq1, dq2, qx, dk, kx, bd, dv, gsub)


def _mla_kernel(q_ref, k_ref, v_ref, o_ref, s0_ref, m0_ref, s1_ref, m1_ref):
    n = pl.program_id(0)
    tq = q_ref.shape[0]
    n_kt = k_ref.shape[0] // tq

    def stage(cur, prev):
        s_cur, m_cur = cur
        s_prev, m_prev = prev
        mx = [None, None]
        m_rows = m_prev[...]
        lsum = None
        acc = [None, None]
        for c in range(n_kt):
            rows = slice(c * tq, (c + 1) * tq)
            v_tiles = [v_ref[rows, hh * HEAD_SLOT:(hh + 1) * HEAD_SLOT] for hh in range(2)]
            lsum = _softmax_pv_tile(s_prev, m_rows, c, tq, lsum, acc, v_tiles, jnp.exp2)
            for hh in range(2):
                sl = slice(hh * HEAD_SLOT, (hh + 1) * HEAD_SLOT)
                sc = lax.dot_general(q_ref[:, sl], k_ref[rows, sl], NT_DIMS, preferred_element_type=F32)
                s_cur[hh * tq:(hh + 1) * tq, rows] = sc
                mx[hh] = _running_max(mx[hh], sc)
        mx = jnp.concatenate(mx, axis=0)
        m_cur[...] = jnp.broadcast_to(jnp.max(mx, axis=-1, keepdims=True), m_cur.shape)
        l = jnp.sum(lsum, axis=-1, keepdims=True)
        o = acc[0] * (1.0 / l[0:tq]) + acc[1] * (1.0 / l[tq:2 * tq])
        o_ref[...] = o.astype(BF16)

    _two_stage(n, stage, ((s0_ref, m0_ref), (s1_ref, m1_ref)))


def _mla_call(qm, km, vm, batch, seq):
    t = qm.shape[0]
    nq = seq // TQ
    sizes = (batch, MLA_HEADS // 2, nq)
    cur = lambda n: _unflatten(n, sizes)
    prev = lambda n: _unflatten(jnp.maximum(n - 1, 0), sizes)

    def qmap(n):
        b, p, qi = cur(n)
        return (b * nq + qi, p)

    def kmap(n):
        b, p, qi = cur(n)
        return (b, p)

    def vmap(n):
        b, p, qi = prev(n)
        return (b, p)

    def omap(n):
        b, p, qi = prev(n)
        return (b * nq + qi, p)

    return pl.pallas_call(
        _mla_kernel, grid=(math.prod(sizes) + 1,),
        in_specs=[pl.BlockSpec((TQ, 2 * HEAD_SLOT), qmap), pl.BlockSpec((seq, 2 * HEAD_SLOT), kmap),
                  pl.BlockSpec((seq, 2 * HEAD_SLOT), vmap)],
        out_specs=pl.BlockSpec((TQ, HEAD_SLOT), omap),
        out_shape=jax.ShapeDtypeStruct((t, MLA_WIDTH), BF16),
        scratch_shapes=_two_stage_scratch(seq),
        compiler_params=_cparams(1),
    )(qm, km, vm)


def _pool_kernel(u_ref, w_ref, scale_ref, o_ref):
    u = u_ref[...]
    seq, width = u.shape
    zpad = jnp.zeros((POOL_PAD, width), F32)
    ue = jnp.concatenate([zpad, u, zpad], axis=0)
    n = seq + 2 * POOL_PAD

    def down(a, k):
        return pltpu.roll(a, k, axis=0)

    def up(a, k):
        return pltpu.roll(a, n - k, axis=0)

    a2 = ue + down(ue, 1)
    a4 = down(a2, 1) + up(a2, 1)
    a8 = down(a4, 2) + up(a4, 2)
    a16 = down(a8, 4) + up(a8, 4)
    core = slice(POOL_PAD, POOL_PAD + seq)
    lane = lax.broadcasted_iota(jnp.int32, (seq, width), 1)
    tpos = lax.broadcasted_iota(jnp.int32, (seq, width), 0)
    grp = lane // POOL_GROUP_DIM
    win_sum = jnp.where(grp == 0, a2[core], jnp.where(grp == 1, a4[core], jnp.where(grp == 2, a8[core], a16[core])))
    half = jnp.where(grp == 0, 1, jnp.where(grp == 1, 2, jnp.where(grp == 2, 4, 8)))
    lo_i = jnp.maximum(tpos - half, 0)
    hi_i = jnp.minimum(tpos + half - 1, seq - 1)
    cnt = (hi_i - lo_i + 1).astype(F32)
    pooled = win_sum / cnt - u
    mixed = jnp.dot(pooled.astype(BF16), w_ref[...], preferred_element_type=F32)
    o_ref[...] = (mixed * scale_ref[...]).astype(BF16)


def _pool_call(pu, w_bd, scale, batch, seq):
    t = pu.shape[0]
    return pl.pallas_call(
        _pool_kernel, grid=(batch,),
        in_specs=[pl.BlockSpec((seq, POOL_WIDTH), lambda b: (b, 0)),
                  _full((POOL_WIDTH, POOL_WIDTH)), _full((1, POOL_WIDTH))],
        out_specs=pl.BlockSpec((seq, POOL_WIDTH), lambda b: (b, 0)),
        out_shape=jax.ShapeDtypeStruct((t, POOL_WIDTH), BF16),
        compiler_params=_cparams(1),
    )(pu, w_bd, scale)


def _outproj_kernel(x_ref, od_ref, om_ref, op_ref, wo_ref, g2_ref, wrh_ref, wrl_ref, br_ref,
                    xo_ref, h2_ref, ri_ref, rg_ref, cnt_ref, carry_ref):
    i = pl.program_id(0)

    @pl.when(i == 0)
    def _():
        carry_ref[...] = jnp.zeros_like(carry_ref)

    xn = (x_ref[...]
          + jnp.dot(od_ref[...], wo_ref[0:DIFF_WIDTH, :], preferred_element_type=F32)
          + jnp.dot(om_ref[...], wo_ref[DIFF_WIDTH:DIFF_WIDTH + MLA_WIDTH, :], preferred_element_type=F32)
          + jnp.dot(op_ref[...], wo_ref[DIFF_WIDTH + MLA_WIDTH:, :], preferred_element_type=F32))
    xo_ref[...] = xn
    h2 = xn * lax.rsqrt(jnp.mean(xn * xn, axis=-1, keepdims=True) + RMS_EPS) * g2_ref[...]
    n_chunk = D_MODEL // LANES
    for j in range(n_chunk):
        h2_ref[pl.ds(j, xn.shape[0], stride=n_chunk), :] = h2[:, j * LANES:(j + 1) * LANES]

    h_hi = h2.astype(BF16)
    h_lo = (h2 - h_hi.astype(F32)).astype(BF16)
    logits = (jnp.dot(h_hi, wrh_ref[...], preferred_element_type=F32)
              + jnp.dot(h_lo, wrh_ref[...], preferred_element_type=F32)
              + jnp.dot(h_hi, wrl_ref[...], preferred_element_type=F32)
              + br_ref[...])
    tm = logits.shape[0]
    lane = lax.broadcasted_iota(jnp.int32, (tm, LANES), 1)
    lane_f = lane.astype(F32)
    neg = jnp.float32(-jnp.inf)
    big = jnp.float32(1e9)

    gmask = lane < N_GROUPS
    gl = jnp.where(gmask, logits, neg)
    gmax = jnp.max(gl, axis=-1, keepdims=True)
    gsum = jnp.sum(jnp.where(gmask, jnp.exp(gl - gmax), 0.0), axis=-1, keepdims=True)
    g_top = 1.0 / gsum
    g_idx = jnp.min(jnp.where(gl == gmax, lane_f, big), axis=-1, keepdims=True)

    e_lo = N_GROUPS + EXPERTS_PER_GROUP * g_idx
    emask = (lane_f >= e_lo) & (lane_f < e_lo + EXPERTS_PER_GROUP)
    el = jnp.where(emask, logits, neg)
    emax = jnp.max(el, axis=-1, keepdims=True)
    eexp = jnp.where(emask, jnp.exp(el - emax), 0.0)
    prob = eexp / jnp.sum(eexp, axis=-1, keepdims=True)
    pm = jnp.where(emask, prob, -1.0)
    p1 = jnp.max(pm, axis=-1, keepdims=True)
    i1 = jnp.min(jnp.where(pm == p1, lane_f, big), axis=-1, keepdims=True)
    pm2 = jnp.where(lane_f == i1, -1.0, pm)
    p2 = jnp.max(pm2, axis=-1, keepdims=True)
    i2 = jnp.min(jnp.where(pm2 == p2, lane_f, big), axis=-1, keepdims=True)
    denom = p1 + p2
    gate1 = g_top * p1 / denom
    gate2 = g_top * p2 / denom

    sel1 = lane_f == i1
    sel2 = lane_f == i2
    onehot = jnp.where(sel1 | sel2, 1.0, 0.0)
    rr = lax.broadcasted_iota(jnp.int32, (tm, tm), 0)
    cc = lax.broadcasted_iota(jnp.int32, (tm, tm), 1)
    ltri = jnp.where(cc < rr, 1.0, 0.0).astype(BF16)
    prefix = jnp.dot(ltri, onehot.astype(BF16), preferred_element_type=F32) + carry_ref[...]
    rank1 = jnp.sum(jnp.where(sel1, prefix, 0.0), axis=-1, keepdims=True)
    rank2 = jnp.sum(jnp.where(sel2, prefix, 0.0), axis=-1, keepdims=True)
    carry_ref[...] = carry_ref[...] + jnp.sum(onehot, axis=0, keepdims=True)
    cnt_ref[...] = carry_ref[...]

    info = jnp.where(lane == 0, i1 - N_GROUPS,
                     jnp.where(lane == 1, i2 - N_GROUPS,
                               jnp.where(lane == 2, rank1, jnp.where(lane == 3, rank2, 0.0))))
    ri_ref[...] = info.astype(jnp.int32)
    rg_ref[...] = jnp.where(lane == 0, gate1, jnp.where(lane == 1, gate2, 0.0))


def _outproj_call(x2, od, om, op, p):
    t = x2.shape[0]
    tm = TM_OUT
    row = lambda i: (i, 0)
    return pl.pallas_call(
        _outproj_kernel, grid=(t // tm,),
        in_specs=[pl.BlockSpec((tm, D_MODEL), row), pl.BlockSpec((tm, DIFF_WIDTH), row),
                  pl.BlockSpec((tm, MLA_WIDTH), row), pl.BlockSpec((tm, POOL_WIDTH), row),
                  _full((D_MODEL, D_MODEL)), _full((1, D_MODEL)),
                  _full((D_MODEL, LANES)), _full((D_MODEL, LANES)), _full((1, LANES))],
        out_specs=[pl.BlockSpec((tm, D_MODEL), row),
                   pl.BlockSpec((tm * D_MODEL // LANES, LANES), row),
                   pl.BlockSpec((tm, LANES), row), pl.BlockSpec((tm, LANES), row),
                   _full((1, LANES))],
        out_shape=[jax.ShapeDtypeStruct((t, D_MODEL), F32),
                   jax.ShapeDtypeStruct((t * D_MODEL // LANES, LANES), F32),
                   jax.ShapeDtypeStruct((t, LANES), jnp.int32), jax.ShapeDtypeStruct((t, LANES), F32),
                   jax.ShapeDtypeStruct((1, LANES), F32)],
        scratch_shapes=[pltpu.VMEM((1, LANES), F32)],
        compiler_params=_cparams(1),
    )(x2, od, om, op, p["wo"], p["g2"], p["wrh"], p["wrl"], p["br"])


PAD_CHUNKS = (128, 64, 32, 16, 8, 4, 2, 1)


def _dispatch_kernel(pad_ref, idx_ref, h_ref, xs_hbm, zero_ref, sem, zsem):
    i = pl.program_id(0)
    n_chunk = D_MODEL // LANES
    tm = h_ref.shape[0] // n_chunk

    @pl.when(i == 0)
    def _():
        zero_ref[...] = jnp.zeros_like(zero_ref)

        def pad_copies(e, wait):
            off = pad_ref[0, e]
            cnt = pad_ref[1, e]
            for size in PAD_CHUNKS:
                take = cnt & size

                @pl.when(take != 0)
                def _():
                    cp = pltpu.make_async_copy(
                        zero_ref.at[pl.ds(0, size * n_chunk), :],
                        xs_hbm.at[pl.ds(pl.multiple_of(off * n_chunk, n_chunk), size * n_chunk), :], zsem)
                    if wait:
                        cp.wait()
                    else:
                        cp.start()
                off = off + take

        def start_body(e, c):
            pad_copies(e, False)
            return c

        def wait_body(e, c):
            pad_copies(e, True)
            return c

        lax.fori_loop(0, N_EXPERTS, start_body, 0)
        lax.fori_loop(0, N_EXPERTS, wait_body, 0)

        tail = pad_ref[0, N_EXPERTS]
        zrows = zero_ref.shape[0]

        def tail_copy(c):
            return pltpu.make_async_copy(
                zero_ref, xs_hbm.at[pl.ds(pl.multiple_of(tail * n_chunk + c * zrows, zrows), zrows), :], zsem)

        def tail_start(c, carry):
            tail_copy(c).start()
            return carry

        def tail_wait(c, carry):
            tail_copy(c).wait()
            return carry

        lax.fori_loop(0, pad_ref[1, N_EXPERTS], tail_start, 0)
        lax.fori_loop(0, pad_ref[1, N_EXPERTS], tail_wait, 0)

    for r in range(tm):
        for kk in range(2):
            dst = pl.multiple_of(idx_ref[0, 0, 2 * r + kk] * n_chunk, n_chunk)
            pltpu.make_async_copy(h_ref.at[pl.ds(r * n_chunk, n_chunk), :],
                                  xs_hbm.at[pl.ds(dst, n_chunk), :], sem).start(priority=kk)
    for kk in range(2):
        pltpu.make_async_copy(h_ref, xs_hbm.at[pl.ds(0, tm * n_chunk), :], sem).wait()


def _dispatch_call(pads, dest3, h3, n_slots):
    n_tiles = dest3.shape[0]
    tm = dest3.shape[2] // 2
    n_chunk = D_MODEL // LANES
    grid_spec = pltpu.PrefetchScalarGridSpec(
        num_scalar_prefetch=1, grid=(n_tiles,),
        in_specs=[pl.BlockSpec((1, 1, 2 * tm), lambda i, pads: (i, 0, 0), memory_space=pltpu.SMEM),
                  pl.BlockSpec((tm * n_chunk, LANES), lambda i, pads: (i, 0))],
        out_specs=pl.BlockSpec(memory_space=pl.ANY),
        scratch_shapes=[pltpu.VMEM((PAD_CHUNKS[0] * n_chunk, LANES), F32),
                        pltpu.SemaphoreType.DMA, pltpu.SemaphoreType.DMA])
    return pl.pallas_call(
        _dispatch_kernel, grid_spec=grid_spec,
        out_shape=jax.ShapeDtypeStruct((n_slots * n_chunk, LANES), F32),
        compiler_params=_cparams(1),
    )(pads, dest3, h3)


def _expert_kernel(be_ref, nu_ref, xs_ref, wg_ref, wu_ref, wd_ref, ys_ref):
    i = pl.program_id(0)
    n_chunk = D_MODEL // LANES

    @pl.when(i < nu_ref[0])
    def _():
        xb = jnp.concatenate([xs_ref[pl.ds(j, ROUTE_BLOCK, stride=n_chunk), :] for j in range(n_chunk)],
                             axis=1).astype(BF16)
        g = jnp.dot(xb, wg_ref[0, 0].astype(BF16), preferred_element_type=F32)
        u = jnp.dot(xb, wu_ref[0, 0].astype(BF16), preferred_element_type=F32)
        hmid = g * (1.0 / (1.0 + jnp.exp(-g))) * u
        y = jnp.dot(hmid.astype(BF16), wd_ref[0, 0].astype(BF16), preferred_element_type=F32)
        for j in range(n_chunk):
            ys_ref[pl.ds(j, ROUTE_BLOCK, stride=n_chunk), :] = y[:, j * LANES:(j + 1) * LANES]

    @pl.when(i >= nu_ref[0])
    def _():
        ys_ref[...] = jnp.zeros_like(ys_ref)


def _expert_call(block_eid, n_used, xs3, wg, wu, wd, layer):
    tile = (ROUTE_BLOCK * D_MODEL // LANES, LANES)
    n_blocks = xs3.shape[0] // tile[0]
    wmap = lambda i, be, nu: (layer, be[i], 0, 0)
    row = lambda i, be, nu: (i, 0)
    grid_spec = pltpu.PrefetchScalarGridSpec(
        num_scalar_prefetch=2, grid=(n_blocks,),
        in_specs=[
            pl.BlockSpec(tile, row),
            pl.BlockSpec((1, 1, D_MODEL, D_FF), wmap),
            pl.BlockSpec((1, 1, D_MODEL, D_FF), wmap),
            pl.BlockSpec((1, 1, D_FF, D_MODEL), wmap),
        ],
        out_specs=pl.BlockSpec(tile, row))
    return pl.pallas_call(
        _expert_kernel, grid_spec=grid_spec,
        out_shape=jax.ShapeDtypeStruct(xs3.shape, F32),
        compiler_params=_cparams(1),
    )(block_eid, n_used, xs3, wg, wu, wd)


def _combine_kernel(idx0_ref, idxn_ref, ys_hbm, x_ref, rg_ref, o_ref, buf0, buf1, sem):
    i = pl.program_id(0)
    n = pl.num_programs(0)
    tm = x_ref.shape[0]
    n_chunk = D_MODEL // LANES

    def issue(idx_ref, buf, sem_slot, rows):
        for r in rows:
            for kk in range(2):
                src = pl.multiple_of(idx_ref[0, 0, 2 * r + kk] * n_chunk, n_chunk)
                pltpu.make_async_copy(ys_hbm.at[pl.ds(src, n_chunk), :],
                                      buf.at[pl.ds((kk * tm + r) * n_chunk, n_chunk), :],
                                      sem_slot).start(priority=kk)

    def wait_tile(buf, sem_slot):
        pltpu.make_async_copy(ys_hbm.at[pl.ds(0, 2 * tm * n_chunk), :], buf, sem_slot).wait()

    @pl.when(i == 0)
    def _():
        issue(idx0_ref, buf0, sem.at[0], range(tm))

    def step(buf, sem_cur, buf_next, sem_next):
        issue(idxn_ref, buf_next, sem_next, range(tm))
        wait_tile(buf, sem_cur)
        rg = rg_ref[...]
        g0 = rg[:, 0:1]
        g1 = rg[:, 1:2]
        for j in range(n_chunk):
            cols = slice(j * LANES, (j + 1) * LANES)
            y0 = buf[pl.ds(j, tm, stride=n_chunk), :]
            y1 = buf[pl.ds(tm * n_chunk + j, tm, stride=n_chunk), :]
            o_ref[:, cols] = x_ref[:, cols] + g0 * y0 + g1 * y1

    @pl.when(i % 2 == 0)
    def _():
        step(buf0, sem.at[0], buf1, sem.at[1])

    @pl.when(i % 2 == 1)
    def _():
        step(buf1, sem.at[1], buf0, sem.at[0])

    @pl.when(i == n - 1)
    def _():
        @pl.when(i % 2 == 0)
        def _():
            wait_tile(buf1, sem.at[1])

        @pl.when(i % 2 == 1)
        def _():
            wait_tile(buf0, sem.at[0])


def _combine_call(dest3, ys, x2, rg):
    t = x2.shape[0]
    tm = TM_COMB
    n = t // tm
    row = lambda i: (i, 0)
    tile = (2 * tm * D_MODEL // LANES, LANES)
    return pl.pallas_call(
        _combine_kernel, grid=(n,),
        in_specs=[
            pl.BlockSpec((1, 1, 2 * tm), lambda i: (0, 0, 0), memory_space=pltpu.SMEM),
            pl.BlockSpec((1, 1, 2 * tm), lambda i: (jnp.minimum(i + 1, n - 1), 0, 0), memory_space=pltpu.SMEM),
            pl.BlockSpec(memory_space=pl.ANY),
            pl.BlockSpec((tm, D_MODEL), row), pl.BlockSpec((tm, LANES), row)],
        out_specs=pl.BlockSpec((tm, D_MODEL), row),
        out_shape=jax.ShapeDtypeStruct((t, D_MODEL), F32),
        scratch_shapes=[pltpu.VMEM(tile, F32), pltpu.VMEM(tile, F32), pltpu.SemaphoreType.DMA((2,))],
        compiler_params=_cparams(1),
    )(dest3, dest3, ys, x2, rg)


def _swap_halves(a):
    half = a.shape[-1] // 2
    return jnp.concatenate([a[..., half:], a[..., :half]], axis=-1)


def _layer_params(l, seq, w):
    p = {}
    row = lambda v: v.reshape(1, -1).astype(F32)
    w_in = w["w_in"][l]
    kr_cols = w_in[:, 1856:1888]
    p["win"] = jnp.concatenate(
        [w_in[:, 0:1536], w_in[:, 1888:2144], w_in[:, 1728:1856], w_in[:, 1536:1728],
         kr_cols, _swap_halves(kr_cols)], axis=1).astype(BF16)
    p["g1"] = row(w["norm1_g"][l])
    p["gq"] = row(jnp.tile(w["diff_q_norm_g"][l], 2) * (DIFF_QK ** -0.5))
    p["gk"] = row(jnp.tile(w["diff_k_norm_g"][l], 2))
    p["gckv"] = row(w["mla_kv_lat_norm_g"][l])
    gcq = w["mla_q_lat_norm_g"][l]
    p["gcqa"] = row(gcq[:LANES])
    p["gcqb"] = row(jnp.concatenate([gcq[LANES:], jnp.zeros((2 * LANES - MLA_Q_RANK,), F32)]))

    wuq = w["mla_w_uq"][l].reshape(MLA_Q_RANK, MLA_HEADS, MLA_NOPE + MLA_ROPE)
    rope_w = wuq[:, :, MLA_NOPE:]
    wuq = jnp.concatenate([wuq[:, :, :MLA_NOPE], rope_w, _swap_halves(rope_w)], axis=-1)
    wuq = wuq.reshape(MLA_Q_RANK, MLA_HEADS * HEAD_SLOT)
    wuq = jnp.concatenate([wuq, jnp.zeros((2 * LANES - MLA_Q_RANK, wuq.shape[1]), F32)], axis=0).astype(BF16)
    p["wuqa"] = wuq[:LANES]
    p["wuqb"] = wuq[LANES:]

    wukv = w["mla_w_ukv"][l].reshape(MLA_KV_RANK, MLA_HEADS, MLA_NOPE + MLA_V)
    zk = jnp.zeros((MLA_KV_RANK, MLA_HEADS, HEAD_SLOT - MLA_NOPE), F32)
    p["wkk"] = jnp.concatenate([wukv[:, :, :MLA_NOPE], zk], axis=-1).reshape(MLA_KV_RANK, -1).astype(BF16)
    vcols = wukv[:, :, MLA_NOPE:]
    zv = jnp.zeros_like(vcols)
    even = (jnp.arange(MLA_HEADS) % 2 == 0)[None, :, None]
    wkv = jnp.concatenate([jnp.where(even, vcols, zv), jnp.where(even, zv, vcols)], axis=-1)
    p["wkv"] = wkv.reshape(MLA_KV_RANK, -1).astype(BF16)
    p["gkn"] = row(jnp.concatenate([w["mla_k_nope_norm_g"][l], jnp.zeros((HEAD_SLOT - MLA_NOPE,), F32)]))

    inv = 1.0 / (ROPE_BASE ** (jnp.arange(0, MLA_ROPE, 2, dtype=F32) / MLA_ROPE))
    ang = jnp.arange(seq, dtype=F32)[:, None] * inv[None, :]
    cosf = jnp.concatenate([jnp.cos(ang), jnp.cos(ang)], axis=-1)
    sinf = jnp.concatenate([-jnp.sin(ang), jnp.sin(ang)], axis=-1)
    scale = (MLA_NOPE + MLA_ROPE) ** -0.5 * math.log2(math.e)
    gqr = w["mla_q_rope_norm_g"][l]
    q_head = jnp.concatenate([jnp.broadcast_to(w["mla_q_nope_norm_g"][l][None, :], (seq, MLA_NOPE)),
                              gqr[None, :] * cosf, _swap_halves(gqr)[None, :] * sinf], axis=-1) * scale
    p["qtab"] = jnp.tile(q_head, (1, MLA_HEADS))
    gkr = w["mla_k_rope_norm_g"][l]
    p["ktab"] = jnp.concatenate([jnp.zeros((seq, MLA_NOPE), F32), gkr[None, :] * cosf,
                                 _swap_halves(gkr)[None, :] * sinf], axis=-1)
    src = jnp.arange(LANES)
    dst = jnp.arange(MLA_HEADS * HEAD_SLOT)
    src_j = jnp.where(src >= MLA_NOPE, (src - MLA_NOPE) % MLA_ROPE, -1)
    dst_l = dst % HEAD_SLOT
    dst_j = jnp.where(dst_l >= MLA_NOPE, (dst_l - MLA_NOPE) % MLA_ROPE, -2)
    p["eplace"] = (src_j[:, None] == dst_j[None, :]).astype(BF16)

    pw = w["pool_w"][l]
    bd = jnp.zeros((POOL_WIDTH, POOL_WIDTH), F32)
    for g in range(POOL_GROUPS):
        s0 = g * POOL_GROUP_DIM
        bd = bd.at[s0:s0 + POOL_GROUP_DIM, s0:s0 + POOL_GROUP_DIM].set(pw[g])
    p["pool_w"] = bd.astype(BF16)
    p["pool_scale"] = row(w["pool_scale"][l])

    lam_init = 0.8 - 0.6 * math.exp(-0.3 * l)
    lv = w["diff_lambda"][l].astype(F32)
    p["lam"] = (jnp.exp(jnp.sum(lv[0] * lv[1])) - jnp.exp(jnp.sum(lv[2] * lv[3])) + lam_init).reshape(1)
    p["gsub"] = row(w["diff_sub_norm_g"][l] * (1.0 - lam_init))

    p["wo"] = w["w_out"][l].astype(BF16)
    p["g2"] = row(w["norm2_g"][l])
    wr = jnp.concatenate([w["router_group_w"][l], w["router_expert_w"][l],
                          jnp.zeros((D_MODEL, LANES - N_GROUPS - N_EXPERTS), F32)], axis=1)
    wr_hi = wr.astype(BF16)
    p["wrh"] = wr_hi
    p["wrl"] = (wr - wr_hi.astype(F32)).astype(BF16)
    p["br"] = row(jnp.concatenate([w["router_group_b"][l], w["router_expert_b"][l],
                                   jnp.zeros((LANES - N_GROUPS - N_EXPERTS,), F32)]))
    return p


def kernel(x, norm1_g, w_in, diff_q_norm_g, diff_k_norm_g, diff_lambda, diff_sub_norm_g, mla_q_lat_norm_g, mla_kv_lat_norm_g, mla_w_uq, mla_w_ukv, mla_q_nope_norm_g, mla_q_rope_norm_g, mla_k_nope_norm_g, mla_k_rope_norm_g, pool_w, pool_scale, w_out, norm2_g, router_group_w, router_group_b, router_expert_w, router_expert_b, expert_w_gate, expert_w_up, expert_w_down):
    w = dict(norm1_g=norm1_g, w_in=w_in, diff_q_norm_g=diff_q_norm_g, diff_k_norm_g=diff_k_norm_g,
             diff_lambda=diff_lambda, diff_sub_norm_g=diff_sub_norm_g, mla_q_lat_norm_g=mla_q_lat_norm_g,
             mla_kv_lat_norm_g=mla_kv_lat_norm_g, mla_w_uq=mla_w_uq, mla_w_ukv=mla_w_ukv,
             mla_q_nope_norm_g=mla_q_nope_norm_g, mla_q_rope_norm_g=mla_q_rope_norm_g,
             mla_k_nope_norm_g=mla_k_nope_norm_g, mla_k_rope_norm_g=mla_k_rope_norm_g,
             pool_w=pool_w, pool_scale=pool_scale, w_out=w_out, norm2_g=norm2_g,
             router_group_w=router_group_w, router_group_b=router_group_b,
             router_expert_w=router_expert_w, router_expert_b=router_expert_b)
    batch, seq, d = x.shape
    t = batch * seq
    n_assign = 2 * t
    n_blocks = n_assign // ROUTE_BLOCK + N_EXPERTS
    alibi = _alibi_tables(seq)

    x2 = x.reshape(t, d)
    for l in range(DEPTH):
        p = _layer_params(l, seq, w)
        dq1, dq2, dk, dv, qm, km, vm, pu = _proj_call(x2, p, seq)
        o_diff = _diff_call(p["lam"], dq1, dq2, dk, dv, p["gsub"], alibi, batch, seq)
        o_mla = _mla_call(qm, km, vm, batch, seq)
        o_pool = _pool_call(pu, p["pool_w"], p["pool_scale"], batch, seq)
        x2, h2, route_i, route_g, counts = _outproj_call(x2, o_diff, o_mla, o_pool, p)

        cnt = counts[0, N_GROUPS:N_GROUPS + N_EXPERTS].astype(jnp.int32)
        padded = (cnt + ROUTE_BLOCK - 1) // ROUTE_BLOCK * ROUTE_BLOCK
        padded_ends = jnp.cumsum(padded)
        padded_starts = padded_ends - padded
        eid = route_i[:, 0:2]
        start_of = jnp.sum(jnp.where(eid[..., None] == jnp.arange(N_EXPERTS, dtype=jnp.int32),
                                     padded_starts, 0), axis=-1)
        dest = start_of + route_i[:, 2:4]
        block_start = jnp.arange(n_blocks, dtype=jnp.int32) * ROUTE_BLOCK
        block_eid = jnp.minimum(jnp.sum(block_start[:, None] >= padded_ends[None, :], axis=1),
                                N_EXPERTS - 1).astype(jnp.int32)
        n_used = (padded_ends[-1] // ROUTE_BLOCK).astype(jnp.int32).reshape(1)
        n_tail = (n_blocks - n_used) * (ROUTE_BLOCK // PAD_CHUNKS[0])
        pads = jnp.stack([jnp.concatenate([padded_starts + cnt, padded_ends[-1:]]),
                          jnp.concatenate([padded - cnt, n_tail])]).astype(jnp.int32)
        dest3 = dest.reshape(t // TM_COMB, 1, 2 * TM_COMB)

        xs = _dispatch_call(pads, dest3, h2, n_blocks * ROUTE_BLOCK)
        ys = _expert_call(block_eid, n_used, xs, expert_w_gate, expert_w_up, expert_w_down, l)
        x2 = _combine_call(dest3, ys, x2, route_g)
    return x2.reshape(batch, seq, d)
```

```python
import functools
import math

import jax
import jax.numpy as jnp
from jax import lax
from jax.experimental import pallas as pl
from jax.experimental.pallas import tpu as pltpu

F32 = jnp.float32
BF16 = jnp.bfloat16

D_MODEL = 1024
DEPTH = 2
DIFF_HEADS = 4
DIFF_QK = 64
DIFF_V = 128
DIFF_WIDTH = 512
MLA_HEADS = 4
MLA_NOPE = 64
MLA_ROPE = 32
MLA_V = 64
MLA_Q_RANK = 192
MLA_KV_RANK = 128
MLA_WIDTH = 256
ROPE_BASE = 10000.0
POOL_WIDTH = 256
POOL_GROUPS = 4
POOL_GROUP_DIM = 64
POOL_WINDOWS = (2, 4, 8, 16)
N_GROUPS = 4
EXPERTS_PER_GROUP = 8
N_EXPERTS = 32
D_FF = 256
ROUTE_BLOCK = 256
RMS_EPS = 1e-6

LANES = 128
HEAD_SLOT = 128
PROJ_WIDTH = 2176
POOL_PAD = 16
VMEM_LIMIT = 48 * 1024 * 1024

TM_PROJ = 512
TQ = 256
TM_OUT = 256
TM_COMB = 256

NT_DIMS = (((1,), (1,)), ((), ()))
SOFTMAX_OVERFLOW_GUARD = 2.0 ** 100


def _cparams(n_axes):
    return pltpu.CompilerParams(dimension_semantics=("arbitrary",) * n_axes,
                                vmem_limit_bytes=VMEM_LIMIT)


def _full(shape):
    return pl.BlockSpec(shape, lambda *_: (0,) * len(shape))


def _proj_kernel(x_ref, g1_ref, win_ref, gq_ref, gk_ref, gckv_ref, gcqa_ref, gcqb_ref,
                 wuqa_ref, wuqb_ref, wkk_ref, wkv_ref, gkn_ref, qtab_ref, ktab_ref, eplace_ref,
                 dq1_ref, dq2_ref, dk_ref, dv_ref, qm_ref, km_ref, vm_ref, pu_ref):
    x = x_ref[...]
    xn = x * lax.rsqrt(jnp.mean(x * x, axis=-1, keepdims=True) + RMS_EPS) * g1_ref[...]
    proj = jnp.dot(xn.astype(BF16), win_ref[...], preferred_element_type=F32)

    tm = x.shape[0]
    lane = lax.broadcasted_iota(jnp.int32, (tm, LANES), 1)
    lo = lane < DIFF_QK

    def half_norm(c, g_row):
        sq = c * c
        s_lo = jnp.sum(jnp.where(lo, sq, 0.0), axis=-1, keepdims=True)
        s_hi = jnp.sum(jnp.where(lo, 0.0, sq), axis=-1, keepdims=True)
        r = jnp.where(lo, lax.rsqrt(s_lo / DIFF_QK + RMS_EPS), lax.rsqrt(s_hi / DIFF_QK + RMS_EPS))
        return c * r * g_row

    for h in range(DIFF_HEADS):
        sl = slice(h * HEAD_SLOT, (h + 1) * HEAD_SLOT)
        qn = half_norm(proj[:, sl], gq_ref[...])
        dq1_ref[:, sl] = jnp.where(lo, qn, 0.0).astype(BF16)
        dq2_ref[:, sl] = jnp.where(lo, 0.0, qn).astype(BF16)
        ksl = slice(512 + h * HEAD_SLOT, 512 + (h + 1) * HEAD_SLOT)
        dk_ref[:, sl] = half_norm(proj[:, ksl], gk_ref[...]).astype(BF16)
    dv_ref[...] = proj[:, 1024:1536].astype(BF16)
    pu_ref[...] = proj[:, 1536:1792]

    ckv = proj[:, 1792:1920]
    ckvn = ckv * lax.rsqrt(jnp.mean(ckv * ckv, axis=-1, keepdims=True) + RMS_EPS) * gckv_ref[...]
    ckvn = ckvn.astype(BF16)
    cqa = proj[:, 1920:2048]
    last = proj[:, 2048:2176]
    lsq = last * last
    ss_q = (jnp.sum(cqa * cqa, axis=-1, keepdims=True)
            + jnp.sum(jnp.where(lo, lsq, 0.0), axis=-1, keepdims=True))
    r_q = lax.rsqrt(ss_q / MLA_Q_RANK + RMS_EPS)
    q_raw = (jnp.dot((cqa * r_q * gcqa_ref[...]).astype(BF16), wuqa_ref[...], preferred_element_type=F32)
             + jnp.dot((last * r_q * gcqb_ref[...]).astype(BF16), wuqb_ref[...], preferred_element_type=F32))

    rope_lanes = (lane >= MLA_NOPE) & (lane < MLA_NOPE + MLA_ROPE)
    ss_kr = jnp.sum(jnp.where(rope_lanes, lsq, 0.0), axis=-1, keepdims=True)
    kr_terms = last * lax.rsqrt(ss_kr / MLA_ROPE + RMS_EPS) * ktab_ref[...]
    kr_placed = jnp.dot(kr_terms.astype(BF16), eplace_ref[...], preferred_element_type=F32)

    k_raw = jnp.dot(ckvn, wkk_ref[...], preferred_element_type=F32)
    vm_ref[...] = jnp.dot(ckvn, wkv_ref[...], preferred_element_type=F32).astype(BF16)
    qtab = qtab_ref[...]
    for h in range(MLA_HEADS):
        sl = slice(h * HEAD_SLOT, (h + 1) * HEAD_SLOT)
        c = q_raw[:, sl]
        sq = c * c
        s_n = jnp.sum(jnp.where(lo, sq, 0.0), axis=-1, keepdims=True)
        s_r = jnp.sum(jnp.where(rope_lanes, sq, 0.0), axis=-1, keepdims=True)
        r = jnp.where(lo, lax.rsqrt(s_n / MLA_NOPE + RMS_EPS), lax.rsqrt(s_r / MLA_ROPE + RMS_EPS))
        qm_ref[:, sl] = (c * r * qtab[:, sl]).astype(BF16)
        kc = k_raw[:, sl]
        r_k = lax.rsqrt(jnp.sum(kc * kc, axis=-1, keepdims=True) / MLA_NOPE + RMS_EPS)
        km_ref[:, sl] = (kc * r_k * gkn_ref[...] + kr_placed[:, sl]).astype(BF16)


def _proj_call(x2, p, seq):
    t = x2.shape[0]
    tm = TM_PROJ
    n_pos = seq // tm
    row = lambda i: (i, 0)
    pos = lambda i: (i % n_pos, 0)
    bf = lambda w: jax.ShapeDtypeStruct((t, w), BF16)
    in_specs = [
        pl.BlockSpec((tm, D_MODEL), row),
        _full((1, D_MODEL)), _full((D_MODEL, PROJ_WIDTH)),
        _full((1, LANES)), _full((1, LANES)), _full((1, LANES)), _full((1, LANES)), _full((1, LANES)),
        _full((LANES, 512)), _full((LANES, 512)), _full((LANES, 512)), _full((LANES, 512)),
        _full((1, LANES)),
        pl.BlockSpec((tm, 512), pos), pl.BlockSpec((tm, LANES), pos),
        _full((LANES, 512)),
    ]
    out_specs = [pl.BlockSpec((tm, 512), row)] * 7 + [pl.BlockSpec((tm, POOL_WIDTH), row)]
    out_shape = [bf(512)] * 7 + [jax.ShapeDtypeStruct((t, POOL_WIDTH), F32)]
    return pl.pallas_call(
        _proj_kernel, grid=(t // tm,), in_specs=in_specs, out_specs=out_specs, out_shape=out_shape,
        compiler_params=_cparams(1),
    )(x2, p["g1"], p["win"], p["gq"], p["gk"], p["gckv"], p["gcqa"], p["gcqb"],
      p["wuqa"], p["wuqb"], p["wkk"], p["wkv"], p["gkn"], p["qtab"], p["ktab"], p["eplace"])


def _running_max(mx, sc):
    for j in range(sc.shape[1] // LANES):
        chunk = sc[:, j * LANES:(j + 1) * LANES]
        mx = chunk if mx is None else jnp.maximum(mx, chunk)
    return mx


def _exp_pv_tile(sc, m_rows, tq, lsum, acc, v_tiles, exp_fn):
    ps = [exp_fn(sc[:, j * LANES:(j + 1) * LANES] - m_rows) for j in range(tq // LANES)]
    for ch in ps:
        lsum = ch if lsum is None else lsum + ch
    pb = jnp.concatenate(ps, axis=1).astype(BF16)
    for g, vt in enumerate(v_tiles):
        pv = jnp.dot(pb[g * tq:(g + 1) * tq], vt, preferred_element_type=F32)
        acc[g] = pv if acc[g] is None else acc[g] + pv
    return lsum


def _row_max_lanes(mx):
    return jnp.broadcast_to(jnp.max(mx, axis=-1, keepdims=True), mx.shape)


def _streamed_softmax_pv(score_tile, value_tiles, finish, n_kt, tq, s_ref, exp_fn):
    sc0 = score_tile(0)
    m_rows = _row_max_lanes(_running_max(None, sc0))
    lsum, acc = None, [None, None]
    for c in range(n_kt):
        sc = sc0 if c == 0 else score_tile(c)
        lsum = _exp_pv_tile(sc, m_rows, tq, lsum, acc, value_tiles(c), exp_fn)
    l = jnp.sum(lsum, axis=-1, keepdims=True)
    finish(acc, l)

    @pl.when(jnp.logical_not(jnp.max(l) < SOFTMAX_OVERFLOW_GUARD))
    def _():
        mx = None
        for c in range(n_kt):
            sc = score_tile(c)
            s_ref[:, c * tq:(c + 1) * tq] = sc
            mx = _running_max(mx, sc)
        m_exact = _row_max_lanes(mx)
        lsum, acc = None, [None, None]
        for c in range(n_kt):
            lsum = _exp_pv_tile(s_ref[:, c * tq:(c + 1) * tq], m_exact, tq, lsum, acc, value_tiles(c), exp_fn)
        finish(acc, jnp.sum(lsum, axis=-1, keepdims=True))


def _diff_kernel(lam_ref, q1_ref, q2_ref, qx_ref, k_ref, kx_ref, bd_ref, v_ref, gsub_ref, o_ref, s_ref):
    qi = pl.program_id(2)
    tq = q1_ref.shape[0]
    n_kt = k_ref.shape[0] // tq
    q1, q2 = q1_ref[...], q2_ref[...]
    qx_left = qx_ref[0, 0]
    qx_right = -qx_left
    bd = bd_ref[0]
    bd2 = jnp.concatenate([bd, bd], axis=0)

    def tile_start(c):
        return pl.multiple_of(((qi + c) % n_kt) * tq, tq)

    def score_tile(c):
        start = tile_start(c)
        if c == 0:
            qx = jnp.zeros_like(qx_left)
        else:
            qx = jnp.where(qi + c >= n_kt, qx_left, qx_right)
        qq = jnp.concatenate([jnp.concatenate([q1, qx], axis=1),
                              jnp.concatenate([q2, qx], axis=1)], axis=0)
        kk = jnp.concatenate([k_ref[pl.ds(start, tq), :], kx_ref[pl.ds(start, tq), :]], axis=1)
        sc = lax.dot_general(qq, kk, NT_DIMS, preferred_element_type=F32)
        return sc + bd2 if c == 0 else sc

    def value_tiles(c):
        vt = v_ref[pl.ds(tile_start(c), tq), :]
        return [vt, vt]

    def finish(acc, l):
        o = acc[0] * (1.0 / l[0:tq]) - acc[1] * (lam_ref[0] / l[tq:2 * tq])
        r = lax.rsqrt(jnp.mean(o * o, axis=-1, keepdims=True) + RMS_EPS)
        o_ref[...] = (o * r * gsub_ref[...]).astype(BF16)

    _streamed_softmax_pv(score_tile, value_tiles, finish, n_kt, tq, s_ref, jnp.exp)


def _alibi_tables(seq):
    nq = seq // TQ
    slopes = 2.0 ** (-8.0 * jnp.arange(1, DIFF_HEADS + 1, dtype=F32) / DIFF_HEADS)
    pos = jnp.arange(seq, dtype=jnp.int32)
    hi = (pos // 256).astype(F32)
    lo = (pos % 256).astype(F32)
    s4 = slopes[:, None]
    ones = jnp.ones((DIFF_HEADS, seq), F32)
    q_left = jnp.stack([-s4 * 256.0 * hi[None], -s4 * lo[None], s4 * 256.0 * ones, s4 * ones], axis=-1)
    qx = jnp.concatenate([q_left, jnp.zeros((DIFF_HEADS, seq, HEAD_SLOT - 4), F32)], axis=-1)
    qx = qx.reshape(DIFF_HEADS, nq, TQ, HEAD_SLOT).astype(BF16)
    k_cols = jnp.stack([jnp.ones((seq,), F32), jnp.ones((seq,), F32), hi, lo], axis=-1)
    kx = jnp.concatenate([k_cols, jnp.zeros((seq, HEAD_SLOT - 4), F32)], axis=-1).astype(BF16)
    loc = jnp.arange(TQ, dtype=jnp.int32)
    bd = -slopes[:, None, None] * jnp.abs(loc[:, None] - loc[None, :]).astype(F32)[None]
    return qx, kx, bd


def _diff_call(lam, dq1, dq2, dk, dv, gsub, tabs, batch, seq):
    t = dq1.shape[0]
    nq = seq // TQ
    qx, kx, bd = tabs
    qmap = lambda b, h, qi, *_: (b * nq + qi, h)
    kmap = lambda b, h, qi, *_: (b, h)
    grid_spec = pltpu.PrefetchScalarGridSpec(
        num_scalar_prefetch=1, grid=(batch, DIFF_HEADS, nq),
        in_specs=[pl.BlockSpec((TQ, HEAD_SLOT), qmap), pl.BlockSpec((TQ, HEAD_SLOT), qmap),
                  pl.BlockSpec((1, 1, TQ, HEAD_SLOT), lambda b, h, qi, *_: (h, qi, 0, 0)),
                  pl.BlockSpec((seq, HEAD_SLOT), kmap),
                  pl.BlockSpec((seq, HEAD_SLOT), lambda *_: (0, 0)),
                  pl.BlockSpec((1, TQ, TQ), lambda b, h, qi, *_: (h, 0, 0)),
                  pl.BlockSpec((seq, HEAD_SLOT), kmap),
                  pl.BlockSpec((1, HEAD_SLOT), lambda *_: (0, 0))],
        out_specs=pl.BlockSpec((TQ, HEAD_SLOT), qmap),
        scratch_shapes=[pltpu.VMEM((2 * TQ, seq), F32)])
    return pl.pallas_call(
        _diff_kernel, grid_spec=grid_spec,
        out_shape=jax.ShapeDtypeStruct((t, DIFF_WIDTH), BF16),
        compiler_params=_cparams(3),
    )(lam, dq1, dq2, qx, dk, kx, bd, dv, gsub)


def _mla_kernel(q_ref, k_ref, v_ref, o_ref, s_ref):
    tq = q_ref.shape[0]
    n_kt = k_ref.shape[0] // tq

    def score_tile(c):
        rows = slice(c * tq, (c + 1) * tq)
        return jnp.concatenate(
            [lax.dot_general(q_ref[:, hh * HEAD_SLOT:(hh + 1) * HEAD_SLOT],
                             k_ref[rows, hh * HEAD_SLOT:(hh + 1) * HEAD_SLOT], NT_DIMS,
                             preferred_element_type=F32) for hh in range(2)], axis=0)

    def value_tiles(c):
        return [v_ref[c * tq:(c + 1) * tq, hh * HEAD_SLOT:(hh + 1) * HEAD_SLOT] for hh in range(2)]

    def finish(acc, l):
        o = acc[0] * (1.0 / l[0:tq]) + acc[1] * (1.0 / l[tq:2 * tq])
        o_ref[...] = o.astype(BF16)

    _streamed_softmax_pv(score_tile, value_tiles, finish, n_kt, tq, s_ref, jnp.exp2)


def _mla_call(qm, km, vm, batch, seq):
    t = qm.shape[0]
    nq = seq // TQ
    qmap = lambda b, p, qi: (b * nq + qi, p)
    kmap = lambda b, p, qi: (b, p)
    return pl.pallas_call(
        _mla_kernel, grid=(batch, MLA_HEADS // 2, nq),
        in_specs=[pl.BlockSpec((TQ, 2 * HEAD_SLOT), qmap), pl.BlockSpec((seq, 2 * HEAD_SLOT), kmap),
                  pl.BlockSpec((seq, 2 * HEAD_SLOT), kmap)],
        out_specs=pl.BlockSpec((TQ, HEAD_SLOT), qmap),
        out_shape=jax.ShapeDtypeStruct((t, MLA_WIDTH), BF16),
        scratch_shapes=[pltpu.VMEM((2 * TQ, seq), F32)],
        compiler_params=_cparams(3),
    )(qm, km, vm)


def _pool_kernel(u_ref, w_ref, scale_ref, o_ref):
    u = u_ref[...]
    seq, width = u.shape
    zpad = jnp.zeros((POOL_PAD, width), F32)
    ue = jnp.concatenate([zpad, u, zpad], axis=0)
    n = seq + 2 * POOL_PAD

    def down(a, k):
        return pltpu.roll(a, k, axis=0)

    def up(a, k):
        return pltpu.roll(a, n - k, axis=0)

    a2 = ue + down(ue, 1)
    a4 = down(a2, 1) + up(a2, 1)
    a8 = down(a4, 2) + up(a4, 2)
    a16 = down(a8, 4) + up(a8, 4)
    core = slice(POOL_PAD, POOL_PAD + seq)
    lane = lax.broadcasted_iota(jnp.int32, (seq, width), 1)
    tpos = lax.broadcasted_iota(jnp.int32, (seq, width), 0)
    grp = lane // POOL_GROUP_DIM
    win_sum = jnp.where(grp == 0, a2[core], jnp.where(grp == 1, a4[core], jnp.where(grp == 2, a8[core], a16[core])))
    half = jnp.where(grp == 0, 1, jnp.where(grp == 1, 2, jnp.where(grp == 2, 4, 8)))
    lo_i = jnp.maximum(tpos - half, 0)
    hi_i = jnp.minimum(tpos + half - 1, seq - 1)
    cnt = (hi_i - lo_i + 1).astype(F32)
    pooled = win_sum / cnt - u
    mixed = jnp.dot(pooled.astype(BF16), w_ref[...], preferred_element_type=F32)
    o_ref[...] = (mixed * scale_ref[...]).astype(BF16)


def _pool_call(pu, w_bd, scale, batch, seq):
    t = pu.shape[0]
    return pl.pallas_call(
        _pool_kernel, grid=(batch,),
        in_specs=[pl.BlockSpec((seq, POOL_WIDTH), lambda b: (b, 0)),
                  _full((POOL_WIDTH, POOL_WIDTH)), _full((1, POOL_WIDTH))],
        out_specs=pl.BlockSpec((seq, POOL_WIDTH), lambda b: (b, 0)),
        out_shape=jax.ShapeDtypeStruct((t, POOL_WIDTH), BF16),
        compiler_params=_cparams(1),
    )(pu, w_bd, scale)


def _outproj_kernel(x_ref, od_ref, om_ref, op_ref, wo_ref, g2_ref, wrh_ref, wrl_ref, br_ref,
                    xo_ref, h2_ref, ri_ref, rg_ref, cnt_ref, carry_ref):
    i = pl.program_id(0)

    @pl.when(i == 0)
    def _():
        carry_ref[...] = jnp.zeros_like(carry_ref)

    xn = (x_ref[...]
          + jnp.dot(od_ref[...], wo_ref[0:DIFF_WIDTH, :], preferred_element_type=F32)
          + jnp.dot(om_ref[...], wo_ref[DIFF_WIDTH:DIFF_WIDTH + MLA_WIDTH, :], preferred_element_type=F32)
          + jnp.dot(op_ref[...], wo_ref[DIFF_WIDTH + MLA_WIDTH:, :], preferred_element_type=F32))
    xo_ref[...] = xn
    h2 = xn * lax.rsqrt(jnp.mean(xn * xn, axis=-1, keepdims=True) + RMS_EPS) * g2_ref[...]
    n_chunk = D_MODEL // LANES
    for j in range(n_chunk):
        h2_ref[pl.ds(j, xn.shape[0], stride=n_chunk), :] = h2[:, j * LANES:(j + 1) * LANES]

    h_hi = h2.astype(BF16)
    h_lo = (h2 - h_hi.astype(F32)).astype(BF16)
    logits = (jnp.dot(h_hi, wrh_ref[...], preferred_element_type=F32)
              + jnp.dot(h_lo, wrh_ref[...], preferred_element_type=F32)
              + jnp.dot(h_hi, wrl_ref[...], preferred_element_type=F32)
              + br_ref[...])
    tm = logits.shape[0]
    lane = lax.broadcasted_iota(jnp.int32, (tm, LANES), 1)
    lane_f = lane.astype(F32)
    neg = jnp.float32(-jnp.inf)
    big = jnp.float32(1e9)

    gmask = lane < N_GROUPS
    gl = jnp.where(gmask, logits, neg)
    gmax = jnp.max(gl, axis=-1, keepdims=True)
    gsum = jnp.sum(jnp.where(gmask, jnp.exp(gl - gmax), 0.0), axis=-1, keepdims=True)
    g_top = 1.0 / gsum
    g_idx = jnp.min(jnp.where(gl == gmax, lane_f, big), axis=-1, keepdims=True)

    e_lo = N_GROUPS + EXPERTS_PER_GROUP * g_idx
    emask = (lane_f >= e_lo) & (lane_f < e_lo + EXPERTS_PER_GROUP)
    el = jnp.where(emask, logits, neg)
    emax = jnp.max(el, axis=-1, keepdims=True)
    eexp = jnp.where(emask, jnp.exp(el - emax), 0.0)
    prob = eexp / jnp.sum(eexp, axis=-1, keepdims=True)
    pm = jnp.where(emask, prob, -1.0)
    p1 = jnp.max(pm, axis=-1, keepdims=True)
    i1 = jnp.min(jnp.where(pm == p1, lane_f, big), axis=-1, keepdims=True)
    pm2 = jnp.where(lane_f == i1, -1.0, pm)
    p2 = jnp.max(pm2, axis=-1, keepdims=True)
    i2 = jnp.min(jnp.where(pm2 == p2, lane_f, big), axis=-1, keepdims=True)
    denom = p1 + p2
    gate1 = g_top * p1 / denom
    gate2 = g_top * p2 / denom

    sel1 = lane_f == i1
    sel2 = lane_f == i2
    onehot = jnp.where(sel1 | sel2, 1.0, 0.0)
    rr = lax.broadcasted_iota(jnp.int32, (tm, tm), 0)
    cc = lax.broadcasted_iota(jnp.int32, (tm, tm), 1)
    ltri = jnp.where(cc < rr, 1.0, 0.0).astype(BF16)
    prefix = jnp.dot(ltri, onehot.astype(BF16), preferred_element_type=F32) + carry_ref[...]
    rank1 = jnp.sum(jnp.where(sel1, prefix, 0.0), axis=-1, keepdims=True)
    rank2 = jnp.sum(jnp.where(sel2, prefix, 0.0), axis=-1, keepdims=True)
    carry_ref[...] = carry_ref[...] + jnp.sum(onehot, axis=0, keepdims=True)
    cnt_ref[...] = carry_ref[...]

    info = jnp.where(lane == 0, i1 - N_GROUPS,
                     jnp.where(lane == 1, i2 - N_GROUPS,
                               jnp.where(lane == 2, rank1, jnp.where(lane == 3, rank2, 0.0))))
    ri_ref[...] = info.astype(jnp.int32)
    rg_ref[...] = jnp.where(lane == 0, gate1, jnp.where(lane == 1, gate2, 0.0))


def _outproj_call(x2, od, om, op, p):
    t = x2.shape[0]
    tm = TM_OUT
    row = lambda i: (i, 0)
    return pl.pallas_call(
        _outproj_kernel, grid=(t // tm,),
        in_specs=[pl.BlockSpec((tm, D_MODEL), row), pl.BlockSpec((tm, DIFF_WIDTH), row),
                  pl.BlockSpec((tm, MLA_WIDTH), row), pl.BlockSpec((tm, POOL_WIDTH), row),
                  _full((D_MODEL, D_MODEL)), _full((1, D_MODEL)),
                  _full((D_MODEL, LANES)), _full((D_MODEL, LANES)), _full((1, LANES))],
        out_specs=[pl.BlockSpec((tm, D_MODEL), row),
                   pl.BlockSpec((tm * D_MODEL // LANES, LANES), row),
                   pl.BlockSpec((tm, LANES), row), pl.BlockSpec((tm, LANES), row),
                   _full((1, LANES))],
        out_shape=[jax.ShapeDtypeStruct((t, D_MODEL), F32),
                   jax.ShapeDtypeStruct((t * D_MODEL // LANES, LANES), F32),
                   jax.ShapeDtypeStruct((t, LANES), jnp.int32), jax.ShapeDtypeStruct((t, LANES), F32),
                   jax.ShapeDtypeStruct((1, LANES), F32)],
        scratch_shapes=[pltpu.VMEM((1, LANES), F32)],
        compiler_params=_cparams(1),
    )(x2, od, om, op, p["wo"], p["g2"], p["wrh"], p["wrl"], p["br"])


PAD_CHUNKS = (128, 64, 32, 16, 8, 4, 2, 1)


def _dispatch_kernel(pad_ref, idx_ref, h_ref, xs_hbm, zero_ref, sem, zsem):
    i = pl.program_id(0)
    n_chunk = D_MODEL // LANES
    tm = h_ref.shape[0] // n_chunk

    @pl.when(i == 0)
    def _():
        zero_ref[...] = jnp.zeros_like(zero_ref)

        def pad_copies(e, wait):
            off = pad_ref[0, e]
            cnt = pad_ref[1, e]
            for size in PAD_CHUNKS:
                take = cnt & size

                @pl.when(take != 0)
                def _():
                    cp = pltpu.make_async_copy(
                        zero_ref.at[pl.ds(0, size * n_chunk), :],
                        xs_hbm.at[pl.ds(pl.multiple_of(off * n_chunk, n_chunk), size * n_chunk), :], zsem)
                    if wait:
                        cp.wait()
                    else:
                        cp.start()
                off = off + take

        def start_body(e, c):
            pad_copies(e, False)
            return c

        def wait_body(e, c):
            pad_copies(e, True)
            return c

        lax.fori_loop(0, N_EXPERTS, start_body, 0)
        lax.fori_loop(0, N_EXPERTS, wait_body, 0)

        tail = pad_ref[0, N_EXPERTS]
        zrows = zero_ref.shape[0]

        def tail_copy(c):
            return pltpu.make_async_copy(
                zero_ref, xs_hbm.at[pl.ds(pl.multiple_of(tail * n_chunk + c * zrows, zrows), zrows), :], zsem)

        def tail_start(c, carry):
            tail_copy(c).start()
            return carry

        def tail_wait(c, carry):
            tail_copy(c).wait()
            return carry

        lax.fori_loop(0, pad_ref[1, N_EXPERTS], tail_start, 0)
        lax.fori_loop(0, pad_ref[1, N_EXPERTS], tail_wait, 0)

    for r in range(tm):
        for kk in range(2):
            dst = pl.multiple_of(idx_ref[0, 0, 2 * r + kk] * n_chunk, n_chunk)
            pltpu.make_async_copy(h_ref.at[pl.ds(r * n_chunk, n_chunk), :],
                                  xs_hbm.at[pl.ds(dst, n_chunk), :], sem).start(priority=kk)
    for kk in range(2):
        pltpu.make_async_copy(h_ref, xs_hbm.at[pl.ds(0, tm * n_chunk), :], sem).wait()


def _dispatch_call(pads, dest3, h3, n_slots):
    n_tiles = dest3.shape[0]
    tm = dest3.shape[2] // 2
    n_chunk = D_MODEL // LANES
    grid_spec = pltpu.PrefetchScalarGridSpec(
        num_scalar_prefetch=1, grid=(n_tiles,),
        in_specs=[pl.BlockSpec((1, 1, 2 * tm), lambda i, pads: (i, 0, 0), memory_space=pltpu.SMEM),
                  pl.BlockSpec((tm * n_chunk, LANES), lambda i, pads: (i, 0))],
        out_specs=pl.BlockSpec(memory_space=pl.ANY),
        scratch_shapes=[pltpu.VMEM((PAD_CHUNKS[0] * n_chunk, LANES), F32),
                        pltpu.SemaphoreType.DMA, pltpu.SemaphoreType.DMA])
    return pl.pallas_call(
        _dispatch_kernel, grid_spec=grid_spec,
        out_shape=jax.ShapeDtypeStruct((n_slots * n_chunk, LANES), F32),
        compiler_params=_cparams(1),
    )(pads, dest3, h3)


def _expert_kernel(be_ref, nu_ref, xs_ref, wg_ref, wu_ref, wd_ref, ys_ref):
    i = pl.program_id(0)
    n_chunk = D_MODEL // LANES

    @pl.when(i < nu_ref[0])
    def _():
        xb = jnp.concatenate([xs_ref[pl.ds(j, ROUTE_BLOCK, stride=n_chunk), :] for j in range(n_chunk)],
                             axis=1).astype(BF16)
        g = jnp.dot(xb, wg_ref[0, 0].astype(BF16), preferred_element_type=F32)
        u = jnp.dot(xb, wu_ref[0, 0].astype(BF16), preferred_element_type=F32)
        hmid = g * (1.0 / (1.0 + jnp.exp(-g))) * u
        y = jnp.dot(hmid.astype(BF16), wd_ref[0, 0].astype(BF16), preferred_element_type=F32)
        for j in range(n_chunk):
            ys_ref[pl.ds(j, ROUTE_BLOCK, stride=n_chunk), :] = y[:, j * LANES:(j + 1) * LANES]

    @pl.when(i >= nu_ref[0])
    def _():
        ys_ref[...] = jnp.zeros_like(ys_ref)


def _expert_call(block_eid, n_used, xs3, wg, wu, wd, layer):
    tile = (ROUTE_BLOCK * D_MODEL // LANES, LANES)
    n_blocks = xs3.shape[0] // tile[0]
    wmap = lambda i, be, nu: (layer, be[i], 0, 0)
    row = lambda i, be, nu: (i, 0)
    grid_spec = pltpu.PrefetchScalarGridSpec(
        num_scalar_prefetch=2, grid=(n_blocks,),
        in_specs=[
            pl.BlockSpec(tile, row),
            pl.BlockSpec((1, 1, D_MODEL, D_FF), wmap),
            pl.BlockSpec((1, 1, D_MODEL, D_FF), wmap),
            pl.BlockSpec((1, 1, D_FF, D_MODEL), wmap),
        ],
        out_specs=pl.BlockSpec(tile, row))
    return pl.pallas_call(
        _expert_kernel, grid_spec=grid_spec,
        out_shape=jax.ShapeDtypeStruct(xs3.shape, F32),
        compiler_params=_cparams(1),
    )(block_eid, n_used, xs3, wg, wu, wd)


def _combine_kernel(idx0_ref, idxn_ref, ys_hbm, x_ref, rg_ref, o_ref, buf0, buf1, sem):
    i = pl.program_id(0)
    n = pl.num_programs(0)
    tm = x_ref.shape[0]
    n_chunk = D_MODEL // LANES

    def issue(idx_ref, buf, sem_slot, rows):
        for r in rows:
            for kk in range(2):
                src = pl.multiple_of(idx_ref[0, 0, 2 * r + kk] * n_chunk, n_chunk)
                pltpu.make_async_copy(ys_hbm.at[pl.ds(src, n_chunk), :],
                                      buf.at[pl.ds((kk * tm + r) * n_chunk, n_chunk), :],
                                      sem_slot).start(priority=kk)

    def wait_tile(buf, sem_slot):
        pltpu.make_async_copy(ys_hbm.at[pl.ds(0, 2 * tm * n_chunk), :], buf, sem_slot).wait()

    @pl.when(i == 0)
    def _():
        issue(idx0_ref, buf0, sem.at[0], range(tm))

    def step(buf, sem_cur, buf_next, sem_next):
        issue(idxn_ref, buf_next, sem_next, range(tm))
        wait_tile(buf, sem_cur)
        rg = rg_ref[...]
        g0 = rg[:, 0:1]
        g1 = rg[:, 1:2]
        for j in range(n_chunk):
            cols = slice(j * LANES, (j + 1) * LANES)
            y0 = buf[pl.ds(j, tm, stride=n_chunk), :]
            y1 = buf[pl.ds(tm * n_chunk + j, tm, stride=n_chunk), :]
            o_ref[:, cols] = x_ref[:, cols] + g0 * y0 + g1 * y1

    @pl.when(i % 2 == 0)
    def _():
        step(buf0, sem.at[0], buf1, sem.at[1])

    @pl.when(i % 2 == 1)
    def _():
        step(buf1, sem.at[1], buf0, sem.at[0])

    @pl.when(i == n - 1)
    def _():
        @pl.when(i % 2 == 0)
        def _():
            wait_tile(buf1, sem.at[1])

        @pl.when(i % 2 == 1)
        def _():
            wait_tile(buf0, sem.at[0])


def _combine_call(dest3, ys, x2, rg):
    t = x2.shape[0]
    tm = TM_COMB
    n = t // tm
    row = lambda i: (i, 0)
    tile = (2 * tm * D_MODEL // LANES, LANES)
    return pl.pallas_call(
        _combine_kernel, grid=(n,),
        in_specs=[
            pl.BlockSpec((1, 1, 2 * tm), lambda i: (0, 0, 0), memory_space=pltpu.SMEM),
            pl.BlockSpec((1, 1, 2 * tm), lambda i: (jnp.minimum(i + 1, n - 1), 0, 0), memory_space=pltpu.SMEM),
            pl.BlockSpec(memory_space=pl.ANY),
            pl.BlockSpec((tm, D_MODEL), row), pl.BlockSpec((tm, LANES), row)],
        out_specs=pl.BlockSpec((tm, D_MODEL), row),
        out_shape=jax.ShapeDtypeStruct((t, D_MODEL), F32),
        scratch_shapes=[pltpu.VMEM(tile, F32), pltpu.VMEM(tile, F32), pltpu.SemaphoreType.DMA((2,))],
        compiler_params=_cparams(1),
    )(dest3, dest3, ys, x2, rg)


def _swap_halves(a):
    half = a.shape[-1] // 2
    return jnp.concatenate([a[..., half:], a[..., :half]], axis=-1)


def _layer_params(l, seq, w):
    p = {}
    row = lambda v: v.reshape(1, -1).astype(F32)
    w_in = w["w_in"][l]
    kr_cols = w_in[:, 1856:1888]
    p["win"] = jnp.concatenate(
        [w_in[:, 0:1536], w_in[:, 1888:2144], w_in[:, 1728:1856], w_in[:, 1536:1728],
         kr_cols, _swap_halves(kr_cols)], axis=1).astype(BF16)
    p["g1"] = row(w["norm1_g"][l])
    p["gq"] = row(jnp.tile(w["diff_q_norm_g"][l], 2) * (DIFF_QK ** -0.5))
    p["gk"] = row(jnp.tile(w["diff_k_norm_g"][l], 2))
    p["gckv"] = row(w["mla_kv_lat_norm_g"][l])
    gcq = w["mla_q_lat_norm_g"][l]
    p["gcqa"] = row(gcq[:LANES])
    p["gcqb"] = row(jnp.concatenate([gcq[LANES:], jnp.zeros((2 * LANES - MLA_Q_RANK,), F32)]))

    wuq = w["mla_w_uq"][l].reshape(MLA_Q_RANK, MLA_HEADS, MLA_NOPE + MLA_ROPE)
    rope_w = wuq[:, :, MLA_NOPE:]
    wuq = jnp.concatenate([wuq[:, :, :MLA_NOPE], rope_w, _swap_halves(rope_w)], axis=-1)
    wuq = wuq.reshape(MLA_Q_RANK, MLA_HEADS * HEAD_SLOT)
    wuq = jnp.concatenate([wuq, jnp.zeros((2 * LANES - MLA_Q_RANK, wuq.shape[1]), F32)], axis=0).astype(BF16)
    p["wuqa"] = wuq[:LANES]
    p["wuqb"] = wuq[LANES:]

    wukv = w["mla_w_ukv"][l].reshape(MLA_KV_RANK, MLA_HEADS, MLA_NOPE + MLA_V)
    zk = jnp.zeros((MLA_KV_RANK, MLA_HEADS, HEAD_SLOT - MLA_NOPE), F32)
    p["wkk"] = jnp.concatenate([wukv[:, :, :MLA_NOPE], zk], axis=-1).reshape(MLA_KV_RANK, -1).astype(BF16)
    vcols = wukv[:, :, MLA_NOPE:]
    zv = jnp.zeros_like(vcols)
    even = (jnp.arange(MLA_HEADS) % 2 == 0)[None, :, None]
    wkv = jnp.concatenate([jnp.where(even, vcols, zv), jnp.where(even, zv, vcols)], axis=-1)
    p["wkv"] = wkv.reshape(MLA_KV_RANK, -1).astype(BF16)
    p["gkn"] = row(jnp.concatenate([w["mla_k_nope_norm_g"][l], jnp.zeros((HEAD_SLOT - MLA_NOPE,), F32)]))

    inv = 1.0 / (ROPE_BASE ** (jnp.arange(0, MLA_ROPE, 2, dtype=F32) / MLA_ROPE))
    ang = jnp.arange(seq, dtype=F32)[:, None] * inv[None, :]
    cosf = jnp.concatenate([jnp.cos(ang), jnp.cos(ang)], axis=-1)
    sinf = jnp.concatenate([-jnp.sin(ang), jnp.sin(ang)], axis=-1)
    scale = (MLA_NOPE + MLA_ROPE) ** -0.5 * math.log2(math.e)
    gqr = w["mla_q_rope_norm_g"][l]
    q_head = jnp.concatenate([jnp.broadcast_to(w["mla_q_nope_norm_g"][l][None, :], (seq, MLA_NOPE)),
                              gqr[None, :] * cosf, _swap_halves(gqr)[None, :] * sinf], axis=-1) * scale
    p["qtab"] = jnp.tile(q_head, (1, MLA_HEADS))
    gkr = w["mla_k_rope_norm_g"][l]
    p["ktab"] = jnp.concatenate([jnp.zeros((seq, MLA_NOPE), F32), gkr[None, :] * cosf,
                                 _swap_halves(gkr)[None, :] * sinf], axis=-1)
    src = jnp.arange(LANES)
    dst = jnp.arange(MLA_HEADS * HEAD_SLOT)
    src_j = jnp.where(src >= MLA_NOPE, (src - MLA_NOPE) % MLA_ROPE, -1)
    dst_l = dst % HEAD_SLOT
    dst_j = jnp.where(dst_l >= MLA_NOPE, (dst_l - MLA_NOPE) % MLA_ROPE, -2)
    p["eplace"] = (src_j[:, None] == dst_j[None, :]).astype(BF16)

    pw = w["pool_w"][l]
    bd = jnp.zeros((POOL_WIDTH, POOL_WIDTH), F32)
    for g in range(POOL_GROUPS):
        s0 = g * POOL_GROUP_DIM
        bd = bd.at[s0:s0 + POOL_GROUP_DIM, s0:s0 + POOL_GROUP_DIM].set(pw[g])
    p["pool_w"] = bd.astype(BF16)
    p["pool_scale"] = row(w["pool_scale"][l])

    lam_init = 0.8 - 0.6 * math.exp(-0.3 * l)
    lv = w["diff_lambda"][l].astype(F32)
    p["lam"] = (jnp.exp(jnp.sum(lv[0] * lv[1])) - jnp.exp(jnp.sum(lv[2] * lv[3])) + lam_init).reshape(1)
    p["gsub"] = row(w["diff_sub_norm_g"][l] * (1.0 - lam_init))

    p["wo"] = w["w_out"][l].astype(BF16)
    p["g2"] = row(w["norm2_g"][l])
    wr = jnp.concatenate([w["router_group_w"][l], w["router_expert_w"][l],
                          jnp.zeros((D_MODEL, LANES - N_GROUPS - N_EXPERTS), F32)], axis=1)
    wr_hi = wr.astype(BF16)
    p["wrh"] = wr_hi
    p["wrl"] = (wr - wr_hi.astype(F32)).astype(BF16)
    p["br"] = row(jnp.concatenate([w["router_group_b"][l], w["router_expert_b"][l],
                                   jnp.zeros((LANES - N_GROUPS - N_EXPERTS,), F32)]))
    return p


def kernel(x, norm1_g, w_in, diff_q_norm_g, diff_k_norm_g, diff_lambda, diff_sub_norm_g, mla_q_lat_norm_g, mla_kv_lat_norm_g, mla_w_uq, mla_w_ukv, mla_q_nope_norm_g, mla_q_rope_norm_g, mla_k_nope_norm_g, mla_k_rope_norm_g, pool_w, pool_scale, w_out, norm2_g, router_group_w, router_group_b, router_expert_w, router_expert_b, expert_w_gate, expert_w_up, expert_w_down):
    w = dict(norm1_g=norm1_g, w_in=w_in, diff_q_norm_g=diff_q_norm_g, diff_k_norm_g=diff_k_norm_g,
             diff_lambda=diff_lambda, diff_sub_norm_g=diff_sub_norm_g, mla_q_lat_norm_g=mla_q_lat_norm_g,
             mla_kv_lat_norm_g=mla_kv_lat_norm_g, mla_w_uq=mla_w_uq, mla_w_ukv=mla_w_ukv,
             mla_q_nope_norm_g=mla_q_nope_norm_g, mla_q_rope_norm_g=mla_q_rope_norm_g,
             mla_k_nope_norm_g=mla_k_nope_norm_g, mla_k_rope_norm_g=mla_k_rope_norm_g,
             pool_w=pool_w, pool_scale=pool_scale, w_out=w_out, norm2_g=norm2_g,
             router_group_w=router_group_w, router_group_b=router_group_b,
             router_expert_w=router_expert_w, router_expert_b=router_expert_b)
    batch, seq, d = x.shape
    t = batch * seq
    n_assign = 2 * t
    n_blocks = n_assign // ROUTE_BLOCK + N_EXPERTS
    alibi = _alibi_tables(seq)

    x2 = x.reshape(t, d)
    for l in range(DEPTH):
        p = _layer_params(l, seq, w)
        dq1, dq2, dk, dv, qm, km, vm, pu = _proj_call(x2, p, seq)
        o_diff = _diff_call(p["lam"], dq1, dq2, dk, dv, p["gsub"], alibi, batch, seq)
        o_mla = _mla_call(qm, km, vm, batch, seq)
        o_pool = _pool_call(pu, p["pool_w"], p["pool_scale"], batch, seq)
        x2, h2, route_i, route_g, counts = _outproj_call(x2, o_diff, o_mla, o_pool, p)

        cnt = counts[0, N_GROUPS:N_GROUPS + N_EXPERTS].astype(jnp.int32)
        padded = (cnt + ROUTE_BLOCK - 1) // ROUTE_BLOCK * ROUTE_BLOCK
        padded_ends = jnp.cumsum(padded)
        padded_starts = padded_ends - padded
        eid = route_i[:, 0:2]
        start_of = jnp.sum(jnp.where(eid[..., None] == jnp.arange(N_EXPERTS, dtype=jnp.int32),
                                     padded_starts, 0), axis=-1)
        dest = start_of + route_i[:, 2:4]
        block_start = jnp.arange(n_blocks, dtype=jnp.int32) * ROUTE_BLOCK
        block_eid = jnp.minimum(jnp.sum(block_start[:, None] >= padded_ends[None, :], axis=1),
                                N_EXPERTS - 1).astype(jnp.int32)
        n_used = (padded_ends[-1] // ROUTE_BLOCK).astype(jnp.int32).reshape(1)
        n_tail = (n_blocks - n_used) * (ROUTE_BLOCK // PAD_CHUNKS[0])
        pads = jnp.stack([jnp.concatenate([padded_starts + cnt, padded_ends[-1:]]),
                          jnp.concatenate([padded - cnt, n_tail])]).astype(jnp.int32)
        dest3 = dest.reshape(t // TM_COMB, 1, 2 * TM_COMB)

        xs = _dispatch_call(pads, dest3, h2, n_blocks * ROUTE_BLOCK)
        ys = _expert_call(block_eid, n_used, xs, expert_w_gate, expert_w_up, expert_w_down, l)
        x2 = _combine_call(dest3, ys, x2, route_g)
    return x2.reshape(batch, seq, d)
```

```python
import functools
import math

import jax
import jax.numpy as jnp
from jax import lax
from jax.experimental import pallas as pl
from jax.experimental.pallas import tpu as pltpu

F32 = jnp.float32
BF16 = jnp.bfloat16

D_MODEL = 1024
DEPTH = 2
DIFF_HEADS = 4
DIFF_QK = 64
DIFF_V = 128
DIFF_WIDTH = 512
MLA_HEADS = 4
MLA_NOPE = 64
MLA_ROPE = 32
MLA_V = 64
MLA_Q_RANK = 192
MLA_KV_RANK = 128
MLA_WIDTH = 256
ROPE_BASE = 10000.0
POOL_WIDTH = 256
POOL_GROUPS = 4
POOL_GROUP_DIM = 64
POOL_WINDOWS = (2, 4, 8, 16)
N_GROUPS = 4
EXPERTS_PER_GROUP = 8
N_EXPERTS = 32
D_FF = 256
ROUTE_BLOCK = 256
RMS_EPS = 1e-6

LANES = 128
HEAD_SLOT = 128
PROJ_WIDTH = 2176
POOL_PAD = 16
VMEM_LIMIT = 48 * 1024 * 1024

TM_PROJ = 512
TQ = 256
TM_OUT = 256
TM_COMB = 256

NT_DIMS = (((1,), (1,)), ((), ()))


def _cparams(n_axes):
    return pltpu.CompilerParams(dimension_semantics=("arbitrary",) * n_axes,
                                vmem_limit_bytes=VMEM_LIMIT)


def _full(shape):
    return pl.BlockSpec(shape, lambda *_: (0,) * len(shape))


def _proj_kernel(x_ref, g1_ref, win_ref, gq_ref, gk_ref, gckv_ref, gcqa_ref, gcqb_ref,
                 wuqa_ref, wuqb_ref, wkk_ref, wkv_ref, gkn_ref, qtab_ref, ktab_ref, eplace_ref,
                 dq1_ref, dq2_ref, dk_ref, dv_ref, qm_ref, km_ref, vm_ref, pu_ref):
    x = x_ref[...]
    xn = x * lax.rsqrt(jnp.mean(x * x, axis=-1, keepdims=True) + RMS_EPS) * g1_ref[...]
    proj = jnp.dot(xn.astype(BF16), win_ref[...], preferred_element_type=F32)

    tm = x.shape[0]
    lane = lax.broadcasted_iota(jnp.int32, (tm, LANES), 1)
    lo = lane < DIFF_QK

    def half_norm(c, g_row):
        sq = c * c
        s_lo = jnp.sum(jnp.where(lo, sq, 0.0), axis=-1, keepdims=True)
        s_hi = jnp.sum(jnp.where(lo, 0.0, sq), axis=-1, keepdims=True)
        r = jnp.where(lo, lax.rsqrt(s_lo / DIFF_QK + RMS_EPS), lax.rsqrt(s_hi / DIFF_QK + RMS_EPS))
        return c * r * g_row

    for h in range(DIFF_HEADS):
        sl = slice(h * HEAD_SLOT, (h + 1) * HEAD_SLOT)
        qn = half_norm(proj[:, sl], gq_ref[...])
        dq1_ref[:, sl] = jnp.where(lo, qn, 0.0).astype(BF16)
        dq2_ref[:, sl] = jnp.where(lo, 0.0, qn).astype(BF16)
        ksl = slice(512 + h * HEAD_SLOT, 512 + (h + 1) * HEAD_SLOT)
        dk_ref[:, sl] = half_norm(proj[:, ksl], gk_ref[...]).astype(BF16)
    dv_ref[...] = proj[:, 1024:1536].astype(BF16)
    pu_ref[...] = proj[:, 1536:1792]

    ckv = proj[:, 1792:1920]
    ckvn = ckv * lax.rsqrt(jnp.mean(ckv * ckv, axis=-1, keepdims=True) + RMS_EPS) * gckv_ref[...]
    ckvn = ckvn.astype(BF16)
    cqa = proj[:, 1920:2048]
    last = proj[:, 2048:2176]
    lsq = last * last
    ss_q = (jnp.sum(cqa * cqa, axis=-1, keepdims=True)
            + jnp.sum(jnp.where(lo, lsq, 0.0), axis=-1, keepdims=True))
    r_q = lax.rsqrt(ss_q / MLA_Q_RANK + RMS_EPS)
    q_raw = (jnp.dot((cqa * r_q * gcqa_ref[...]).astype(BF16), wuqa_ref[...], preferred_element_type=F32)
             + jnp.dot((last * r_q * gcqb_ref[...]).astype(BF16), wuqb_ref[...], preferred_element_type=F32))

    rope_lanes = (lane >= MLA_NOPE) & (lane < MLA_NOPE + MLA_ROPE)
    ss_kr = jnp.sum(jnp.where(rope_lanes, lsq, 0.0), axis=-1, keepdims=True)
    kr_terms = last * lax.rsqrt(ss_kr / MLA_ROPE + RMS_EPS) * ktab_ref[...]
    kr_placed = jnp.dot(kr_terms.astype(BF16), eplace_ref[...], preferred_element_type=F32)

    k_raw = jnp.dot(ckvn, wkk_ref[...], preferred_element_type=F32)
    vm_ref[...] = jnp.dot(ckvn, wkv_ref[...], preferred_element_type=F32).astype(BF16)
    qtab = qtab_ref[...]
    for h in range(MLA_HEADS):
        sl = slice(h * HEAD_SLOT, (h + 1) * HEAD_SLOT)
        c = q_raw[:, sl]
        sq = c * c
        s_n = jnp.sum(jnp.where(lo, sq, 0.0), axis=-1, keepdims=True)
        s_r = jnp.sum(jnp.where(rope_lanes, sq, 0.0), axis=-1, keepdims=True)
        r = jnp.where(lo, lax.rsqrt(s_n / MLA_NOPE + RMS_EPS), lax.rsqrt(s_r / MLA_ROPE + RMS_EPS))
        qm_ref[:, sl] = (c * r * qtab[:, sl]).astype(BF16)
        kc = k_raw[:, sl]
        r_k = lax.rsqrt(jnp.sum(kc * kc, axis=-1, keepdims=True) / MLA_NOPE + RMS_EPS)
        km_ref[:, sl] = (kc * r_k * gkn_ref[...] + kr_placed[:, sl]).astype(BF16)


def _proj_call(x2, p, seq):
    t = x2.shape[0]
    tm = TM_PROJ
    n_pos = seq // tm
    row = lambda i: (i, 0)
    pos = lambda i: (i % n_pos, 0)
    bf = lambda w: jax.ShapeDtypeStruct((t, w), BF16)
    in_specs = [
        pl.BlockSpec((tm, D_MODEL), row),
        _full((1, D_MODEL)), _full((D_MODEL, PROJ_WIDTH)),
        _full((1, LANES)), _full((1, LANES)), _full((1, LANES)), _full((1, LANES)), _full((1, LANES)),
        _full((LANES, 512)), _full((LANES, 512)), _full((LANES, 512)), _full((LANES, 512)),
        _full((1, LANES)),
        pl.BlockSpec((tm, 512), pos), pl.BlockSpec((tm, LANES), pos),
        _full((LANES, 512)),
    ]
    out_specs = [pl.BlockSpec((tm, 512), row)] * 7 + [pl.BlockSpec((tm, POOL_WIDTH), row)]
    out_shape = [bf(512)] * 7 + [jax.ShapeDtypeStruct((t, POOL_WIDTH), F32)]
    return pl.pallas_call(
        _proj_kernel, grid=(t // tm,), in_specs=in_specs, out_specs=out_specs, out_shape=out_shape,
        compiler_params=_cparams(1),
    )(x2, p["g1"], p["win"], p["gq"], p["gk"], p["gckv"], p["gcqa"], p["gcqb"],
      p["wuqa"], p["wuqb"], p["wkk"], p["wkv"], p["gkn"], p["qtab"], p["ktab"], p["eplace"])


def _unflatten(n, sizes):
    n = jnp.minimum(n, math.prod(sizes) - 1)
    coords = []
    for size in reversed(sizes):
        coords.append(n % size)
        n = n // size
    return tuple(reversed(coords))


def _two_stage(n, stage, bufs):
    (s0, m0), (s1, m1) = bufs

    @pl.when(n == 0)
    def _():
        s1[...] = jnp.zeros_like(s1)
        m1[...] = jnp.zeros_like(m1)

    @pl.when(n % 2 == 0)
    def _():
        stage((s0, m0), (s1, m1))

    @pl.when(n % 2 == 1)
    def _():
        stage((s1, m1), (s0, m0))


def _two_stage_scratch(seq):
    pair = [pltpu.VMEM((2 * TQ, seq), F32), pltpu.VMEM((2 * TQ, LANES), F32)]
    return pair + pair


def _softmax_pv_tile(s_prev, m_rows, c, tq, lsum, acc, v_tiles, exp_fn):
    n_half = tq // LANES
    ps = [exp_fn(s_prev[:, (c * n_half + j) * LANES:(c * n_half + j + 1) * LANES] - m_rows)
          for j in range(n_half)]
    for ch in ps:
        lsum = ch if lsum is None else lsum + ch
    pb = jnp.concatenate(ps, axis=1).astype(BF16)
    for g, vt in enumerate(v_tiles):
        pv = jnp.dot(pb[g * tq:(g + 1) * tq], vt, preferred_element_type=F32)
        acc[g] = pv if acc[g] is None else acc[g] + pv
    return lsum


def _running_max(mx, sc):
    for j in range(sc.shape[1] // LANES):
        chunk = sc[:, j * LANES:(j + 1) * LANES]
        mx = chunk if mx is None else jnp.maximum(mx, chunk)
    return mx


def _diff_kernel(lam_ref, q1_ref, q2_ref, qx_ref, k_ref, kx_ref, bd_ref, v_ref, gsub_ref, o_ref,
                 s0_ref, m0_ref, s1_ref, m1_ref, *, sizes):
    n = pl.program_id(0)
    tq = q1_ref.shape[0]
    n_kt = k_ref.shape[0] // tq
    qi_cur = _unflatten(n, sizes)[2]
    qi_prev = _unflatten(jnp.maximum(n - 1, 0), sizes)[2]

    def stage(cur, prev):
        s_cur, m_cur = cur
        s_prev, m_prev = prev
        q1, q2 = q1_ref[...], q2_ref[...]
        qx_left = qx_ref[0, 0]
        qx_right = -qx_left
        bd = bd_ref[0]
        bd2 = jnp.concatenate([bd, bd], axis=0)
        mx = None
        m_rows = m_prev[...]
        lsum = None
        acc = [None, None]
        for c in range(n_kt):
            start = pl.multiple_of(((qi_prev + c) % n_kt) * tq, tq)
            vt = v_ref[pl.ds(start, tq), :]
            lsum = _softmax_pv_tile(s_prev, m_rows, c, tq, lsum, acc, [vt, vt], jnp.exp)

            tile = (qi_cur + c) % n_kt
            start = pl.multiple_of(tile * tq, tq)
            if c == 0:
                qx = jnp.zeros_like(qx_left)
            else:
                qx = jnp.where(qi_cur + c >= n_kt, qx_left, qx_right)
            qq = jnp.concatenate([jnp.concatenate([q1, qx], axis=1),
                                  jnp.concatenate([q2, qx], axis=1)], axis=0)
            kk = jnp.concatenate([k_ref[pl.ds(start, tq), :], kx_ref[pl.ds(start, tq), :]], axis=1)
            sc = lax.dot_general(qq, kk, NT_DIMS, preferred_element_type=F32)
            if c == 0:
                sc = sc + bd2
            s_cur[:, c * tq:(c + 1) * tq] = sc
            mx = _running_max(mx, sc)
        m_cur[...] = jnp.broadcast_to(jnp.max(mx, axis=-1, keepdims=True), m_cur.shape)
        l = jnp.sum(lsum, axis=-1, keepdims=True)
        o = acc[0] * (1.0 / l[0:tq]) - acc[1] * (lam_ref[0] / l[tq:2 * tq])
        r = lax.rsqrt(jnp.mean(o * o, axis=-1, keepdims=True) + RMS_EPS)
        o_ref[...] = (o * r * gsub_ref[...]).astype(BF16)

    _two_stage(n, stage, ((s0_ref, m0_ref), (s1_ref, m1_ref)))


def _alibi_tables(seq):
    nq = seq // TQ
    slopes = 2.0 ** (-8.0 * jnp.arange(1, DIFF_HEADS + 1, dtype=F32) / DIFF_HEADS)
    pos = jnp.arange(seq, dtype=jnp.int32)
    hi = (pos // 256).astype(F32)
    lo = (pos % 256).astype(F32)
    s4 = slopes[:, None]
    ones = jnp.ones((DIFF_HEADS, seq), F32)
    q_left = jnp.stack([-s4 * 256.0 * hi[None], -s4 * lo[None], s4 * 256.0 * ones, s4 * ones], axis=-1)
    qx = jnp.concatenate([q_left, jnp.zeros((DIFF_HEADS, seq, HEAD_SLOT - 4), F32)], axis=-1)
    qx = qx.reshape(DIFF_HEADS, nq, TQ, HEAD_SLOT).astype(BF16)
    k_cols = jnp.stack([jnp.ones((seq,), F32), jnp.ones((seq,), F32), hi, lo], axis=-1)
    kx = jnp.concatenate([k_cols, jnp.zeros((seq, HEAD_SLOT - 4), F32)], axis=-1).astype(BF16)
    loc = jnp.arange(TQ, dtype=jnp.int32)
    bd = -slopes[:, None, None] * jnp.abs(loc[:, None] - loc[None, :]).astype(F32)[None]
    return qx, kx, bd


def _diff_call(lam, dq1, dq2, dk, dv, gsub, tabs, batch, seq):
    t = dq1.shape[0]
    nq = seq // TQ
    qx, kx, bd = tabs
    sizes = (batch, DIFF_HEADS, nq)
    cur = lambda n: _unflatten(n, sizes)
    prev = lambda n: _unflatten(jnp.maximum(n - 1, 0), sizes)

    def qmap(n, *_):
        b, h, qi = cur(n)
        return (b * nq + qi, h)

    def kmap(n, *_):
        b, h, qi = cur(n)
        return (b, h)

    def vmap(n, *_):
        b, h, qi = prev(n)
        return (b, h)

    def omap(n, *_):
        b, h, qi = prev(n)
        return (b * nq + qi, h)

    grid_spec = pltpu.PrefetchScalarGridSpec(
        num_scalar_prefetch=1, grid=(math.prod(sizes) + 1,),
        in_specs=[pl.BlockSpec((TQ, HEAD_SLOT), qmap), pl.BlockSpec((TQ, HEAD_SLOT), qmap),
                  pl.BlockSpec((1, 1, TQ, HEAD_SLOT), lambda n, *_: cur(n)[1:] + (0, 0)),
                  pl.BlockSpec((seq, HEAD_SLOT), kmap),
                  pl.BlockSpec((seq, HEAD_SLOT), lambda *_: (0, 0)),
                  pl.BlockSpec((1, TQ, TQ), lambda n, *_: (cur(n)[1], 0, 0)),
                  pl.BlockSpec((seq, HEAD_SLOT), vmap),
                  pl.BlockSpec((1, HEAD_SLOT), lambda *_: (0, 0))],
        out_specs=pl.BlockSpec((TQ, HEAD_SLOT), omap),
        scratch_shapes=_two_stage_scratch(seq))
    return pl.pallas_call(
        functools.partial(_diff_kernel, sizes=sizes), grid_spec=grid_spec,
        out_shape=jax.ShapeDtypeStruct((t, DIFF_WIDTH), BF16),
        compiler_params=_cparams(1),
    )(lam, dq1, dq2, qx, dk, kx, bd, dv, gsub)


def _mla_kernel(q_ref, k_ref, v_ref, o_ref, s0_ref, m0_ref, s1_ref, m1_ref):
    n = pl.program_id(0)
    tq = q_ref.shape[0]
    n_kt = k_ref.shape[0] // tq

    def stage(cur, prev):
        s_cur, m_cur = cur
        s_prev, m_prev = prev
        mx = [None, None]
        m_rows = m_prev[...]
        lsum = None
        acc = [None, None]
        for c in range(n_kt):
            rows = slice(c * tq, (c + 1) * tq)
            v_tiles = [v_ref[rows, hh * HEAD_SLOT:(hh + 1) * HEAD_SLOT] for hh in range(2)]
            lsum = _softmax_pv_tile(s_prev, m_rows, c, tq, lsum, acc, v_tiles, jnp.exp2)
            for hh in range(2):
                sl = slice(hh * HEAD_SLOT, (hh + 1) * HEAD_SLOT)
                sc = lax.dot_general(q_ref[:, sl], k_ref[rows, sl], NT_DIMS, preferred_element_type=F32)
                s_cur[hh * tq:(hh + 1) * tq, rows] = sc
                mx[hh] = _running_max(mx[hh], sc)
        mx = jnp.concatenate(mx, axis=0)
        m_cur[...] = jnp.broadcast_to(jnp.max(mx, axis=-1, keepdims=True), m_cur.shape)
        l = jnp.sum(lsum, axis=-1, keepdims=True)
        o = acc[0] * (1.0 / l[0:tq]) + acc[1] * (1.0 / l[tq:2 * tq])
        o_ref[...] = o.astype(BF16)

    _two_stage(n, stage, ((s0_ref, m0_ref), (s1_ref, m1_ref)))


def _mla_call(qm, km, vm, batch, seq):
    t = qm.shape[0]
    nq = seq // TQ
    sizes = (batch, MLA_HEADS // 2, nq)
    cur = lambda n: _unflatten(n, sizes)
    prev = lambda n: _unflatten(jnp.maximum(n - 1, 0), sizes)

    def qmap(n):
        b, p, qi = cur(n)
        return (b * nq + qi, p)

    def kmap(n):
        b, p, qi = cur(n)
        return (b, p)

    def vmap(n):
        b, p, qi = prev(n)
        return (b, p)

    def omap(n):
        b, p, qi = prev(n)
        return (b * nq + qi, p)

    return pl.pallas_call(
        _mla_kernel, grid=(math.prod(sizes) + 1,),
        in_specs=[pl.BlockSpec((TQ, 2 * HEAD_SLOT), qmap), pl.BlockSpec((seq, 2 * HEAD_SLOT), kmap),
                  pl.BlockSpec((seq, 2 * HEAD_SLOT), vmap)],
        out_specs=pl.BlockSpec((TQ, HEAD_SLOT), omap),
        out_shape=jax.ShapeDtypeStruct((t, MLA_WIDTH), BF16),
        scratch_shapes=_two_stage_scratch(seq),
        compiler_params=_cparams(1),
    )(qm, km, vm)


def _pool_kernel(u_ref, w_ref, scale_ref, o_ref):
    u = u_ref[...]
    seq, width = u.shape
    zpad = jnp.zeros((POOL_PAD, width), F32)
    ue = jnp.concatenate([zpad, u, zpad], axis=0)
    n = seq + 2 * POOL_PAD

    def down(a, k):
        return pltpu.roll(a, k, axis=0)

    def up(a, k):
        return pltpu.roll(a, n - k, axis=0)

    a2 = ue + down(ue, 1)
    a4 = down(a2, 1) + up(a2, 1)
    a8 = down(a4, 2) + up(a4, 2)
    a16 = down(a8, 4) + up(a8, 4)
    core = slice(POOL_PAD, POOL_PAD + seq)
    lane = lax.broadcasted_iota(jnp.int32, (seq, width), 1)
    tpos = lax.broadcasted_iota(jnp.int32, (seq, width), 0)
    grp = lane // POOL_GROUP_DIM
    win_sum = jnp.where(grp == 0, a2[core], jnp.where(grp == 1, a4[core], jnp.where(grp == 2, a8[core], a16[core])))
    half = jnp.where(grp == 0, 1, jnp.where(grp == 1, 2, jnp.where(grp == 2, 4, 8)))
    lo_i = jnp.maximum(tpos - half, 0)
    hi_i = jnp.minimum(tpos + half - 1, seq - 1)
    cnt = (hi_i - lo_i + 1).astype(F32)
    pooled = win_sum / cnt - u
    mixed = jnp.dot(pooled.astype(BF16), w_ref[...], preferred_element_type=F32)
    o_ref[...] = (mixed * scale_ref[...]).astype(BF16)


def _pool_call(pu, w_bd, scale, batch, seq):
    t = pu.shape[0]
    return pl.pallas_call(
        _pool_kernel, grid=(batch,),
        in_specs=[pl.BlockSpec((seq, POOL_WIDTH), lambda b: (b, 0)),
                  _full((POOL_WIDTH, POOL_WIDTH)), _full((1, POOL_WIDTH))],
        out_specs=pl.BlockSpec((seq, POOL_WIDTH), lambda b: (b, 0)),
        out_shape=jax.ShapeDtypeStruct((t, POOL_WIDTH), BF16),
        compiler_params=_cparams(1),
    )(pu, w_bd, scale)


def _outproj_kernel(x_ref, od_ref, om_ref, op_ref, wo_ref, g2_ref, wrh_ref, wrl_ref, br_ref,
                    xo_ref, h2_ref, ri_ref, rg_ref, cnt_ref, carry_ref):
    i = pl.program_id(0)

    @pl.when(i == 0)
    def _():
        carry_ref[...] = jnp.zeros_like(carry_ref)

    xn = (x_ref[...]
          + jnp.dot(od_ref[...], wo_ref[0:DIFF_WIDTH, :], preferred_element_type=F32)
          + jnp.dot(om_ref[...], wo_ref[DIFF_WIDTH:DIFF_WIDTH + MLA_WIDTH, :], preferred_element_type=F32)
          + jnp.dot(op_ref[...], wo_ref[DIFF_WIDTH + MLA_WIDTH:, :], preferred_element_type=F32))
    xo_ref[...] = xn
    h2 = xn * lax.rsqrt(jnp.mean(xn * xn, axis=-1, keepdims=True) + RMS_EPS) * g2_ref[...]
    n_chunk = D_MODEL // LANES
    for j in range(n_chunk):
        h2_ref[pl.ds(j, xn.shape[0], stride=n_chunk), :] = h2[:, j * LANES:(j + 1) * LANES]

    h_hi = h2.astype(BF16)
    h_lo = (h2 - h_hi.astype(F32)).astype(BF16)
    logits = (jnp.dot(h_hi, wrh_ref[...], preferred_element_type=F32)
              + jnp.dot(h_lo, wrh_ref[...], preferred_element_type=F32)
              + jnp.dot(h_hi, wrl_ref[...], preferred_element_type=F32)
              + br_ref[...])
    tm = logits.shape[0]
    lane = lax.broadcasted_iota(jnp.int32, (tm, LANES), 1)
    lane_f = lane.astype(F32)
    neg = jnp.float32(-jnp.inf)
    big = jnp.float32(1e9)

    gmask = lane < N_GROUPS
    gl = jnp.where(gmask, logits, neg)
    gmax = jnp.max(gl, axis=-1, keepdims=True)
    gsum = jnp.sum(jnp.where(gmask, jnp.exp(gl - gmax), 0.0), axis=-1, keepdims=True)
    g_top = 1.0 / gsum
    g_idx = jnp.min(jnp.where(gl == gmax, lane_f, big), axis=-1, keepdims=True)

    e_lo = N_GROUPS + EXPERTS_PER_GROUP * g_idx
    emask = (lane_f >= e_lo) & (lane_f < e_lo + EXPERTS_PER_GROUP)
    el = jnp.where(emask, logits, neg)
    emax = jnp.max(el, axis=-1, keepdims=True)
    eexp = jnp.where(emask, jnp.exp(el - emax), 0.0)
    prob = eexp / jnp.sum(eexp, axis=-1, keepdims=True)
    pm = jnp.where(emask, prob, -1.0)
    p1 = jnp.max(pm, axis=-1, keepdims=True)
    i1 = jnp.min(jnp.where(pm == p1, lane_f, big), axis=-1, keepdims=True)
    pm2 = jnp.where(lane_f == i1, -1.0, pm)
    p2 = jnp.max(pm2, axis=-1, keepdims=True)
    i2 = jnp.min(jnp.where(pm2 == p2, lane_f, big), axis=-1, keepdims=True)
    denom = p1 + p2
    gate1 = g_top * p1 / denom
    gate2 = g_top * p2 / denom

    sel1 = lane_f == i1
    sel2 = lane_f == i2
    onehot = jnp.where(sel1 | sel2, 1.0, 0.0)
    rr = lax.broadcasted_iota(jnp.int32, (tm, tm), 0)
    cc = lax.broadcasted_iota(jnp.int32, (tm, tm), 1)
    ltri = jnp.where(cc < rr, 1.0, 0.0).astype(BF16)
    prefix = jnp.dot(ltri, onehot.astype(BF16), preferred_element_type=F32) + carry_ref[...]
    rank1 = jnp.sum(jnp.where(sel1, prefix, 0.0), axis=-1, keepdims=True)
    rank2 = jnp.sum(jnp.where(sel2, prefix, 0.0), axis=-1, keepdims=True)
    carry_ref[...] = carry_ref[...] + jnp.sum(onehot, axis=0, keepdims=True)
    cnt_ref[...] = carry_ref[...]

    info = jnp.where(lane == 0, i1 - N_GROUPS,
                     jnp.where(lane == 1, i2 - N_GROUPS,
                               jnp.where(lane == 2, rank1, jnp.where(lane == 3, rank2, 0.0))))
    ri_ref[...] = info.astype(jnp.int32)
    rg_ref[...] = jnp.where(lane == 0, gate1, jnp.where(lane == 1, gate2, 0.0))


def _outproj_call(x2, od, om, op, p):
    t = x2.shape[0]
    tm = TM_OUT
    row = lambda i: (i, 0)
    return pl.pallas_call(
        _outproj_kernel, grid=(t // tm,),
        in_specs=[pl.BlockSpec((tm, D_MODEL), row), pl.BlockSpec((tm, DIFF_WIDTH), row),
                  pl.BlockSpec((tm, MLA_WIDTH), row), pl.BlockSpec((tm, POOL_WIDTH), row),
                  _full((D_MODEL, D_MODEL)), _full((1, D_MODEL)),
                  _full((D_MODEL, LANES)), _full((D_MODEL, LANES)), _full((1, LANES))],
        out_specs=[pl.BlockSpec((tm, D_MODEL), row),
                   pl.BlockSpec((tm * D_MODEL // LANES, LANES), row),
                   pl.BlockSpec((tm, LANES), row), pl.BlockSpec((tm, LANES), row),
                   _full((1, LANES))],
        out_shape=[jax.ShapeDtypeStruct((t, D_MODEL), F32),
                   jax.ShapeDtypeStruct((t * D_MODEL // LANES, LANES), F32),
                   jax.ShapeDtypeStruct((t, LANES), jnp.int32), jax.ShapeDtypeStruct((t, LANES), F32),
                   jax.ShapeDtypeStruct((1, LANES), F32)],
        scratch_shapes=[pltpu.VMEM((1, LANES), F32)],
        compiler_params=_cparams(1),
    )(x2, od, om, op, p["wo"], p["g2"], p["wrh"], p["wrl"], p["br"])


PAD_CHUNKS = (128, 64, 32, 16, 8, 4, 2, 1)


def _dispatch_kernel(pad_ref, idx_ref, h_ref, xs_hbm, zero_ref, sem, zsem):
    i = pl.program_id(0)
    n_chunk = D_MODEL // LANES
    tm = h_ref.shape[0] // n_chunk

    @pl.when(i == 0)
    def _():
        zero_ref[...] = jnp.zeros_like(zero_ref)

        def pad_copies(e, wait):
            off = pad_ref[0, e]
            cnt = pad_ref[1, e]
            for size in PAD_CHUNKS:
                take = cnt & size

                @pl.when(take != 0)
                def _():
                    cp = pltpu.make_async_copy(
                        zero_ref.at[pl.ds(0, size * n_chunk), :],
                        xs_hbm.at[pl.ds(pl.multiple_of(off * n_chunk, n_chunk), size * n_chunk), :], zsem)
                    if wait:
                        cp.wait()
                    else:
                        cp.start()
                off = off + take

        def start_body(e, c):
            pad_copies(e, False)
            return c

        def wait_body(e, c):
            pad_copies(e, True)
            return c

        lax.fori_loop(0, N_EXPERTS, start_body, 0)
        lax.fori_loop(0, N_EXPERTS, wait_body, 0)

        tail = pad_ref[0, N_EXPERTS]
        zrows = zero_ref.shape[0]

        def tail_copy(c):
            return pltpu.make_async_copy(
                zero_ref, xs_hbm.at[pl.ds(pl.multiple_of(tail * n_chunk + c * zrows, zrows), zrows), :], zsem)

        def tail_start(c, carry):
            tail_copy(c).start()
            return carry

        def tail_wait(c, carry):
            tail_copy(c).wait()
            return carry

        lax.fori_loop(0, pad_ref[1, N_EXPERTS], tail_start, 0)
        lax.fori_loop(0, pad_ref[1, N_EXPERTS], tail_wait, 0)

    for r in range(tm):
        for kk in range(2):
            dst = pl.multiple_of(idx_ref[0, 0, 2 * r + kk] * n_chunk, n_chunk)
            pltpu.make_async_copy(h_ref.at[pl.ds(r * n_chunk, n_chunk), :],
                                  xs_hbm.at[pl.ds(dst, n_chunk), :], sem).start(priority=kk)
    for kk in range(2):
        pltpu.make_async_copy(h_ref, xs_hbm.at[pl.ds(0, tm * n_chunk), :], sem).wait()


def _dispatch_call(pads, dest3, h3, n_slots):
    n_tiles = dest3.shape[0]
    tm = dest3.shape[2] // 2
    n_chunk = D_MODEL // LANES
    grid_spec = pltpu.PrefetchScalarGridSpec(
        num_scalar_prefetch=1, grid=(n_tiles,),
        in_specs=[pl.BlockSpec((1, 1, 2 * tm), lambda i, pads: (i, 0, 0), memory_space=pltpu.SMEM),
                  pl.BlockSpec((tm * n_chunk, LANES), lambda i, pads: (i, 0))],
        out_specs=pl.BlockSpec(memory_space=pl.ANY),
        scratch_shapes=[pltpu.VMEM((PAD_CHUNKS[0] * n_chunk, LANES), F32),
                        pltpu.SemaphoreType.DMA, pltpu.SemaphoreType.DMA])
    return pl.pallas_call(
        _dispatch_kernel, grid_spec=grid_spec,
        out_shape=jax.ShapeDtypeStruct((n_slots * n_chunk, LANES), F32),
        compiler_params=_cparams(1),
    )(pads, dest3, h3)


def _expert_kernel(be_ref, nu_ref, xs_ref, wg_ref, wu_ref, wd_ref, ys_ref, wgb_ref, wub_ref, wdb_ref):
    i = pl.program_id(0)
    n_chunk = D_MODEL // LANES

    @pl.when((i == 0) | (be_ref[i] != be_ref[jnp.maximum(i - 1, 0)]))
    def _():
        wgb_ref[...] = wg_ref[0, 0].astype(BF16)
        wub_ref[...] = wu_ref[0, 0].astype(BF16)
        wdb_ref[...] = wd_ref[0, 0].astype(BF16)

    @pl.when(i < nu_ref[0])
    def _():
        xb = jnp.concatenate([xs_ref[pl.ds(j, ROUTE_BLOCK, stride=n_chunk), :] for j in range(n_chunk)],
                             axis=1).astype(BF16)
        g = jnp.dot(xb, wgb_ref[...], preferred_element_type=F32)
        u = jnp.dot(xb, wub_ref[...], preferred_element_type=F32)
        hmid = g * (1.0 / (1.0 + jnp.exp(-g))) * u
        y = jnp.dot(hmid.astype(BF16), wdb_ref[...], preferred_element_type=F32)
        for j in range(n_chunk):
            ys_ref[pl.ds(j, ROUTE_BLOCK, stride=n_chunk), :] = y[:, j * LANES:(j + 1) * LANES]

    @pl.when(i >= nu_ref[0])
    def _():
        ys_ref[...] = jnp.zeros_like(ys_ref)


def _expert_call(block_eid, n_used, xs3, wg, wu, wd, layer):
    tile = (ROUTE_BLOCK * D_MODEL // LANES, LANES)
    n_blocks = xs3.shape[0] // tile[0]
    wmap = lambda i, be, nu: (layer, be[i], 0, 0)
    row = lambda i, be, nu: (i, 0)
    grid_spec = pltpu.PrefetchScalarGridSpec(
        num_scalar_prefetch=2, grid=(n_blocks,),
        in_specs=[
            pl.BlockSpec(tile, row),
            pl.BlockSpec((1, 1, D_MODEL, D_FF), wmap),
            pl.BlockSpec((1, 1, D_MODEL, D_FF), wmap),
            pl.BlockSpec((1, 1, D_FF, D_MODEL), wmap),
        ],
        out_specs=pl.BlockSpec(tile, row),
        scratch_shapes=[pltpu.VMEM((D_MODEL, D_FF), BF16), pltpu.VMEM((D_MODEL, D_FF), BF16),
                        pltpu.VMEM((D_FF, D_MODEL), BF16)])
    return pl.pallas_call(
        _expert_kernel, grid_spec=grid_spec,
        out_shape=jax.ShapeDtypeStruct(xs3.shape, F32),
        compiler_params=_cparams(1),
    )(block_eid, n_used, xs3, wg, wu, wd)


def _combine_kernel(idx0_ref, idxn_ref, ys_hbm, x_ref, rg_ref, o_ref, buf0, buf1, sem):
    i = pl.program_id(0)
    n = pl.num_programs(0)
    tm = x_ref.shape[0]
    n_chunk = D_MODEL // LANES

    def issue(idx_ref, buf, sem_slot, rows):
        for r in rows:
            for kk in range(2):
                src = pl.multiple_of(idx_ref[0, 0, 2 * r + kk] * n_chunk, n_chunk)
                pltpu.make_async_copy(ys_hbm.at[pl.ds(src, n_chunk), :],
                                      buf.at[pl.ds((kk * tm + r) * n_chunk, n_chunk), :],
                                      sem_slot).start(priority=kk)

    def wait_tile(buf, sem_slot):
        pltpu.make_async_copy(ys_hbm.at[pl.ds(0, 2 * tm * n_chunk), :], buf, sem_slot).wait()

    @pl.when(i == 0)
    def _():
        issue(idx0_ref, buf0, sem.at[0], range(tm))

    def step(buf, sem_cur, buf_next, sem_next):
        issue(idxn_ref, buf_next, sem_next, range(tm))
        wait_tile(buf, sem_cur)
        rg = rg_ref[...]
        g0 = rg[:, 0:1]
        g1 = rg[:, 1:2]
        for j in range(n_chunk):
            cols = slice(j * LANES, (j + 1) * LANES)
            y0 = buf[pl.ds(j, tm, stride=n_chunk), :]
            y1 = buf[pl.ds(tm * n_chunk + j, tm, stride=n_chunk), :]
            o_ref[:, cols] = x_ref[:, cols] + g0 * y0 + g1 * y1

    @pl.when(i % 2 == 0)
    def _():
        step(buf0, sem.at[0], buf1, sem.at[1])

    @pl.when(i % 2 == 1)
    def _():
        step(buf1, sem.at[1], buf0, sem.at[0])

    @pl.when(i == n - 1)
    def _():
        @pl.when(i % 2 == 0)
        def _():
            wait_tile(buf1, sem.at[1])

        @pl.when(i % 2 == 1)
        def _():
            wait_tile(buf0, sem.at[0])


def _combine_call(dest3, ys, x2, rg):
    t = x2.shape[0]
    tm = TM_COMB
    n = t // tm
    row = lambda i: (i, 0)
    tile = (2 * tm * D_MODEL // LANES, LANES)
    return pl.pallas_call(
        _combine_kernel, grid=(n,),
        in_specs=[
            pl.BlockSpec((1, 1, 2 * tm), lambda i: (0, 0, 0), memory_space=pltpu.SMEM),
            pl.BlockSpec((1, 1, 2 * tm), lambda i: (jnp.minimum(i + 1, n - 1), 0, 0), memory_space=pltpu.SMEM),
            pl.BlockSpec(memory_space=pl.ANY),
            pl.BlockSpec((tm, D_MODEL), row), pl.BlockSpec((tm, LANES), row)],
        out_specs=pl.BlockSpec((tm, D_MODEL), row),
        out_shape=jax.ShapeDtypeStruct((t, D_MODEL), F32),
        scratch_shapes=[pltpu.VMEM(tile, F32), pltpu.VMEM(tile, F32), pltpu.SemaphoreType.DMA((2,))],
        compiler_params=_cparams(1),
    )(dest3, dest3, ys, x2, rg)


def _swap_halves(a):
    half = a.shape[-1] // 2
    return jnp.concatenate([a[..., half:], a[..., :half]], axis=-1)


def _layer_params(l, seq, w):
    p = {}
    row = lambda v: v.reshape(1, -1).astype(F32)
    w_in = w["w_in"][l]
    kr_cols = w_in[:, 1856:1888]
    p["win"] = jnp.concatenate(
        [w_in[:, 0:1536], w_in[:, 1888:2144], w_in[:, 1728:1856], w_in[:, 1536:1728],
         kr_cols, _swap_halves(kr_cols)], axis=1).astype(BF16)
    p["g1"] = row(w["norm1_g"][l])
    p["gq"] = row(jnp.tile(w["diff_q_norm_g"][l], 2) * (DIFF_QK ** -0.5))
    p["gk"] = row(jnp.tile(w["diff_k_norm_g"][l], 2))
    p["gckv"] = row(w["mla_kv_lat_norm_g"][l])
    gcq = w["mla_q_lat_norm_g"][l]
    p["gcqa"] = row(gcq[:LANES])
    p["gcqb"] = row(jnp.concatenate([gcq[LANES:], jnp.zeros((2 * LANES - MLA_Q_RANK,), F32)]))

    wuq = w["mla_w_uq"][l].reshape(MLA_Q_RANK, MLA_HEADS, MLA_NOPE + MLA_ROPE)
    rope_w = wuq[:, :, MLA_NOPE:]
    wuq = jnp.concatenate([wuq[:, :, :MLA_NOPE], rope_w, _swap_halves(rope_w)], axis=-1)
    wuq = wuq.reshape(MLA_Q_RANK, MLA_HEADS * HEAD_SLOT)
    wuq = jnp.concatenate([wuq, jnp.zeros((2 * LANES - MLA_Q_RANK, wuq.shape[1]), F32)], axis=0).astype(BF16)
    p["wuqa"] = wuq[:LANES]
    p["wuqb"] = wuq[LANES:]

    wukv = w["mla_w_ukv"][l].reshape(MLA_KV_RANK, MLA_HEADS, MLA_NOPE + MLA_V)
    zk = jnp.zeros((MLA_KV_RANK, MLA_HEADS, HEAD_SLOT - MLA_NOPE), F32)
    p["wkk"] = jnp.concatenate([wukv[:, :, :MLA_NOPE], zk], axis=-1).reshape(MLA_KV_RANK, -1).astype(BF16)
    vcols = wukv[:, :, MLA_NOPE:]
    zv = jnp.zeros_like(vcols)
    even = (jnp.arange(MLA_HEADS) % 2 == 0)[None, :, None]
    wkv = jnp.concatenate([jnp.where(even, vcols, zv), jnp.where(even, zv, vcols)], axis=-1)
    p["wkv"] = wkv.reshape(MLA_KV_RANK, -1).astype(BF16)
    p["gkn"] = row(jnp.concatenate([w["mla_k_nope_norm_g"][l], jnp.zeros((HEAD_SLOT - MLA_NOPE,), F32)]))

    inv = 1.0 / (ROPE_BASE ** (jnp.arange(0, MLA_ROPE, 2, dtype=F32) / MLA_ROPE))
    ang = jnp.arange(seq, dtype=F32)[:, None] * inv[None, :]
    cosf = jnp.concatenate([jnp.cos(ang), jnp.cos(ang)], axis=-1)
    sinf = jnp.concatenate([-jnp.sin(ang), jnp.sin(ang)], axis=-1)
    scale = (MLA_NOPE + MLA_ROPE) ** -0.5 * math.log2(math.e)
    gqr = w["mla_q_rope_norm_g"][l]
    q_head = jnp.concatenate([jnp.broadcast_to(w["mla_q_nope_norm_g"][l][None, :], (seq, MLA_NOPE)),
                              gqr[None, :] * cosf, _swap_halves(gqr)[None, :] * sinf], axis=-1) * scale
    p["qtab"] = jnp.tile(q_head, (1, MLA_HEADS))
    gkr = w["mla_k_rope_norm_g"][l]
    p["ktab"] = jnp.concatenate([jnp.zeros((seq, MLA_NOPE), F32), gkr[None, :] * cosf,
                                 _swap_halves(gkr)[None, :] * sinf], axis=-1)
    src = jnp.arange(LANES)
    dst = jnp.arange(MLA_HEADS * HEAD_SLOT)
    src_j = jnp.where(src >= MLA_NOPE, (src - MLA_NOPE) % MLA_ROPE, -1)
    dst_l = dst % HEAD_SLOT
    dst_j = jnp.where(dst_l >= MLA_NOPE, (dst_l - MLA_NOPE) % MLA_ROPE, -2)
    p["eplace"] = (src_j[:, None] == dst_j[None, :]).astype(BF16)

    pw = w["pool_w"][l]
    bd = jnp.zeros((POOL_WIDTH, POOL_WIDTH), F32)
    for g in range(POOL_GROUPS):
        s0 = g * POOL_GROUP_DIM
        bd = bd.at[s0:s0 + POOL_GROUP_DIM, s0:s0 + POOL_GROUP_DIM].set(pw[g])
    p["pool_w"] = bd.astype(BF16)
    p["pool_scale"] = row(w["pool_scale"][l])

    lam_init = 0.8 - 0.6 * math.exp(-0.3 * l)
    lv = w["diff_lambda"][l].astype(F32)
    p["lam"] = (jnp.exp(jnp.sum(lv[0] * lv[1])) - jnp.exp(jnp.sum(lv[2] * lv[3])) + lam_init).reshape(1)
    p["gsub"] = row(w["diff_sub_norm_g"][l] * (1.0 - lam_init))

    p["wo"] = w["w_out"][l].astype(BF16)
    p["g2"] = row(w["norm2_g"][l])
    wr = jnp.concatenate([w["router_group_w"][l], w["router_expert_w"][l],
                          jnp.zeros((D_MODEL, LANES - N_GROUPS - N_EXPERTS), F32)], axis=1)
    wr_hi = wr.astype(BF16)
    p["wrh"] = wr_hi
    p["wrl"] = (wr - wr_hi.astype(F32)).astype(BF16)
    p["br"] = row(jnp.concatenate([w["router_group_b"][l], w["router_expert_b"][l],
                                   jnp.zeros((LANES - N_GROUPS - N_EXPERTS,), F32)]))
    return p


def kernel(x, norm1_g, w_in, diff_q_norm_g, diff_k_norm_g, diff_lambda, diff_sub_norm_g, mla_q_lat_norm_g, mla_kv_lat_norm_g, mla_w_uq, mla_w_ukv, mla_q_nope_norm_g, mla_q_rope_norm_g, mla_k_nope_norm_g, mla_k_rope_norm_g, pool_w, pool_scale, w_out, norm2_g, router_group_w, router_group_b, router_expert_w, router_expert_b, expert_w_gate, expert_w_up, expert_w_down):
    w = dict(norm1_g=norm1_g, w_in=w_in, diff_q_norm_g=diff_q_norm_g, diff_k_norm_g=diff_k_norm_g,
             diff_lambda=diff_lambda, diff_sub_norm_g=diff_sub_norm_g, mla_q_lat_norm_g=mla_q_lat_norm_g,
             mla_kv_lat_norm_g=mla_kv_lat_norm_g, mla_w_uq=mla_w_uq, mla_w_ukv=mla_w_ukv,
             mla_q_nope_norm_g=mla_q_nope_norm_g, mla_q_rope_norm_g=mla_q_rope_norm_g,
             mla_k_nope_norm_g=mla_k_nope_norm_g, mla_k_rope_norm_g=mla_k_rope_norm_g,
             pool_w=pool_w, pool_scale=pool_scale, w_out=w_out, norm2_g=norm2_g,
             router_group_w=router_group_w, router_group_b=router_group_b,
             router_expert_w=router_expert_w, router_expert_b=router_expert_b)
    batch, seq, d = x.shape
    t = batch * seq
    n_assign = 2 * t
    n_blocks = n_assign // ROUTE_BLOCK + N_EXPERTS
    alibi = _alibi_tables(seq)

    x2 = x.reshape(t, d)
    for l in range(DEPTH):
        p = _layer_params(l, seq, w)
        dq1, dq2, dk, dv, qm, km, vm, pu = _proj_call(x2, p, seq)
        o_diff = _diff_call(p["lam"], dq1, dq2, dk, dv, p["gsub"], alibi, batch, seq)
        o_mla = _mla_call(qm, km, vm, batch, seq)
        o_pool = _pool_call(pu, p["pool_w"], p["pool_scale"], batch, seq)
        x2, h2, route_i, route_g, counts = _outproj_call(x2, o_diff, o_mla, o_pool, p)

        cnt = counts[0, N_GROUPS:N_GROUPS + N_EXPERTS].astype(jnp.int32)
        padded = (cnt + ROUTE_BLOCK - 1) // ROUTE_BLOCK * ROUTE_BLOCK
        padded_ends = jnp.cumsum(padded)
        padded_starts = padded_ends - padded
        eid = route_i[:, 0:2]
        start_of = jnp.sum(jnp.where(eid[..., None] == jnp.arange(N_EXPERTS, dtype=jnp.int32),
                                     padded_starts, 0), axis=-1)
        dest = start_of + route_i[:, 2:4]
        block_start = jnp.arange(n_blocks, dtype=jnp.int32) * ROUTE_BLOCK
        block_eid = jnp.minimum(jnp.sum(block_start[:, None] >= padded_ends[None, :], axis=1),
                                N_EXPERTS - 1).astype(jnp.int32)
        n_used = (padded_ends[-1] // ROUTE_BLOCK).astype(jnp.int32).reshape(1)
        n_tail = (n_blocks - n_used) * (ROUTE_BLOCK // PAD_CHUNKS[0])
        pads = jnp.stack([jnp.concatenate([padded_starts + cnt, padded_ends[-1:]]),
                          jnp.concatenate([padded - cnt, n_tail])]).astype(jnp.int32)
        dest3 = dest.reshape(t // TM_COMB, 1, 2 * TM_COMB)

        xs = _dispatch_call(pads, dest3, h2, n_blocks * ROUTE_BLOCK)
        ys = _expert_call(block_eid, n_used, xs, expert_w_gate, expert_w_up, expert_w_down, l)
        x2 = _combine_call(dest3, ys, x2, route_g)
    return x2.reshape(batch, seq, d)
```

```python
import functools
import math

import jax
import jax.numpy as jnp
from jax import lax
from jax.experimental import pallas as pl
from jax.experimental.pallas import tpu as pltpu

F32 = jnp.float32
BF16 = jnp.bfloat16

D_MODEL = 1024
DEPTH = 2
DIFF_HEADS = 4
DIFF_QK = 64
DIFF_V = 128
DIFF_WIDTH = 512
MLA_HEADS = 4
MLA_NOPE = 64
MLA_ROPE = 32
MLA_V = 64
MLA_Q_RANK = 192
MLA_KV_RANK = 128
MLA_WIDTH = 256
ROPE_BASE = 10000.0
POOL_WIDTH = 256
POOL_GROUPS = 4
POOL_GROUP_DIM = 64
POOL_WINDOWS = (2, 4, 8, 16)
N_GROUPS = 4
EXPERTS_PER_GROUP = 8
N_EXPERTS = 32
D_FF = 256
ROUTE_BLOCK = 256
RMS_EPS = 1e-6

LANES = 128
HEAD_SLOT = 128
PROJ_WIDTH = 2176
POOL_PAD = 16
VMEM_LIMIT = 48 * 1024 * 1024

TM_PROJ = 512
TQ = 512
TM_OUT = 256
TM_COMB = 256

NT_DIMS = (((1,), (1,)), ((), ()))


def _cparams(n_axes):
    return pltpu.CompilerParams(dimension_semantics=("arbitrary",) * n_axes,
                                vmem_limit_bytes=VMEM_LIMIT)


def _full(shape):
    return pl.BlockSpec(shape, lambda *_: (0,) * len(shape))


def _proj_kernel(x_ref, g1_ref, win_ref, gq_ref, gk_ref, gckv_ref, gcqa_ref, gcqb_ref,
                 wuqa_ref, wuqb_ref, wkk_ref, wkv_ref, gkn_ref, qtab_ref, ktab_ref, eplace_ref,
                 dq1_ref, dq2_ref, dk_ref, dv_ref, qm_ref, km_ref, vm_ref, pu_ref):
    x = x_ref[...]
    xn = x * lax.rsqrt(jnp.mean(x * x, axis=-1, keepdims=True) + RMS_EPS) * g1_ref[...]
    proj = jnp.dot(xn.astype(BF16), win_ref[...], preferred_element_type=F32)

    tm = x.shape[0]
    lane = lax.broadcasted_iota(jnp.int32, (tm, LANES), 1)
    lo = lane < DIFF_QK

    def half_norm(c, g_row):
        sq = c * c
        s_lo = jnp.sum(jnp.where(lo, sq, 0.0), axis=-1, keepdims=True)
        s_hi = jnp.sum(jnp.where(lo, 0.0, sq), axis=-1, keepdims=True)
        r = jnp.where(lo, lax.rsqrt(s_lo / DIFF_QK + RMS_EPS), lax.rsqrt(s_hi / DIFF_QK + RMS_EPS))
        return c * r * g_row

    for h in range(DIFF_HEADS):
        sl = slice(h * HEAD_SLOT, (h + 1) * HEAD_SLOT)
        qn = half_norm(proj[:, sl], gq_ref[...])
        dq1_ref[:, sl] = jnp.where(lo, qn, 0.0).astype(BF16)
        dq2_ref[:, sl] = jnp.where(lo, 0.0, qn).astype(BF16)
        ksl = slice(512 + h * HEAD_SLOT, 512 + (h + 1) * HEAD_SLOT)
        dk_ref[:, sl] = half_norm(proj[:, ksl], gk_ref[...]).astype(BF16)
    dv_ref[...] = proj[:, 1024:1536].astype(BF16)
    pu_ref[...] = proj[:, 1536:1792]

    ckv = proj[:, 1792:1920]
    ckvn = ckv * lax.rsqrt(jnp.mean(ckv * ckv, axis=-1, keepdims=True) + RMS_EPS) * gckv_ref[...]
    ckvn = ckvn.astype(BF16)
    cqa = proj[:, 1920:2048]
    last = proj[:, 2048:2176]
    lsq = last * last
    ss_q = (jnp.sum(cqa * cqa, axis=-1, keepdims=True)
            + jnp.sum(jnp.where(lo, lsq, 0.0), axis=-1, keepdims=True))
    r_q = lax.rsqrt(ss_q / MLA_Q_RANK + RMS_EPS)
    q_raw = (jnp.dot((cqa * r_q * gcqa_ref[...]).astype(BF16), wuqa_ref[...], preferred_element_type=F32)
             + jnp.dot((last * r_q * gcqb_ref[...]).astype(BF16), wuqb_ref[...], preferred_element_type=F32))

    rope_lanes = (lane >= MLA_NOPE) & (lane < MLA_NOPE + MLA_ROPE)
    ss_kr = jnp.sum(jnp.where(rope_lanes, lsq, 0.0), axis=-1, keepdims=True)
    kr_terms = last * lax.rsqrt(ss_kr / MLA_ROPE + RMS_EPS) * ktab_ref[...]
    kr_placed = jnp.dot(kr_terms.astype(BF16), eplace_ref[...], preferred_element_type=F32)

    k_raw = jnp.dot(ckvn, wkk_ref[...], preferred_element_type=F32)
    vm_ref[...] = jnp.dot(ckvn, wkv_ref[...], preferred_element_type=F32).astype(BF16)
    qtab = qtab_ref[...]
    for h in range(MLA_HEADS):
        sl = slice(h * HEAD_SLOT, (h + 1) * HEAD_SLOT)
        c = q_raw[:, sl]
        sq = c * c
        s_n = jnp.sum(jnp.where(lo, sq, 0.0), axis=-1, keepdims=True)
        s_r = jnp.sum(jnp.where(rope_lanes, sq, 0.0), axis=-1, keepdims=True)
        r = jnp.where(lo, lax.rsqrt(s_n / MLA_NOPE + RMS_EPS), lax.rsqrt(s_r / MLA_ROPE + RMS_EPS))
        qm_ref[:, sl] = (c * r * qtab[:, sl]).astype(BF16)
        kc = k_raw[:, sl]
        r_k = lax.rsqrt(jnp.sum(kc * kc, axis=-1, keepdims=True) / MLA_NOPE + RMS_EPS)
        km_ref[:, sl] = (kc * r_k * gkn_ref[...] + kr_placed[:, sl]).astype(BF16)


def _proj_call(x2, p, seq):
    t = x2.shape[0]
    tm = TM_PROJ
    n_pos = seq // tm
    row = lambda i: (i, 0)
    pos = lambda i: (i % n_pos, 0)
    bf = lambda w: jax.ShapeDtypeStruct((t, w), BF16)
    in_specs = [
        pl.BlockSpec((tm, D_MODEL), row),
        _full((1, D_MODEL)), _full((D_MODEL, PROJ_WIDTH)),
        _full((1, LANES)), _full((1, LANES)), _full((1, LANES)), _full((1, LANES)), _full((1, LANES)),
        _full((LANES, 512)), _full((LANES, 512)), _full((LANES, 512)), _full((LANES, 512)),
        _full((1, LANES)),
        pl.BlockSpec((tm, 512), pos), pl.BlockSpec((tm, LANES), pos),
        _full((LANES, 512)),
    ]
    out_specs = [pl.BlockSpec((tm, 512), row)] * 7 + [pl.BlockSpec((tm, POOL_WIDTH), row)]
    out_shape = [bf(512)] * 7 + [jax.ShapeDtypeStruct((t, POOL_WIDTH), F32)]
    return pl.pallas_call(
        _proj_kernel, grid=(t // tm,), in_specs=in_specs, out_specs=out_specs, out_shape=out_shape,
        compiler_params=_cparams(1),
    )(x2, p["g1"], p["win"], p["gq"], p["gk"], p["gckv"], p["gcqa"], p["gcqb"],
      p["wuqa"], p["wuqb"], p["wkk"], p["wkv"], p["gkn"], p["qtab"], p["ktab"], p["eplace"])


def _unflatten(n, sizes):
    n = jnp.minimum(n, math.prod(sizes) - 1)
    coords = []
    for size in reversed(sizes):
        coords.append(n % size)
        n = n // size
    return tuple(reversed(coords))


def _two_stage(n, stage, bufs):
    (s0, m0), (s1, m1) = bufs

    @pl.when(n == 0)
    def _():
        s1[...] = jnp.zeros_like(s1)
        m1[...] = jnp.zeros_like(m1)

    @pl.when(n % 2 == 0)
    def _():
        stage((s0, m0), (s1, m1))

    @pl.when(n % 2 == 1)
    def _():
        stage((s1, m1), (s0, m0))


def _two_stage_scratch(seq):
    pair = [pltpu.VMEM((2 * TQ, seq), F32), pltpu.VMEM((2 * TQ, LANES), F32)]
    return pair + pair


def _softmax_pv_tile(s_prev, m_rows, c, tq, lsum, acc, v_tiles, exp_fn):
    n_half = tq // LANES
    ps = [exp_fn(s_prev[:, (c * n_half + j) * LANES:(c * n_half + j + 1) * LANES] - m_rows)
          for j in range(n_half)]
    for ch in ps:
        lsum = ch if lsum is None else lsum + ch
    pb = jnp.concatenate(ps, axis=1).astype(BF16)
    for g, vt in enumerate(v_tiles):
        pv = jnp.dot(pb[g * tq:(g + 1) * tq], vt, preferred_element_type=F32)
        acc[g] = pv if acc[g] is None else acc[g] + pv
    return lsum


def _running_max(mx, sc):
    for j in range(sc.shape[1] // LANES):
        chunk = sc[:, j * LANES:(j + 1) * LANES]
        mx = chunk if mx is None else jnp.maximum(mx, chunk)
    return mx


def _diff_kernel(lam_ref, q1_ref, q2_ref, qx_ref, k_ref, kx_ref, bd_ref, v_ref, gsub_ref, o_ref,
                 s0_ref, m0_ref, s1_ref, m1_ref, *, sizes):
    n = pl.program_id(0)
    tq = q1_ref.shape[0]
    n_kt = k_ref.shape[0] // tq
    qi_cur = _unflatten(n, sizes)[2]
    qi_prev = _unflatten(jnp.maximum(n - 1, 0), sizes)[2]

    def stage(cur, prev):
        s_cur, m_cur = cur
        s_prev, m_prev = prev
        q1, q2 = q1_ref[...], q2_ref[...]
        qx_left = qx_ref[0, 0]
        qx_right = -qx_left
        bd = bd_ref[0]
        bd2 = jnp.concatenate([bd, bd], axis=0)
        mx = None
        m_rows = m_prev[...]
        lsum = None
        acc = [None, None]
        for c in range(n_kt):
            start = pl.multiple_of(((qi_prev + c) % n_kt) * tq, tq)
            vt = v_ref[pl.ds(start, tq), :]
            lsum = _softmax_pv_tile(s_prev, m_rows, c, tq, lsum, acc, [vt, vt], jnp.exp)

            tile = (qi_cur + c) % n_kt
            start = pl.multiple_of(tile * tq, tq)
            if c == 0:
                qx = jnp.zeros_like(qx_left)
            else:
                qx = jnp.where(qi_cur + c >= n_kt, qx_left, qx_right)
            qq = jnp.concatenate([jnp.concatenate([q1, qx], axis=1),
                                  jnp.concatenate([q2, qx], axis=1)], axis=0)
            kk = jnp.concatenate([k_ref[pl.ds(start, tq), :], kx_ref[pl.ds(start, tq), :]], axis=1)
            sc = lax.dot_general(qq, kk, NT_DIMS, preferred_element_type=F32)
            if c == 0:
                sc = sc + bd2
            s_cur[:, c * tq:(c + 1) * tq] = sc
            mx = _running_max(mx, sc)
        m_cur[...] = jnp.broadcast_to(jnp.max(mx, axis=-1, keepdims=True), m_cur.shape)
        l = jnp.sum(lsum, axis=-1, keepdims=True)
        o = acc[0] * (1.0 / l[0:tq]) - acc[1] * (lam_ref[0] / l[tq:2 * tq])
        r = lax.rsqrt(jnp.mean(o * o, axis=-1, keepdims=True) + RMS_EPS)
        o_ref[...] = (o * r * gsub_ref[...]).astype(BF16)

    _two_stage(n, stage, ((s0_ref, m0_ref), (s1_ref, m1_ref)))


def _alibi_tables(seq):
    nq = seq // TQ
    slopes = 2.0 ** (-8.0 * jnp.arange(1, DIFF_HEADS + 1, dtype=F32) / DIFF_HEADS)
    pos = jnp.arange(seq, dtype=jnp.int32)
    hi = (pos // 256).astype(F32)
    lo = (pos % 256).astype(F32)
    s4 = slopes[:, None]
    ones = jnp.ones((DIFF_HEADS, seq), F32)
    q_left = jnp.stack([-s4 * 256.0 * hi[None], -s4 * lo[None], s4 * 256.0 * ones, s4 * ones], axis=-1)
    qx = jnp.concatenate([q_left, jnp.zeros((DIFF_HEADS, seq, HEAD_SLOT - 4), F32)], axis=-1)
    qx = qx.reshape(DIFF_HEADS, nq, TQ, HEAD_SLOT).astype(BF16)
    k_cols = jnp.stack([jnp.ones((seq,), F32), jnp.ones((seq,), F32), hi, lo], axis=-1)
    kx = jnp.concatenate([k_cols, jnp.zeros((seq, HEAD_SLOT - 4), F32)], axis=-1).astype(BF16)
    loc = jnp.arange(TQ, dtype=jnp.int32)
    bd = -slopes[:, None, None] * jnp.abs(loc[:, None] - loc[None, :]).astype(F32)[None]
    return qx, kx, bd


def _diff_call(lam, dq1, dq2, dk, dv, gsub, tabs, batch, seq):
    t = dq1.shape[0]
    nq = seq // TQ
    qx, kx, bd = tabs
    sizes = (batch, DIFF_HEADS, nq)
    cur = lambda n: _unflatten(n, sizes)
    prev = lambda n: _unflatten(jnp.maximum(n - 1, 0), sizes)

    def qmap(n, *_):
        b, h, qi = cur(n)
        return (b * nq + qi, h)

    def kmap(n, *_):
        b, h, qi = cur(n)
        return (b, h)

    def vmap(n, *_):
        b, h, qi = prev(n)
        return (b, h)

    def omap(n, *_):
        b, h, qi = prev(n)
        return (b * nq + qi, h)

    grid_spec = pltpu.PrefetchScalarGridSpec(
        num_scalar_prefetch=1, grid=(math.prod(sizes) + 1,),
        in_specs=[pl.BlockSpec((TQ, HEAD_SLOT), qmap), pl.BlockSpec((TQ, HEAD_SLOT), qmap),
                  pl.BlockSpec((1, 1, TQ, HEAD_SLOT), lambda n, *_: cur(n)[1:] + (0, 0)),
                  pl.BlockSpec((seq, HEAD_SLOT), kmap),
                  pl.BlockSpec((seq, HEAD_SLOT), lambda *_: (0, 0)),
                  pl.BlockSpec((1, TQ, TQ), lambda n, *_: (cur(n)[1], 0, 0)),
                  pl.BlockSpec((seq, HEAD_SLOT), vmap),
                  pl.BlockSpec((1, HEAD_SLOT), lambda *_: (0, 0))],
        out_specs=pl.BlockSpec((TQ, HEAD_SLOT), omap),
        scratch_shapes=_two_stage_scratch(seq))
    return pl.pallas_call(
        functools.partial(_diff_kernel, sizes=sizes), grid_spec=grid_spec,
        out_shape=jax.ShapeDtypeStruct((t, DIFF_WIDTH), BF16),
        compiler_params=_cparams(1),
    )(lam, dq1, dq2, qx, dk, kx, bd, dv, gsub)


def _mla_kernel(q_ref, k_ref, v_ref, o_ref, s0_ref, m0_ref, s1_ref, m1_ref):
    n = pl.program_id(0)
    tq = q_ref.shape[0]
    n_kt = k_ref.shape[0] // tq

    def stage(cur, prev):
        s_cur, m_cur = cur
        s_prev, m_prev = prev
        mx = [None, None]
        m_rows = m_prev[...]
        lsum = None
        acc = [None, None]
        for c in range(n_kt):
            rows = slice(c * tq, (c + 1) * tq)
            v_tiles = [v_ref[rows, hh * HEAD_SLOT:(hh + 1) * HEAD_SLOT] for hh in range(2)]
            lsum = _softmax_pv_tile(s_prev, m_rows, c, tq, lsum, acc, v_tiles, jnp.exp2)
            for hh in range(2):
                sl = slice(hh * HEAD_SLOT, (hh + 1) * HEAD_SLOT)
                sc = lax.dot_general(q_ref[:, sl], k_ref[rows, sl], NT_DIMS, preferred_element_type=F32)
                s_cur[hh * tq:(hh + 1) * tq, rows] = sc
                mx[hh] = _running_max(mx[hh], sc)
        mx = jnp.concatenate(mx, axis=0)
        m_cur[...] = jnp.broadcast_to(jnp.max(mx, axis=-1, keepdims=True), m_cur.shape)
        l = jnp.sum(lsum, axis=-1, keepdims=True)
        o = acc[0] * (1.0 / l[0:tq]) + acc[1] * (1.0 / l[tq:2 * tq])
        o_ref[...] = o.astype(BF16)

    _two_stage(n, stage, ((s0_ref, m0_ref), (s1_ref, m1_ref)))


def _mla_call(qm, km, vm, batch, seq):
    t = qm.shape[0]
    nq = seq // TQ
    sizes = (batch, MLA_HEADS // 2, nq)
    cur = lambda n: _unflatten(n, sizes)
    prev = lambda n: _unflatten(jnp.maximum(n - 1, 0), sizes)

    def qmap(n):
        b, p, qi = cur(n)
        return (b * nq + qi, p)

    def kmap(n):
        b, p, qi = cur(n)
        return (b, p)

    def vmap(n):
        b, p, qi = prev(n)
        return (b, p)

    def omap(n):
        b, p, qi = prev(n)
        return (b * nq + qi, p)

    return pl.pallas_call(
        _mla_kernel, grid=(math.prod(sizes) + 1,),
        in_specs=[pl.BlockSpec((TQ, 2 * HEAD_SLOT), qmap), pl.BlockSpec((seq, 2 * HEAD_SLOT), kmap),
                  pl.BlockSpec((seq, 2 * HEAD_SLOT), vmap)],
        out_specs=pl.BlockSpec((TQ, HEAD_SLOT), omap),
        out_shape=jax.ShapeDtypeStruct((t, MLA_WIDTH), BF16),
        scratch_shapes=_two_stage_scratch(seq),
        compiler_params=_cparams(1),
    )(qm, km, vm)


def _pool_kernel(u_ref, w_ref, scale_ref, o_ref):
    u = u_ref[...]
    seq, width = u.shape
    zpad = jnp.zeros((POOL_PAD, width), F32)
    ue = jnp.concatenate([zpad, u, zpad], axis=0)
    n = seq + 2 * POOL_PAD

    def down(a, k):
        return pltpu.roll(a, k, axis=0)

    def up(a, k):
        return pltpu.roll(a, n - k, axis=0)

    a2 = ue + down(ue, 1)
    a4 = down(a2, 1) + up(a2, 1)
    a8 = down(a4, 2) + up(a4, 2)
    a16 = down(a8, 4) + up(a8, 4)
    core = slice(POOL_PAD, POOL_PAD + seq)
    lane = lax.broadcasted_iota(jnp.int32, (seq, width), 1)
    tpos = lax.broadcasted_iota(jnp.int32, (seq, width), 0)
    grp = lane // POOL_GROUP_DIM
    win_sum = jnp.where(grp == 0, a2[core], jnp.where(grp == 1, a4[core], jnp.where(grp == 2, a8[core], a16[core])))
    half = jnp.where(grp == 0, 1, jnp.where(grp == 1, 2, jnp.where(grp == 2, 4, 8)))
    lo_i = jnp.maximum(tpos - half, 0)
    hi_i = jnp.minimum(tpos + half - 1, seq - 1)
    cnt = (hi_i - lo_i + 1).astype(F32)
    pooled = win_sum / cnt - u
    mixed = jnp.dot(pooled.astype(BF16), w_ref[...], preferred_element_type=F32)
    o_ref[...] = (mixed * scale_ref[...]).astype(BF16)


def _pool_call(pu, w_bd, scale, batch, seq):
    t = pu.shape[0]
    return pl.pallas_call(
        _pool_kernel, grid=(batch,),
        in_specs=[pl.BlockSpec((seq, POOL_WIDTH), lambda b: (b, 0)),
                  _full((POOL_WIDTH, POOL_WIDTH)), _full((1, POOL_WIDTH))],
        out_specs=pl.BlockSpec((seq, POOL_WIDTH), lambda b: (b, 0)),
        out_shape=jax.ShapeDtypeStruct((t, POOL_WIDTH), BF16),
        compiler_params=_cparams(1),
    )(pu, w_bd, scale)


def _outproj_kernel(x_ref, od_ref, om_ref, op_ref, wo_ref, g2_ref, wrh_ref, wrl_ref, br_ref,
                    xo_ref, h2_ref, ri_ref, rg_ref, cnt_ref, carry_ref):
    i = pl.program_id(0)

    @pl.when(i == 0)
    def _():
        carry_ref[...] = jnp.zeros_like(carry_ref)

    xn = (x_ref[...]
          + jnp.dot(od_ref[...], wo_ref[0:DIFF_WIDTH, :], preferred_element_type=F32)
          + jnp.dot(om_ref[...], wo_ref[DIFF_WIDTH:DIFF_WIDTH + MLA_WIDTH, :], preferred_element_type=F32)
          + jnp.dot(op_ref[...], wo_ref[DIFF_WIDTH + MLA_WIDTH:, :], preferred_element_type=F32))
    xo_ref[...] = xn
    h2 = xn * lax.rsqrt(jnp.mean(xn * xn, axis=-1, keepdims=True) + RMS_EPS) * g2_ref[...]
    n_chunk = D_MODEL // LANES
    for j in range(n_chunk):
        h2_ref[pl.ds(j, xn.shape[0], stride=n_chunk), :] = h2[:, j * LANES:(j + 1) * LANES]

    h_hi = h2.astype(BF16)
    h_lo = (h2 - h_hi.astype(F32)).astype(BF16)
    logits = (jnp.dot(h_hi, wrh_ref[...], preferred_element_type=F32)
              + jnp.dot(h_lo, wrh_ref[...], preferred_element_type=F32)
              + jnp.dot(h_hi, wrl_ref[...], preferred_element_type=F32)
              + br_ref[...])
    tm = logits.shape[0]
    lane = lax.broadcasted_iota(jnp.int32, (tm, LANES), 1)
    lane_f = lane.astype(F32)
    neg = jnp.float32(-jnp.inf)
    big = jnp.float32(1e9)

    gmask = lane < N_GROUPS
    gl = jnp.where(gmask, logits, neg)
    gmax = jnp.max(gl, axis=-1, keepdims=True)
    gsum = jnp.sum(jnp.where(gmask, jnp.exp(gl - gmax), 0.0), axis=-1, keepdims=True)
    g_top = 1.0 / gsum
    g_idx = jnp.min(jnp.where(gl == gmax, lane_f, big), axis=-1, keepdims=True)

    e_lo = N_GROUPS + EXPERTS_PER_GROUP * g_idx
    emask = (lane_f >= e_lo) & (lane_f < e_lo + EXPERTS_PER_GROUP)
    el = jnp.where(emask, logits, neg)
    emax = jnp.max(el, axis=-1, keepdims=True)
    eexp = jnp.where(emask, jnp.exp(el - emax), 0.0)
    prob = eexp / jnp.sum(eexp, axis=-1, keepdims=True)
    pm = jnp.where(emask, prob, -1.0)
    p1 = jnp.max(pm, axis=-1, keepdims=True)
    i1 = jnp.min(jnp.where(pm == p1, lane_f, big), axis=-1, keepdims=True)
    pm2 = jnp.where(lane_f == i1, -1.0, pm)
    p2 = jnp.max(pm2, axis=-1, keepdims=True)
    i2 = jnp.min(jnp.where(pm2 == p2, lane_f, big), axis=-1, keepdims=True)
    denom = p1 + p2
    gate1 = g_top * p1 / denom
    gate2 = g_top * p2 / denom

    sel1 = lane_f == i1
    sel2 = lane_f == i2
    onehot = jnp.where(sel1 | sel2, 1.0, 0.0)
    rr = lax.broadcasted_iota(jnp.int32, (tm, tm), 0)
    cc = lax.broadcasted_iota(jnp.int32, (tm, tm), 1)
    ltri = jnp.where(cc < rr, 1.0, 0.0).astype(BF16)
    prefix = jnp.dot(ltri, onehot.astype(BF16), preferred_element_type=F32) + carry_ref[...]
    rank1 = jnp.sum(jnp.where(sel1, prefix, 0.0), axis=-1, keepdims=True)
    rank2 = jnp.sum(jnp.where(sel2, prefix, 0.0), axis=-1, keepdims=True)
    carry_ref[...] = carry_ref[...] + jnp.sum(onehot, axis=0, keepdims=True)
    cnt_ref[...] = carry_ref[...]

    info = jnp.where(lane == 0, i1 - N_GROUPS,
                     jnp.where(lane == 1, i2 - N_GROUPS,
                               jnp.where(lane == 2, rank1, jnp.where(lane == 3, rank2, 0.0))))
    ri_ref[...] = info.astype(jnp.int32)
    rg_ref[...] = jnp.where(lane == 0, gate1, jnp.where(lane == 1, gate2, 0.0))


def _outproj_call(x2, od, om, op, p):
    t = x2.shape[0]
    tm = TM_OUT
    row = lambda i: (i, 0)
    return pl.pallas_call(
        _outproj_kernel, grid=(t // tm,),
        in_specs=[pl.BlockSpec((tm, D_MODEL), row), pl.BlockSpec((tm, DIFF_WIDTH), row),
                  pl.BlockSpec((tm, MLA_WIDTH), row), pl.BlockSpec((tm, POOL_WIDTH), row),
                  _full((D_MODEL, D_MODEL)), _full((1, D_MODEL)),
                  _full((D_MODEL, LANES)), _full((D_MODEL, LANES)), _full((1, LANES))],
        out_specs=[pl.BlockSpec((tm, D_MODEL), row),
                   pl.BlockSpec((tm * D_MODEL // LANES, LANES), row),
                   pl.BlockSpec((tm, LANES), row), pl.BlockSpec((tm, LANES), row),
                   _full((1, LANES))],
        out_shape=[jax.ShapeDtypeStruct((t, D_MODEL), F32),
                   jax.ShapeDtypeStruct((t * D_MODEL // LANES, LANES), F32),
                   jax.ShapeDtypeStruct((t, LANES), jnp.int32), jax.ShapeDtypeStruct((t, LANES), F32),
                   jax.ShapeDtypeStruct((1, LANES), F32)],
        scratch_shapes=[pltpu.VMEM((1, LANES), F32)],
        compiler_params=_cparams(1),
    )(x2, od, om, op, p["wo"], p["g2"], p["wrh"], p["wrl"], p["br"])


PAD_CHUNKS = (128, 64, 32, 16, 8, 4, 2, 1)


def _dispatch_kernel(pad_ref, idx_ref, h_ref, xs_hbm, zero_ref, sem, zsem):
    i = pl.program_id(0)
    n_chunk = D_MODEL // LANES
    tm = h_ref.shape[0] // n_chunk

    @pl.when(i == 0)
    def _():
        zero_ref[...] = jnp.zeros_like(zero_ref)

        def pad_copies(e, wait):
            off = pad_ref[0, e]
            cnt = pad_ref[1, e]
            for size in PAD_CHUNKS:
                take = cnt & size

                @pl.when(take != 0)
                def _():
                    cp = pltpu.make_async_copy(
                        zero_ref.at[pl.ds(0, size * n_chunk), :],
                        xs_hbm.at[pl.ds(pl.multiple_of(off * n_chunk, n_chunk), size * n_chunk), :], zsem)
                    if wait:
                        cp.wait()
                    else:
                        cp.start()
                off = off + take

        def start_body(e, c):
            pad_copies(e, False)
            return c

        def wait_body(e, c):
            pad_copies(e, True)
            return c

        lax.fori_loop(0, N_EXPERTS, start_body, 0)
        lax.fori_loop(0, N_EXPERTS, wait_body, 0)

        tail = pad_ref[0, N_EXPERTS]
        zrows = zero_ref.shape[0]

        def tail_copy(c):
            return pltpu.make_async_copy(
                zero_ref, xs_hbm.at[pl.ds(pl.multiple_of(tail * n_chunk + c * zrows, zrows), zrows), :], zsem)

        def tail_start(c, carry):
            tail_copy(c).start()
            return carry

        def tail_wait(c, carry):
            tail_copy(c).wait()
            return carry

        lax.fori_loop(0, pad_ref[1, N_EXPERTS], tail_start, 0)
        lax.fori_loop(0, pad_ref[1, N_EXPERTS], tail_wait, 0)

    for r in range(tm):
        for kk in range(2):
            dst = pl.multiple_of(idx_ref[0, 0, 2 * r + kk] * n_chunk, n_chunk)
            pltpu.make_async_copy(h_ref.at[pl.ds(r * n_chunk, n_chunk), :],
                                  xs_hbm.at[pl.ds(dst, n_chunk), :], sem).start(priority=kk)
    for kk in range(2):
        pltpu.make_async_copy(h_ref, xs_hbm.at[pl.ds(0, tm * n_chunk), :], sem).wait()


def _dispatch_call(pads, dest3, h3, n_slots):
    n_tiles = dest3.shape[0]
    tm = dest3.shape[2] // 2
    n_chunk = D_MODEL // LANES
    grid_spec = pltpu.PrefetchScalarGridSpec(
        num_scalar_prefetch=1, grid=(n_tiles,),
        in_specs=[pl.BlockSpec((1, 1, 2 * tm), lambda i, pads: (i, 0, 0), memory_space=pltpu.SMEM),
                  pl.BlockSpec((tm * n_chunk, LANES), lambda i, pads: (i, 0))],
        out_specs=pl.BlockSpec(memory_space=pl.ANY),
        scratch_shapes=[pltpu.VMEM((PAD_CHUNKS[0] * n_chunk, LANES), F32),
                        pltpu.SemaphoreType.DMA, pltpu.SemaphoreType.DMA])
    return pl.pallas_call(
        _dispatch_kernel, grid_spec=grid_spec,
        out_shape=jax.ShapeDtypeStruct((n_slots * n_chunk, LANES), F32),
        compiler_params=_cparams(1),
    )(pads, dest3, h3)


def _expert_kernel(be_ref, nu_ref, xs_ref, wg_ref, wu_ref, wd_ref, ys_ref, wgb_ref, wub_ref, wdb_ref):
    i = pl.program_id(0)
    n_chunk = D_MODEL // LANES

    @pl.when((i == 0) | (be_ref[i] != be_ref[jnp.maximum(i - 1, 0)]))
    def _():
        wgb_ref[...] = wg_ref[0, 0].astype(BF16)
        wub_ref[...] = wu_ref[0, 0].astype(BF16)
        wdb_ref[...] = wd_ref[0, 0].astype(BF16)

    @pl.when(i < nu_ref[0])
    def _():
        xb = jnp.concatenate([xs_ref[pl.ds(j, ROUTE_BLOCK, stride=n_chunk), :] for j in range(n_chunk)],
                             axis=1).astype(BF16)
        g = jnp.dot(xb, wgb_ref[...], preferred_element_type=F32)
        u = jnp.dot(xb, wub_ref[...], preferred_element_type=F32)
        hmid = g * (1.0 / (1.0 + jnp.exp(-g))) * u
        y = jnp.dot(hmid.astype(BF16), wdb_ref[...], preferred_element_type=F32)
        for j in range(n_chunk):
            ys_ref[pl.ds(j, ROUTE_BLOCK, stride=n_chunk), :] = y[:, j * LANES:(j + 1) * LANES]

    @pl.when(i >= nu_ref[0])
    def _():
        ys_ref[...] = jnp.zeros_like(ys_ref)


def _expert_call(block_eid, n_used, xs3, wg, wu, wd, layer):
    tile = (ROUTE_BLOCK * D_MODEL // LANES, LANES)
    n_blocks = xs3.shape[0] // tile[0]
    wmap = lambda i, be, nu: (layer, be[i], 0, 0)
    row = lambda i, be, nu: (i, 0)
    grid_spec = pltpu.PrefetchScalarGridSpec(
        num_scalar_prefetch=2, grid=(n_blocks,),
        in_specs=[
            pl.BlockSpec(tile, row),
            pl.BlockSpec((1, 1, D_MODEL, D_FF), wmap),
            pl.BlockSpec((1, 1, D_MODEL, D_FF), wmap),
            pl.BlockSpec((1, 1, D_FF, D_MODEL), wmap),
        ],
        out_specs=pl.BlockSpec(tile, row),
        scratch_shapes=[pltpu.VMEM((D_MODEL, D_FF), BF16), pltpu.VMEM((D_MODEL, D_FF), BF16),
                        pltpu.VMEM((D_FF, D_MODEL), BF16)])
    return pl.pallas_call(
        _expert_kernel, grid_spec=grid_spec,
        out_shape=jax.ShapeDtypeStruct(xs3.shape, F32),
        compiler_params=_cparams(1),
    )(block_eid, n_used, xs3, wg, wu, wd)


def _combine_kernel(idx0_ref, idxn_ref, ys_hbm, x_ref, rg_ref, o_ref, buf0, buf1, sem):
    i = pl.program_id(0)
    n = pl.num_programs(0)
    tm = x_ref.shape[0]
    n_chunk = D_MODEL // LANES

    def issue(idx_ref, buf, sem_slot, rows):
        for r in rows:
            for kk in range(2):
                src = pl.multiple_of(idx_ref[0, 0, 2 * r + kk] * n_chunk, n_chunk)
                pltpu.make_async_copy(ys_hbm.at[pl.ds(src, n_chunk), :],
                                      buf.at[pl.ds((kk * tm + r) * n_chunk, n_chunk), :],
                                      sem_slot).start(priority=kk)

    def wait_tile(buf, sem_slot):
        pltpu.make_async_copy(ys_hbm.at[pl.ds(0, 2 * tm * n_chunk), :], buf, sem_slot).wait()

    @pl.when(i == 0)
    def _():
        issue(idx0_ref, buf0, sem.at[0], range(tm))

    def step(buf, sem_cur, buf_next, sem_next):
        issue(idxn_ref, buf_next, sem_next, range(tm))
        wait_tile(buf, sem_cur)
        rg = rg_ref[...]
        g0 = rg[:, 0:1]
        g1 = rg[:, 1:2]
        for j in range(n_chunk):
            cols = slice(j * LANES, (j + 1) * LANES)
            y0 = buf[pl.ds(j, tm, stride=n_chunk), :]
            y1 = buf[pl.ds(tm * n_chunk + j, tm, stride=n_chunk), :]
            o_ref[:, cols] = x_ref[:, cols] + g0 * y0 + g1 * y1

    @pl.when(i % 2 == 0)
    def _():
        step(buf0, sem.at[0], buf1, sem.at[1])

    @pl.when(i % 2 == 1)
    def _():
        step(buf1, sem.at[1], buf0, sem.at[0])

    @pl.when(i == n - 1)
    def _():
        @pl.when(i % 2 == 0)
        def _():
            wait_tile(buf1, sem.at[1])

        @pl.when(i % 2 == 1)
        def _():
            wait_tile(buf0, sem.at[0])


def _combine_call(dest3, ys, x2, rg):
    t = x2.shape[0]
    tm = TM_COMB
    n = t // tm
    row = lambda i: (i, 0)
    tile = (2 * tm * D_MODEL // LANES, LANES)
    return pl.pallas_call(
        _combine_kernel, grid=(n,),
        in_specs=[
            pl.BlockSpec((1, 1, 2 * tm), lambda i: (0, 0, 0), memory_space=pltpu.SMEM),
            pl.BlockSpec((1, 1, 2 * tm), lambda i: (jnp.minimum(i + 1, n - 1), 0, 0), memory_space=pltpu.SMEM),
            pl.BlockSpec(memory_space=pl.ANY),
            pl.BlockSpec((tm, D_MODEL), row), pl.BlockSpec((tm, LANES), row)],
        out_specs=pl.BlockSpec((tm, D_MODEL), row),
        out_shape=jax.ShapeDtypeStruct((t, D_MODEL), F32),
        scratch_shapes=[pltpu.VMEM(tile, F32), pltpu.VMEM(tile, F32), pltpu.SemaphoreType.DMA((2,))],
        compiler_params=_cparams(1),
    )(dest3, dest3, ys, x2, rg)


def _swap_halves(a):
    half = a.shape[-1] // 2
    return jnp.concatenate([a[..., half:], a[..., :half]], axis=-1)


def _layer_params(l, seq, w):
    p = {}
    row = lambda v: v.reshape(1, -1).astype(F32)
    w_in = w["w_in"][l]
    kr_cols = w_in[:, 1856:1888]
    p["win"] = jnp.concatenate(
        [w_in[:, 0:1536], w_in[:, 1888:2144], w_in[:, 1728:1856], w_in[:, 1536:1728],
         kr_cols, _swap_halves(kr_cols)], axis=1).astype(BF16)
    p["g1"] = row(w["norm1_g"][l])
    p["gq"] = row(jnp.tile(w["diff_q_norm_g"][l], 2) * (DIFF_QK ** -0.5))
    p["gk"] = row(jnp.tile(w["diff_k_norm_g"][l], 2))
    p["gckv"] = row(w["mla_kv_lat_norm_g"][l])
    gcq = w["mla_q_lat_norm_g"][l]
    p["gcqa"] = row(gcq[:LANES])
    p["gcqb"] = row(jnp.concatenate([gcq[LANES:], jnp.zeros((2 * LANES - MLA_Q_RANK,), F32)]))

    wuq = w["mla_w_uq"][l].reshape(MLA_Q_RANK, MLA_HEADS, MLA_NOPE + MLA_ROPE)
    rope_w = wuq[:, :, MLA_NOPE:]
    wuq = jnp.concatenate([wuq[:, :, :MLA_NOPE], rope_w, _swap_halves(rope_w)], axis=-1)
    wuq = wuq.reshape(MLA_Q_RANK, MLA_HEADS * HEAD_SLOT)
    wuq = jnp.concatenate([wuq, jnp.zeros((2 * LANES - MLA_Q_RANK, wuq.shape[1]), F32)], axis=0).astype(BF16)
    p["wuqa"] = wuq[:LANES]
    p["wuqb"] = wuq[LANES:]

    wukv = w["mla_w_ukv"][l].reshape(MLA_KV_RANK, MLA_HEADS, MLA_NOPE + MLA_V)
    zk = jnp.zeros((MLA_KV_RANK, MLA_HEADS, HEAD_SLOT - MLA_NOPE), F32)
    p["wkk"] = jnp.concatenate([wukv[:, :, :MLA_NOPE], zk], axis=-1).reshape(MLA_KV_RANK, -1).astype(BF16)
    vcols = wukv[:, :, MLA_NOPE:]
    zv = jnp.zeros_like(vcols)
    even = (jnp.arange(MLA_HEADS) % 2 == 0)[None, :, None]
    wkv = jnp.concatenate([jnp.where(even, vcols, zv), jnp.where(even, zv, vcols)], axis=-1)
    p["wkv"] = wkv.reshape(MLA_KV_RANK, -1).astype(BF16)
    p["gkn"] = row(jnp.concatenate([w["mla_k_nope_norm_g"][l], jnp.zeros((HEAD_SLOT - MLA_NOPE,), F32)]))

    inv = 1.0 / (ROPE_BASE ** (jnp.arange(0, MLA_ROPE, 2, dtype=F32) / MLA_ROPE))
    ang = jnp.arange(seq, dtype=F32)[:, None] * inv[None, :]
    cosf = jnp.concatenate([jnp.cos(ang), jnp.cos(ang)], axis=-1)
    sinf = jnp.concatenate([-jnp.sin(ang), jnp.sin(ang)], axis=-1)
    scale = (MLA_NOPE + MLA_ROPE) ** -0.5 * math.log2(math.e)
    gqr = w["mla_q_rope_norm_g"][l]
    q_head = jnp.concatenate([jnp.broadcast_to(w["mla_q_nope_norm_g"][l][None, :], (seq, MLA_NOPE)),
                              gqr[None, :] * cosf, _swap_halves(gqr)[None, :] * sinf], axis=-1) * scale
    p["qtab"] = jnp.tile(q_head, (1, MLA_HEADS))
    gkr = w["mla_k_rope_norm_g"][l]
    p["ktab"] = jnp.concatenate([jnp.zeros((seq, MLA_NOPE), F32), gkr[None, :] * cosf,
                                 _swap_halves(gkr)[None, :] * sinf], axis=-1)
    src = jnp.arange(LANES)
    dst = jnp.arange(MLA_HEADS * HEAD_SLOT)
    src_j = jnp.where(src >= MLA_NOPE, (src - MLA_NOPE) % MLA_ROPE, -1)
    dst_l = dst % HEAD_SLOT
    dst_j = jnp.where(dst_l >= MLA_NOPE, (dst_l - MLA_NOPE) % MLA_ROPE, -2)
    p["eplace"] = (src_j[:, None] == dst_j[None, :]).astype(BF16)

    pw = w["pool_w"][l]
    bd = jnp.zeros((POOL_WIDTH, POOL_WIDTH), F32)
    for g in range(POOL_GROUPS):
        s0 = g * POOL_GROUP_DIM
        bd = bd.at[s0:s0 + POOL_GROUP_DIM, s0:s0 + POOL_GROUP_DIM].set(pw[g])
    p["pool_w"] = bd.astype(BF16)
    p["pool_scale"] = row(w["pool_scale"][l])

    lam_init = 0.8 - 0.6 * math.exp(-0.3 * l)
    lv = w["diff_lambda"][l].astype(F32)
    p["lam"] = (jnp.exp(jnp.sum(lv[0] * lv[1])) - jnp.exp(jnp.sum(lv[2] * lv[3])) + lam_init).reshape(1)
    p["gsub"] = row(w["diff_sub_norm_g"][l] * (1.0 - lam_init))

    p["wo"] = w["w_out"][l].astype(BF16)
    p["g2"] = row(w["norm2_g"][l])
    wr = jnp.concatenate([w["router_group_w"][l], w["router_expert_w"][l],
                          jnp.zeros((D_MODEL, LANES - N_GROUPS - N_EXPERTS), F32)], axis=1)
    wr_hi = wr.astype(BF16)
    p["wrh"] = wr_hi
    p["wrl"] = (wr - wr_hi.astype(F32)).astype(BF16)
    p["br"] = row(jnp.concatenate([w["router_group_b"][l], w["router_expert_b"][l],
                                   jnp.zeros((LANES - N_GROUPS - N_EXPERTS,), F32)]))
    return p


def kernel(x, norm1_g, w_in, diff_q_norm_g, diff_k_norm_g, diff_lambda, diff_sub_norm_g, mla_q_lat_norm_g, mla_kv_lat_norm_g, mla_w_uq, mla_w_ukv, mla_q_nope_norm_g, mla_q_rope_norm_g, mla_k_nope_norm_g, mla_k_rope_norm_g, pool_w, pool_scale, w_out, norm2_g, router_group_w, router_group_b, router_expert_w, router_expert_b, expert_w_gate, expert_w_up, expert_w_down):
    w = dict(norm1_g=norm1_g, w_in=w_in, diff_q_norm_g=diff_q_norm_g, diff_k_norm_g=diff_k_norm_g,
             diff_lambda=diff_lambda, diff_sub_norm_g=diff_sub_norm_g, mla_q_lat_norm_g=mla_q_lat_norm_g,
             mla_kv_lat_norm_g=mla_kv_lat_norm_g, mla_w_uq=mla_w_uq, mla_w_ukv=mla_w_ukv,
             mla_q_nope_norm_g=mla_q_nope_norm_g, mla_q_rope_norm_g=mla_q_rope_norm_g,
             mla_k_nope_norm_g=mla_k_nope_norm_g, mla_k_rope_norm_g=mla_k_rope_norm_g,
             pool_w=pool_w, pool_scale=pool_scale, w_out=w_out, norm2_g=norm2_g,
             router_group_w=router_group_w, router_group_b=router_group_b,
             router_expert_w=router_expert_w, router_expert_b=router_expert_b)
    batch, seq, d = x.shape
    t = batch * seq
    n_assign = 2 * t
    n_blocks = n_assign // ROUTE_BLOCK + N_EXPERTS
    alibi = _alibi_tables(seq)

    x2 = x.reshape(t, d)
    for l in range(DEPTH):
        p = _layer_params(l, seq, w)
        dq1, dq2, dk, dv, qm, km, vm, pu = _proj_call(x2, p, seq)
        o_diff = _diff_call(p["lam"], dq1, dq2, dk, dv, p["gsub"], alibi, batch, seq)
        o_mla = _mla_call(qm, km, vm, batch, seq)
        o_pool = _pool_call(pu, p["pool_w"], p["pool_scale"], batch, seq)
        x2, h2, route_i, route_g, counts = _outproj_call(x2, o_diff, o_mla, o_pool, p)

        cnt = counts[0, N_GROUPS:N_GROUPS + N_EXPERTS].astype(jnp.int32)
        padded = (cnt + ROUTE_BLOCK - 1) // ROUTE_BLOCK * ROUTE_BLOCK
        padded_ends = jnp.cumsum(padded)
        padded_starts = padded_ends - padded
        eid = route_i[:, 0:2]
        start_of = jnp.sum(jnp.where(eid[..., None] == jnp.arange(N_EXPERTS, dtype=jnp.int32),
                                     padded_starts, 0), axis=-1)
        dest = start_of + route_i[:, 2:4]
        block_start = jnp.arange(n_blocks, dtype=jnp.int32) * ROUTE_BLOCK
        block_eid = jnp.minimum(jnp.sum(block_start[:, None] >= padded_ends[None, :], axis=1),
                                N_EXPERTS - 1).astype(jnp.int32)
        n_used = (padded_ends[-1] // ROUTE_BLOCK).astype(jnp.int32).reshape(1)
        n_tail = (n_blocks - n_used) * (ROUTE_BLOCK // PAD_CHUNKS[0])
        pads = jnp.stack([jnp.concatenate([padded_starts + cnt, padded_ends[-1:]]),
                          jnp.concatenate([padded - cnt, n_tail])]).astype(jnp.int32)
        dest3 = dest.reshape(t // TM_COMB, 1, 2 * TM_COMB)

        xs = _dispatch_call(pads, dest3, h2, n_blocks * ROUTE_BLOCK)
        ys = _expert_call(block_eid, n_used, xs, expert_w_gate, expert_w_up, expert_w_down, l)
        x2 = _combine_call(dest3, ys, x2, route_g)
    return x2.reshape(batch, seq, d)
```

```python
import functools
import math

import jax
import jax.numpy as jnp
from jax import lax
from jax.experimental import pallas as pl
from jax.experimental.pallas import tpu as pltpu

F32 = jnp.float32
BF16 = jnp.bfloat16

D_MODEL = 1024
DEPTH = 2
DIFF_HEADS = 4
DIFF_QK = 64
DIFF_V = 128
DIFF_WIDTH = 512
MLA_HEADS = 4
MLA_NOPE = 64
MLA_ROPE = 32
MLA_V = 64
MLA_Q_RANK = 192
MLA_KV_RANK = 128
MLA_WIDTH = 256
ROPE_BASE = 10000.0
POOL_WIDTH = 256
POOL_GROUPS = 4
POOL_GROUP_DIM = 64
POOL_WINDOWS = (2, 4, 8, 16)
N_GROUPS = 4
EXPERTS_PER_GROUP = 8
N_EXPERTS = 32
D_FF = 256
ROUTE_BLOCK = 256
RMS_EPS = 1e-6

LANES = 128
HEAD_SLOT = 128
PROJ_WIDTH = 2176
POOL_PAD = 16
VMEM_LIMIT = 48 * 1024 * 1024

TM_PROJ = 512
TQ = 512
TM_OUT = 1024
TM_COMB = 256

NT_DIMS = (((1,), (1,)), ((), ()))


def _cparams(n_axes):
    return pltpu.CompilerParams(dimension_semantics=("arbitrary",) * n_axes,
                                vmem_limit_bytes=VMEM_LIMIT)


def _full(shape):
    return pl.BlockSpec(shape, lambda *_: (0,) * len(shape))


def _proj_kernel(x_ref, g1_ref, win_ref, gq_ref, gk_ref, gckv_ref, gcqa_ref, gcqb_ref,
                 wuqa_ref, wuqb_ref, wkk_ref, wkv_ref, gkn_ref, qtab_ref, ktab_ref, eplace_ref,
                 dq1_ref, dq2_ref, dk_ref, dv_ref, qm_ref, km_ref, vm_ref, pu_ref):
    x = x_ref[...]
    xn = x * lax.rsqrt(jnp.mean(x * x, axis=-1, keepdims=True) + RMS_EPS) * g1_ref[...]
    proj = jnp.dot(xn.astype(BF16), win_ref[...], preferred_element_type=F32)

    tm = x.shape[0]
    lane = lax.broadcasted_iota(jnp.int32, (tm, LANES), 1)
    lo = lane < DIFF_QK

    def half_norm(c, g_row):
        sq = c * c
        s_lo = jnp.sum(jnp.where(lo, sq, 0.0), axis=-1, keepdims=True)
        s_hi = jnp.sum(jnp.where(lo, 0.0, sq), axis=-1, keepdims=True)
        r = jnp.where(lo, lax.rsqrt(s_lo / DIFF_QK + RMS_EPS), lax.rsqrt(s_hi / DIFF_QK + RMS_EPS))
        return c * r * g_row

    for h in range(DIFF_HEADS):
        sl = slice(h * HEAD_SLOT, (h + 1) * HEAD_SLOT)
        qn = half_norm(proj[:, sl], gq_ref[...])
        dq1_ref[:, sl] = jnp.where(lo, qn, 0.0).astype(BF16)
        dq2_ref[:, sl] = jnp.where(lo, 0.0, qn).astype(BF16)
        ksl = slice(512 + h * HEAD_SLOT, 512 + (h + 1) * HEAD_SLOT)
        dk_ref[:, sl] = half_norm(proj[:, ksl], gk_ref[...]).astype(BF16)
    dv_ref[...] = proj[:, 1024:1536].astype(BF16)
    pu_ref[...] = proj[:, 1536:1792]

    ckv = proj[:, 1792:1920]
    ckvn = ckv * lax.rsqrt(jnp.mean(ckv * ckv, axis=-1, keepdims=True) + RMS_EPS) * gckv_ref[...]
    ckvn = ckvn.astype(BF16)
    cqa = proj[:, 1920:2048]
    last = proj[:, 2048:2176]
    lsq = last * last
    ss_q = (jnp.sum(cqa * cqa, axis=-1, keepdims=True)
            + jnp.sum(jnp.where(lo, lsq, 0.0), axis=-1, keepdims=True))
    r_q = lax.rsqrt(ss_q / MLA_Q_RANK + RMS_EPS)
    q_raw = (jnp.dot((cqa * r_q * gcqa_ref[...]).astype(BF16), wuqa_ref[...], preferred_element_type=F32)
             + jnp.dot((last * r_q * gcqb_ref[...]).astype(BF16), wuqb_ref[...], preferred_element_type=F32))

    rope_lanes = (lane >= MLA_NOPE) & (lane < MLA_NOPE + MLA_ROPE)
    ss_kr = jnp.sum(jnp.where(rope_lanes, lsq, 0.0), axis=-1, keepdims=True)
    kr_terms = last * lax.rsqrt(ss_kr / MLA_ROPE + RMS_EPS) * ktab_ref[...]
    kr_placed = jnp.dot(kr_terms.astype(BF16), eplace_ref[...], preferred_element_type=F32)

    k_raw = jnp.dot(ckvn, wkk_ref[...], preferred_element_type=F32)
    vm_ref[...] = jnp.dot(ckvn, wkv_ref[...], preferred_element_type=F32).astype(BF16)
    qtab = qtab_ref[...]
    for h in range(MLA_HEADS):
        sl = slice(h * HEAD_SLOT, (h + 1) * HEAD_SLOT)
        c = q_raw[:, sl]
        sq = c * c
        s_n = jnp.sum(jnp.where(lo, sq, 0.0), axis=-1, keepdims=True)
        s_r = jnp.sum(jnp.where(rope_lanes, sq, 0.0), axis=-1, keepdims=True)
        r = jnp.where(lo, lax.rsqrt(s_n / MLA_NOPE + RMS_EPS), lax.rsqrt(s_r / MLA_ROPE + RMS_EPS))
        qm_ref[:, sl] = (c * r * qtab[:, sl]).astype(BF16)
        kc = k_raw[:, sl]
        r_k = lax.rsqrt(jnp.sum(kc * kc, axis=-1, keepdims=True) / MLA_NOPE + RMS_EPS)
        km_ref[:, sl] = (kc * r_k * gkn_ref[...] + kr_placed[:, sl]).astype(BF16)


def _proj_call(x2, p, seq):
    t = x2.shape[0]
    tm = TM_PROJ
    n_pos = seq // tm
    row = lambda i: (i, 0)
    pos = lambda i: (i % n_pos, 0)
    bf = lambda w: jax.ShapeDtypeStruct((t, w), BF16)
    in_specs = [
        pl.BlockSpec((tm, D_MODEL), row),
        _full((1, D_MODEL)), _full((D_MODEL, PROJ_WIDTH)),
        _full((1, LANES)), _full((1, LANES)), _full((1, LANES)), _full((1, LANES)), _full((1, LANES)),
        _full((LANES, 512)), _full((LANES, 512)), _full((LANES, 512)), _full((LANES, 512)),
        _full((1, LANES)),
        pl.BlockSpec((tm, 512), pos), pl.BlockSpec((tm, LANES), pos),
        _full((LANES, 512)),
    ]
    out_specs = [pl.BlockSpec((tm, 512), row)] * 7 + [pl.BlockSpec((tm, POOL_WIDTH), row)]
    out_shape = [bf(512)] * 7 + [jax.ShapeDtypeStruct((t, POOL_WIDTH), F32)]
    return pl.pallas_call(
        _proj_kernel, grid=(t // tm,), in_specs=in_specs, out_specs=out_specs, out_shape=out_shape,
        compiler_params=_cparams(1),
    )(x2, p["g1"], p["win"], p["gq"], p["gk"], p["gckv"], p["gcqa"], p["gcqb"],
      p["wuqa"], p["wuqb"], p["wkk"], p["wkv"], p["gkn"], p["qtab"], p["ktab"], p["eplace"])


def _unflatten(n, sizes):
    n = jnp.minimum(n, math.prod(sizes) - 1)
    coords = []
    for size in reversed(sizes):
        coords.append(n % size)
        n = n // size
    return tuple(reversed(coords))


def _two_stage(n, stage, bufs):
    (s0, m0), (s1, m1) = bufs

    @pl.when(n == 0)
    def _():
        s1[...] = jnp.zeros_like(s1)
        m1[...] = jnp.zeros_like(m1)

    @pl.when(n % 2 == 0)
    def _():
        stage((s0, m0), (s1, m1))

    @pl.when(n % 2 == 1)
    def _():
        stage((s1, m1), (s0, m0))


def _two_stage_scratch(seq):
    pair = [pltpu.VMEM((2 * TQ, seq), F32), pltpu.VMEM((2 * TQ, LANES), F32)]
    return pair + pair


def _softmax_pv_tile(s_prev, m_rows, c, tq, lsum, acc, v_tiles, exp_fn):
    n_half = tq // LANES
    ps = [exp_fn(s_prev[:, (c * n_half + j) * LANES:(c * n_half + j + 1) * LANES] - m_rows)
          for j in range(n_half)]
    for ch in ps:
        lsum = ch if lsum is None else lsum + ch
    pb = jnp.concatenate(ps, axis=1).astype(BF16)
    for g, vt in enumerate(v_tiles):
        pv = jnp.dot(pb[g * tq:(g + 1) * tq], vt, preferred_element_type=F32)
        acc[g] = pv if acc[g] is None else acc[g] + pv
    return lsum


def _running_max(mx, sc):
    for j in range(sc.shape[1] // LANES):
        chunk = sc[:, j * LANES:(j + 1) * LANES]
        mx = chunk if mx is None else jnp.maximum(mx, chunk)
    return mx


def _diff_kernel(lam_ref, q1_ref, q2_ref, qx_ref, k_ref, kx_ref, bd_ref, v_ref, gsub_ref, o_ref,
                 s0_ref, m0_ref, s1_ref, m1_ref, *, sizes):
    n = pl.program_id(0)
    tq = q1_ref.shape[0]
    n_kt = k_ref.shape[0] // tq
    qi_cur = _unflatten(n, sizes)[2]
    qi_prev = _unflatten(jnp.maximum(n - 1, 0), sizes)[2]

    def stage(cur, prev):
        s_cur, m_cur = cur
        s_prev, m_prev = prev
        q1, q2 = q1_ref[...], q2_ref[...]
        qx_left = qx_ref[0, 0]
        qx_right = -qx_left
        bd = bd_ref[0]
        bd2 = jnp.concatenate([bd, bd], axis=0)
        mx = None
        m_rows = m_prev[...]
        lsum = None
        acc = [None, None]
        for c in range(n_kt):
            start = pl.multiple_of(((qi_prev + c) % n_kt) * tq, tq)
            vt = v_ref[pl.ds(start, tq), :]
            lsum = _softmax_pv_tile(s_prev, m_rows, c, tq, lsum, acc, [vt, vt], jnp.exp)

            tile = (qi_cur + c) % n_kt
            start = pl.multiple_of(tile * tq, tq)
            if c == 0:
                qx = jnp.zeros_like(qx_left)
            else:
                qx = jnp.where(qi_cur + c >= n_kt, qx_left, qx_right)
            qq = jnp.concatenate([jnp.concatenate([q1, qx], axis=1),
                                  jnp.concatenate([q2, qx], axis=1)], axis=0)
            kk = jnp.concatenate([k_ref[pl.ds(start, tq), :], kx_ref[pl.ds(start, tq), :]], axis=1)
            sc = lax.dot_general(qq, kk, NT_DIMS, preferred_element_type=F32)
            if c == 0:
                sc = sc + bd2
            s_cur[:, c * tq:(c + 1) * tq] = sc
            mx = _running_max(mx, sc)
        m_cur[...] = jnp.broadcast_to(jnp.max(mx, axis=-1, keepdims=True), m_cur.shape)
        l = jnp.sum(lsum, axis=-1, keepdims=True)
        o = acc[0] * (1.0 / l[0:tq]) - acc[1] * (lam_ref[0] / l[tq:2 * tq])
        r = lax.rsqrt(jnp.mean(o * o, axis=-1, keepdims=True) + RMS_EPS)
        o_ref[...] = (o * r * gsub_ref[...]).astype(BF16)

    _two_stage(n, stage, ((s0_ref, m0_ref), (s1_ref, m1_ref)))


def _alibi_tables(seq):
    nq = seq // TQ
    slopes = 2.0 ** (-8.0 * jnp.arange(1, DIFF_HEADS + 1, dtype=F32) / DIFF_HEADS)
    pos = jnp.arange(seq, dtype=jnp.int32)
    hi = (pos // 256).astype(F32)
    lo = (pos % 256).astype(F32)
    s4 = slopes[:, None]
    ones = jnp.ones((DIFF_HEADS, seq), F32)
    q_left = jnp.stack([-s4 * 256.0 * hi[None], -s4 * lo[None], s4 * 256.0 * ones, s4 * ones], axis=-1)
    qx = jnp.concatenate([q_left, jnp.zeros((DIFF_HEADS, seq, HEAD_SLOT - 4), F32)], axis=-1)
    qx = qx.reshape(DIFF_HEADS, nq, TQ, HEAD_SLOT).astype(BF16)
    k_cols = jnp.stack([jnp.ones((seq,), F32), jnp.ones((seq,), F32), hi, lo], axis=-1)
    kx = jnp.concatenate([k_cols, jnp.zeros((seq, HEAD_SLOT - 4), F32)], axis=-1).astype(BF16)
    loc = jnp.arange(TQ, dtype=jnp.int32)
    bd = -slopes[:, None, None] * jnp.abs(loc[:, None] - loc[None, :]).astype(F32)[None]
    return qx, kx, bd


def _diff_call(lam, dq1, dq2, dk, dv, gsub, tabs, batch, seq):
    t = dq1.shape[0]
    nq = seq // TQ
    qx, kx, bd = tabs
    sizes = (batch, DIFF_HEADS, nq)
    cur = lambda n: _unflatten(n, sizes)
    prev = lambda n: _unflatten(jnp.maximum(n - 1, 0), sizes)

    def qmap(n, *_):
        b, h, qi = cur(n)
        return (b * nq + qi, h)

    def kmap(n, *_):
        b, h, qi = cur(n)
        return (b, h)

    def vmap(n, *_):
        b, h, qi = prev(n)
        return (b, h)

    def omap(n, *_):
        b, h, qi = prev(n)
        return (b * nq + qi, h)

    grid_spec = pltpu.PrefetchScalarGridSpec(
        num_scalar_prefetch=1, grid=(math.prod(sizes) + 1,),
        in_specs=[pl.BlockSpec((TQ, HEAD_SLOT), qmap), pl.BlockSpec((TQ, HEAD_SLOT), qmap),
                  pl.BlockSpec((1, 1, TQ, HEAD_SLOT), lambda n, *_: cur(n)[1:] + (0, 0)),
                  pl.BlockSpec((seq, HEAD_SLOT), kmap),
                  pl.BlockSpec((seq, HEAD_SLOT), lambda *_: (0, 0)),
                  pl.BlockSpec((1, TQ, TQ), lambda n, *_: (cur(n)[1], 0, 0)),
                  pl.BlockSpec((seq, HEAD_SLOT), vmap),
                  pl.BlockSpec((1, HEAD_SLOT), lambda *_: (0, 0))],
        out_specs=pl.BlockSpec((TQ, HEAD_SLOT), omap),
        scratch_shapes=_two_stage_scratch(seq))
    return pl.pallas_call(
        functools.partial(_diff_kernel, sizes=sizes), grid_spec=grid_spec,
        out_shape=jax.ShapeDtypeStruct((t, DIFF_WIDTH), BF16),
        compiler_params=_cparams(1),
    )(lam, dq1, dq2, qx, dk, kx, bd, dv, gsub)


def _mla_kernel(q_ref, k_ref, v_ref, o_ref, s0_ref, m0_ref, s1_ref, m1_ref):
    n = pl.program_id(0)
    tq = q_ref.shape[0]
    n_kt = k_ref.shape[0] // tq

    def stage(cur, prev):
        s_cur, m_cur = cur
        s_prev, m_prev = prev
        mx = [None, None]
        m_rows = m_prev[...]
        lsum = None
        acc = [None, None]
        for c in range(n_kt):
            rows = slice(c * tq, (c + 1) * tq)
            v_tiles = [v_ref[rows, hh * HEAD_SLOT:(hh + 1) * HEAD_SLOT] for hh in range(2)]
            lsum = _softmax_pv_tile(s_prev, m_rows, c, tq, lsum, acc, v_tiles, jnp.exp2)
            for hh in range(2):
                sl = slice(hh * HEAD_SLOT, (hh + 1) * HEAD_SLOT)
                sc = lax.dot_general(q_ref[:, sl], k_ref[rows, sl], NT_DIMS, preferred_element_type=F32)
                s_cur[hh * tq:(hh + 1) * tq, rows] = sc
                mx[hh] = _running_max(mx[hh], sc)
        mx = jnp.concatenate(mx, axis=0)
        m_cur[...] = jnp.broadcast_to(jnp.max(mx, axis=-1, keepdims=True), m_cur.shape)
        l = jnp.sum(lsum, axis=-1, keepdims=True)
        o = acc[0] * (1.0 / l[0:tq]) + acc[1] * (1.0 / l[tq:2 * tq])
        o_ref[...] = o.astype(BF16)

    _two_stage(n, stage, ((s0_ref, m0_ref), (s1_ref, m1_ref)))


def _mla_call(qm, km, vm, batch, seq):
    t = qm.shape[0]
    nq = seq // TQ
    sizes = (batch, MLA_HEADS // 2, nq)
    cur = lambda n: _unflatten(n, sizes)
    prev = lambda n: _unflatten(jnp.maximum(n - 1, 0), sizes)

    def qmap(n):
        b, p, qi = cur(n)
        return (b * nq + qi, p)

    def kmap(n):
        b, p, qi = cur(n)
        return (b, p)

    def vmap(n):
        b, p, qi = prev(n)
        return (b, p)

    def omap(n):
        b, p, qi = prev(n)
        return (b * nq + qi, p)

    return pl.pallas_call(
        _mla_kernel, grid=(math.prod(sizes) + 1,),
        in_specs=[pl.BlockSpec((TQ, 2 * HEAD_SLOT), qmap), pl.BlockSpec((seq, 2 * HEAD_SLOT), kmap),
                  pl.BlockSpec((seq, 2 * HEAD_SLOT), vmap)],
        out_specs=pl.BlockSpec((TQ, HEAD_SLOT), omap),
        out_shape=jax.ShapeDtypeStruct((t, MLA_WIDTH), BF16),
        scratch_shapes=_two_stage_scratch(seq),
        compiler_params=_cparams(1),
    )(qm, km, vm)


def _pool_kernel(u_ref, w_ref, scale_ref, o_ref):
    u = u_ref[...]
    seq, width = u.shape
    zpad = jnp.zeros((POOL_PAD, width), F32)
    ue = jnp.concatenate([zpad, u, zpad], axis=0)
    n = seq + 2 * POOL_PAD

    def down(a, k):
        return pltpu.roll(a, k, axis=0)

    def up(a, k):
        return pltpu.roll(a, n - k, axis=0)

    a2 = ue + down(ue, 1)
    a4 = down(a2, 1) + up(a2, 1)
    a8 = down(a4, 2) + up(a4, 2)
    a16 = down(a8, 4) + up(a8, 4)
    core = slice(POOL_PAD, POOL_PAD + seq)
    lane = lax.broadcasted_iota(jnp.int32, (seq, width), 1)
    tpos = lax.broadcasted_iota(jnp.int32, (seq, width), 0)
    grp = lane // POOL_GROUP_DIM
    win_sum = jnp.where(grp == 0, a2[core], jnp.where(grp == 1, a4[core], jnp.where(grp == 2, a8[core], a16[core])))
    half = jnp.where(grp == 0, 1, jnp.where(grp == 1, 2, jnp.where(grp == 2, 4, 8)))
    lo_i = jnp.maximum(tpos - half, 0)
    hi_i = jnp.minimum(tpos + half - 1, seq - 1)
    cnt = (hi_i - lo_i + 1).astype(F32)
    pooled = win_sum / cnt - u
    mixed = jnp.dot(pooled.astype(BF16), w_ref[...], preferred_element_type=F32)
    o_ref[...] = (mixed * scale_ref[...]).astype(BF16)


def _pool_call(pu, w_bd, scale, batch, seq):
    t = pu.shape[0]
    return pl.pallas_call(
        _pool_kernel, grid=(batch,),
        in_specs=[pl.BlockSpec((seq, POOL_WIDTH), lambda b: (b, 0)),
                  _full((POOL_WIDTH, POOL_WIDTH)), _full((1, POOL_WIDTH))],
        out_specs=pl.BlockSpec((seq, POOL_WIDTH), lambda b: (b, 0)),
        out_shape=jax.ShapeDtypeStruct((t, POOL_WIDTH), BF16),
        compiler_params=_cparams(1),
    )(pu, w_bd, scale)


def _outproj_kernel(x_ref, od_ref, om_ref, op_ref, wo_ref, g2_ref, wrh_ref, wrl_ref, br_ref,
                    xo_ref, h2_ref, ri_ref, rg_ref, cnt_ref, carry_ref):
    i = pl.program_id(0)

    @pl.when(i == 0)
    def _():
        carry_ref[...] = jnp.zeros_like(carry_ref)

    xn = (x_ref[...]
          + jnp.dot(od_ref[...], wo_ref[0:DIFF_WIDTH, :], preferred_element_type=F32)
          + jnp.dot(om_ref[...], wo_ref[DIFF_WIDTH:DIFF_WIDTH + MLA_WIDTH, :], preferred_element_type=F32)
          + jnp.dot(op_ref[...], wo_ref[DIFF_WIDTH + MLA_WIDTH:, :], preferred_element_type=F32))
    xo_ref[...] = xn
    h2 = xn * lax.rsqrt(jnp.mean(xn * xn, axis=-1, keepdims=True) + RMS_EPS) * g2_ref[...]
    n_chunk = D_MODEL // LANES
    for j in range(n_chunk):
        h2_ref[pl.ds(j, xn.shape[0], stride=n_chunk), :] = h2[:, j * LANES:(j + 1) * LANES]

    h_hi = h2.astype(BF16)
    h_lo = (h2 - h_hi.astype(F32)).astype(BF16)
    logits = (jnp.dot(h_hi, wrh_ref[...], preferred_element_type=F32)
              + jnp.dot(h_lo, wrh_ref[...], preferred_element_type=F32)
              + jnp.dot(h_hi, wrl_ref[...], preferred_element_type=F32)
              + br_ref[...])
    tm = logits.shape[0]
    lane = lax.broadcasted_iota(jnp.int32, (tm, LANES), 1)
    lane_f = lane.astype(F32)
    neg = jnp.float32(-jnp.inf)
    big = jnp.float32(1e9)

    gmask = lane < N_GROUPS
    gl = jnp.where(gmask, logits, neg)
    gmax = jnp.max(gl, axis=-1, keepdims=True)
    gsum = jnp.sum(jnp.where(gmask, jnp.exp(gl - gmax), 0.0), axis=-1, keepdims=True)
    g_top = 1.0 / gsum
    g_idx = jnp.min(jnp.where(gl == gmax, lane_f, big), axis=-1, keepdims=True)

    e_lo = N_GROUPS + EXPERTS_PER_GROUP * g_idx
    emask = (lane_f >= e_lo) & (lane_f < e_lo + EXPERTS_PER_GROUP)
    el = jnp.where(emask, logits, neg)
    emax = jnp.max(el, axis=-1, keepdims=True)
    eexp = jnp.where(emask, jnp.exp(el - emax), 0.0)
    prob = eexp / jnp.sum(eexp, axis=-1, keepdims=True)
    pm = jnp.where(emask, prob, -1.0)
    p1 = jnp.max(pm, axis=-1, keepdims=True)
    i1 = jnp.min(jnp.where(pm == p1, lane_f, big), axis=-1, keepdims=True)
    pm2 = jnp.where(lane_f == i1, -1.0, pm)
    p2 = jnp.max(pm2, axis=-1, keepdims=True)
    i2 = jnp.min(jnp.where(pm2 == p2, lane_f, big), axis=-1, keepdims=True)
    denom = p1 + p2
    gate1 = g_top * p1 / denom
    gate2 = g_top * p2 / denom

    sel1 = lane_f == i1
    sel2 = lane_f == i2
    onehot = jnp.where(sel1 | sel2, 1.0, 0.0)
    rr = lax.broadcasted_iota(jnp.int32, (tm, tm), 0)
    cc = lax.broadcasted_iota(jnp.int32, (tm, tm), 1)
    ltri = jnp.where(cc < rr, 1.0, 0.0).astype(BF16)
    prefix = jnp.dot(ltri, onehot.astype(BF16), preferred_element_type=F32) + carry_ref[...]
    rank1 = jnp.sum(jnp.where(sel1, prefix, 0.0), axis=-1, keepdims=True)
    rank2 = jnp.sum(jnp.where(sel2, prefix, 0.0), axis=-1, keepdims=True)
    carry_ref[...] = carry_ref[...] + jnp.sum(onehot, axis=0, keepdims=True)
    cnt_ref[...] = carry_ref[...]

    info = jnp.where(lane == 0, i1 - N_GROUPS,
                     jnp.where(lane == 1, i2 - N_GROUPS,
                               jnp.where(lane == 2, rank1, jnp.where(lane == 3, rank2, 0.0))))
    ri_ref[...] = info.astype(jnp.int32)
    rg_ref[...] = jnp.where(lane == 0, gate1, jnp.where(lane == 1, gate2, 0.0))


def _outproj_call(x2, od, om, op, p):
    t = x2.shape[0]
    tm = TM_OUT
    row = lambda i: (i, 0)
    return pl.pallas_call(
        _outproj_kernel, grid=(t // tm,),
        in_specs=[pl.BlockSpec((tm, D_MODEL), row), pl.BlockSpec((tm, DIFF_WIDTH), row),
                  pl.BlockSpec((tm, MLA_WIDTH), row), pl.BlockSpec((tm, POOL_WIDTH), row),
                  _full((D_MODEL, D_MODEL)), _full((1, D_MODEL)),
                  _full((D_MODEL, LANES)), _full((D_MODEL, LANES)), _full((1, LANES))],
        out_specs=[pl.BlockSpec((tm, D_MODEL), row),
                   pl.BlockSpec((tm * D_MODEL // LANES, LANES), row),
                   pl.BlockSpec((tm, LANES), row), pl.BlockSpec((tm, LANES), row),
                   _full((1, LANES))],
        out_shape=[jax.ShapeDtypeStruct((t, D_MODEL), F32),
                   jax.ShapeDtypeStruct((t * D_MODEL // LANES, LANES), F32),
                   jax.ShapeDtypeStruct((t, LANES), jnp.int32), jax.ShapeDtypeStruct((t, LANES), F32),
                   jax.ShapeDtypeStruct((1, LANES), F32)],
        scratch_shapes=[pltpu.VMEM((1, LANES), F32)],
        compiler_params=_cparams(1),
    )(x2, od, om, op, p["wo"], p["g2"], p["wrh"], p["wrl"], p["br"])


PAD_CHUNKS = (128, 64, 32, 16, 8, 4, 2, 1)


def _dispatch_kernel(pad_ref, idx_ref, h_ref, xs_hbm, zero_ref, sem, zsem):
    i = pl.program_id(0)
    n_chunk = D_MODEL // LANES
    tm = h_ref.shape[0] // n_chunk

    @pl.when(i == 0)
    def _():
        zero_ref[...] = jnp.zeros_like(zero_ref)

        def pad_copies(e, wait):
            off = pad_ref[0, e]
            cnt = pad_ref[1, e]
            for size in PAD_CHUNKS:
                take = cnt & size

                @pl.when(take != 0)
                def _():
                    cp = pltpu.make_async_copy(
                        zero_ref.at[pl.ds(0, size * n_chunk), :],
                        xs_hbm.at[pl.ds(pl.multiple_of(off * n_chunk, n_chunk), size * n_chunk), :], zsem)
                    if wait:
                        cp.wait()
                    else:
                        cp.start()
                off = off + take

        def start_body(e, c):
            pad_copies(e, False)
            return c

        def wait_body(e, c):
            pad_copies(e, True)
            return c

        lax.fori_loop(0, N_EXPERTS, start_body, 0)
        lax.fori_loop(0, N_EXPERTS, wait_body, 0)

        tail = pad_ref[0, N_EXPERTS]
        zrows = zero_ref.shape[0]

        def tail_copy(c):
            return pltpu.make_async_copy(
                zero_ref, xs_hbm.at[pl.ds(pl.multiple_of(tail * n_chunk + c * zrows, zrows), zrows), :], zsem)

        def tail_start(c, carry):
            tail_copy(c).start()
            return carry

        def tail_wait(c, carry):
            tail_copy(c).wait()
            return carry

        lax.fori_loop(0, pad_ref[1, N_EXPERTS], tail_start, 0)
        lax.fori_loop(0, pad_ref[1, N_EXPERTS], tail_wait, 0)

    for r in range(tm):
        for kk in range(2):
            dst = pl.multiple_of(idx_ref[0, 0, 2 * r + kk] * n_chunk, n_chunk)
            pltpu.make_async_copy(h_ref.at[pl.ds(r * n_chunk, n_chunk), :],
                                  xs_hbm.at[pl.ds(dst, n_chunk), :], sem).start(priority=kk)
    for kk in range(2):
        pltpu.make_async_copy(h_ref, xs_hbm.at[pl.ds(0, tm * n_chunk), :], sem).wait()


def _dispatch_call(pads, dest3, h3, n_slots):
    n_tiles = dest3.shape[0]
    tm = dest3.shape[2] // 2
    n_chunk = D_MODEL // LANES
    grid_spec = pltpu.PrefetchScalarGridSpec(
        num_scalar_prefetch=1, grid=(n_tiles,),
        in_specs=[pl.BlockSpec((1, 1, 2 * tm), lambda i, pads: (i, 0, 0), memory_space=pltpu.SMEM),
                  pl.BlockSpec((tm * n_chunk, LANES), lambda i, pads: (i, 0))],
        out_specs=pl.BlockSpec(memory_space=pl.ANY),
        scratch_shapes=[pltpu.VMEM((PAD_CHUNKS[0] * n_chunk, LANES), F32),
                        pltpu.SemaphoreType.DMA, pltpu.SemaphoreType.DMA])
    return pl.pallas_call(
        _dispatch_kernel, grid_spec=grid_spec,
        out_shape=jax.ShapeDtypeStruct((n_slots * n_chunk, LANES), F32),
        compiler_params=_cparams(1),
    )(pads, dest3, h3)


EXPERT_BLOCKS_PER_STEP = 2


def _expert_kernel(be_ref, nu_ref, xs_ref, *refs):
    i = pl.program_id(0)
    n_chunk = D_MODEL // LANES
    rows_per_block = ROUTE_BLOCK * n_chunk
    ys_ref = refs[-1]
    for b in range(EXPERT_BLOCKS_PER_STEP):
        wg_ref, wu_ref, wd_ref = refs[3 * b:3 * b + 3]
        base = b * rows_per_block
        block = i * EXPERT_BLOCKS_PER_STEP + b

        @pl.when(block < nu_ref[0])
        def _():
            xb = jnp.concatenate([xs_ref[pl.ds(base + j, ROUTE_BLOCK, stride=n_chunk), :]
                                  for j in range(n_chunk)], axis=1).astype(BF16)
            g = jnp.dot(xb, wg_ref[0, 0].astype(BF16), preferred_element_type=F32)
            u = jnp.dot(xb, wu_ref[0, 0].astype(BF16), preferred_element_type=F32)
            hmid = g * (1.0 / (1.0 + jnp.exp(-g))) * u
            y = jnp.dot(hmid.astype(BF16), wd_ref[0, 0].astype(BF16), preferred_element_type=F32)
            for j in range(n_chunk):
                ys_ref[pl.ds(base + j, ROUTE_BLOCK, stride=n_chunk), :] = y[:, j * LANES:(j + 1) * LANES]

        @pl.when(block >= nu_ref[0])
        def _():
            ys_ref[base:base + rows_per_block, :] = jnp.zeros((rows_per_block, LANES), F32)


def _expert_call(block_eid, n_used, xs3, wg, wu, wd, layer):
    per_step = EXPERT_BLOCKS_PER_STEP
    tile = (per_step * ROUTE_BLOCK * D_MODEL // LANES, LANES)
    n_steps = xs3.shape[0] // tile[0]
    row = lambda i, be, nu: (i, 0)
    weight_specs = []
    for b in range(per_step):
        wmap = lambda i, be, nu, b=b: (layer, be[i * per_step + b], 0, 0)
        weight_specs += [pl.BlockSpec((1, 1, D_MODEL, D_FF), wmap), pl.BlockSpec((1, 1, D_MODEL, D_FF), wmap),
                         pl.BlockSpec((1, 1, D_FF, D_MODEL), wmap)]
    grid_spec = pltpu.PrefetchScalarGridSpec(
        num_scalar_prefetch=2, grid=(n_steps,),
        in_specs=[pl.BlockSpec(tile, row)] + weight_specs,
        out_specs=pl.BlockSpec(tile, row))
    return pl.pallas_call(
        _expert_kernel, grid_spec=grid_spec,
        out_shape=jax.ShapeDtypeStruct(xs3.shape, F32),
        compiler_params=_cparams(1),
    )(block_eid, n_used, xs3, *([wg, wu, wd] * per_step))


def _combine_kernel(idx0_ref, idxn_ref, ys_hbm, x_ref, rg_ref, o_ref, buf0, buf1, sem):
    i = pl.program_id(0)
    n = pl.num_programs(0)
    tm = x_ref.shape[0]
    n_chunk = D_MODEL // LANES

    def issue(idx_ref, buf, sem_slot, rows):
        for r in rows:
            for kk in range(2):
                src = pl.multiple_of(idx_ref[0, 0, 2 * r + kk] * n_chunk, n_chunk)
                pltpu.make_async_copy(ys_hbm.at[pl.ds(src, n_chunk), :],
                                      buf.at[pl.ds((kk * tm + r) * n_chunk, n_chunk), :],
                                      sem_slot).start(priority=kk)

    def wait_tile(buf, sem_slot):
        pltpu.make_async_copy(ys_hbm.at[pl.ds(0, 2 * tm * n_chunk), :], buf, sem_slot).wait()

    @pl.when(i == 0)
    def _():
        issue(idx0_ref, buf0, sem.at[0], range(tm))

    def step(buf, sem_cur, buf_next, sem_next):
        issue(idxn_ref, buf_next, sem_next, range(tm))
        wait_tile(buf, sem_cur)
        rg = rg_ref[...]
        g0 = rg[:, 0:1]
        g1 = rg[:, 1:2]
        for j in range(n_chunk):
            cols = slice(j * LANES, (j + 1) * LANES)
            y0 = buf[pl.ds(j, tm, stride=n_chunk), :]
            y1 = buf[pl.ds(tm * n_chunk + j, tm, stride=n_chunk), :]
            o_ref[:, cols] = x_ref[:, cols] + g0 * y0 + g1 * y1

    @pl.when(i % 2 == 0)
    def _():
        step(buf0, sem.at[0], buf1, sem.at[1])

    @pl.when(i % 2 == 1)
    def _():
        step(buf1, sem.at[1], buf0, sem.at[0])

    @pl.when(i == n - 1)
    def _():
        @pl.when(i % 2 == 0)
        def _():
            wait_tile(buf1, sem.at[1])

        @pl.when(i % 2 == 1)
        def _():
            wait_tile(buf0, sem.at[0])


def _combine_call(dest3, ys, x2, rg):
    t = x2.shape[0]
    tm = TM_COMB
    n = t // tm
    row = lambda i: (i, 0)
    tile = (2 * tm * D_MODEL // LANES, LANES)
    return pl.pallas_call(
        _combine_kernel, grid=(n,),
        in_specs=[
            pl.BlockSpec((1, 1, 2 * tm), lambda i: (0, 0, 0), memory_space=pltpu.SMEM),
            pl.BlockSpec((1, 1, 2 * tm), lambda i: (jnp.minimum(i + 1, n - 1), 0, 0), memory_space=pltpu.SMEM),
            pl.BlockSpec(memory_space=pl.ANY),
            pl.BlockSpec((tm, D_MODEL), row), pl.BlockSpec((tm, LANES), row)],
        out_specs=pl.BlockSpec((tm, D_MODEL), row),
        out_shape=jax.ShapeDtypeStruct((t, D_MODEL), F32),
        scratch_shapes=[pltpu.VMEM(tile, F32), pltpu.VMEM(tile, F32), pltpu.SemaphoreType.DMA((2,))],
        compiler_params=_cparams(1),
    )(dest3, dest3, ys, x2, rg)


def _swap_halves(a):
    half = a.shape[-1] // 2
    return jnp.concatenate([a[..., half:], a[..., :half]], axis=-1)


def _layer_params(l, seq, w):
    p = {}
    row = lambda v: v.reshape(1, -1).astype(F32)
    w_in = w["w_in"][l]
    kr_cols = w_in[:, 1856:1888]
    p["win"] = jnp.concatenate(
        [w_in[:, 0:1536], w_in[:, 1888:2144], w_in[:, 1728:1856], w_in[:, 1536:1728],
         kr_cols, _swap_halves(kr_cols)], axis=1).astype(BF16)
    p["g1"] = row(w["norm1_g"][l])
    p["gq"] = row(jnp.tile(w["diff_q_norm_g"][l], 2) * (DIFF_QK ** -0.5))
    p["gk"] = row(jnp.tile(w["diff_k_norm_g"][l], 2))
    p["gckv"] = row(w["mla_kv_lat_norm_g"][l])
    gcq = w["mla_q_lat_norm_g"][l]
    p["gcqa"] = row(gcq[:LANES])
    p["gcqb"] = row(jnp.concatenate([gcq[LANES:], jnp.zeros((2 * LANES - MLA_Q_RANK,), F32)]))

    wuq = w["mla_w_uq"][l].reshape(MLA_Q_RANK, MLA_HEADS, MLA_NOPE + MLA_ROPE)
    rope_w = wuq[:, :, MLA_NOPE:]
    wuq = jnp.concatenate([wuq[:, :, :MLA_NOPE], rope_w, _swap_halves(rope_w)], axis=-1)
    wuq = wuq.reshape(MLA_Q_RANK, MLA_HEADS * HEAD_SLOT)
    wuq = jnp.concatenate([wuq, jnp.zeros((2 * LANES - MLA_Q_RANK, wuq.shape[1]), F32)], axis=0).astype(BF16)
    p["wuqa"] = wuq[:LANES]
    p["wuqb"] = wuq[LANES:]

    wukv = w["mla_w_ukv"][l].reshape(MLA_KV_RANK, MLA_HEADS, MLA_NOPE + MLA_V)
    zk = jnp.zeros((MLA_KV_RANK, MLA_HEADS, HEAD_SLOT - MLA_NOPE), F32)
    p["wkk"] = jnp.concatenate([wukv[:, :, :MLA_NOPE], zk], axis=-1).reshape(MLA_KV_RANK, -1).astype(BF16)
    vcols = wukv[:, :, MLA_NOPE:]
    zv = jnp.zeros_like(vcols)
    even = (jnp.arange(MLA_HEADS) % 2 == 0)[None, :, None]
    wkv = jnp.concatenate([jnp.where(even, vcols, zv), jnp.where(even, zv, vcols)], axis=-1)
    p["wkv"] = wkv.reshape(MLA_KV_RANK, -1).astype(BF16)
    p["gkn"] = row(jnp.concatenate([w["mla_k_nope_norm_g"][l], jnp.zeros((HEAD_SLOT - MLA_NOPE,), F32)]))

    inv = 1.0 / (ROPE_BASE ** (jnp.arange(0, MLA_ROPE, 2, dtype=F32) / MLA_ROPE))
    ang = jnp.arange(seq, dtype=F32)[:, None] * inv[None, :]
    cosf = jnp.concatenate([jnp.cos(ang), jnp.cos(ang)], axis=-1)
    sinf = jnp.concatenate([-jnp.sin(ang), jnp.sin(ang)], axis=-1)
    scale = (MLA_NOPE + MLA_ROPE) ** -0.5 * math.log2(math.e)
    gqr = w["mla_q_rope_norm_g"][l]
    q_head = jnp.concatenate([jnp.broadcast_to(w["mla_q_nope_norm_g"][l][None, :], (seq, MLA_NOPE)),
                              gqr[None, :] * cosf, _swap_halves(gqr)[None, :] * sinf], axis=-1) * scale
    p["qtab"] = jnp.tile(q_head, (1, MLA_HEADS))
    gkr = w["mla_k_rope_norm_g"][l]
    p["ktab"] = jnp.concatenate([jnp.zeros((seq, MLA_NOPE), F32), gkr[None, :] * cosf,
                                 _swap_halves(gkr)[None, :] * sinf], axis=-1)
    src = jnp.arange(LANES)
    dst = jnp.arange(MLA_HEADS * HEAD_SLOT)
    src_j = jnp.where(src >= MLA_NOPE, (src - MLA_NOPE) % MLA_ROPE, -1)
    dst_l = dst % HEAD_SLOT
    dst_j = jnp.where(dst_l >= MLA_NOPE, (dst_l - MLA_NOPE) % MLA_ROPE, -2)
    p["eplace"] = (src_j[:, None] == dst_j[None, :]).astype(BF16)

    pw = w["pool_w"][l]
    bd = jnp.zeros((POOL_WIDTH, POOL_WIDTH), F32)
    for g in range(POOL_GROUPS):
        s0 = g * POOL_GROUP_DIM
        bd = bd.at[s0:s0 + POOL_GROUP_DIM, s0:s0 + POOL_GROUP_DIM].set(pw[g])
    p["pool_w"] = bd.astype(BF16)
    p["pool_scale"] = row(w["pool_scale"][l])

    lam_init = 0.8 - 0.6 * math.exp(-0.3 * l)
    lv = w["diff_lambda"][l].astype(F32)
    p["lam"] = (jnp.exp(jnp.sum(lv[0] * lv[1])) - jnp.exp(jnp.sum(lv[2] * lv[3])) + lam_init).reshape(1)
    p["gsub"] = row(w["diff_sub_norm_g"][l] * (1.0 - lam_init))

    p["wo"] = w["w_out"][l].astype(BF16)
    p["g2"] = row(w["norm2_g"][l])
    wr = jnp.concatenate([w["router_group_w"][l], w["router_expert_w"][l],
                          jnp.zeros((D_MODEL, LANES - N_GROUPS - N_EXPERTS), F32)], axis=1)
    wr_hi = wr.astype(BF16)
    p["wrh"] = wr_hi
    p["wrl"] = (wr - wr_hi.astype(F32)).astype(BF16)
    p["br"] = row(jnp.concatenate([w["router_group_b"][l], w["router_expert_b"][l],
                                   jnp.zeros((LANES - N_GROUPS - N_EXPERTS,), F32)]))
    return p


def kernel(x, norm1_g, w_in, diff_q_norm_g, diff_k_norm_g, diff_lambda, diff_sub_norm_g, mla_q_lat_norm_g, mla_kv_lat_norm_g, mla_w_uq, mla_w_ukv, mla_q_nope_norm_g, mla_q_rope_norm_g, mla_k_nope_norm_g, mla_k_rope_norm_g, pool_w, pool_scale, w_out, norm2_g, router_group_w, router_group_b, router_expert_w, router_expert_b, expert_w_gate, expert_w_up, expert_w_down):
    w = dict(norm1_g=norm1_g, w_in=w_in, diff_q_norm_g=diff_q_norm_g, diff_k_norm_g=diff_k_norm_g,
             diff_lambda=diff_lambda, diff_sub_norm_g=diff_sub_norm_g, mla_q_lat_norm_g=mla_q_lat_norm_g,
             mla_kv_lat_norm_g=mla_kv_lat_norm_g, mla_w_uq=mla_w_uq, mla_w_ukv=mla_w_ukv,
             mla_q_nope_norm_g=mla_q_nope_norm_g, mla_q_rope_norm_g=mla_q_rope_norm_g,
             mla_k_nope_norm_g=mla_k_nope_norm_g, mla_k_rope_norm_g=mla_k_rope_norm_g,
             pool_w=pool_w, pool_scale=pool_scale, w_out=w_out, norm2_g=norm2_g,
             router_group_w=router_group_w, router_group_b=router_group_b,
             router_expert_w=router_expert_w, router_expert_b=router_expert_b)
    batch, seq, d = x.shape
    t = batch * seq
    n_assign = 2 * t
    n_blocks = n_assign // ROUTE_BLOCK + N_EXPERTS
    alibi = _alibi_tables(seq)

    x2 = x.reshape(t, d)
    for l in range(DEPTH):
        p = _layer_params(l, seq, w)
        dq1, dq2, dk, dv, qm, km, vm, pu = _proj_call(x2, p, seq)
        o_diff = _diff_call(p["lam"], dq1, dq2, dk, dv, p["gsub"], alibi, batch, seq)
        o_mla = _mla_call(qm, km, vm, batch, seq)
        o_pool = _pool_call(pu, p["pool_w"], p["pool_scale"], batch, seq)
        x2, h2, route_i, route_g, counts = _outproj_call(x2, o_diff, o_mla, o_pool, p)

        cnt = counts[0, N_GROUPS:N_GROUPS + N_EXPERTS].astype(jnp.int32)
        padded = (cnt + ROUTE_BLOCK - 1) // ROUTE_BLOCK * ROUTE_BLOCK
        padded_ends = jnp.cumsum(padded)
        padded_starts = padded_ends - padded
        eid = route_i[:, 0:2]
        start_of = jnp.sum(jnp.where(eid[..., None] == jnp.arange(N_EXPERTS, dtype=jnp.int32),
                                     padded_starts, 0), axis=-1)
        dest = start_of + route_i[:, 2:4]
        block_start = jnp.arange(n_blocks, dtype=jnp.int32) * ROUTE_BLOCK
        block_eid = jnp.minimum(jnp.sum(block_start[:, None] >= padded_ends[None, :], axis=1),
                                N_EXPERTS - 1).astype(jnp.int32)
        n_used = (padded_ends[-1] // ROUTE_BLOCK).astype(jnp.int32).reshape(1)
        n_tail = (n_blocks - n_used) * (ROUTE_BLOCK // PAD_CHUNKS[0])
        pads = jnp.stack([jnp.concatenate([padded_starts + cnt, padded_ends[-1:]]),
                          jnp.concatenate([padded - cnt, n_tail])]).astype(jnp.int32)
        dest3 = dest.reshape(t // TM_COMB, 1, 2 * TM_COMB)

        xs = _dispatch_call(pads, dest3, h2, n_blocks * ROUTE_BLOCK)
        ys = _expert_call(block_eid, n_used, xs, expert_w_gate, expert_w_up, expert_w_down, l)
        x2 = _combine_call(dest3, ys, x2, route_g)
    return x2.reshape(batch, seq, d)
```

```python
import functools
import math

import jax
import jax.numpy as jnp
from jax import lax
from jax.experimental import pallas as pl
from jax.experimental.pallas import tpu as pltpu

F32 = jnp.float32
BF16 = jnp.bfloat16

D_MODEL = 1024
DEPTH = 2
DIFF_HEADS = 4
DIFF_QK = 64
DIFF_V = 128
DIFF_WIDTH = 512
MLA_HEADS = 4
MLA_NOPE = 64
MLA_ROPE = 32
MLA_V = 64
MLA_Q_RANK = 192
MLA_KV_RANK = 128
MLA_WIDTH = 256
ROPE_BASE = 10000.0
POOL_WIDTH = 256
POOL_GROUPS = 4
POOL_GROUP_DIM = 64
POOL_WINDOWS = (2, 4, 8, 16)
N_GROUPS = 4
EXPERTS_PER_GROUP = 8
N_EXPERTS = 32
D_FF = 256
ROUTE_BLOCK = 256
RMS_EPS = 1e-6

LANES = 128
HEAD_SLOT = 128
PROJ_WIDTH = 2176
POOL_PAD = 16
VMEM_LIMIT = 48 * 1024 * 1024

TM_PROJ = 512
TQ = 512
TK_DIFF = 256
TK_MLA = 512
TM_OUT = 1024
TM_COMB = 256

NT_DIMS = (((1,), (1,)), ((), ()))


def _cparams(n_axes):
    return pltpu.CompilerParams(dimension_semantics=("arbitrary",) * n_axes,
                                vmem_limit_bytes=VMEM_LIMIT)


def _full(shape):
    return pl.BlockSpec(shape, lambda *_: (0,) * len(shape))


def _proj_kernel(x_ref, g1_ref, win_ref, gq_ref, gk_ref, gckv_ref, gcqa_ref, gcqb_ref,
                 wuqa_ref, wuqb_ref, wkk_ref, wkv_ref, gkn_ref, qtab_ref, ktab_ref, eplace_ref,
                 dq1_ref, dq2_ref, dk_ref, dv_ref, qm_ref, km_ref, vm_ref, pu_ref):
    x = x_ref[...]
    xn = x * lax.rsqrt(jnp.mean(x * x, axis=-1, keepdims=True) + RMS_EPS) * g1_ref[...]
    proj = jnp.dot(xn.astype(BF16), win_ref[...], preferred_element_type=F32)

    tm = x.shape[0]
    lane = lax.broadcasted_iota(jnp.int32, (tm, LANES), 1)
    lo = lane < DIFF_QK

    def half_norm(c, g_row):
        sq = c * c
        s_lo = jnp.sum(jnp.where(lo, sq, 0.0), axis=-1, keepdims=True)
        s_hi = jnp.sum(jnp.where(lo, 0.0, sq), axis=-1, keepdims=True)
        r = jnp.where(lo, lax.rsqrt(s_lo / DIFF_QK + RMS_EPS), lax.rsqrt(s_hi / DIFF_QK + RMS_EPS))
        return c * r * g_row

    for h in range(DIFF_HEADS):
        sl = slice(h * HEAD_SLOT, (h + 1) * HEAD_SLOT)
        qn = half_norm(proj[:, sl], gq_ref[...])
        dq1_ref[:, sl] = jnp.where(lo, qn, 0.0).astype(BF16)
        dq2_ref[:, sl] = jnp.where(lo, 0.0, qn).astype(BF16)
        ksl = slice(512 + h * HEAD_SLOT, 512 + (h + 1) * HEAD_SLOT)
        dk_ref[:, sl] = half_norm(proj[:, ksl], gk_ref[...]).astype(BF16)
    dv_ref[...] = proj[:, 1024:1536].astype(BF16)
    pu_ref[...] = proj[:, 1536:1792]

    ckv = proj[:, 1792:1920]
    ckvn = ckv * lax.rsqrt(jnp.mean(ckv * ckv, axis=-1, keepdims=True) + RMS_EPS) * gckv_ref[...]
    ckvn = ckvn.astype(BF16)
    cqa = proj[:, 1920:2048]
    last = proj[:, 2048:2176]
    lsq = last * last
    ss_q = (jnp.sum(cqa * cqa, axis=-1, keepdims=True)
            + jnp.sum(jnp.where(lo, lsq, 0.0), axis=-1, keepdims=True))
    r_q = lax.rsqrt(ss_q / MLA_Q_RANK + RMS_EPS)
    q_raw = (jnp.dot((cqa * r_q * gcqa_ref[...]).astype(BF16), wuqa_ref[...], preferred_element_type=F32)
             + jnp.dot((last * r_q * gcqb_ref[...]).astype(BF16), wuqb_ref[...], preferred_element_type=F32))

    rope_lanes = (lane >= MLA_NOPE) & (lane < MLA_NOPE + MLA_ROPE)
    ss_kr = jnp.sum(jnp.where(rope_lanes, lsq, 0.0), axis=-1, keepdims=True)
    kr_terms = last * lax.rsqrt(ss_kr / MLA_ROPE + RMS_EPS) * ktab_ref[...]
    kr_placed = jnp.dot(kr_terms.astype(BF16), eplace_ref[...], preferred_element_type=F32)

    k_raw = jnp.dot(ckvn, wkk_ref[...], preferred_element_type=F32)
    vm_ref[...] = jnp.dot(ckvn, wkv_ref[...], preferred_element_type=F32).astype(BF16)
    qtab = qtab_ref[...]
    for h in range(MLA_HEADS):
        sl = slice(h * HEAD_SLOT, (h + 1) * HEAD_SLOT)
        c = q_raw[:, sl]
        sq = c * c
        s_n = jnp.sum(jnp.where(lo, sq, 0.0), axis=-1, keepdims=True)
        s_r = jnp.sum(jnp.where(rope_lanes, sq, 0.0), axis=-1, keepdims=True)
        r = jnp.where(lo, lax.rsqrt(s_n / MLA_NOPE + RMS_EPS), lax.rsqrt(s_r / MLA_ROPE + RMS_EPS))
        qm_ref[:, sl] = (c * r * qtab[:, sl]).astype(BF16)
        kc = k_raw[:, sl]
        r_k = lax.rsqrt(jnp.sum(kc * kc, axis=-1, keepdims=True) / MLA_NOPE + RMS_EPS)
        km_ref[:, sl] = (kc * r_k * gkn_ref[...] + kr_placed[:, sl]).astype(BF16)


def _proj_call(x2, p, seq):
    t = x2.shape[0]
    tm = TM_PROJ
    n_pos = seq // tm
    row = lambda i: (i, 0)
    pos = lambda i: (i % n_pos, 0)
    bf = lambda w: jax.ShapeDtypeStruct((t, w), BF16)
    in_specs = [
        pl.BlockSpec((tm, D_MODEL), row),
        _full((1, D_MODEL)), _full((D_MODEL, PROJ_WIDTH)),
        _full((1, LANES)), _full((1, LANES)), _full((1, LANES)), _full((1, LANES)), _full((1, LANES)),
        _full((LANES, 512)), _full((LANES, 512)), _full((LANES, 512)), _full((LANES, 512)),
        _full((1, LANES)),
        pl.BlockSpec((tm, 512), pos), pl.BlockSpec((tm, LANES), pos),
        _full((LANES, 512)),
    ]
    out_specs = [pl.BlockSpec((tm, 512), row)] * 7 + [pl.BlockSpec((tm, POOL_WIDTH), row)]
    out_shape = [bf(512)] * 7 + [jax.ShapeDtypeStruct((t, POOL_WIDTH), F32)]
    return pl.pallas_call(
        _proj_kernel, grid=(t // tm,), in_specs=in_specs, out_specs=out_specs, out_shape=out_shape,
        compiler_params=_cparams(1),
    )(x2, p["g1"], p["win"], p["gq"], p["gk"], p["gckv"], p["gcqa"], p["gcqb"],
      p["wuqa"], p["wuqb"], p["wkk"], p["wkv"], p["gkn"], p["qtab"], p["ktab"], p["eplace"])


def _unflatten(n, sizes):
    n = jnp.minimum(n, math.prod(sizes) - 1)
    coords = []
    for size in reversed(sizes):
        coords.append(n % size)
        n = n // size
    return tuple(reversed(coords))


def _two_stage(n, stage, bufs):
    (s0, m0), (s1, m1) = bufs

    @pl.when(n == 0)
    def _():
        s1[...] = jnp.zeros_like(s1)
        m1[...] = jnp.zeros_like(m1)

    @pl.when(n % 2 == 0)
    def _():
        stage((s0, m0), (s1, m1))

    @pl.when(n % 2 == 1)
    def _():
        stage((s1, m1), (s0, m0))


def _two_stage_scratch(seq):
    pair = [pltpu.VMEM((2 * TQ, seq), F32), pltpu.VMEM((2 * TQ, LANES), F32)]
    return pair + pair


def _softmax_pv_tile(s_prev, m_rows, c, tq, tk, lsum, acc, v_tiles, exp_fn):
    n_half = tk // LANES
    ps = [exp_fn(s_prev[:, (c * n_half + j) * LANES:(c * n_half + j + 1) * LANES] - m_rows)
          for j in range(n_half)]
    for ch in ps:
        lsum = ch if lsum is None else lsum + ch
    pb = jnp.concatenate(ps, axis=1).astype(BF16)
    for g, vt in enumerate(v_tiles):
        pv = jnp.dot(pb[g * tq:(g + 1) * tq], vt, preferred_element_type=F32)
        acc[g] = pv if acc[g] is None else acc[g] + pv
    return lsum


def _running_max(mx, sc):
    for j in range(sc.shape[1] // LANES):
        chunk = sc[:, j * LANES:(j + 1) * LANES]
        mx = chunk if mx is None else jnp.maximum(mx, chunk)
    return mx


def _diff_kernel(lam_ref, q1_ref, q2_ref, qx_ref, k_ref, kx_ref, bd_ref, v_ref, gsub_ref, o_ref,
                 s0_ref, m0_ref, s1_ref, m1_ref, *, sizes):
    n = pl.program_id(0)
    tq = q1_ref.shape[0]
    tk = TK_DIFF
    n_kt = k_ref.shape[0] // tk
    n_diag = tq // tk
    first_cur = _unflatten(n, sizes)[2] * n_diag
    first_prev = _unflatten(jnp.maximum(n - 1, 0), sizes)[2] * n_diag

    def stage(cur, prev):
        s_cur, m_cur = cur
        s_prev, m_prev = prev
        q1, q2 = q1_ref[...], q2_ref[...]
        qx_left = qx_ref[0, 0]
        qx_right = -qx_left
        bd = bd_ref[0]
        bd2 = jnp.concatenate([bd, bd], axis=0)
        mx = None
        m_rows = m_prev[...]
        lsum = None
        acc = [None, None]
        for c in range(n_kt):
            start = pl.multiple_of(((first_prev + c) % n_kt) * tk, tk)
            vt = v_ref[pl.ds(start, tk), :]
            lsum = _softmax_pv_tile(s_prev, m_rows, c, tq, tk, lsum, acc, [vt, vt], jnp.exp)

            start = pl.multiple_of(((first_cur + c) % n_kt) * tk, tk)
            if c < n_diag:
                qx = jnp.zeros_like(qx_left)
            else:
                qx = jnp.where(first_cur + c >= n_kt, qx_left, qx_right)
            qq = jnp.concatenate([jnp.concatenate([q1, qx], axis=1),
                                  jnp.concatenate([q2, qx], axis=1)], axis=0)
            kk = jnp.concatenate([k_ref[pl.ds(start, tk), :], kx_ref[pl.ds(start, tk), :]], axis=1)
            sc = lax.dot_general(qq, kk, NT_DIMS, preferred_element_type=F32)
            if c < n_diag:
                sc = sc + bd2[:, c * tk:(c + 1) * tk]
            s_cur[:, c * tk:(c + 1) * tk] = sc
            mx = _running_max(mx, sc)
        m_cur[...] = jnp.broadcast_to(jnp.max(mx, axis=-1, keepdims=True), m_cur.shape)
        l = jnp.sum(lsum, axis=-1, keepdims=True)
        o = acc[0] * (1.0 / l[0:tq]) - acc[1] * (lam_ref[0] / l[tq:2 * tq])
        r = lax.rsqrt(jnp.mean(o * o, axis=-1, keepdims=True) + RMS_EPS)
        o_ref[...] = (o * r * gsub_ref[...]).astype(BF16)

    _two_stage(n, stage, ((s0_ref, m0_ref), (s1_ref, m1_ref)))


def _alibi_tables(seq):
    nq = seq // TQ
    slopes = 2.0 ** (-8.0 * jnp.arange(1, DIFF_HEADS + 1, dtype=F32) / DIFF_HEADS)
    pos = jnp.arange(seq, dtype=jnp.int32)
    hi = (pos // 256).astype(F32)
    lo = (pos % 256).astype(F32)
    s4 = slopes[:, None]
    ones = jnp.ones((DIFF_HEADS, seq), F32)
    q_left = jnp.stack([-s4 * 256.0 * hi[None], -s4 * lo[None], s4 * 256.0 * ones, s4 * ones], axis=-1)
    qx = jnp.concatenate([q_left, jnp.zeros((DIFF_HEADS, seq, HEAD_SLOT - 4), F32)], axis=-1)
    qx = qx.reshape(DIFF_HEADS, nq, TQ, HEAD_SLOT).astype(BF16)
    k_cols = jnp.stack([jnp.ones((seq,), F32), jnp.ones((seq,), F32), hi, lo], axis=-1)
    kx = jnp.concatenate([k_cols, jnp.zeros((seq, HEAD_SLOT - 4), F32)], axis=-1).astype(BF16)
    loc = jnp.arange(TQ, dtype=jnp.int32)
    bd = -slopes[:, None, None] * jnp.abs(loc[:, None] - loc[None, :]).astype(F32)[None]
    return qx, kx, bd


def _diff_call(lam, dq1, dq2, dk, dv, gsub, tabs, batch, seq):
    t = dq1.shape[0]
    nq = seq // TQ
    qx, kx, bd = tabs
    sizes = (batch, DIFF_HEADS, nq)
    cur = lambda n: _unflatten(n, sizes)
    prev = lambda n: _unflatten(jnp.maximum(n - 1, 0), sizes)

    def qmap(n, *_):
        b, h, qi = cur(n)
        return (b * nq + qi, h)

    def kmap(n, *_):
        b, h, qi = cur(n)
        return (b, h)

    def vmap(n, *_):
        b, h, qi = prev(n)
        return (b, h)

    def omap(n, *_):
        b, h, qi = prev(n)
        return (b * nq + qi, h)

    grid_spec = pltpu.PrefetchScalarGridSpec(
        num_scalar_prefetch=1, grid=(math.prod(sizes) + 1,),
        in_specs=[pl.BlockSpec((TQ, HEAD_SLOT), qmap), pl.BlockSpec((TQ, HEAD_SLOT), qmap),
                  pl.BlockSpec((1, 1, TQ, HEAD_SLOT), lambda n, *_: cur(n)[1:] + (0, 0)),
                  pl.BlockSpec((seq, HEAD_SLOT), kmap),
                  pl.BlockSpec((seq, HEAD_SLOT), lambda *_: (0, 0)),
                  pl.BlockSpec((1, TQ, TQ), lambda n, *_: (cur(n)[1], 0, 0)),
                  pl.BlockSpec((seq, HEAD_SLOT), vmap),
                  pl.BlockSpec((1, HEAD_SLOT), lambda *_: (0, 0))],
        out_specs=pl.BlockSpec((TQ, HEAD_SLOT), omap),
        scratch_shapes=_two_stage_scratch(seq))
    return pl.pallas_call(
        functools.partial(_diff_kernel, sizes=sizes), grid_spec=grid_spec,
        out_shape=jax.ShapeDtypeStruct((t, DIFF_WIDTH), BF16),
        compiler_params=_cparams(1),
    )(lam, dq1, dq2, qx, dk, kx, bd, dv, gsub)


def _mla_kernel(q_ref, k_ref, v_ref, o_ref, s0_ref, m0_ref, s1_ref, m1_ref):
    n = pl.program_id(0)
    tq = q_ref.shape[0]
    tk = TK_MLA
    n_kt = k_ref.shape[0] // tk

    def stage(cur, prev):
        s_cur, m_cur = cur
        s_prev, m_prev = prev
        mx = [None, None]
        m_rows = m_prev[...]
        lsum = None
        acc = [None, None]
        for c in range(n_kt):
            rows = slice(c * tk, (c + 1) * tk)
            v_tiles = [v_ref[rows, hh * HEAD_SLOT:(hh + 1) * HEAD_SLOT] for hh in range(2)]
            lsum = _softmax_pv_tile(s_prev, m_rows, c, tq, tk, lsum, acc, v_tiles, jnp.exp2)
            for hh in range(2):
                sl = slice(hh * HEAD_SLOT, (hh + 1) * HEAD_SLOT)
                sc = lax.dot_general(q_ref[:, sl], k_ref[rows, sl], NT_DIMS, preferred_element_type=F32)
                s_cur[hh * tq:(hh + 1) * tq, rows] = sc
                mx[hh] = _running_max(mx[hh], sc)
        mx = jnp.concatenate(mx, axis=0)
        m_cur[...] = jnp.broadcast_to(jnp.max(mx, axis=-1, keepdims=True), m_cur.shape)
        l = jnp.sum(lsum, axis=-1, keepdims=True)
        o = acc[0] * (1.0 / l[0:tq]) + acc[1] * (1.0 / l[tq:2 * tq])
        o_ref[...] = o.astype(BF16)

    _two_stage(n, stage, ((s0_ref, m0_ref), (s1_ref, m1_ref)))


def _mla_call(qm, km, vm, batch, seq):
    t = qm.shape[0]
    nq = seq // TQ
    sizes = (batch, MLA_HEADS // 2, nq)
    cur = lambda n: _unflatten(n, sizes)
    prev = lambda n: _unflatten(jnp.maximum(n - 1, 0), sizes)

    def qmap(n):
        b, p, qi = cur(n)
        return (b * nq + qi, p)

    def kmap(n):
        b, p, qi = cur(n)
        return (b, p)

    def vmap(n):
        b, p, qi = prev(n)
        return (b, p)

    def omap(n):
        b, p, qi = prev(n)
        return (b * nq + qi, p)

    return pl.pallas_call(
        _mla_kernel, grid=(math.prod(sizes) + 1,),
        in_specs=[pl.BlockSpec((TQ, 2 * HEAD_SLOT), qmap), pl.BlockSpec((seq, 2 * HEAD_SLOT), kmap),
                  pl.BlockSpec((seq, 2 * HEAD_SLOT), vmap)],
        out_specs=pl.BlockSpec((TQ, HEAD_SLOT), omap),
        out_shape=jax.ShapeDtypeStruct((t, MLA_WIDTH), BF16),
        scratch_shapes=_two_stage_scratch(seq),
        compiler_params=_cparams(1),
    )(qm, km, vm)


def _pool_kernel(u_ref, w_ref, scale_ref, o_ref):
    u = u_ref[...]
    seq, width = u.shape
    zpad = jnp.zeros((POOL_PAD, width), F32)
    ue = jnp.concatenate([zpad, u, zpad], axis=0)
    n = seq + 2 * POOL_PAD

    def down(a, k):
        return pltpu.roll(a, k, axis=0)

    def up(a, k):
        return pltpu.roll(a, n - k, axis=0)

    a2 = ue + down(ue, 1)
    a4 = down(a2, 1) + up(a2, 1)
    a8 = down(a4, 2) + up(a4, 2)
    a16 = down(a8, 4) + up(a8, 4)
    core = slice(POOL_PAD, POOL_PAD + seq)
    lane = lax.broadcasted_iota(jnp.int32, (seq, width), 1)
    tpos = lax.broadcasted_iota(jnp.int32, (seq, width), 0)
    grp = lane // POOL_GROUP_DIM
    win_sum = jnp.where(grp == 0, a2[core], jnp.where(grp == 1, a4[core], jnp.where(grp == 2, a8[core], a16[core])))
    half = jnp.where(grp == 0, 1, jnp.where(grp == 1, 2, jnp.where(grp == 2, 4, 8)))
    lo_i = jnp.maximum(tpos - half, 0)
    hi_i = jnp.minimum(tpos + half - 1, seq - 1)
    cnt = (hi_i - lo_i + 1).astype(F32)
    pooled = win_sum / cnt - u
    mixed = jnp.dot(pooled.astype(BF16), w_ref[...], preferred_element_type=F32)
    o_ref[...] = (mixed * scale_ref[...]).astype(BF16)


def _pool_call(pu, w_bd, scale, batch, seq):
    t = pu.shape[0]
    return pl.pallas_call(
        _pool_kernel, grid=(batch,),
        in_specs=[pl.BlockSpec((seq, POOL_WIDTH), lambda b: (b, 0)),
                  _full((POOL_WIDTH, POOL_WIDTH)), _full((1, POOL_WIDTH))],
        out_specs=pl.BlockSpec((seq, POOL_WIDTH), lambda b: (b, 0)),
        out_shape=jax.ShapeDtypeStruct((t, POOL_WIDTH), BF16),
        compiler_params=_cparams(1),
    )(pu, w_bd, scale)


def _outproj_kernel(x_ref, od_ref, om_ref, op_ref, wo_ref, g2_ref, wrh_ref, wrl_ref, br_ref,
                    xo_ref, h2_ref, ri_ref, rg_ref, cnt_ref, carry_ref):
    i = pl.program_id(0)

    @pl.when(i == 0)
    def _():
        carry_ref[...] = jnp.zeros_like(carry_ref)

    xn = (x_ref[...]
          + jnp.dot(od_ref[...], wo_ref[0:DIFF_WIDTH, :], preferred_element_type=F32)
          + jnp.dot(om_ref[...], wo_ref[DIFF_WIDTH:DIFF_WIDTH + MLA_WIDTH, :], preferred_element_type=F32)
          + jnp.dot(op_ref[...], wo_ref[DIFF_WIDTH + MLA_WIDTH:, :], preferred_element_type=F32))
    xo_ref[...] = xn
    h2 = xn * lax.rsqrt(jnp.mean(xn * xn, axis=-1, keepdims=True) + RMS_EPS) * g2_ref[...]
    n_chunk = D_MODEL // LANES
    for j in range(n_chunk):
        h2_ref[pl.ds(j, xn.shape[0], stride=n_chunk), :] = h2[:, j * LANES:(j + 1) * LANES]

    h_hi = h2.astype(BF16)
    h_lo = (h2 - h_hi.astype(F32)).astype(BF16)
    logits = (jnp.dot(h_hi, wrh_ref[...], preferred_element_type=F32)
              + jnp.dot(h_lo, wrh_ref[...], preferred_element_type=F32)
              + jnp.dot(h_hi, wrl_ref[...], preferred_element_type=F32)
              + br_ref[...])
    tm = logits.shape[0]
    lane = lax.broadcasted_iota(jnp.int32, (tm, LANES), 1)
    lane_f = lane.astype(F32)
    neg = jnp.float32(-jnp.inf)
    big = jnp.float32(1e9)

    gmask = lane < N_GROUPS
    gl = jnp.where(gmask, logits, neg)
    gmax = jnp.max(gl, axis=-1, keepdims=True)
    gsum = jnp.sum(jnp.where(gmask, jnp.exp(gl - gmax), 0.0), axis=-1, keepdims=True)
    g_top = 1.0 / gsum
    g_idx = jnp.min(jnp.where(gl == gmax, lane_f, big), axis=-1, keepdims=True)

    e_lo = N_GROUPS + EXPERTS_PER_GROUP * g_idx
    emask = (lane_f >= e_lo) & (lane_f < e_lo + EXPERTS_PER_GROUP)
    el = jnp.where(emask, logits, neg)
    emax = jnp.max(el, axis=-1, keepdims=True)
    eexp = jnp.where(emask, jnp.exp(el - emax), 0.0)
    prob = eexp / jnp.sum(eexp, axis=-1, keepdims=True)
    pm = jnp.where(emask, prob, -1.0)
    p1 = jnp.max(pm, axis=-1, keepdims=True)
    i1 = jnp.min(jnp.where(pm == p1, lane_f, big), axis=-1, keepdims=True)
    pm2 = jnp.where(lane_f == i1, -1.0, pm)
    p2 = jnp.max(pm2, axis=-1, keepdims=True)
    i2 = jnp.min(jnp.where(pm2 == p2, lane_f, big), axis=-1, keepdims=True)
    denom = p1 + p2
    gate1 = g_top * p1 / denom
    gate2 = g_top * p2 / denom

    sel1 = lane_f == i1
    sel2 = lane_f == i2
    onehot = jnp.where(sel1 | sel2, 1.0, 0.0)
    rr = lax.broadcasted_iota(jnp.int32, (tm, tm), 0)
    cc = lax.broadcasted_iota(jnp.int32, (tm, tm), 1)
    ltri = jnp.where(cc < rr, 1.0, 0.0).astype(BF16)
    prefix = jnp.dot(ltri, onehot.astype(BF16), preferred_element_type=F32) + carry_ref[...]
    rank1 = jnp.sum(jnp.where(sel1, prefix, 0.0), axis=-1, keepdims=True)
    rank2 = jnp.sum(jnp.where(sel2, prefix, 0.0), axis=-1, keepdims=True)
    carry_ref[...] = carry_ref[...] + jnp.sum(onehot, axis=0, keepdims=True)
    cnt_ref[...] = carry_ref[...]

    info = jnp.where(lane == 0, i1 - N_GROUPS,
                     jnp.where(lane == 1, i2 - N_GROUPS,
                               jnp.where(lane == 2, rank1, jnp.where(lane == 3, rank2, 0.0))))
    ri_ref[...] = info.astype(jnp.int32)
    rg_ref[...] = jnp.where(lane == 0, gate1, jnp.where(lane == 1, gate2, 0.0))


def _outproj_call(x2, od, om, op, p):
    t = x2.shape[0]
    tm = TM_OUT
    row = lambda i: (i, 0)
    return pl.pallas_call(
        _outproj_kernel, grid=(t // tm,),
        in_specs=[pl.BlockSpec((tm, D_MODEL), row), pl.BlockSpec((tm, DIFF_WIDTH), row),
                  pl.BlockSpec((tm, MLA_WIDTH), row), pl.BlockSpec((tm, POOL_WIDTH), row),
                  _full((D_MODEL, D_MODEL)), _full((1, D_MODEL)),
                  _full((D_MODEL, LANES)), _full((D_MODEL, LANES)), _full((1, LANES))],
        out_specs=[pl.BlockSpec((tm, D_MODEL), row),
                   pl.BlockSpec((tm * D_MODEL // LANES, LANES), row),
                   pl.BlockSpec((tm, LANES), row), pl.BlockSpec((tm, LANES), row),
                   _full((1, LANES))],
        out_shape=[jax.ShapeDtypeStruct((t, D_MODEL), F32),
                   jax.ShapeDtypeStruct((t * D_MODEL // LANES, LANES), F32),
                   jax.ShapeDtypeStruct((t, LANES), jnp.int32), jax.ShapeDtypeStruct((t, LANES), F32),
                   jax.ShapeDtypeStruct((1, LANES), F32)],
        scratch_shapes=[pltpu.VMEM((1, LANES), F32)],
        compiler_params=_cparams(1),
    )(x2, od, om, op, p["wo"], p["g2"], p["wrh"], p["wrl"], p["br"])


PAD_CHUNKS = (128, 64, 32, 16, 8, 4, 2, 1)


def _dispatch_kernel(pad_ref, idx_ref, h_ref, xs_hbm, zero_ref, sem, zsem):
    i = pl.program_id(0)
    n_chunk = D_MODEL // LANES
    tm = h_ref.shape[0] // n_chunk

    @pl.when(i == 0)
    def _():
        zero_ref[...] = jnp.zeros_like(zero_ref)

        def pad_copies(e, wait):
            off = pad_ref[0, e]
            cnt = pad_ref[1, e]
            for size in PAD_CHUNKS:
                take = cnt & size

                @pl.when(take != 0)
                def _():
                    cp = pltpu.make_async_copy(
                        zero_ref.at[pl.ds(0, size * n_chunk), :],
                        xs_hbm.at[pl.ds(pl.multiple_of(off * n_chunk, n_chunk), size * n_chunk), :], zsem)
                    if wait:
                        cp.wait()
                    else:
                        cp.start()
                off = off + take

        def start_body(e, c):
            pad_copies(e, False)
            return c

        def wait_body(e, c):
            pad_copies(e, True)
            return c

        lax.fori_loop(0, N_EXPERTS, start_body, 0)
        lax.fori_loop(0, N_EXPERTS, wait_body, 0)

        tail = pad_ref[0, N_EXPERTS]
        zrows = zero_ref.shape[0]

        def tail_copy(c):
            return pltpu.make_async_copy(
                zero_ref, xs_hbm.at[pl.ds(pl.multiple_of(tail * n_chunk + c * zrows, zrows), zrows), :], zsem)

        def tail_start(c, carry):
            tail_copy(c).start()
            return carry

        def tail_wait(c, carry):
            tail_copy(c).wait()
            return carry

        lax.fori_loop(0, pad_ref[1, N_EXPERTS], tail_start, 0)
        lax.fori_loop(0, pad_ref[1, N_EXPERTS], tail_wait, 0)

    for r in range(tm):
        for kk in range(2):
            dst = pl.multiple_of(idx_ref[0, 0, 2 * r + kk] * n_chunk, n_chunk)
            pltpu.make_async_copy(h_ref.at[pl.ds(r * n_chunk, n_chunk), :],
                                  xs_hbm.at[pl.ds(dst, n_chunk), :], sem).start(priority=kk)
    for kk in range(2):
        pltpu.make_async_copy(h_ref, xs_hbm.at[pl.ds(0, tm * n_chunk), :], sem).wait()


def _dispatch_call(pads, dest3, h3, n_slots):
    n_tiles = dest3.shape[0]
    tm = dest3.shape[2] // 2
    n_chunk = D_MODEL // LANES
    grid_spec = pltpu.PrefetchScalarGridSpec(
        num_scalar_prefetch=1, grid=(n_tiles,),
        in_specs=[pl.BlockSpec((1, 1, 2 * tm), lambda i, pads: (i, 0, 0), memory_space=pltpu.SMEM),
                  pl.BlockSpec((tm * n_chunk, LANES), lambda i, pads: (i, 0))],
        out_specs=pl.BlockSpec(memory_space=pl.ANY),
        scratch_shapes=[pltpu.VMEM((PAD_CHUNKS[0] * n_chunk, LANES), F32),
                        pltpu.SemaphoreType.DMA, pltpu.SemaphoreType.DMA])
    return pl.pallas_call(
        _dispatch_kernel, grid_spec=grid_spec,
        out_shape=jax.ShapeDtypeStruct((n_slots * n_chunk, LANES), F32),
        compiler_params=_cparams(1),
    )(pads, dest3, h3)


EXPERT_BLOCKS_PER_STEP = 2


def _expert_kernel(be_ref, nu_ref, xs_ref, *refs):
    i = pl.program_id(0)
    n_chunk = D_MODEL // LANES
    rows_per_block = ROUTE_BLOCK * n_chunk
    ys_ref = refs[-1]
    for b in range(EXPERT_BLOCKS_PER_STEP):
        wg_ref, wu_ref, wd_ref = refs[3 * b:3 * b + 3]
        base = b * rows_per_block
        block = i * EXPERT_BLOCKS_PER_STEP + b

        @pl.when(block < nu_ref[0])
        def _():
            xb = jnp.concatenate([xs_ref[pl.ds(base + j, ROUTE_BLOCK, stride=n_chunk), :]
                                  for j in range(n_chunk)], axis=1).astype(BF16)
            g = jnp.dot(xb, wg_ref[0, 0].astype(BF16), preferred_element_type=F32)
            u = jnp.dot(xb, wu_ref[0, 0].astype(BF16), preferred_element_type=F32)
            hmid = g * (1.0 / (1.0 + jnp.exp(-g))) * u
            y = jnp.dot(hmid.astype(BF16), wd_ref[0, 0].astype(BF16), preferred_element_type=F32)
            for j in range(n_chunk):
                ys_ref[pl.ds(base + j, ROUTE_BLOCK, stride=n_chunk), :] = y[:, j * LANES:(j + 1) * LANES]

        @pl.when(block >= nu_ref[0])
        def _():
            ys_ref[base:base + rows_per_block, :] = jnp.zeros((rows_per_block, LANES), F32)


def _expert_call(block_eid, n_used, xs3, wg, wu, wd, layer):
    per_step = EXPERT_BLOCKS_PER_STEP
    tile = (per_step * ROUTE_BLOCK * D_MODEL // LANES, LANES)
    n_steps = xs3.shape[0] // tile[0]
    row = lambda i, be, nu: (i, 0)
    weight_specs = []
    for b in range(per_step):
        wmap = lambda i, be, nu, b=b: (layer, be[i * per_step + b], 0, 0)
        weight_specs += [pl.BlockSpec((1, 1, D_MODEL, D_FF), wmap), pl.BlockSpec((1, 1, D_MODEL, D_FF), wmap),
                         pl.BlockSpec((1, 1, D_FF, D_MODEL), wmap)]
    grid_spec = pltpu.PrefetchScalarGridSpec(
        num_scalar_prefetch=2, grid=(n_steps,),
        in_specs=[pl.BlockSpec(tile, row)] + weight_specs,
        out_specs=pl.BlockSpec(tile, row))
    return pl.pallas_call(
        _expert_kernel, grid_spec=grid_spec,
        out_shape=jax.ShapeDtypeStruct(xs3.shape, F32),
        compiler_params=_cparams(1),
    )(block_eid, n_used, xs3, *([wg, wu, wd] * per_step))


def _combine_kernel(idx0_ref, idxn_ref, ys_hbm, x_ref, rg_ref, o_ref, buf0, buf1, sem):
    i = pl.program_id(0)
    n = pl.num_programs(0)
    tm = x_ref.shape[0]
    n_chunk = D_MODEL // LANES

    def issue(idx_ref, buf, sem_slot, rows):
        for r in rows:
            for kk in range(2):
                src = pl.multiple_of(idx_ref[0, 0, 2 * r + kk] * n_chunk, n_chunk)
                pltpu.make_async_copy(ys_hbm.at[pl.ds(src, n_chunk), :],
                                      buf.at[pl.ds((kk * tm + r) * n_chunk, n_chunk), :],
                                      sem_slot).start(priority=kk)

    def wait_tile(buf, sem_slot):
        pltpu.make_async_copy(ys_hbm.at[pl.ds(0, 2 * tm * n_chunk), :], buf, sem_slot).wait()

    @pl.when(i == 0)
    def _():
        issue(idx0_ref, buf0, sem.at[0], range(tm))

    def step(buf, sem_cur, buf_next, sem_next):
        issue(idxn_ref, buf_next, sem_next, range(tm))
        wait_tile(buf, sem_cur)
        rg = rg_ref[...]
        g0 = rg[:, 0:1]
        g1 = rg[:, 1:2]
        for j in range(n_chunk):
            cols = slice(j * LANES, (j + 1) * LANES)
            y0 = buf[pl.ds(j, tm, stride=n_chunk), :]
            y1 = buf[pl.ds(tm * n_chunk + j, tm, stride=n_chunk), :]
            o_ref[:, cols] = x_ref[:, cols] + g0 * y0 + g1 * y1

    @pl.when(i % 2 == 0)
    def _():
        step(buf0, sem.at[0], buf1, sem.at[1])

    @pl.when(i % 2 == 1)
    def _():
        step(buf1, sem.at[1], buf0, sem.at[0])

    @pl.when(i == n - 1)
    def _():
        @pl.when(i % 2 == 0)
        def _():
            wait_tile(buf1, sem.at[1])

        @pl.when(i % 2 == 1)
        def _():
            wait_tile(buf0, sem.at[0])


def _combine_call(dest3, ys, x2, rg):
    t = x2.shape[0]
    tm = TM_COMB
    n = t // tm
    row = lambda i: (i, 0)
    tile = (2 * tm * D_MODEL // LANES, LANES)
    return pl.pallas_call(
        _combine_kernel, grid=(n,),
        in_specs=[
            pl.BlockSpec((1, 1, 2 * tm), lambda i: (0, 0, 0), memory_space=pltpu.SMEM),
            pl.BlockSpec((1, 1, 2 * tm), lambda i: (jnp.minimum(i + 1, n - 1), 0, 0), memory_space=pltpu.SMEM),
            pl.BlockSpec(memory_space=pl.ANY),
            pl.BlockSpec((tm, D_MODEL), row), pl.BlockSpec((tm, LANES), row)],
        out_specs=pl.BlockSpec((tm, D_MODEL), row),
        out_shape=jax.ShapeDtypeStruct((t, D_MODEL), F32),
        scratch_shapes=[pltpu.VMEM(tile, F32), pltpu.VMEM(tile, F32), pltpu.SemaphoreType.DMA((2,))],
        compiler_params=_cparams(1),
    )(dest3, dest3, ys, x2, rg)


def _swap_halves(a):
    half = a.shape[-1] // 2
    return jnp.concatenate([a[..., half:], a[..., :half]], axis=-1)


def _layer_params(l, seq, w):
    p = {}
    row = lambda v: v.reshape(1, -1).astype(F32)
    w_in = w["w_in"][l]
    kr_cols = w_in[:, 1856:1888]
    p["win"] = jnp.concatenate(
        [w_in[:, 0:1536], w_in[:, 1888:2144], w_in[:, 1728:1856], w_in[:, 1536:1728],
         kr_cols, _swap_halves(kr_cols)], axis=1).astype(BF16)
    p["g1"] = row(w["norm1_g"][l])
    p["gq"] = row(jnp.tile(w["diff_q_norm_g"][l], 2) * (DIFF_QK ** -0.5))
    p["gk"] = row(jnp.tile(w["diff_k_norm_g"][l], 2))
    p["gckv"] = row(w["mla_kv_lat_norm_g"][l])
    gcq = w["mla_q_lat_norm_g"][l]
    p["gcqa"] = row(gcq[:LANES])
    p["gcqb"] = row(jnp.concatenate([gcq[LANES:], jnp.zeros((2 * LANES - MLA_Q_RANK,), F32)]))

    wuq = w["mla_w_uq"][l].reshape(MLA_Q_RANK, MLA_HEADS, MLA_NOPE + MLA_ROPE)
    rope_w = wuq[:, :, MLA_NOPE:]
    wuq = jnp.concatenate([wuq[:, :, :MLA_NOPE], rope_w, _swap_halves(rope_w)], axis=-1)
    wuq = wuq.reshape(MLA_Q_RANK, MLA_HEADS * HEAD_SLOT)
    wuq = jnp.concatenate([wuq, jnp.zeros((2 * LANES - MLA_Q_RANK, wuq.shape[1]), F32)], axis=0).astype(BF16)
    p["wuqa"] = wuq[:LANES]
    p["wuqb"] = wuq[LANES:]

    wukv = w["mla_w_ukv"][l].reshape(MLA_KV_RANK, MLA_HEADS, MLA_NOPE + MLA_V)
    zk = jnp.zeros((MLA_KV_RANK, MLA_HEADS, HEAD_SLOT - MLA_NOPE), F32)
    p["wkk"] = jnp.concatenate([wukv[:, :, :MLA_NOPE], zk], axis=-1).reshape(MLA_KV_RANK, -1).astype(BF16)
    vcols = wukv[:, :, MLA_NOPE:]
    zv = jnp.zeros_like(vcols)
    even = (jnp.arange(MLA_HEADS) % 2 == 0)[None, :, None]
    wkv = jnp.concatenate([jnp.where(even, vcols, zv), jnp.where(even, zv, vcols)], axis=-1)
    p["wkv"] = wkv.reshape(MLA_KV_RANK, -1).astype(BF16)
    p["gkn"] = row(jnp.concatenate([w["mla_k_nope_norm_g"][l], jnp.zeros((HEAD_SLOT - MLA_NOPE,), F32)]))

    inv = 1.0 / (ROPE_BASE ** (jnp.arange(0, MLA_ROPE, 2, dtype=F32) / MLA_ROPE))
    ang = jnp.arange(seq, dtype=F32)[:, None] * inv[None, :]
    cosf = jnp.concatenate([jnp.cos(ang), jnp.cos(ang)], axis=-1)
    sinf = jnp.concatenate([-jnp.sin(ang), jnp.sin(ang)], axis=-1)
    scale = (MLA_NOPE + MLA_ROPE) ** -0.5 * math.log2(math.e)
    gqr = w["mla_q_rope_norm_g"][l]
    q_head = jnp.concatenate([jnp.broadcast_to(w["mla_q_nope_norm_g"][l][None, :], (seq, MLA_NOPE)),
                              gqr[None, :] * cosf, _swap_halves(gqr)[None, :] * sinf], axis=-1) * scale
    p["qtab"] = jnp.tile(q_head, (1, MLA_HEADS))
    gkr = w["mla_k_rope_norm_g"][l]
    p["ktab"] = jnp.concatenate([jnp.zeros((seq, MLA_NOPE), F32), gkr[None, :] * cosf,
                                 _swap_halves(gkr)[None, :] * sinf], axis=-1)
    src = jnp.arange(LANES)
    dst = jnp.arange(MLA_HEADS * HEAD_SLOT)
    src_j = jnp.where(src >= MLA_NOPE, (src - MLA_NOPE) % MLA_ROPE, -1)
    dst_l = dst % HEAD_SLOT
    dst_j = jnp.where(dst_l >= MLA_NOPE, (dst_l - MLA_NOPE) % MLA_ROPE, -2)
    p["eplace"] = (src_j[:, None] == dst_j[None, :]).astype(BF16)

    pw = w["pool_w"][l]
    bd = jnp.zeros((POOL_WIDTH, POOL_WIDTH), F32)
    for g in range(POOL_GROUPS):
        s0 = g * POOL_GROUP_DIM
        bd = bd.at[s0:s0 + POOL_GROUP_DIM, s0:s0 + POOL_GROUP_DIM].set(pw[g])
    p["pool_w"] = bd.astype(BF16)
    p["pool_scale"] = row(w["pool_scale"][l])

    lam_init = 0.8 - 0.6 * math.exp(-0.3 * l)
    lv = w["diff_lambda"][l].astype(F32)
    p["lam"] = (jnp.exp(jnp.sum(lv[0] * lv[1])) - jnp.exp(jnp.sum(lv[2] * lv[3])) + lam_init).reshape(1)
    p["gsub"] = row(w["diff_sub_norm_g"][l] * (1.0 - lam_init))

    p["wo"] = w["w_out"][l].astype(BF16)
    p["g2"] = row(w["norm2_g"][l])
    wr = jnp.concatenate([w["router_group_w"][l], w["router_expert_w"][l],
                          jnp.zeros((D_MODEL, LANES - N_GROUPS - N_EXPERTS), F32)], axis=1)
    wr_hi = wr.astype(BF16)
    p["wrh"] = wr_hi
    p["wrl"] = (wr - wr_hi.astype(F32)).astype(BF16)
    p["br"] = row(jnp.concatenate([w["router_group_b"][l], w["router_expert_b"][l],
                                   jnp.zeros((LANES - N_GROUPS - N_EXPERTS,), F32)]))
    return p


def kernel(x, norm1_g, w_in, diff_q_norm_g, diff_k_norm_g, diff_lambda, diff_sub_norm_g, mla_q_lat_norm_g, mla_kv_lat_norm_g, mla_w_uq, mla_w_ukv, mla_q_nope_norm_g, mla_q_rope_norm_g, mla_k_nope_norm_g, mla_k_rope_norm_g, pool_w, pool_scale, w_out, norm2_g, router_group_w, router_group_b, router_expert_w, router_expert_b, expert_w_gate, expert_w_up, expert_w_down):
    w = dict(norm1_g=norm1_g, w_in=w_in, diff_q_norm_g=diff_q_norm_g, diff_k_norm_g=diff_k_norm_g,
             diff_lambda=diff_lambda, diff_sub_norm_g=diff_sub_norm_g, mla_q_lat_norm_g=mla_q_lat_norm_g,
             mla_kv_lat_norm_g=mla_kv_lat_norm_g, mla_w_uq=mla_w_uq, mla_w_ukv=mla_w_ukv,
             mla_q_nope_norm_g=mla_q_nope_norm_g, mla_q_rope_norm_g=mla_q_rope_norm_g,
             mla_k_nope_norm_g=mla_k_nope_norm_g, mla_k_rope_norm_g=mla_k_rope_norm_g,
             pool_w=pool_w, pool_scale=pool_scale, w_out=w_out, norm2_g=norm2_g,
             router_group_w=router_group_w, router_group_b=router_group_b,
             router_expert_w=router_expert_w, router_expert_b=router_expert_b)
    batch, seq, d = x.shape
    t = batch * seq
    n_assign = 2 * t
    n_blocks = n_assign // ROUTE_BLOCK + N_EXPERTS
    alibi = _alibi_tables(seq)

    x2 = x.reshape(t, d)
    for l in range(DEPTH):
        p = _layer_params(l, seq, w)
        dq1, dq2, dk, dv, qm, km, vm, pu = _proj_call(x2, p, seq)
        o_diff = _diff_call(p["lam"], dq1, dq2, dk, dv, p["gsub"], alibi, batch, seq)
        o_mla = _mla_call(qm, km, vm, batch, seq)
        o_pool = _pool_call(pu, p["pool_w"], p["pool_scale"], batch, seq)
        x2, h2, route_i, route_g, counts = _outproj_call(x2, o_diff, o_mla, o_pool, p)

        cnt = counts[0, N_GROUPS:N_GROUPS + N_EXPERTS].astype(jnp.int32)
        padded = (cnt + ROUTE_BLOCK - 1) // ROUTE_BLOCK * ROUTE_BLOCK
        padded_ends = jnp.cumsum(padded)
        padded_starts = padded_ends - padded
        eid = route_i[:, 0:2]
        start_of = jnp.sum(jnp.where(eid[..., None] == jnp.arange(N_EXPERTS, dtype=jnp.int32),
                                     padded_starts, 0), axis=-1)
        dest = start_of + route_i[:, 2:4]
        block_start = jnp.arange(n_blocks, dtype=jnp.int32) * ROUTE_BLOCK
        block_eid = jnp.minimum(jnp.sum(block_start[:, None] >= padded_ends[None, :], axis=1),
                                N_EXPERTS - 1).astype(jnp.int32)
        n_used = (padded_ends[-1] // ROUTE_BLOCK).astype(jnp.int32).reshape(1)
        n_tail = (n_blocks - n_used) * (ROUTE_BLOCK // PAD_CHUNKS[0])
        pads = jnp.stack([jnp.concatenate([padded_starts + cnt, padded_ends[-1:]]),
                          jnp.concatenate([padded - cnt, n_tail])]).astype(jnp.int32)
        dest3 = dest.reshape(t // TM_COMB, 1, 2 * TM_COMB)

        xs = _dispatch_call(pads, dest3, h2, n_blocks * ROUTE_BLOCK)
        ys = _expert_call(block_eid, n_used, xs, expert_w_gate, expert_w_up, expert_w_down, l)
        x2 = _combine_call(dest3, ys, x2, route_g)
    return x2.reshape(batch, seq, d)
```

```python
import functools
import math

import jax
import jax.numpy as jnp
from jax import lax
from jax.experimental import pallas as pl
from jax.experimental.pallas import tpu as pltpu

F32 = jnp.float32
BF16 = jnp.bfloat16

D_MODEL = 1024
DEPTH = 2
DIFF_HEADS = 4
DIFF_QK = 64
DIFF_V = 128
DIFF_WIDTH = 512
MLA_HEADS = 4
MLA_NOPE = 64
MLA_ROPE = 32
MLA_V = 64
MLA_Q_RANK = 192
MLA_KV_RANK = 128
MLA_WIDTH = 256
ROPE_BASE = 10000.0
POOL_WIDTH = 256
POOL_GROUPS = 4
POOL_GROUP_DIM = 64
POOL_WINDOWS = (2, 4, 8, 16)
N_GROUPS = 4
EXPERTS_PER_GROUP = 8
N_EXPERTS = 32
D_FF = 256
ROUTE_BLOCK = 256
RMS_EPS = 1e-6

LANES = 128
HEAD_SLOT = 128
PROJ_WIDTH = 2176
POOL_PAD = 16
VMEM_LIMIT = 48 * 1024 * 1024

TM_PROJ = 512
TQ = 512
TK_DIFF = 256
TK_MLA = 512
TM_OUT = 1024
TM_COMB = 512

NT_DIMS = (((1,), (1,)), ((), ()))


def _cparams(n_axes):
    return pltpu.CompilerParams(dimension_semantics=("arbitrary",) * n_axes,
                                vmem_limit_bytes=VMEM_LIMIT)


def _full(shape):
    return pl.BlockSpec(shape, lambda *_: (0,) * len(shape))


def _proj_kernel(x_ref, g1_ref, win_ref, gq_ref, gk_ref, gckv_ref, gcqa_ref, gcqb_ref,
                 wuqa_ref, wuqb_ref, wkk_ref, wkv_ref, gkn_ref, qtab_ref, ktab_ref, eplace_ref,
                 dq1_ref, dq2_ref, dk_ref, dv_ref, qm_ref, km_ref, vm_ref, pu_ref):
    x = x_ref[...]
    xn = x * lax.rsqrt(jnp.mean(x * x, axis=-1, keepdims=True) + RMS_EPS) * g1_ref[...]
    proj = jnp.dot(xn.astype(BF16), win_ref[...], preferred_element_type=F32)

    tm = x.shape[0]
    lane = lax.broadcasted_iota(jnp.int32, (tm, LANES), 1)
    lo = lane < DIFF_QK

    def half_norm(c, g_row):
        sq = c * c
        s_lo = jnp.sum(jnp.where(lo, sq, 0.0), axis=-1, keepdims=True)
        s_hi = jnp.sum(jnp.where(lo, 0.0, sq), axis=-1, keepdims=True)
        r = jnp.where(lo, lax.rsqrt(s_lo / DIFF_QK + RMS_EPS), lax.rsqrt(s_hi / DIFF_QK + RMS_EPS))
        return c * r * g_row

    for h in range(DIFF_HEADS):
        sl = slice(h * HEAD_SLOT, (h + 1) * HEAD_SLOT)
        qn = half_norm(proj[:, sl], gq_ref[...])
        dq1_ref[:, sl] = jnp.where(lo, qn, 0.0).astype(BF16)
        dq2_ref[:, sl] = jnp.where(lo, 0.0, qn).astype(BF16)
        ksl = slice(512 + h * HEAD_SLOT, 512 + (h + 1) * HEAD_SLOT)
        dk_ref[:, sl] = half_norm(proj[:, ksl], gk_ref[...]).astype(BF16)
    dv_ref[...] = proj[:, 1024:1536].astype(BF16)
    pu_ref[...] = proj[:, 1536:1792]

    ckv = proj[:, 1792:1920]
    ckvn = ckv * lax.rsqrt(jnp.mean(ckv * ckv, axis=-1, keepdims=True) + RMS_EPS) * gckv_ref[...]
    ckvn = ckvn.astype(BF16)
    cqa = proj[:, 1920:2048]
    last = proj[:, 2048:2176]
    lsq = last * last
    ss_q = (jnp.sum(cqa * cqa, axis=-1, keepdims=True)
            + jnp.sum(jnp.where(lo, lsq, 0.0), axis=-1, keepdims=True))
    r_q = lax.rsqrt(ss_q / MLA_Q_RANK + RMS_EPS)
    q_raw = (jnp.dot((cqa * r_q * gcqa_ref[...]).astype(BF16), wuqa_ref[...], preferred_element_type=F32)
             + jnp.dot((last * r_q * gcqb_ref[...]).astype(BF16), wuqb_ref[...], preferred_element_type=F32))

    rope_lanes = (lane >= MLA_NOPE) & (lane < MLA_NOPE + MLA_ROPE)
    ss_kr = jnp.sum(jnp.where(rope_lanes, lsq, 0.0), axis=-1, keepdims=True)
    kr_terms = last * lax.rsqrt(ss_kr / MLA_ROPE + RMS_EPS) * ktab_ref[...]
    kr_placed = jnp.dot(kr_terms.astype(BF16), eplace_ref[...], preferred_element_type=F32)

    k_raw = jnp.dot(ckvn, wkk_ref[...], preferred_element_type=F32)
    vm_ref[...] = jnp.dot(ckvn, wkv_ref[...], preferred_element_type=F32).astype(BF16)
    qtab = qtab_ref[...]
    for h in range(MLA_HEADS):
        sl = slice(h * HEAD_SLOT, (h + 1) * HEAD_SLOT)
        c = q_raw[:, sl]
        sq = c * c
        s_n = jnp.sum(jnp.where(lo, sq, 0.0), axis=-1, keepdims=True)
        s_r = jnp.sum(jnp.where(rope_lanes, sq, 0.0), axis=-1, keepdims=True)
        r = jnp.where(lo, lax.rsqrt(s_n / MLA_NOPE + RMS_EPS), lax.rsqrt(s_r / MLA_ROPE + RMS_EPS))
        qm_ref[:, sl] = (c * r * qtab[:, sl]).astype(BF16)
        kc = k_raw[:, sl]
        r_k = lax.rsqrt(jnp.sum(kc * kc, axis=-1, keepdims=True) / MLA_NOPE + RMS_EPS)
        km_ref[:, sl] = (kc * r_k * gkn_ref[...] + kr_placed[:, sl]).astype(BF16)


def _proj_call(x2, p, seq):
    t = x2.shape[0]
    tm = TM_PROJ
    n_pos = seq // tm
    row = lambda i: (i, 0)
    pos = lambda i: (i % n_pos, 0)
    bf = lambda w: jax.ShapeDtypeStruct((t, w), BF16)
    in_specs = [
        pl.BlockSpec((tm, D_MODEL), row),
        _full((1, D_MODEL)), _full((D_MODEL, PROJ_WIDTH)),
        _full((1, LANES)), _full((1, LANES)), _full((1, LANES)), _full((1, LANES)), _full((1, LANES)),
        _full((LANES, 512)), _full((LANES, 512)), _full((LANES, 512)), _full((LANES, 512)),
        _full((1, LANES)),
        pl.BlockSpec((tm, 512), pos), pl.BlockSpec((tm, LANES), pos),
        _full((LANES, 512)),
    ]
    out_specs = [pl.BlockSpec((tm, 512), row)] * 7 + [pl.BlockSpec((tm, POOL_WIDTH), row)]
    out_shape = [bf(512)] * 7 + [jax.ShapeDtypeStruct((t, POOL_WIDTH), F32)]
    return pl.pallas_call(
        _proj_kernel, grid=(t // tm,), in_specs=in_specs, out_specs=out_specs, out_shape=out_shape,
        compiler_params=_cparams(1),
    )(x2, p["g1"], p["win"], p["gq"], p["gk"], p["gckv"], p["gcqa"], p["gcqb"],
      p["wuqa"], p["wuqb"], p["wkk"], p["wkv"], p["gkn"], p["qtab"], p["ktab"], p["eplace"])


def _unflatten(n, sizes):
    n = jnp.minimum(n, math.prod(sizes) - 1)
    coords = []
    for size in reversed(sizes):
        coords.append(n % size)
        n = n // size
    return tuple(reversed(coords))


def _two_stage(n, stage, bufs):
    (s0, m0), (s1, m1) = bufs

    @pl.when(n == 0)
    def _():
        s1[...] = jnp.zeros_like(s1)
        m1[...] = jnp.zeros_like(m1)

    @pl.when(n % 2 == 0)
    def _():
        stage((s0, m0), (s1, m1))

    @pl.when(n % 2 == 1)
    def _():
        stage((s1, m1), (s0, m0))


def _two_stage_scratch(seq):
    pair = [pltpu.VMEM((2 * TQ, seq), F32), pltpu.VMEM((2 * TQ, LANES), F32)]
    return pair + pair


def _softmax_pv_tile(s_prev, m_rows, c, tq, tk, lsum, acc, v_tiles, exp_fn):
    n_half = tk // LANES
    ps = [exp_fn(s_prev[:, (c * n_half + j) * LANES:(c * n_half + j + 1) * LANES] - m_rows)
          for j in range(n_half)]
    for ch in ps:
        lsum = ch if lsum is None else lsum + ch
    pb = jnp.concatenate(ps, axis=1).astype(BF16)
    for g, vt in enumerate(v_tiles):
        pv = jnp.dot(pb[g * tq:(g + 1) * tq], vt, preferred_element_type=F32)
        acc[g] = pv if acc[g] is None else acc[g] + pv
    return lsum


def _running_max(mx, sc):
    for j in range(sc.shape[1] // LANES):
        chunk = sc[:, j * LANES:(j + 1) * LANES]
        mx = chunk if mx is None else jnp.maximum(mx, chunk)
    return mx


def _diff_kernel(lam_ref, q1_ref, q2_ref, qx_ref, k_ref, kx_ref, bd_ref, v_ref, gsub_ref, o_ref,
                 s0_ref, m0_ref, s1_ref, m1_ref, *, sizes):
    n = pl.program_id(0)
    tq = q1_ref.shape[0]
    tk = TK_DIFF
    n_kt = k_ref.shape[0] // tk
    n_diag = tq // tk
    first_cur = _unflatten(n, sizes)[2] * n_diag
    first_prev = _unflatten(jnp.maximum(n - 1, 0), sizes)[2] * n_diag

    def stage(cur, prev):
        s_cur, m_cur = cur
        s_prev, m_prev = prev
        q1, q2 = q1_ref[...], q2_ref[...]
        qx_left = qx_ref[0, 0]
        qx_right = -qx_left
        bd = bd_ref[0]
        bd2 = jnp.concatenate([bd, bd], axis=0)
        mx = None
        m_rows = m_prev[...]
        lsum = None
        acc = [None, None]
        for c in range(n_kt):
            start = pl.multiple_of(((first_prev + c) % n_kt) * tk, tk)
            vt = v_ref[pl.ds(start, tk), :]
            lsum = _softmax_pv_tile(s_prev, m_rows, c, tq, tk, lsum, acc, [vt, vt], jnp.exp)

            start = pl.multiple_of(((first_cur + c) % n_kt) * tk, tk)
            if c < n_diag:
                qx = jnp.zeros_like(qx_left)
            else:
                qx = jnp.where(first_cur + c >= n_kt, qx_left, qx_right)
            qq = jnp.concatenate([jnp.concatenate([q1, qx], axis=1),
                                  jnp.concatenate([q2, qx], axis=1)], axis=0)
            kk = jnp.concatenate([k_ref[pl.ds(start, tk), :], kx_ref[pl.ds(start, tk), :]], axis=1)
            sc = lax.dot_general(qq, kk, NT_DIMS, preferred_element_type=F32)
            if c < n_diag:
                sc = sc + bd2[:, c * tk:(c + 1) * tk]
            s_cur[:, c * tk:(c + 1) * tk] = sc
            mx = _running_max(mx, sc)
        m_cur[...] = jnp.broadcast_to(jnp.max(mx, axis=-1, keepdims=True), m_cur.shape)
        l = jnp.sum(lsum, axis=-1, keepdims=True)
        o = acc[0] * (1.0 / l[0:tq]) - acc[1] * (lam_ref[0] / l[tq:2 * tq])
        r = lax.rsqrt(jnp.mean(o * o, axis=-1, keepdims=True) + RMS_EPS)
        o_ref[...] = (o * r * gsub_ref[...]).astype(BF16)

    _two_stage(n, stage, ((s0_ref, m0_ref), (s1_ref, m1_ref)))


def _alibi_tables(seq):
    nq = seq // TQ
    slopes = 2.0 ** (-8.0 * jnp.arange(1, DIFF_HEADS + 1, dtype=F32) / DIFF_HEADS)
    pos = jnp.arange(seq, dtype=jnp.int32)
    hi = (pos // 256).astype(F32)
    lo = (pos % 256).astype(F32)
    s4 = slopes[:, None]
    ones = jnp.ones((DIFF_HEADS, seq), F32)
    q_left = jnp.stack([-s4 * 256.0 * hi[None], -s4 * lo[None], s4 * 256.0 * ones, s4 * ones], axis=-1)
    qx = jnp.concatenate([q_left, jnp.zeros((DIFF_HEADS, seq, HEAD_SLOT - 4), F32)], axis=-1)
    qx = qx.reshape(DIFF_HEADS, nq, TQ, HEAD_SLOT).astype(BF16)
    k_cols = jnp.stack([jnp.ones((seq,), F32), jnp.ones((seq,), F32), hi, lo], axis=-1)
    kx = jnp.concatenate([k_cols, jnp.zeros((seq, HEAD_SLOT - 4), F32)], axis=-1).astype(BF16)
    loc = jnp.arange(TQ, dtype=jnp.int32)
    bd = -slopes[:, None, None] * jnp.abs(loc[:, None] - loc[None, :]).astype(F32)[None]
    return qx, kx, bd


def _diff_call(lam, dq1, dq2, dk, dv, gsub, tabs, batch, seq):
    t = dq1.shape[0]
    nq = seq // TQ
    qx, kx, bd = tabs
    sizes = (batch, DIFF_HEADS, nq)
    cur = lambda n: _unflatten(n, sizes)
    prev = lambda n: _unflatten(jnp.maximum(n - 1, 0), sizes)

    def qmap(n, *_):
        b, h, qi = cur(n)
        return (b * nq + qi, h)

    def kmap(n, *_):
        b, h, qi = cur(n)
        return (b, h)

    def vmap(n, *_):
        b, h, qi = prev(n)
        return (b, h)

    def omap(n, *_):
        b, h, qi = prev(n)
        return (b * nq + qi, h)

    grid_spec = pltpu.PrefetchScalarGridSpec(
        num_scalar_prefetch=1, grid=(math.prod(sizes) + 1,),
        in_specs=[pl.BlockSpec((TQ, HEAD_SLOT), qmap), pl.BlockSpec((TQ, HEAD_SLOT), qmap),
                  pl.BlockSpec((1, 1, TQ, HEAD_SLOT), lambda n, *_: cur(n)[1:] + (0, 0)),
                  pl.BlockSpec((seq, HEAD_SLOT), kmap),
                  pl.BlockSpec((seq, HEAD_SLOT), lambda *_: (0, 0)),
                  pl.BlockSpec((1, TQ, TQ), lambda n, *_: (cur(n)[1], 0, 0)),
                  pl.BlockSpec((seq, HEAD_SLOT), vmap),
                  pl.BlockSpec((1, HEAD_SLOT), lambda *_: (0, 0))],
        out_specs=pl.BlockSpec((TQ, HEAD_SLOT), omap),
        scratch_shapes=_two_stage_scratch(seq))
    return pl.pallas_call(
        functools.partial(_diff_kernel, sizes=sizes), grid_spec=grid_spec,
        out_shape=jax.ShapeDtypeStruct((t, DIFF_WIDTH), BF16),
        compiler_params=_cparams(1),
    )(lam, dq1, dq2, qx, dk, kx, bd, dv, gsub)


def _mla_kernel(q_ref, k_ref, v_ref, o_ref, s0_ref, m0_ref, s1_ref, m1_ref):
    n = pl.program_id(0)
    tq = q_ref.shape[0]
    tk = TK_MLA
    n_kt = k_ref.shape[0] // tk

    def stage(cur, prev):
        s_cur, m_cur = cur
        s_prev, m_prev = prev
        mx = [None, None]
        m_rows = m_prev[...]
        lsum = None
        acc = [None, None]
        for c in range(n_kt):
            rows = slice(c * tk, (c + 1) * tk)
            v_tiles = [v_ref[rows, hh * HEAD_SLOT:(hh + 1) * HEAD_SLOT] for hh in range(2)]
            lsum = _softmax_pv_tile(s_prev, m_rows, c, tq, tk, lsum, acc, v_tiles, jnp.exp2)
            for hh in range(2):
                sl = slice(hh * HEAD_SLOT, (hh + 1) * HEAD_SLOT)
                sc = lax.dot_general(q_ref[:, sl], k_ref[rows, sl], NT_DIMS, preferred_element_type=F32)
                s_cur[hh * tq:(hh + 1) * tq, rows] = sc
                mx[hh] = _running_max(mx[hh], sc)
        mx = jnp.concatenate(mx, axis=0)
        m_cur[...] = jnp.broadcast_to(jnp.max(mx, axis=-1, keepdims=True), m_cur.shape)
        l = jnp.sum(lsum, axis=-1, keepdims=True)
        o = acc[0] * (1.0 / l[0:tq]) + acc[1] * (1.0 / l[tq:2 * tq])
        o_ref[...] = o.astype(BF16)

    _two_stage(n, stage, ((s0_ref, m0_ref), (s1_ref, m1_ref)))


def _mla_call(qm, km, vm, batch, seq):
    t = qm.shape[0]
    nq = seq // TQ
    sizes = (batch, MLA_HEADS // 2, nq)
    cur = lambda n: _unflatten(n, sizes)
    prev = lambda n: _unflatten(jnp.maximum(n - 1, 0), sizes)

    def qmap(n):
        b, p, qi = cur(n)
        return (b * nq + qi, p)

    def kmap(n):
        b, p, qi = cur(n)
        return (b, p)

    def vmap(n):
        b, p, qi = prev(n)
        return (b, p)

    def omap(n):
        b, p, qi = prev(n)
        return (b * nq + qi, p)

    return pl.pallas_call(
        _mla_kernel, grid=(math.prod(sizes) + 1,),
        in_specs=[pl.BlockSpec((TQ, 2 * HEAD_SLOT), qmap), pl.BlockSpec((seq, 2 * HEAD_SLOT), kmap),
                  pl.BlockSpec((seq, 2 * HEAD_SLOT), vmap)],
        out_specs=pl.BlockSpec((TQ, HEAD_SLOT), omap),
        out_shape=jax.ShapeDtypeStruct((t, MLA_WIDTH), BF16),
        scratch_shapes=_two_stage_scratch(seq),
        compiler_params=_cparams(1),
    )(qm, km, vm)


def _pool_kernel(u_ref, w_ref, scale_ref, o_ref):
    u = u_ref[...]
    seq, width = u.shape
    zpad = jnp.zeros((POOL_PAD, width), F32)
    ue = jnp.concatenate([zpad, u, zpad], axis=0)
    n = seq + 2 * POOL_PAD

    def down(a, k):
        return pltpu.roll(a, k, axis=0)

    def up(a, k):
        return pltpu.roll(a, n - k, axis=0)

    a2 = ue + down(ue, 1)
    a4 = down(a2, 1) + up(a2, 1)
    a8 = down(a4, 2) + up(a4, 2)
    a16 = down(a8, 4) + up(a8, 4)
    core = slice(POOL_PAD, POOL_PAD + seq)
    lane = lax.broadcasted_iota(jnp.int32, (seq, width), 1)
    tpos = lax.broadcasted_iota(jnp.int32, (seq, width), 0)
    grp = lane // POOL_GROUP_DIM
    win_sum = jnp.where(grp == 0, a2[core], jnp.where(grp == 1, a4[core], jnp.where(grp == 2, a8[core], a16[core])))
    half = jnp.where(grp == 0, 1, jnp.where(grp == 1, 2, jnp.where(grp == 2, 4, 8)))
    lo_i = jnp.maximum(tpos - half, 0)
    hi_i = jnp.minimum(tpos + half - 1, seq - 1)
    cnt = (hi_i - lo_i + 1).astype(F32)
    pooled = win_sum / cnt - u
    mixed = jnp.dot(pooled.astype(BF16), w_ref[...], preferred_element_type=F32)
    o_ref[...] = (mixed * scale_ref[...]).astype(BF16)


def _pool_call(pu, w_bd, scale, batch, seq):
    t = pu.shape[0]
    return pl.pallas_call(
        _pool_kernel, grid=(batch,),
        in_specs=[pl.BlockSpec((seq, POOL_WIDTH), lambda b: (b, 0)),
                  _full((POOL_WIDTH, POOL_WIDTH)), _full((1, POOL_WIDTH))],
        out_specs=pl.BlockSpec((seq, POOL_WIDTH), lambda b: (b, 0)),
        out_shape=jax.ShapeDtypeStruct((t, POOL_WIDTH), BF16),
        compiler_params=_cparams(1),
    )(pu, w_bd, scale)


def _outproj_kernel(x_ref, od_ref, om_ref, op_ref, wo_ref, g2_ref, wrh_ref, wrl_ref, br_ref,
                    xo_ref, h2_ref, ri_ref, rg_ref, cnt_ref, carry_ref):
    i = pl.program_id(0)

    @pl.when(i == 0)
    def _():
        carry_ref[...] = jnp.zeros_like(carry_ref)

    xn = (x_ref[...]
          + jnp.dot(od_ref[...], wo_ref[0:DIFF_WIDTH, :], preferred_element_type=F32)
          + jnp.dot(om_ref[...], wo_ref[DIFF_WIDTH:DIFF_WIDTH + MLA_WIDTH, :], preferred_element_type=F32)
          + jnp.dot(op_ref[...], wo_ref[DIFF_WIDTH + MLA_WIDTH:, :], preferred_element_type=F32))
    xo_ref[...] = xn
    h2 = xn * lax.rsqrt(jnp.mean(xn * xn, axis=-1, keepdims=True) + RMS_EPS) * g2_ref[...]
    n_chunk = D_MODEL // LANES
    for j in range(n_chunk):
        h2_ref[pl.ds(j, xn.shape[0], stride=n_chunk), :] = h2[:, j * LANES:(j + 1) * LANES]

    h_hi = h2.astype(BF16)
    h_lo = (h2 - h_hi.astype(F32)).astype(BF16)
    logits = (jnp.dot(h_hi, wrh_ref[...], preferred_element_type=F32)
              + jnp.dot(h_lo, wrh_ref[...], preferred_element_type=F32)
              + jnp.dot(h_hi, wrl_ref[...], preferred_element_type=F32)
              + br_ref[...])
    tm = logits.shape[0]
    lane = lax.broadcasted_iota(jnp.int32, (tm, LANES), 1)
    lane_f = lane.astype(F32)
    neg = jnp.float32(-jnp.inf)
    big = jnp.float32(1e9)

    gmask = lane < N_GROUPS
    gl = jnp.where(gmask, logits, neg)
    gmax = jnp.max(gl, axis=-1, keepdims=True)
    gsum = jnp.sum(jnp.where(gmask, jnp.exp(gl - gmax), 0.0), axis=-1, keepdims=True)
    g_top = 1.0 / gsum
    g_idx = jnp.min(jnp.where(gl == gmax, lane_f, big), axis=-1, keepdims=True)

    e_lo = N_GROUPS + EXPERTS_PER_GROUP * g_idx
    emask = (lane_f >= e_lo) & (lane_f < e_lo + EXPERTS_PER_GROUP)
    el = jnp.where(emask, logits, neg)
    emax = jnp.max(el, axis=-1, keepdims=True)
    eexp = jnp.where(emask, jnp.exp(el - emax), 0.0)
    prob = eexp / jnp.sum(eexp, axis=-1, keepdims=True)
    pm = jnp.where(emask, prob, -1.0)
    p1 = jnp.max(pm, axis=-1, keepdims=True)
    i1 = jnp.min(jnp.where(pm == p1, lane_f, big), axis=-1, keepdims=True)
    pm2 = jnp.where(lane_f == i1, -1.0, pm)
    p2 = jnp.max(pm2, axis=-1, keepdims=True)
    i2 = jnp.min(jnp.where(pm2 == p2, lane_f, big), axis=-1, keepdims=True)
    denom = p1 + p2
    gate1 = g_top * p1 / denom
    gate2 = g_top * p2 / denom

    sel1 = lane_f == i1
    sel2 = lane_f == i2
    onehot = jnp.where(sel1 | sel2, 1.0, 0.0)
    rr = lax.broadcasted_iota(jnp.int32, (tm, tm), 0)
    cc = lax.broadcasted_iota(jnp.int32, (tm, tm), 1)
    ltri = jnp.where(cc < rr, 1.0, 0.0).astype(BF16)
    prefix = jnp.dot(ltri, onehot.astype(BF16), preferred_element_type=F32) + carry_ref[...]
    rank1 = jnp.sum(jnp.where(sel1, prefix, 0.0), axis=-1, keepdims=True)
    rank2 = jnp.sum(jnp.where(sel2, prefix, 0.0), axis=-1, keepdims=True)
    carry_ref[...] = carry_ref[...] + jnp.sum(onehot, axis=0, keepdims=True)
    cnt_ref[...] = carry_ref[...]

    info = jnp.where(lane == 0, i1 - N_GROUPS,
                     jnp.where(lane == 1, i2 - N_GROUPS,
                               jnp.where(lane == 2, rank1, jnp.where(lane == 3, rank2, 0.0))))
    ri_ref[...] = info.astype(jnp.int32)
    rg_ref[...] = jnp.where(lane == 0, gate1, jnp.where(lane == 1, gate2, 0.0))


def _outproj_call(x2, od, om, op, p):
    t = x2.shape[0]
    tm = TM_OUT
    row = lambda i: (i, 0)
    return pl.pallas_call(
        _outproj_kernel, grid=(t // tm,),
        in_specs=[pl.BlockSpec((tm, D_MODEL), row), pl.BlockSpec((tm, DIFF_WIDTH), row),
                  pl.BlockSpec((tm, MLA_WIDTH), row), pl.BlockSpec((tm, POOL_WIDTH), row),
                  _full((D_MODEL, D_MODEL)), _full((1, D_MODEL)),
                  _full((D_MODEL, LANES)), _full((D_MODEL, LANES)), _full((1, LANES))],
        out_specs=[pl.BlockSpec((tm, D_MODEL), row),
                   pl.BlockSpec((tm * D_MODEL // LANES, LANES), row),
                   pl.BlockSpec((tm, LANES), row), pl.BlockSpec((tm, LANES), row),
                   _full((1, LANES))],
        out_shape=[jax.ShapeDtypeStruct((t, D_MODEL), F32),
                   jax.ShapeDtypeStruct((t * D_MODEL // LANES, LANES), F32),
                   jax.ShapeDtypeStruct((t, LANES), jnp.int32), jax.ShapeDtypeStruct((t, LANES), F32),
                   jax.ShapeDtypeStruct((1, LANES), F32)],
        scratch_shapes=[pltpu.VMEM((1, LANES), F32)],
        compiler_params=_cparams(1),
    )(x2, od, om, op, p["wo"], p["g2"], p["wrh"], p["wrl"], p["br"])


PAD_CHUNKS = (128, 64, 32, 16, 8, 4, 2, 1)


def _dispatch_kernel(pad_ref, idx_ref, h_ref, xs_hbm, zero_ref, sem, zsem):
    i = pl.program_id(0)
    n_chunk = D_MODEL // LANES
    tm = h_ref.shape[0] // n_chunk

    @pl.when(i == 0)
    def _():
        zero_ref[...] = jnp.zeros_like(zero_ref)

        def pad_copies(e, wait):
            off = pad_ref[0, e]
            cnt = pad_ref[1, e]
            for size in PAD_CHUNKS:
                take = cnt & size

                @pl.when(take != 0)
                def _():
                    cp = pltpu.make_async_copy(
                        zero_ref.at[pl.ds(0, size * n_chunk), :],
                        xs_hbm.at[pl.ds(pl.multiple_of(off * n_chunk, n_chunk), size * n_chunk), :], zsem)
                    if wait:
                        cp.wait()
                    else:
                        cp.start()
                off = off + take

        def start_body(e, c):
            pad_copies(e, False)
            return c

        def wait_body(e, c):
            pad_copies(e, True)
            return c

        lax.fori_loop(0, N_EXPERTS, start_body, 0)
        lax.fori_loop(0, N_EXPERTS, wait_body, 0)

        tail = pad_ref[0, N_EXPERTS]
        zrows = zero_ref.shape[0]

        def tail_copy(c):
            return pltpu.make_async_copy(
                zero_ref, xs_hbm.at[pl.ds(pl.multiple_of(tail * n_chunk + c * zrows, zrows), zrows), :], zsem)

        def tail_start(c, carry):
            tail_copy(c).start()
            return carry

        def tail_wait(c, carry):
            tail_copy(c).wait()
            return carry

        lax.fori_loop(0, pad_ref[1, N_EXPERTS], tail_start, 0)
        lax.fori_loop(0, pad_ref[1, N_EXPERTS], tail_wait, 0)

    for r in range(tm):
        for kk in range(2):
            dst = pl.multiple_of(idx_ref[0, 0, 2 * r + kk] * n_chunk, n_chunk)
            pltpu.make_async_copy(h_ref.at[pl.ds(r * n_chunk, n_chunk), :],
                                  xs_hbm.at[pl.ds(dst, n_chunk), :], sem).start(priority=kk)
    for kk in range(2):
        pltpu.make_async_copy(h_ref, xs_hbm.at[pl.ds(0, tm * n_chunk), :], sem).wait()


def _dispatch_call(pads, dest3, h3, n_slots):
    n_tiles = dest3.shape[0]
    tm = dest3.shape[2] // 2
    n_chunk = D_MODEL // LANES
    grid_spec = pltpu.PrefetchScalarGridSpec(
        num_scalar_prefetch=1, grid=(n_tiles,),
        in_specs=[pl.BlockSpec((1, 1, 2 * tm), lambda i, pads: (i, 0, 0), memory_space=pltpu.SMEM),
                  pl.BlockSpec((tm * n_chunk, LANES), lambda i, pads: (i, 0))],
        out_specs=pl.BlockSpec(memory_space=pl.ANY),
        scratch_shapes=[pltpu.VMEM((PAD_CHUNKS[0] * n_chunk, LANES), F32),
                        pltpu.SemaphoreType.DMA, pltpu.SemaphoreType.DMA])
    return pl.pallas_call(
        _dispatch_kernel, grid_spec=grid_spec,
        out_shape=jax.ShapeDtypeStruct((n_slots * n_chunk, LANES), F32),
        compiler_params=_cparams(1),
    )(pads, dest3, h3)


EXPERT_BLOCKS_PER_STEP = 2


def _expert_kernel(be_ref, nu_ref, xs_ref, *refs):
    i = pl.program_id(0)
    n_chunk = D_MODEL // LANES
    rows_per_block = ROUTE_BLOCK * n_chunk
    ys_ref = refs[-1]
    for b in range(EXPERT_BLOCKS_PER_STEP):
        wg_ref, wu_ref, wd_ref = refs[3 * b:3 * b + 3]
        base = b * rows_per_block
        block = i * EXPERT_BLOCKS_PER_STEP + b

        @pl.when(block < nu_ref[0])
        def _():
            xb = jnp.concatenate([xs_ref[pl.ds(base + j, ROUTE_BLOCK, stride=n_chunk), :]
                                  for j in range(n_chunk)], axis=1).astype(BF16)
            g = jnp.dot(xb, wg_ref[0, 0].astype(BF16), preferred_element_type=F32)
            u = jnp.dot(xb, wu_ref[0, 0].astype(BF16), preferred_element_type=F32)
            hmid = g * (1.0 / (1.0 + jnp.exp(-g))) * u
            y = jnp.dot(hmid.astype(BF16), wd_ref[0, 0].astype(BF16), preferred_element_type=F32)
            for j in range(n_chunk):
                ys_ref[pl.ds(base + j, ROUTE_BLOCK, stride=n_chunk), :] = y[:, j * LANES:(j + 1) * LANES]

        @pl.when(block >= nu_ref[0])
        def _():
            ys_ref[base:base + rows_per_block, :] = jnp.zeros((rows_per_block, LANES), F32)


def _expert_call(block_eid, n_used, xs3, wg, wu, wd, layer):
    per_step = EXPERT_BLOCKS_PER_STEP
    tile = (per_step * ROUTE_BLOCK * D_MODEL // LANES, LANES)
    n_steps = xs3.shape[0] // tile[0]
    row = lambda i, be, nu: (i, 0)
    weight_specs = []
    for b in range(per_step):
        wmap = lambda i, be, nu, b=b: (layer, be[i * per_step + b], 0, 0)
        weight_specs += [pl.BlockSpec((1, 1, D_MODEL, D_FF), wmap), pl.BlockSpec((1, 1, D_MODEL, D_FF), wmap),
                         pl.BlockSpec((1, 1, D_FF, D_MODEL), wmap)]
    grid_spec = pltpu.PrefetchScalarGridSpec(
        num_scalar_prefetch=2, grid=(n_steps,),
        in_specs=[pl.BlockSpec(tile, row)] + weight_specs,
        out_specs=pl.BlockSpec(tile, row))
    return pl.pallas_call(
        _expert_kernel, grid_spec=grid_spec,
        out_shape=jax.ShapeDtypeStruct(xs3.shape, F32),
        compiler_params=_cparams(1),
    )(block_eid, n_used, xs3, *([wg, wu, wd] * per_step))


def _combine_kernel(idx0_ref, idxn_ref, ys_hbm, x_ref, rg_ref, o_ref, buf0, buf1, sem):
    i = pl.program_id(0)
    n = pl.num_programs(0)
    tm = x_ref.shape[0]
    n_chunk = D_MODEL // LANES

    def issue(idx_ref, buf, sem_slot, rows):
        for r in rows:
            for kk in range(2):
                src = pl.multiple_of(idx_ref[0, 0, 2 * r + kk] * n_chunk, n_chunk)
                pltpu.make_async_copy(ys_hbm.at[pl.ds(src, n_chunk), :],
                                      buf.at[pl.ds((kk * tm + r) * n_chunk, n_chunk), :],
                                      sem_slot).start(priority=kk)

    def wait_tile(buf, sem_slot):
        pltpu.make_async_copy(ys_hbm.at[pl.ds(0, 2 * tm * n_chunk), :], buf, sem_slot).wait()

    @pl.when(i == 0)
    def _():
        issue(idx0_ref, buf0, sem.at[0], range(tm))

    def step(buf, sem_cur, buf_next, sem_next):
        issue(idxn_ref, buf_next, sem_next, range(tm))
        wait_tile(buf, sem_cur)
        rg = rg_ref[...]
        g0 = rg[:, 0:1]
        g1 = rg[:, 1:2]
        for j in range(n_chunk):
            cols = slice(j * LANES, (j + 1) * LANES)
            y0 = buf[pl.ds(j, tm, stride=n_chunk), :]
            y1 = buf[pl.ds(tm * n_chunk + j, tm, stride=n_chunk), :]
            o_ref[:, cols] = x_ref[:, cols] + g0 * y0 + g1 * y1

    @pl.when(i % 2 == 0)
    def _():
        step(buf0, sem.at[0], buf1, sem.at[1])

    @pl.when(i % 2 == 1)
    def _():
        step(buf1, sem.at[1], buf0, sem.at[0])

    @pl.when(i == n - 1)
    def _():
        @pl.when(i % 2 == 0)
        def _():
            wait_tile(buf1, sem.at[1])

        @pl.when(i % 2 == 1)
        def _():
            wait_tile(buf0, sem.at[0])


def _combine_call(dest3, ys, x2, rg):
    t = x2.shape[0]
    tm = TM_COMB
    n = t // tm
    row = lambda i: (i, 0)
    tile = (2 * tm * D_MODEL // LANES, LANES)
    return pl.pallas_call(
        _combine_kernel, grid=(n,),
        in_specs=[
            pl.BlockSpec((1, 1, 2 * tm), lambda i: (0, 0, 0), memory_space=pltpu.SMEM),
            pl.BlockSpec((1, 1, 2 * tm), lambda i: (jnp.minimum(i + 1, n - 1), 0, 0), memory_space=pltpu.SMEM),
            pl.BlockSpec(memory_space=pl.ANY),
            pl.BlockSpec((tm, D_MODEL), row), pl.BlockSpec((tm, LANES), row)],
        out_specs=pl.BlockSpec((tm, D_MODEL), row),
        out_shape=jax.ShapeDtypeStruct((t, D_MODEL), F32),
        scratch_shapes=[pltpu.VMEM(tile, F32), pltpu.VMEM(tile, F32), pltpu.SemaphoreType.DMA((2,))],
        compiler_params=_cparams(1),
    )(dest3, dest3, ys, x2, rg)


def _swap_halves(a):
    half = a.shape[-1] // 2
    return jnp.concatenate([a[..., half:], a[..., :half]], axis=-1)


def _layer_params(l, seq, w):
    p = {}
    row = lambda v: v.reshape(1, -1).astype(F32)
    w_in = w["w_in"][l]
    kr_cols = w_in[:, 1856:1888]
    p["win"] = jnp.concatenate(
        [w_in[:, 0:1536], w_in[:, 1888:2144], w_in[:, 1728:1856], w_in[:, 1536:1728],
         kr_cols, _swap_halves(kr_cols)], axis=1).astype(BF16)
    p["g1"] = row(w["norm1_g"][l])
    p["gq"] = row(jnp.tile(w["diff_q_norm_g"][l], 2) * (DIFF_QK ** -0.5))
    p["gk"] = row(jnp.tile(w["diff_k_norm_g"][l], 2))
    p["gckv"] = row(w["mla_kv_lat_norm_g"][l])
    gcq = w["mla_q_lat_norm_g"][l]
    p["gcqa"] = row(gcq[:LANES])
    p["gcqb"] = row(jnp.concatenate([gcq[LANES:], jnp.zeros((2 * LANES - MLA_Q_RANK,), F32)]))

    wuq = w["mla_w_uq"][l].reshape(MLA_Q_RANK, MLA_HEADS, MLA_NOPE + MLA_ROPE)
    rope_w = wuq[:, :, MLA_NOPE:]
    wuq = jnp.concatenate([wuq[:, :, :MLA_NOPE], rope_w, _swap_halves(rope_w)], axis=-1)
    wuq = wuq.reshape(MLA_Q_RANK, MLA_HEADS * HEAD_SLOT)
    wuq = jnp.concatenate([wuq, jnp.zeros((2 * LANES - MLA_Q_RANK, wuq.shape[1]), F32)], axis=0).astype(BF16)
    p["wuqa"] = wuq[:LANES]
    p["wuqb"] = wuq[LANES:]

    wukv = w["mla_w_ukv"][l].reshape(MLA_KV_RANK, MLA_HEADS, MLA_NOPE + MLA_V)
    zk = jnp.zeros((MLA_KV_RANK, MLA_HEADS, HEAD_SLOT - MLA_NOPE), F32)
    p["wkk"] = jnp.concatenate([wukv[:, :, :MLA_NOPE], zk], axis=-1).reshape(MLA_KV_RANK, -1).astype(BF16)
    vcols = wukv[:, :, MLA_NOPE:]
    zv = jnp.zeros_like(vcols)
    even = (jnp.arange(MLA_HEADS) % 2 == 0)[None, :, None]
    wkv = jnp.concatenate([jnp.where(even, vcols, zv), jnp.where(even, zv, vcols)], axis=-1)
    p["wkv"] = wkv.reshape(MLA_KV_RANK, -1).astype(BF16)
    p["gkn"] = row(jnp.concatenate([w["mla_k_nope_norm_g"][l], jnp.zeros((HEAD_SLOT - MLA_NOPE,), F32)]))

    inv = 1.0 / (ROPE_BASE ** (jnp.arange(0, MLA_ROPE, 2, dtype=F32) / MLA_ROPE))
    ang = jnp.arange(seq, dtype=F32)[:, None] * inv[None, :]
    cosf = jnp.concatenate([jnp.cos(ang), jnp.cos(ang)], axis=-1)
    sinf = jnp.concatenate([-jnp.sin(ang), jnp.sin(ang)], axis=-1)
    scale = (MLA_NOPE + MLA_ROPE) ** -0.5 * math.log2(math.e)
    gqr = w["mla_q_rope_norm_g"][l]
    q_head = jnp.concatenate([jnp.broadcast_to(w["mla_q_nope_norm_g"][l][None, :], (seq, MLA_NOPE)),
                              gqr[None, :] * cosf, _swap_halves(gqr)[None, :] * sinf], axis=-1) * scale
    p["qtab"] = jnp.tile(q_head, (1, MLA_HEADS))
    gkr = w["mla_k_rope_norm_g"][l]
    p["ktab"] = jnp.concatenate([jnp.zeros((seq, MLA_NOPE), F32), gkr[None, :] * cosf,
                                 _swap_halves(gkr)[None, :] * sinf], axis=-1)
    src = jnp.arange(LANES)
    dst = jnp.arange(MLA_HEADS * HEAD_SLOT)
    src_j = jnp.where(src >= MLA_NOPE, (src - MLA_NOPE) % MLA_ROPE, -1)
    dst_l = dst % HEAD_SLOT
    dst_j = jnp.where(dst_l >= MLA_NOPE, (dst_l - MLA_NOPE) % MLA_ROPE, -2)
    p["eplace"] = (src_j[:, None] == dst_j[None, :]).astype(BF16)

    pw = w["pool_w"][l]
    bd = jnp.zeros((POOL_WIDTH, POOL_WIDTH), F32)
    for g in range(POOL_GROUPS):
        s0 = g * POOL_GROUP_DIM
        bd = bd.at[s0:s0 + POOL_GROUP_DIM, s0:s0 + POOL_GROUP_DIM].set(pw[g])
    p["pool_w"] = bd.astype(BF16)
    p["pool_scale"] = row(w["pool_scale"][l])

    lam_init = 0.8 - 0.6 * math.exp(-0.3 * l)
    lv = w["diff_lambda"][l].astype(F32)
    p["lam"] = (jnp.exp(jnp.sum(lv[0] * lv[1])) - jnp.exp(jnp.sum(lv[2] * lv[3])) + lam_init).reshape(1)
    p["gsub"] = row(w["diff_sub_norm_g"][l] * (1.0 - lam_init))

    p["wo"] = w["w_out"][l].astype(BF16)
    p["g2"] = row(w["norm2_g"][l])
    wr = jnp.concatenate([w["router_group_w"][l], w["router_expert_w"][l],
                          jnp.zeros((D_MODEL, LANES - N_GROUPS - N_EXPERTS), F32)], axis=1)
    wr_hi = wr.astype(BF16)
    p["wrh"] = wr_hi
    p["wrl"] = (wr - wr_hi.astype(F32)).astype(BF16)
    p["br"] = row(jnp.concatenate([w["router_group_b"][l], w["router_expert_b"][l],
                                   jnp.zeros((LANES - N_GROUPS - N_EXPERTS,), F32)]))
    return p


def kernel(x, norm1_g, w_in, diff_q_norm_g, diff_k_norm_g, diff_lambda, diff_sub_norm_g, mla_q_lat_norm_g, mla_kv_lat_norm_g, mla_w_uq, mla_w_ukv, mla_q_nope_norm_g, mla_q_rope_norm_g, mla_k_nope_norm_g, mla_k_rope_norm_g, pool_w, pool_scale, w_out, norm2_g, router_group_w, router_group_b, router_expert_w, router_expert_b, expert_w_gate, expert_w_up, expert_w_down):
    w = dict(norm1_g=norm1_g, w_in=w_in, diff_q_norm_g=diff_q_norm_g, diff_k_norm_g=diff_k_norm_g,
             diff_lambda=diff_lambda, diff_sub_norm_g=diff_sub_norm_g, mla_q_lat_norm_g=mla_q_lat_norm_g,
             mla_kv_lat_norm_g=mla_kv_lat_norm_g, mla_w_uq=mla_w_uq, mla_w_ukv=mla_w_ukv,
             mla_q_nope_norm_g=mla_q_nope_norm_g, mla_q_rope_norm_g=mla_q_rope_norm_g,
             mla_k_nope_norm_g=mla_k_nope_norm_g, mla_k_rope_norm_g=mla_k_rope_norm_g,
             pool_w=pool_w, pool_scale=pool_scale, w_out=w_out, norm2_g=norm2_g,
             router_group_w=router_group_w, router_group_b=router_group_b,
             router_expert_w=router_expert_w, router_expert_b=router_expert_b)
    batch, seq, d = x.shape
    t = batch * seq
    n_assign = 2 * t
    n_blocks = n_assign // ROUTE_BLOCK + N_EXPERTS
    alibi = _alibi_tables(seq)

    x2 = x.reshape(t, d)
    for l in range(DEPTH):
        p = _layer_params(l, seq, w)
        dq1, dq2, dk, dv, qm, km, vm, pu = _proj_call(x2, p, seq)
        o_diff = _diff_call(p["lam"], dq1, dq2, dk, dv, p["gsub"], alibi, batch, seq)
        o_mla = _mla_call(qm, km, vm, batch, seq)
        o_pool = _pool_call(pu, p["pool_w"], p["pool_scale"], batch, seq)
        x2, h2, route_i, route_g, counts = _outproj_call(x2, o_diff, o_mla, o_pool, p)

        cnt = counts[0, N_GROUPS:N_GROUPS + N_EXPERTS].astype(jnp.int32)
        padded = (cnt + ROUTE_BLOCK - 1) // ROUTE_BLOCK * ROUTE_BLOCK
        padded_ends = jnp.cumsum(padded)
        padded_starts = padded_ends - padded
        eid = route_i[:, 0:2]
        start_of = jnp.sum(jnp.where(eid[..., None] == jnp.arange(N_EXPERTS, dtype=jnp.int32),
                                     padded_starts, 0), axis=-1)
        dest = start_of + route_i[:, 2:4]
        block_start = jnp.arange(n_blocks, dtype=jnp.int32) * ROUTE_BLOCK
        block_eid = jnp.minimum(jnp.sum(block_start[:, None] >= padded_ends[None, :], axis=1),
                                N_EXPERTS - 1).astype(jnp.int32)
        n_used = (padded_ends[-1] // ROUTE_BLOCK).astype(jnp.int32).reshape(1)
        n_tail = (n_blocks - n_used) * (ROUTE_BLOCK // PAD_CHUNKS[0])
        pads = jnp.stack([jnp.concatenate([padded_starts + cnt, padded_ends[-1:]]),
                          jnp.concatenate([padded - cnt, n_tail])]).astype(jnp.int32)
        dest3 = dest.reshape(t // TM_COMB, 1, 2 * TM_COMB)

        xs = _dispatch_call(pads, dest3, h2, n_blocks * ROUTE_BLOCK)
        ys = _expert_call(block_eid, n_used, xs, expert_w_gate, expert_w_up, expert_w_down, l)
        x2 = _combine_call(dest3, ys, x2, route_g)
    return x2.reshape(batch, seq, d)
```

```python
import functools
import math

import jax
import jax.numpy as jnp
from jax import lax
from jax.experimental import pallas as pl
from jax.experimental.pallas import tpu as pltpu

F32 = jnp.float32
BF16 = jnp.bfloat16

D_MODEL = 1024
DEPTH = 2
DIFF_HEADS = 4
DIFF_QK = 64
DIFF_V = 128
DIFF_WIDTH = 512
MLA_HEADS = 4
MLA_NOPE = 64
MLA_ROPE = 32
MLA_V = 64
MLA_Q_RANK = 192
MLA_KV_RANK = 128
MLA_WIDTH = 256
ROPE_BASE = 10000.0
POOL_WIDTH = 256
POOL_GROUPS = 4
POOL_GROUP_DIM = 64
POOL_WINDOWS = (2, 4, 8, 16)
N_GROUPS = 4
EXPERTS_PER_GROUP = 8
N_EXPERTS = 32
D_FF = 256
ROUTE_BLOCK = 256
RMS_EPS = 1e-6

LANES = 128
HEAD_SLOT = 128
PROJ_WIDTH = 2176
POOL_PAD = 16
VMEM_LIMIT = 48 * 1024 * 1024

TM_PROJ = 512
TQ = 512
TK_DIFF = 256
TK_MLA = 512
TM_OUT = 1024
TM_DISP = 1024
TM_COMB = 256

NT_DIMS = (((1,), (1,)), ((), ()))


def _cparams(n_axes):
    return pltpu.CompilerParams(dimension_semantics=("arbitrary",) * n_axes,
                                vmem_limit_bytes=VMEM_LIMIT)


def _full(shape):
    return pl.BlockSpec(shape, lambda *_: (0,) * len(shape))


def _proj_kernel(x_ref, g1_ref, win_ref, gq_ref, gk_ref, gckv_ref, gcqa_ref, gcqb_ref,
                 wuqa_ref, wuqb_ref, wkk_ref, wkv_ref, gkn_ref, qtab_ref, ktab_ref, eplace_ref,
                 dq1_ref, dq2_ref, dk_ref, dv_ref, qm_ref, km_ref, vm_ref, pu_ref):
    x = x_ref[...]
    xn = x * lax.rsqrt(jnp.mean(x * x, axis=-1, keepdims=True) + RMS_EPS) * g1_ref[...]
    proj = jnp.dot(xn.astype(BF16), win_ref[...], preferred_element_type=F32)

    tm = x.shape[0]
    lane = lax.broadcasted_iota(jnp.int32, (tm, LANES), 1)
    lo = lane < DIFF_QK

    def half_norm(c, g_row):
        sq = c * c
        s_lo = jnp.sum(jnp.where(lo, sq, 0.0), axis=-1, keepdims=True)
        s_hi = jnp.sum(jnp.where(lo, 0.0, sq), axis=-1, keepdims=True)
        r = jnp.where(lo, lax.rsqrt(s_lo / DIFF_QK + RMS_EPS), lax.rsqrt(s_hi / DIFF_QK + RMS_EPS))
        return c * r * g_row

    for h in range(DIFF_HEADS):
        sl = slice(h * HEAD_SLOT, (h + 1) * HEAD_SLOT)
        qn = half_norm(proj[:, sl], gq_ref[...])
        dq1_ref[:, sl] = jnp.where(lo, qn, 0.0).astype(BF16)
        dq2_ref[:, sl] = jnp.where(lo, 0.0, qn).astype(BF16)
        ksl = slice(512 + h * HEAD_SLOT, 512 + (h + 1) * HEAD_SLOT)
        dk_ref[:, sl] = half_norm(proj[:, ksl], gk_ref[...]).astype(BF16)
    dv_ref[...] = proj[:, 1024:1536].astype(BF16)
    pu_ref[...] = proj[:, 1536:1792]

    ckv = proj[:, 1792:1920]
    ckvn = ckv * lax.rsqrt(jnp.mean(ckv * ckv, axis=-1, keepdims=True) + RMS_EPS) * gckv_ref[...]
    ckvn = ckvn.astype(BF16)
    cqa = proj[:, 1920:2048]
    last = proj[:, 2048:2176]
    lsq = last * last
    ss_q = (jnp.sum(cqa * cqa, axis=-1, keepdims=True)
            + jnp.sum(jnp.where(lo, lsq, 0.0), axis=-1, keepdims=True))
    r_q = lax.rsqrt(ss_q / MLA_Q_RANK + RMS_EPS)
    q_raw = (jnp.dot((cqa * r_q * gcqa_ref[...]).astype(BF16), wuqa_ref[...], preferred_element_type=F32)
             + jnp.dot((last * r_q * gcqb_ref[...]).astype(BF16), wuqb_ref[...], preferred_element_type=F32))

    rope_lanes = (lane >= MLA_NOPE) & (lane < MLA_NOPE + MLA_ROPE)
    ss_kr = jnp.sum(jnp.where(rope_lanes, lsq, 0.0), axis=-1, keepdims=True)
    kr_terms = last * lax.rsqrt(ss_kr / MLA_ROPE + RMS_EPS) * ktab_ref[...]
    kr_placed = jnp.dot(kr_terms.astype(BF16), eplace_ref[...], preferred_element_type=F32)

    k_raw = jnp.dot(ckvn, wkk_ref[...], preferred_element_type=F32)
    vm_ref[...] = jnp.dot(ckvn, wkv_ref[...], preferred_element_type=F32).astype(BF16)
    qtab = qtab_ref[...]
    for h in range(MLA_HEADS):
        sl = slice(h * HEAD_SLOT, (h + 1) * HEAD_SLOT)
        c = q_raw[:, sl]
        sq = c * c
        s_n = jnp.sum(jnp.where(lo, sq, 0.0), axis=-1, keepdims=True)
        s_r = jnp.sum(jnp.where(rope_lanes, sq, 0.0), axis=-1, keepdims=True)
        r = jnp.where(lo, lax.rsqrt(s_n / MLA_NOPE + RMS_EPS), lax.rsqrt(s_r / MLA_ROPE + RMS_EPS))
        qm_ref[:, sl] = (c * r * qtab[:, sl]).astype(BF16)
        kc = k_raw[:, sl]
        r_k = lax.rsqrt(jnp.sum(kc * kc, axis=-1, keepdims=True) / MLA_NOPE + RMS_EPS)
        km_ref[:, sl] = (kc * r_k * gkn_ref[...] + kr_placed[:, sl]).astype(BF16)


def _proj_call(x2, p, seq):
    t = x2.shape[0]
    tm = TM_PROJ
    n_pos = seq // tm
    row = lambda i: (i, 0)
    pos = lambda i: (i % n_pos, 0)
    bf = lambda w: jax.ShapeDtypeStruct((t, w), BF16)
    in_specs = [
        pl.BlockSpec((tm, D_MODEL), row),
        _full((1, D_MODEL)), _full((D_MODEL, PROJ_WIDTH)),
        _full((1, LANES)), _full((1, LANES)), _full((1, LANES)), _full((1, LANES)), _full((1, LANES)),
        _full((LANES, 512)), _full((LANES, 512)), _full((LANES, 512)), _full((LANES, 512)),
        _full((1, LANES)),
        pl.BlockSpec((tm, 512), pos), pl.BlockSpec((tm, LANES), pos),
        _full((LANES, 512)),
    ]
    out_specs = [pl.BlockSpec((tm, 512), row)] * 7 + [pl.BlockSpec((tm, POOL_WIDTH), row)]
    out_shape = [bf(512)] * 7 + [jax.ShapeDtypeStruct((t, POOL_WIDTH), F32)]
    return pl.pallas_call(
        _proj_kernel, grid=(t // tm,), in_specs=in_specs, out_specs=out_specs, out_shape=out_shape,
        compiler_params=_cparams(1),
    )(x2, p["g1"], p["win"], p["gq"], p["gk"], p["gckv"], p["gcqa"], p["gcqb"],
      p["wuqa"], p["wuqb"], p["wkk"], p["wkv"], p["gkn"], p["qtab"], p["ktab"], p["eplace"])


def _unflatten(n, sizes):
    n = jnp.minimum(n, math.prod(sizes) - 1)
    coords = []
    for size in reversed(sizes):
        coords.append(n % size)
        n = n // size
    return tuple(reversed(coords))


def _two_stage(n, stage, bufs):
    (s0, m0), (s1, m1) = bufs

    @pl.when(n == 0)
    def _():
        s1[...] = jnp.zeros_like(s1)
        m1[...] = jnp.zeros_like(m1)

    @pl.when(n % 2 == 0)
    def _():
        stage((s0, m0), (s1, m1))

    @pl.when(n % 2 == 1)
    def _():
        stage((s1, m1), (s0, m0))


def _two_stage_scratch(seq):
    pair = [pltpu.VMEM((2 * TQ, seq), F32), pltpu.VMEM((2 * TQ, LANES), F32)]
    return pair + pair


def _softmax_pv_tile(s_prev, m_rows, c, tq, tk, lsum, acc, v_tiles, exp_fn):
    n_half = tk // LANES
    ps = [exp_fn(s_prev[:, (c * n_half + j) * LANES:(c * n_half + j + 1) * LANES] - m_rows)
          for j in range(n_half)]
    for ch in ps:
        lsum = ch if lsum is None else lsum + ch
    pb = jnp.concatenate(ps, axis=1).astype(BF16)
    for g, vt in enumerate(v_tiles):
        pv = jnp.dot(pb[g * tq:(g + 1) * tq], vt, preferred_element_type=F32)
        acc[g] = pv if acc[g] is None else acc[g] + pv
    return lsum


def _running_max(mx, sc):
    for j in range(sc.shape[1] // LANES):
        chunk = sc[:, j * LANES:(j + 1) * LANES]
        mx = chunk if mx is None else jnp.maximum(mx, chunk)
    return mx


def _diff_kernel(lam_ref, q1_ref, q2_ref, qx_ref, k_ref, kx_ref, bd_ref, v_ref, gsub_ref, o_ref,
                 s0_ref, m0_ref, s1_ref, m1_ref, *, sizes):
    n = pl.program_id(0)
    tq = q1_ref.shape[0]
    tk = TK_DIFF
    n_kt = k_ref.shape[0] // tk
    n_diag = tq // tk
    first_cur = _unflatten(n, sizes)[2] * n_diag
    first_prev = _unflatten(jnp.maximum(n - 1, 0), sizes)[2] * n_diag

    def stage(cur, prev):
        s_cur, m_cur = cur
        s_prev, m_prev = prev
        q1, q2 = q1_ref[...], q2_ref[...]
        qx_left = qx_ref[0, 0]
        qx_right = -qx_left
        bd = bd_ref[0]
        bd2 = jnp.concatenate([bd, bd], axis=0)
        mx = None
        m_rows = m_prev[...]
        lsum = None
        acc = [None, None]
        for c in range(n_kt):
            start = pl.multiple_of(((first_prev + c) % n_kt) * tk, tk)
            vt = v_ref[pl.ds(start, tk), :]
            lsum = _softmax_pv_tile(s_prev, m_rows, c, tq, tk, lsum, acc, [vt, vt], jnp.exp)

            start = pl.multiple_of(((first_cur + c) % n_kt) * tk, tk)
            if c < n_diag:
                qx = jnp.zeros_like(qx_left)
            else:
                qx = jnp.where(first_cur + c >= n_kt, qx_left, qx_right)
            qq = jnp.concatenate([jnp.concatenate([q1, qx], axis=1),
                                  jnp.concatenate([q2, qx], axis=1)], axis=0)
            kk = jnp.concatenate([k_ref[pl.ds(start, tk), :], kx_ref[pl.ds(start, tk), :]], axis=1)
            sc = lax.dot_general(qq, kk, NT_DIMS, preferred_element_type=F32)
            if c < n_diag:
                sc = sc + bd2[:, c * tk:(c + 1) * tk]
            s_cur[:, c * tk:(c + 1) * tk] = sc
            mx = _running_max(mx, sc)
        m_cur[...] = jnp.broadcast_to(jnp.max(mx, axis=-1, keepdims=True), m_cur.shape)
        l = jnp.sum(lsum, axis=-1, keepdims=True)
        o = acc[0] * (1.0 / l[0:tq]) - acc[1] * (lam_ref[0] / l[tq:2 * tq])
        r = lax.rsqrt(jnp.mean(o * o, axis=-1, keepdims=True) + RMS_EPS)
        o_ref[...] = (o * r * gsub_ref[...]).astype(BF16)

    _two_stage(n, stage, ((s0_ref, m0_ref), (s1_ref, m1_ref)))


def _alibi_tables(seq):
    nq = seq // TQ
    slopes = 2.0 ** (-8.0 * jnp.arange(1, DIFF_HEADS + 1, dtype=F32) / DIFF_HEADS)
    pos = jnp.arange(seq, dtype=jnp.int32)
    hi = (pos // 256).astype(F32)
    lo = (pos % 256).astype(F32)
    s4 = slopes[:, None]
    ones = jnp.ones((DIFF_HEADS, seq), F32)
    q_left = jnp.stack([-s4 * 256.0 * hi[None], -s4 * lo[None], s4 * 256.0 * ones, s4 * ones], axis=-1)
    qx = jnp.concatenate([q_left, jnp.zeros((DIFF_HEADS, seq, HEAD_SLOT - 4), F32)], axis=-1)
    qx = qx.reshape(DIFF_HEADS, nq, TQ, HEAD_SLOT).astype(BF16)
    k_cols = jnp.stack([jnp.ones((seq,), F32), jnp.ones((seq,), F32), hi, lo], axis=-1)
    kx = jnp.concatenate([k_cols, jnp.zeros((seq, HEAD_SLOT - 4), F32)], axis=-1).astype(BF16)
    loc = jnp.arange(TQ, dtype=jnp.int32)
    bd = -slopes[:, None, None] * jnp.abs(loc[:, None] - loc[None, :]).astype(F32)[None]
    return qx, kx, bd


def _diff_call(lam, dq1, dq2, dk, dv, gsub, tabs, batch, seq):
    t = dq1.shape[0]
    nq = seq // TQ
    qx, kx, bd = tabs
    sizes = (batch, DIFF_HEADS, nq)
    cur = lambda n: _unflatten(n, sizes)
    prev = lambda n: _unflatten(jnp.maximum(n - 1, 0), sizes)

    def qmap(n, *_):
        b, h, qi = cur(n)
        return (b * nq + qi, h)

    def kmap(n, *_):
        b, h, qi = cur(n)
        return (b, h)

    def vmap(n, *_):
        b, h, qi = prev(n)
        return (b, h)

    def omap(n, *_):
        b, h, qi = prev(n)
        return (b * nq + qi, h)

    grid_spec = pltpu.PrefetchScalarGridSpec(
        num_scalar_prefetch=1, grid=(math.prod(sizes) + 1,),
        in_specs=[pl.BlockSpec((TQ, HEAD_SLOT), qmap), pl.BlockSpec((TQ, HEAD_SLOT), qmap),
                  pl.BlockSpec((1, 1, TQ, HEAD_SLOT), lambda n, *_: cur(n)[1:] + (0, 0)),
                  pl.BlockSpec((seq, HEAD_SLOT), kmap),
                  pl.BlockSpec((seq, HEAD_SLOT), lambda *_: (0, 0)),
                  pl.BlockSpec((1, TQ, TQ), lambda n, *_: (cur(n)[1], 0, 0)),
                  pl.BlockSpec((seq, HEAD_SLOT), vmap),
                  pl.BlockSpec((1, HEAD_SLOT), lambda *_: (0, 0))],
        out_specs=pl.BlockSpec((TQ, HEAD_SLOT), omap),
        scratch_shapes=_two_stage_scratch(seq))
    return pl.pallas_call(
        functools.partial(_diff_kernel, sizes=sizes), grid_spec=grid_spec,
        out_shape=jax.ShapeDtypeStruct((t, DIFF_WIDTH), BF16),
        compiler_params=_cparams(1),
    )(lam, dq1, dq2, qx, dk, kx, bd, dv, gsub)


def _mla_kernel(q_ref, k_ref, v_ref, o_ref, s0_ref, m0_ref, s1_ref, m1_ref):
    n = pl.program_id(0)
    tq = q_ref.shape[0]
    tk = TK_MLA
    n_kt = k_ref.shape[0] // tk

    def stage(cur, prev):
        s_cur, m_cur = cur
        s_prev, m_prev = prev
        mx = [None, None]
        m_rows = m_prev[...]
        lsum = None
        acc = [None, None]
        for c in range(n_kt):
            rows = slice(c * tk, (c + 1) * tk)
            v_tiles = [v_ref[rows, hh * HEAD_SLOT:(hh + 1) * HEAD_SLOT] for hh in range(2)]
            lsum = _softmax_pv_tile(s_prev, m_rows, c, tq, tk, lsum, acc, v_tiles, jnp.exp2)
            for hh in range(2):
                sl = slice(hh * HEAD_SLOT, (hh + 1) * HEAD_SLOT)
                sc = lax.dot_general(q_ref[:, sl], k_ref[rows, sl], NT_DIMS, preferred_element_type=F32)
                s_cur[hh * tq:(hh + 1) * tq, rows] = sc
                mx[hh] = _running_max(mx[hh], sc)
        mx = jnp.concatenate(mx, axis=0)
        m_cur[...] = jnp.broadcast_to(jnp.max(mx, axis=-1, keepdims=True), m_cur.shape)
        l = jnp.sum(lsum, axis=-1, keepdims=True)
        o = acc[0] * (1.0 / l[0:tq]) + acc[1] * (1.0 / l[tq:2 * tq])
        o_ref[...] = o.astype(BF16)

    _two_stage(n, stage, ((s0_ref, m0_ref), (s1_ref, m1_ref)))


def _mla_call(qm, km, vm, batch, seq):
    t = qm.shape[0]
    nq = seq // TQ
    sizes = (batch, MLA_HEADS // 2, nq)
    cur = lambda n: _unflatten(n, sizes)
    prev = lambda n: _unflatten(jnp.maximum(n - 1, 0), sizes)

    def qmap(n):
        b, p, qi = cur(n)
        return (b * nq + qi, p)

    def kmap(n):
        b, p, qi = cur(n)
        return (b, p)

    def vmap(n):
        b, p, qi = prev(n)
        return (b, p)

    def omap(n):
        b, p, qi = prev(n)
        return (b * nq + qi, p)

    return pl.pallas_call(
        _mla_kernel, grid=(math.prod(sizes) + 1,),
        in_specs=[pl.BlockSpec((TQ, 2 * HEAD_SLOT), qmap), pl.BlockSpec((seq, 2 * HEAD_SLOT), kmap),
                  pl.BlockSpec((seq, 2 * HEAD_SLOT), vmap)],
        out_specs=pl.BlockSpec((TQ, HEAD_SLOT), omap),
        out_shape=jax.ShapeDtypeStruct((t, MLA_WIDTH), BF16),
        scratch_shapes=_two_stage_scratch(seq),
        compiler_params=_cparams(1),
    )(qm, km, vm)


def _pool_kernel(u_ref, w_ref, scale_ref, o_ref):
    u = u_ref[...]
    seq, width = u.shape
    zpad = jnp.zeros((POOL_PAD, width), F32)
    ue = jnp.concatenate([zpad, u, zpad], axis=0)
    n = seq + 2 * POOL_PAD

    def down(a, k):
        return pltpu.roll(a, k, axis=0)

    def up(a, k):
        return pltpu.roll(a, n - k, axis=0)

    a2 = ue + down(ue, 1)
    a4 = down(a2, 1) + up(a2, 1)
    a8 = down(a4, 2) + up(a4, 2)
    a16 = down(a8, 4) + up(a8, 4)
    core = slice(POOL_PAD, POOL_PAD + seq)
    lane = lax.broadcasted_iota(jnp.int32, (seq, width), 1)
    tpos = lax.broadcasted_iota(jnp.int32, (seq, width), 0)
    grp = lane // POOL_GROUP_DIM
    win_sum = jnp.where(grp == 0, a2[core], jnp.where(grp == 1, a4[core], jnp.where(grp == 2, a8[core], a16[core])))
    half = jnp.where(grp == 0, 1, jnp.where(grp == 1, 2, jnp.where(grp == 2, 4, 8)))
    lo_i = jnp.maximum(tpos - half, 0)
    hi_i = jnp.minimum(tpos + half - 1, seq - 1)
    cnt = (hi_i - lo_i + 1).astype(F32)
    pooled = win_sum / cnt - u
    mixed = jnp.dot(pooled.astype(BF16), w_ref[...], preferred_element_type=F32)
    o_ref[...] = (mixed * scale_ref[...]).astype(BF16)


def _pool_call(pu, w_bd, scale, batch, seq):
    t = pu.shape[0]
    return pl.pallas_call(
        _pool_kernel, grid=(batch,),
        in_specs=[pl.BlockSpec((seq, POOL_WIDTH), lambda b: (b, 0)),
                  _full((POOL_WIDTH, POOL_WIDTH)), _full((1, POOL_WIDTH))],
        out_specs=pl.BlockSpec((seq, POOL_WIDTH), lambda b: (b, 0)),
        out_shape=jax.ShapeDtypeStruct((t, POOL_WIDTH), BF16),
        compiler_params=_cparams(1),
    )(pu, w_bd, scale)


def _outproj_kernel(x_ref, od_ref, om_ref, op_ref, wo_ref, g2_ref, wrh_ref, wrl_ref, br_ref,
                    xo_ref, h2_ref, ri_ref, rg_ref, cnt_ref, carry_ref):
    i = pl.program_id(0)

    @pl.when(i == 0)
    def _():
        carry_ref[...] = jnp.zeros_like(carry_ref)

    xn = (x_ref[...]
          + jnp.dot(od_ref[...], wo_ref[0:DIFF_WIDTH, :], preferred_element_type=F32)
          + jnp.dot(om_ref[...], wo_ref[DIFF_WIDTH:DIFF_WIDTH + MLA_WIDTH, :], preferred_element_type=F32)
          + jnp.dot(op_ref[...], wo_ref[DIFF_WIDTH + MLA_WIDTH:, :], preferred_element_type=F32))
    xo_ref[...] = xn
    h2 = xn * lax.rsqrt(jnp.mean(xn * xn, axis=-1, keepdims=True) + RMS_EPS) * g2_ref[...]
    n_chunk = D_MODEL // LANES
    for j in range(n_chunk):
        h2_ref[pl.ds(j, xn.shape[0], stride=n_chunk), :] = h2[:, j * LANES:(j + 1) * LANES]

    h_hi = h2.astype(BF16)
    h_lo = (h2 - h_hi.astype(F32)).astype(BF16)
    logits = (jnp.dot(h_hi, wrh_ref[...], preferred_element_type=F32)
              + jnp.dot(h_lo, wrh_ref[...], preferred_element_type=F32)
              + jnp.dot(h_hi, wrl_ref[...], preferred_element_type=F32)
              + br_ref[...])
    tm = logits.shape[0]
    lane = lax.broadcasted_iota(jnp.int32, (tm, LANES), 1)
    lane_f = lane.astype(F32)
    neg = jnp.float32(-jnp.inf)
    big = jnp.float32(1e9)

    gmask = lane < N_GROUPS
    gl = jnp.where(gmask, logits, neg)
    gmax = jnp.max(gl, axis=-1, keepdims=True)
    gsum = jnp.sum(jnp.where(gmask, jnp.exp(gl - gmax), 0.0), axis=-1, keepdims=True)
    g_top = 1.0 / gsum
    g_idx = jnp.min(jnp.where(gl == gmax, lane_f, big), axis=-1, keepdims=True)

    e_lo = N_GROUPS + EXPERTS_PER_GROUP * g_idx
    emask = (lane_f >= e_lo) & (lane_f < e_lo + EXPERTS_PER_GROUP)
    el = jnp.where(emask, logits, neg)
    emax = jnp.max(el, axis=-1, keepdims=True)
    eexp = jnp.where(emask, jnp.exp(el - emax), 0.0)
    prob = eexp / jnp.sum(eexp, axis=-1, keepdims=True)
    pm = jnp.where(emask, prob, -1.0)
    p1 = jnp.max(pm, axis=-1, keepdims=True)
    i1 = jnp.min(jnp.where(pm == p1, lane_f, big), axis=-1, keepdims=True)
    pm2 = jnp.where(lane_f == i1, -1.0, pm)
    p2 = jnp.max(pm2, axis=-1, keepdims=True)
    i2 = jnp.min(jnp.where(pm2 == p2, lane_f, big), axis=-1, keepdims=True)
    denom = p1 + p2
    gate1 = g_top * p1 / denom
    gate2 = g_top * p2 / denom

    sel1 = lane_f == i1
    sel2 = lane_f == i2
    onehot = jnp.where(sel1 | sel2, 1.0, 0.0)
    rr = lax.broadcasted_iota(jnp.int32, (tm, tm), 0)
    cc = lax.broadcasted_iota(jnp.int32, (tm, tm), 1)
    ltri = jnp.where(cc < rr, 1.0, 0.0).astype(BF16)
    prefix = jnp.dot(ltri, onehot.astype(BF16), preferred_element_type=F32) + carry_ref[...]
    rank1 = jnp.sum(jnp.where(sel1, prefix, 0.0), axis=-1, keepdims=True)
    rank2 = jnp.sum(jnp.where(sel2, prefix, 0.0), axis=-1, keepdims=True)
    carry_ref[...] = carry_ref[...] + jnp.sum(onehot, axis=0, keepdims=True)
    cnt_ref[...] = carry_ref[...]

    info = jnp.where(lane == 0, i1 - N_GROUPS,
                     jnp.where(lane == 1, i2 - N_GROUPS,
                               jnp.where(lane == 2, rank1, jnp.where(lane == 3, rank2, 0.0))))
    ri_ref[...] = info.astype(jnp.int32)
    rg_ref[...] = jnp.where(lane == 0, gate1, jnp.where(lane == 1, gate2, 0.0))


def _outproj_call(x2, od, om, op, p):
    t = x2.shape[0]
    tm = TM_OUT
    row = lambda i: (i, 0)
    return pl.pallas_call(
        _outproj_kernel, grid=(t // tm,),
        in_specs=[pl.BlockSpec((tm, D_MODEL), row), pl.BlockSpec((tm, DIFF_WIDTH), row),
                  pl.BlockSpec((tm, MLA_WIDTH), row), pl.BlockSpec((tm, POOL_WIDTH), row),
                  _full((D_MODEL, D_MODEL)), _full((1, D_MODEL)),
                  _full((D_MODEL, LANES)), _full((D_MODEL, LANES)), _full((1, LANES))],
        out_specs=[pl.BlockSpec((tm, D_MODEL), row),
                   pl.BlockSpec((tm * D_MODEL // LANES, LANES), row),
                   pl.BlockSpec((tm, LANES), row), pl.BlockSpec((tm, LANES), row),
                   _full((1, LANES))],
        out_shape=[jax.ShapeDtypeStruct((t, D_MODEL), F32),
                   jax.ShapeDtypeStruct((t * D_MODEL // LANES, LANES), F32),
                   jax.ShapeDtypeStruct((t, LANES), jnp.int32), jax.ShapeDtypeStruct((t, LANES), F32),
                   jax.ShapeDtypeStruct((1, LANES), F32)],
        scratch_shapes=[pltpu.VMEM((1, LANES), F32)],
        compiler_params=_cparams(1),
    )(x2, od, om, op, p["wo"], p["g2"], p["wrh"], p["wrl"], p["br"])


PAD_CHUNKS = (128, 64, 32, 16, 8, 4, 2, 1)


def _dispatch_kernel(pad_ref, idx_ref, h_ref, xs_hbm, zero_ref, sem, zsem):
    i = pl.program_id(0)
    n_chunk = D_MODEL // LANES
    tm = h_ref.shape[0] // n_chunk

    @pl.when(i == 0)
    def _():
        zero_ref[...] = jnp.zeros_like(zero_ref)

        def pad_copies(e, wait):
            off = pad_ref[0, e]
            cnt = pad_ref[1, e]
            for size in PAD_CHUNKS:
                take = cnt & size

                @pl.when(take != 0)
                def _():
                    cp = pltpu.make_async_copy(
                        zero_ref.at[pl.ds(0, size * n_chunk), :],
                        xs_hbm.at[pl.ds(pl.multiple_of(off * n_chunk, n_chunk), size * n_chunk), :], zsem)
                    if wait:
                        cp.wait()
                    else:
                        cp.start()
                off = off + take

        def start_body(e, c):
            pad_copies(e, False)
            return c

        def wait_body(e, c):
            pad_copies(e, True)
            return c

        lax.fori_loop(0, N_EXPERTS, start_body, 0)
        lax.fori_loop(0, N_EXPERTS, wait_body, 0)

        tail = pad_ref[0, N_EXPERTS]
        zrows = zero_ref.shape[0]

        def tail_copy(c):
            return pltpu.make_async_copy(
                zero_ref, xs_hbm.at[pl.ds(pl.multiple_of(tail * n_chunk + c * zrows, zrows), zrows), :], zsem)

        def tail_start(c, carry):
            tail_copy(c).start()
            return carry

        def tail_wait(c, carry):
            tail_copy(c).wait()
            return carry

        lax.fori_loop(0, pad_ref[1, N_EXPERTS], tail_start, 0)
        lax.fori_loop(0, pad_ref[1, N_EXPERTS], tail_wait, 0)

    for r in range(tm):
        for kk in range(2):
            dst = pl.multiple_of(idx_ref[0, 0, 2 * r + kk] * n_chunk, n_chunk)
            pltpu.make_async_copy(h_ref.at[pl.ds(r * n_chunk, n_chunk), :],
                                  xs_hbm.at[pl.ds(dst, n_chunk), :], sem).start(priority=kk)
    for kk in range(2):
        pltpu.make_async_copy(h_ref, xs_hbm.at[pl.ds(0, tm * n_chunk), :], sem).wait()


def _dispatch_call(pads, dest3, h3, n_slots):
    n_tiles = dest3.shape[0]
    tm = dest3.shape[2] // 2
    n_chunk = D_MODEL // LANES
    grid_spec = pltpu.PrefetchScalarGridSpec(
        num_scalar_prefetch=1, grid=(n_tiles,),
        in_specs=[pl.BlockSpec((1, 1, 2 * tm), lambda i, pads: (i, 0, 0), memory_space=pltpu.SMEM),
                  pl.BlockSpec((tm * n_chunk, LANES), lambda i, pads: (i, 0))],
        out_specs=pl.BlockSpec(memory_space=pl.ANY),
        scratch_shapes=[pltpu.VMEM((PAD_CHUNKS[0] * n_chunk, LANES), F32),
                        pltpu.SemaphoreType.DMA, pltpu.SemaphoreType.DMA])
    return pl.pallas_call(
        _dispatch_kernel, grid_spec=grid_spec,
        out_shape=jax.ShapeDtypeStruct((n_slots * n_chunk, LANES), F32),
        compiler_params=_cparams(1),
    )(pads, dest3, h3)


EXPERT_BLOCKS_PER_STEP = 4


def _expert_kernel(be_ref, nu_ref, xs_ref, *refs):
    i = pl.program_id(0)
    n_chunk = D_MODEL // LANES
    rows_per_block = ROUTE_BLOCK * n_chunk
    ys_ref = refs[-1]
    for b in range(EXPERT_BLOCKS_PER_STEP):
        wg_ref, wu_ref, wd_ref = refs[3 * b:3 * b + 3]
        base = b * rows_per_block
        block = i * EXPERT_BLOCKS_PER_STEP + b

        @pl.when(block < nu_ref[0])
        def _():
            xb = jnp.concatenate([xs_ref[pl.ds(base + j, ROUTE_BLOCK, stride=n_chunk), :]
                                  for j in range(n_chunk)], axis=1).astype(BF16)
            g = jnp.dot(xb, wg_ref[0, 0].astype(BF16), preferred_element_type=F32)
            u = jnp.dot(xb, wu_ref[0, 0].astype(BF16), preferred_element_type=F32)
            hmid = g * (1.0 / (1.0 + jnp.exp(-g))) * u
            y = jnp.dot(hmid.astype(BF16), wd_ref[0, 0].astype(BF16), preferred_element_type=F32)
            for j in range(n_chunk):
                ys_ref[pl.ds(base + j, ROUTE_BLOCK, stride=n_chunk), :] = y[:, j * LANES:(j + 1) * LANES]

        @pl.when(block >= nu_ref[0])
        def _():
            ys_ref[base:base + rows_per_block, :] = jnp.zeros((rows_per_block, LANES), F32)


def _expert_call(block_eid, n_used, xs3, wg, wu, wd, layer):
    per_step = EXPERT_BLOCKS_PER_STEP
    tile = (per_step * ROUTE_BLOCK * D_MODEL // LANES, LANES)
    n_steps = xs3.shape[0] // tile[0]
    row = lambda i, be, nu: (i, 0)
    weight_specs = []
    for b in range(per_step):
        wmap = lambda i, be, nu, b=b: (layer, be[i * per_step + b], 0, 0)
        weight_specs += [pl.BlockSpec((1, 1, D_MODEL, D_FF), wmap), pl.BlockSpec((1, 1, D_MODEL, D_FF), wmap),
                         pl.BlockSpec((1, 1, D_FF, D_MODEL), wmap)]
    grid_spec = pltpu.PrefetchScalarGridSpec(
        num_scalar_prefetch=2, grid=(n_steps,),
        in_specs=[pl.BlockSpec(tile, row)] + weight_specs,
        out_specs=pl.BlockSpec(tile, row))
    return pl.pallas_call(
        _expert_kernel, grid_spec=grid_spec,
        out_shape=jax.ShapeDtypeStruct(xs3.shape, F32),
        compiler_params=_cparams(1),
    )(block_eid, n_used, xs3, *([wg, wu, wd] * per_step))


def _combine_kernel(idx0_ref, idxn_ref, ys_hbm, x_ref, rg_ref, o_ref, buf0, buf1, sem):
    i = pl.program_id(0)
    n = pl.num_programs(0)
    tm = x_ref.shape[0]
    n_chunk = D_MODEL // LANES

    def issue(idx_ref, buf, sem_slot, rows):
        for r in rows:
            for kk in range(2):
                src = pl.multiple_of(idx_ref[0, 0, 2 * r + kk] * n_chunk, n_chunk)
                pltpu.make_async_copy(ys_hbm.at[pl.ds(src, n_chunk), :],
                                      buf.at[pl.ds((kk * tm + r) * n_chunk, n_chunk), :],
                                      sem_slot).start(priority=kk)

    def wait_tile(buf, sem_slot):
        pltpu.make_async_copy(ys_hbm.at[pl.ds(0, 2 * tm * n_chunk), :], buf, sem_slot).wait()

    @pl.when(i == 0)
    def _():
        issue(idx0_ref, buf0, sem.at[0], range(tm))

    def step(buf, sem_cur, buf_next, sem_next):
        issue(idxn_ref, buf_next, sem_next, range(tm))
        wait_tile(buf, sem_cur)
        rg = rg_ref[...]
        g0 = rg[:, 0:1]
        g1 = rg[:, 1:2]
        for j in range(n_chunk):
            cols = slice(j * LANES, (j + 1) * LANES)
            y0 = buf[pl.ds(j, tm, stride=n_chunk), :]
            y1 = buf[pl.ds(tm * n_chunk + j, tm, stride=n_chunk), :]
            o_ref[:, cols] = x_ref[:, cols] + g0 * y0 + g1 * y1

    @pl.when(i % 2 == 0)
    def _():
        step(buf0, sem.at[0], buf1, sem.at[1])

    @pl.when(i % 2 == 1)
    def _():
        step(buf1, sem.at[1], buf0, sem.at[0])

    @pl.when(i == n - 1)
    def _():
        @pl.when(i % 2 == 0)
        def _():
            wait_tile(buf1, sem.at[1])

        @pl.when(i % 2 == 1)
        def _():
            wait_tile(buf0, sem.at[0])


def _combine_call(dest3, ys, x2, rg):
    t = x2.shape[0]
    tm = TM_COMB
    n = t // tm
    row = lambda i: (i, 0)
    tile = (2 * tm * D_MODEL // LANES, LANES)
    return pl.pallas_call(
        _combine_kernel, grid=(n,),
        in_specs=[
            pl.BlockSpec((1, 1, 2 * tm), lambda i: (0, 0, 0), memory_space=pltpu.SMEM),
            pl.BlockSpec((1, 1, 2 * tm), lambda i: (jnp.minimum(i + 1, n - 1), 0, 0), memory_space=pltpu.SMEM),
            pl.BlockSpec(memory_space=pl.ANY),
            pl.BlockSpec((tm, D_MODEL), row), pl.BlockSpec((tm, LANES), row)],
        out_specs=pl.BlockSpec((tm, D_MODEL), row),
        out_shape=jax.ShapeDtypeStruct((t, D_MODEL), F32),
        scratch_shapes=[pltpu.VMEM(tile, F32), pltpu.VMEM(tile, F32), pltpu.SemaphoreType.DMA((2,))],
        compiler_params=_cparams(1),
    )(dest3, dest3, ys, x2, rg)


def _swap_halves(a):
    half = a.shape[-1] // 2
    return jnp.concatenate([a[..., half:], a[..., :half]], axis=-1)


def _layer_params(l, seq, w):
    p = {}
    row = lambda v: v.reshape(1, -1).astype(F32)
    w_in = w["w_in"][l]
    kr_cols = w_in[:, 1856:1888]
    p["win"] = jnp.concatenate(
        [w_in[:, 0:1536], w_in[:, 1888:2144], w_in[:, 1728:1856], w_in[:, 1536:1728],
         kr_cols, _swap_halves(kr_cols)], axis=1).astype(BF16)
    p["g1"] = row(w["norm1_g"][l])
    p["gq"] = row(jnp.tile(w["diff_q_norm_g"][l], 2) * (DIFF_QK ** -0.5))
    p["gk"] = row(jnp.tile(w["diff_k_norm_g"][l], 2))
    p["gckv"] = row(w["mla_kv_lat_norm_g"][l])
    gcq = w["mla_q_lat_norm_g"][l]
    p["gcqa"] = row(gcq[:LANES])
    p["gcqb"] = row(jnp.concatenate([gcq[LANES:], jnp.zeros((2 * LANES - MLA_Q_RANK,), F32)]))

    wuq = w["mla_w_uq"][l].reshape(MLA_Q_RANK, MLA_HEADS, MLA_NOPE + MLA_ROPE)
    rope_w = wuq[:, :, MLA_NOPE:]
    wuq = jnp.concatenate([wuq[:, :, :MLA_NOPE], rope_w, _swap_halves(rope_w)], axis=-1)
    wuq = wuq.reshape(MLA_Q_RANK, MLA_HEADS * HEAD_SLOT)
    wuq = jnp.concatenate([wuq, jnp.zeros((2 * LANES - MLA_Q_RANK, wuq.shape[1]), F32)], axis=0).astype(BF16)
    p["wuqa"] = wuq[:LANES]
    p["wuqb"] = wuq[LANES:]

    wukv = w["mla_w_ukv"][l].reshape(MLA_KV_RANK, MLA_HEADS, MLA_NOPE + MLA_V)
    zk = jnp.zeros((MLA_KV_RANK, MLA_HEADS, HEAD_SLOT - MLA_NOPE), F32)
    p["wkk"] = jnp.concatenate([wukv[:, :, :MLA_NOPE], zk], axis=-1).reshape(MLA_KV_RANK, -1).astype(BF16)
    vcols = wukv[:, :, MLA_NOPE:]
    zv = jnp.zeros_like(vcols)
    even = (jnp.arange(MLA_HEADS) % 2 == 0)[None, :, None]
    wkv = jnp.concatenate([jnp.where(even, vcols, zv), jnp.where(even, zv, vcols)], axis=-1)
    p["wkv"] = wkv.reshape(MLA_KV_RANK, -1).astype(BF16)
    p["gkn"] = row(jnp.concatenate([w["mla_k_nope_norm_g"][l], jnp.zeros((HEAD_SLOT - MLA_NOPE,), F32)]))

    inv = 1.0 / (ROPE_BASE ** (jnp.arange(0, MLA_ROPE, 2, dtype=F32) / MLA_ROPE))
    ang = jnp.arange(seq, dtype=F32)[:, None] * inv[None, :]
    cosf = jnp.concatenate([jnp.cos(ang), jnp.cos(ang)], axis=-1)
    sinf = jnp.concatenate([-jnp.sin(ang), jnp.sin(ang)], axis=-1)
    scale = (MLA_NOPE + MLA_ROPE) ** -0.5 * math.log2(math.e)
    gqr = w["mla_q_rope_norm_g"][l]
    q_head = jnp.concatenate([jnp.broadcast_to(w["mla_q_nope_norm_g"][l][None, :], (seq, MLA_NOPE)),
                              gqr[None, :] * cosf, _swap_halves(gqr)[None, :] * sinf], axis=-1) * scale
    p["qtab"] = jnp.tile(q_head, (1, MLA_HEADS))
    gkr = w["mla_k_rope_norm_g"][l]
    p["ktab"] = jnp.concatenate([jnp.zeros((seq, MLA_NOPE), F32), gkr[None, :] * cosf,
                                 _swap_halves(gkr)[None, :] * sinf], axis=-1)
    src = jnp.arange(LANES)
    dst = jnp.arange(MLA_HEADS * HEAD_SLOT)
    src_j = jnp.where(src >= MLA_NOPE, (src - MLA_NOPE) % MLA_ROPE, -1)
    dst_l = dst % HEAD_SLOT
    dst_j = jnp.where(dst_l >= MLA_NOPE, (dst_l - MLA_NOPE) % MLA_ROPE, -2)
    p["eplace"] = (src_j[:, None] == dst_j[None, :]).astype(BF16)

    pw = w["pool_w"][l]
    bd = jnp.zeros((POOL_WIDTH, POOL_WIDTH), F32)
    for g in range(POOL_GROUPS):
        s0 = g * POOL_GROUP_DIM
        bd = bd.at[s0:s0 + POOL_GROUP_DIM, s0:s0 + POOL_GROUP_DIM].set(pw[g])
    p["pool_w"] = bd.astype(BF16)
    p["pool_scale"] = row(w["pool_scale"][l])

    lam_init = 0.8 - 0.6 * math.exp(-0.3 * l)
    lv = w["diff_lambda"][l].astype(F32)
    p["lam"] = (jnp.exp(jnp.sum(lv[0] * lv[1])) - jnp.exp(jnp.sum(lv[2] * lv[3])) + lam_init).reshape(1)
    p["gsub"] = row(w["diff_sub_norm_g"][l] * (1.0 - lam_init))

    p["wo"] = w["w_out"][l].astype(BF16)
    p["g2"] = row(w["norm2_g"][l])
    wr = jnp.concatenate([w["router_group_w"][l], w["router_expert_w"][l],
                          jnp.zeros((D_MODEL, LANES - N_GROUPS - N_EXPERTS), F32)], axis=1)
    wr_hi = wr.astype(BF16)
    p["wrh"] = wr_hi
    p["wrl"] = (wr - wr_hi.astype(F32)).astype(BF16)
    p["br"] = row(jnp.concatenate([w["router_group_b"][l], w["router_expert_b"][l],
                                   jnp.zeros((LANES - N_GROUPS - N_EXPERTS,), F32)]))
    return p


def kernel(x, norm1_g, w_in, diff_q_norm_g, diff_k_norm_g, diff_lambda, diff_sub_norm_g, mla_q_lat_norm_g, mla_kv_lat_norm_g, mla_w_uq, mla_w_ukv, mla_q_nope_norm_g, mla_q_rope_norm_g, mla_k_nope_norm_g, mla_k_rope_norm_g, pool_w, pool_scale, w_out, norm2_g, router_group_w, router_group_b, router_expert_w, router_expert_b, expert_w_gate, expert_w_up, expert_w_down):
    w = dict(norm1_g=norm1_g, w_in=w_in, diff_q_norm_g=diff_q_norm_g, diff_k_norm_g=diff_k_norm_g,
             diff_lambda=diff_lambda, diff_sub_norm_g=diff_sub_norm_g, mla_q_lat_norm_g=mla_q_lat_norm_g,
             mla_kv_lat_norm_g=mla_kv_lat_norm_g, mla_w_uq=mla_w_uq, mla_w_ukv=mla_w_ukv,
             mla_q_nope_norm_g=mla_q_nope_norm_g, mla_q_rope_norm_g=mla_q_rope_norm_g,
             mla_k_nope_norm_g=mla_k_nope_norm_g, mla_k_rope_norm_g=mla_k_rope_norm_g,
             pool_w=pool_w, pool_scale=pool_scale, w_out=w_out, norm2_g=norm2_g,
             router_group_w=router_group_w, router_group_b=router_group_b,
             router_expert_w=router_expert_w, router_expert_b=router_expert_b)
    batch, seq, d = x.shape
    t = batch * seq
    n_assign = 2 * t
    n_blocks = n_assign // ROUTE_BLOCK + N_EXPERTS
    alibi = _alibi_tables(seq)

    x2 = x.reshape(t, d)
    for l in range(DEPTH):
        p = _layer_params(l, seq, w)
        dq1, dq2, dk, dv, qm, km, vm, pu = _proj_call(x2, p, seq)
        o_diff = _diff_call(p["lam"], dq1, dq2, dk, dv, p["gsub"], alibi, batch, seq)
        o_mla = _mla_call(qm, km, vm, batch, seq)
        o_pool = _pool_call(pu, p["pool_w"], p["pool_scale"], batch, seq)
        x2, h2, route_i, route_g, counts = _outproj_call(x2, o_diff, o_mla, o_pool, p)

        cnt = counts[0, N_GROUPS:N_GROUPS + N_EXPERTS].astype(jnp.int32)
        padded = (cnt + ROUTE_BLOCK - 1) // ROUTE_BLOCK * ROUTE_BLOCK
        padded_ends = jnp.cumsum(padded)
        padded_starts = padded_ends - padded
        eid = route_i[:, 0:2]
        start_of = jnp.sum(jnp.where(eid[..., None] == jnp.arange(N_EXPERTS, dtype=jnp.int32),
                                     padded_starts, 0), axis=-1)
        dest = start_of + route_i[:, 2:4]
        block_start = jnp.arange(n_blocks, dtype=jnp.int32) * ROUTE_BLOCK
        block_eid = jnp.minimum(jnp.sum(block_start[:, None] >= padded_ends[None, :], axis=1),
                                N_EXPERTS - 1).astype(jnp.int32)
        n_used = (padded_ends[-1] // ROUTE_BLOCK).astype(jnp.int32).reshape(1)
        n_tail = (n_blocks - n_used) * (ROUTE_BLOCK // PAD_CHUNKS[0])
        pads = jnp.stack([jnp.concatenate([padded_starts + cnt, padded_ends[-1:]]),
                          jnp.concatenate([padded - cnt, n_tail])]).astype(jnp.int32)
        xs = _dispatch_call(pads, dest.reshape(t // TM_DISP, 1, 2 * TM_DISP), h2, n_blocks * ROUTE_BLOCK)
        ys = _expert_call(block_eid, n_used, xs, expert_w_gate, expert_w_up, expert_w_down, l)
        x2 = _combine_call(dest.reshape(t // TM_COMB, 1, 2 * TM_COMB), ys, x2, route_g)
    return x2.reshape(batch, seq, d)
```

```python
import functools
import math

import jax
import jax.numpy as jnp
from jax import lax
from jax.experimental import pallas as pl
from jax.experimental.pallas import tpu as pltpu

F32 = jnp.float32
BF16 = jnp.bfloat16

D_MODEL = 1024
DEPTH = 2
DIFF_HEADS = 4
DIFF_QK = 64
DIFF_V = 128
DIFF_WIDTH = 512
MLA_HEADS = 4
MLA_NOPE = 64
MLA_ROPE = 32
MLA_V = 64
MLA_Q_RANK = 192
MLA_KV_RANK = 128
MLA_WIDTH = 256
ROPE_BASE = 10000.0
POOL_WIDTH = 256
POOL_GROUPS = 4
POOL_GROUP_DIM = 64
POOL_WINDOWS = (2, 4, 8, 16)
N_GROUPS = 4
EXPERTS_PER_GROUP = 8
N_EXPERTS = 32
D_FF = 256
ROUTE_BLOCK = 256
RMS_EPS = 1e-6

LANES = 128
HEAD_SLOT = 128
PROJ_WIDTH = 2176
POOL_PAD = 16
VMEM_LIMIT = 48 * 1024 * 1024
VMEM_LIMIT_MLA = 58 * 1024 * 1024

TM_PROJ = 512
TQ = 512
TQ_MLA = 1024
TK_DIFF = 256
TK_MLA = 512
TM_OUT = 1024
TM_DISP = 1024
TM_COMB = 256

NT_DIMS = (((1,), (1,)), ((), ()))


def _cparams(n_axes, vmem_limit=VMEM_LIMIT):
    return pltpu.CompilerParams(dimension_semantics=("arbitrary",) * n_axes,
                                vmem_limit_bytes=vmem_limit)


def _full(shape):
    return pl.BlockSpec(shape, lambda *_: (0,) * len(shape))


def _proj_kernel(x_ref, g1_ref, win_ref, gq_ref, gk_ref, gckv_ref, gcqa_ref, gcqb_ref,
                 wuqa_ref, wuqb_ref, wkk_ref, wkv_ref, gkn_ref, qtab_ref, ktab_ref, eplace_ref,
                 dq1_ref, dq2_ref, dk_ref, dv_ref, qm_ref, km_ref, vm_ref, pu_ref):
    x = x_ref[...]
    xn = x * lax.rsqrt(jnp.mean(x * x, axis=-1, keepdims=True) + RMS_EPS) * g1_ref[...]
    proj = jnp.dot(xn.astype(BF16), win_ref[...], preferred_element_type=F32)

    tm = x.shape[0]
    lane = lax.broadcasted_iota(jnp.int32, (tm, LANES), 1)
    lo = lane < DIFF_QK

    def half_norm(c, g_row):
        sq = c * c
        s_lo = jnp.sum(jnp.where(lo, sq, 0.0), axis=-1, keepdims=True)
        s_hi = jnp.sum(jnp.where(lo, 0.0, sq), axis=-1, keepdims=True)
        r = jnp.where(lo, lax.rsqrt(s_lo / DIFF_QK + RMS_EPS), lax.rsqrt(s_hi / DIFF_QK + RMS_EPS))
        return c * r * g_row

    for h in range(DIFF_HEADS):
        sl = slice(h * HEAD_SLOT, (h + 1) * HEAD_SLOT)
        qn = half_norm(proj[:, sl], gq_ref[...])
        dq1_ref[:, sl] = jnp.where(lo, qn, 0.0).astype(BF16)
        dq2_ref[:, sl] = jnp.where(lo, 0.0, qn).astype(BF16)
        ksl = slice(512 + h * HEAD_SLOT, 512 + (h + 1) * HEAD_SLOT)
        dk_ref[:, sl] = half_norm(proj[:, ksl], gk_ref[...]).astype(BF16)
    dv_ref[...] = proj[:, 1024:1536].astype(BF16)
    pu_ref[...] = proj[:, 1536:1792]

    ckv = proj[:, 1792:1920]
    ckvn = ckv * lax.rsqrt(jnp.mean(ckv * ckv, axis=-1, keepdims=True) + RMS_EPS) * gckv_ref[...]
    ckvn = ckvn.astype(BF16)
    cqa = proj[:, 1920:2048]
    last = proj[:, 2048:2176]
    lsq = last * last
    ss_q = (jnp.sum(cqa * cqa, axis=-1, keepdims=True)
            + jnp.sum(jnp.where(lo, lsq, 0.0), axis=-1, keepdims=True))
    r_q = lax.rsqrt(ss_q / MLA_Q_RANK + RMS_EPS)
    q_raw = (jnp.dot((cqa * r_q * gcqa_ref[...]).astype(BF16), wuqa_ref[...], preferred_element_type=F32)
             + jnp.dot((last * r_q * gcqb_ref[...]).astype(BF16), wuqb_ref[...], preferred_element_type=F32))

    rope_lanes = (lane >= MLA_NOPE) & (lane < MLA_NOPE + MLA_ROPE)
    ss_kr = jnp.sum(jnp.where(rope_lanes, lsq, 0.0), axis=-1, keepdims=True)
    kr_terms = last * lax.rsqrt(ss_kr / MLA_ROPE + RMS_EPS) * ktab_ref[...]
    kr_placed = jnp.dot(kr_terms.astype(BF16), eplace_ref[...], preferred_element_type=F32)

    k_raw = jnp.dot(ckvn, wkk_ref[...], preferred_element_type=F32)
    vm_ref[...] = jnp.dot(ckvn, wkv_ref[...], preferred_element_type=F32).astype(BF16)
    qtab = qtab_ref[...]
    for h in range(MLA_HEADS):
        sl = slice(h * HEAD_SLOT, (h + 1) * HEAD_SLOT)
        c = q_raw[:, sl]
        sq = c * c
        s_n = jnp.sum(jnp.where(lo, sq, 0.0), axis=-1, keepdims=True)
        s_r = jnp.sum(jnp.where(rope_lanes, sq, 0.0), axis=-1, keepdims=True)
        r = jnp.where(lo, lax.rsqrt(s_n / MLA_NOPE + RMS_EPS), lax.rsqrt(s_r / MLA_ROPE + RMS_EPS))
        qm_ref[:, sl] = (c * r * qtab[:, sl]).astype(BF16)
        kc = k_raw[:, sl]
        r_k = lax.rsqrt(jnp.sum(kc * kc, axis=-1, keepdims=True) / MLA_NOPE + RMS_EPS)
        km_ref[:, sl] = (kc * r_k * gkn_ref[...] + kr_placed[:, sl]).astype(BF16)


def _proj_call(x2, p, seq):
    t = x2.shape[0]
    tm = TM_PROJ
    n_pos = seq // tm
    row = lambda i: (i, 0)
    pos = lambda i: (i % n_pos, 0)
    bf = lambda w: jax.ShapeDtypeStruct((t, w), BF16)
    in_specs = [
        pl.BlockSpec((tm, D_MODEL), row),
        _full((1, D_MODEL)), _full((D_MODEL, PROJ_WIDTH)),
        _full((1, LANES)), _full((1, LANES)), _full((1, LANES)), _full((1, LANES)), _full((1, LANES)),
        _full((LANES, 512)), _full((LANES, 512)), _full((LANES, 512)), _full((LANES, 512)),
        _full((1, LANES)),
        pl.BlockSpec((tm, 512), pos), pl.BlockSpec((tm, LANES), pos),
        _full((LANES, 512)),
    ]
    out_specs = [pl.BlockSpec((tm, 512), row)] * 7 + [pl.BlockSpec((tm, POOL_WIDTH), row)]
    out_shape = [bf(512)] * 7 + [jax.ShapeDtypeStruct((t, POOL_WIDTH), F32)]
    return pl.pallas_call(
        _proj_kernel, grid=(t // tm,), in_specs=in_specs, out_specs=out_specs, out_shape=out_shape,
        compiler_params=_cparams(1),
    )(x2, p["g1"], p["win"], p["gq"], p["gk"], p["gckv"], p["gcqa"], p["gcqb"],
      p["wuqa"], p["wuqb"], p["wkk"], p["wkv"], p["gkn"], p["qtab"], p["ktab"], p["eplace"])


def _unflatten(n, sizes):
    n = jnp.minimum(n, math.prod(sizes) - 1)
    coords = []
    for size in reversed(sizes):
        coords.append(n % size)
        n = n // size
    return tuple(reversed(coords))


def _two_stage(n, stage, bufs):
    (s0, m0), (s1, m1) = bufs

    @pl.when(n == 0)
    def _():
        s1[...] = jnp.zeros_like(s1)
        m1[...] = jnp.zeros_like(m1)

    @pl.when(n % 2 == 0)
    def _():
        stage((s0, m0), (s1, m1))

    @pl.when(n % 2 == 1)
    def _():
        stage((s1, m1), (s0, m0))


def _two_stage_scratch(seq, tq):
    pair = [pltpu.VMEM((2 * tq, seq), F32), pltpu.VMEM((2 * tq, LANES), F32)]
    return pair + pair


def _softmax_pv_tile(s_prev, m_rows, c, tq, tk, lsum, acc, v_tiles, exp_fn):
    n_half = tk // LANES
    ps = [exp_fn(s_prev[:, (c * n_half + j) * LANES:(c * n_half + j + 1) * LANES] - m_rows)
          for j in range(n_half)]
    for ch in ps:
        lsum = ch if lsum is None else lsum + ch
    pb = jnp.concatenate(ps, axis=1).astype(BF16)
    for g, vt in enumerate(v_tiles):
        pv = jnp.dot(pb[g * tq:(g + 1) * tq], vt, preferred_element_type=F32)
        acc[g] = pv if acc[g] is None else acc[g] + pv
    return lsum


def _running_max(mx, sc):
    for j in range(sc.shape[1] // LANES):
        chunk = sc[:, j * LANES:(j + 1) * LANES]
        mx = chunk if mx is None else jnp.maximum(mx, chunk)
    return mx


def _diff_kernel(lam_ref, q1_ref, q2_ref, qx_ref, k_ref, kx_ref, bd_ref, v_ref, gsub_ref, o_ref,
                 s0_ref, m0_ref, s1_ref, m1_ref, *, sizes):
    n = pl.program_id(0)
    tq = q1_ref.shape[0]
    tk = TK_DIFF
    n_kt = k_ref.shape[0] // tk
    n_diag = tq // tk
    first_cur = _unflatten(n, sizes)[2] * n_diag
    first_prev = _unflatten(jnp.maximum(n - 1, 0), sizes)[2] * n_diag

    def stage(cur, prev):
        s_cur, m_cur = cur
        s_prev, m_prev = prev
        q1, q2 = q1_ref[...], q2_ref[...]
        qx_left = qx_ref[0, 0]
        qx_right = -qx_left
        bd = bd_ref[0]
        bd2 = jnp.concatenate([bd, bd], axis=0)
        mx = None
        m_rows = m_prev[...]
        lsum = None
        acc = [None, None]
        for c in range(n_kt):
            start = pl.multiple_of(((first_prev + c) % n_kt) * tk, tk)
            vt = v_ref[pl.ds(start, tk), :]
            lsum = _softmax_pv_tile(s_prev, m_rows, c, tq, tk, lsum, acc, [vt, vt], jnp.exp)

            start = pl.multiple_of(((first_cur + c) % n_kt) * tk, tk)
            if c < n_diag:
                qx = jnp.zeros_like(qx_left)
            else:
                qx = jnp.where(first_cur + c >= n_kt, qx_left, qx_right)
            qq = jnp.concatenate([jnp.concatenate([q1, qx], axis=1),
                                  jnp.concatenate([q2, qx], axis=1)], axis=0)
            kk = jnp.concatenate([k_ref[pl.ds(start, tk), :], kx_ref[pl.ds(start, tk), :]], axis=1)
            sc = lax.dot_general(qq, kk, NT_DIMS, preferred_element_type=F32)
            if c < n_diag:
                sc = sc + bd2[:, c * tk:(c + 1) * tk]
            s_cur[:, c * tk:(c + 1) * tk] = sc
            mx = _running_max(mx, sc)
        m_cur[...] = jnp.broadcast_to(jnp.max(mx, axis=-1, keepdims=True), m_cur.shape)
        l = jnp.sum(lsum, axis=-1, keepdims=True)
        o = acc[0] * (1.0 / l[0:tq]) - acc[1] * (lam_ref[0] / l[tq:2 * tq])
        r = lax.rsqrt(jnp.mean(o * o, axis=-1, keepdims=True) + RMS_EPS)
        o_ref[...] = (o * r * gsub_ref[...]).astype(BF16)

    _two_stage(n, stage, ((s0_ref, m0_ref), (s1_ref, m1_ref)))


def _alibi_tables(seq):
    nq = seq // TQ
    slopes = 2.0 ** (-8.0 * jnp.arange(1, DIFF_HEADS + 1, dtype=F32) / DIFF_HEADS)
    pos = jnp.arange(seq, dtype=jnp.int32)
    hi = (pos // 256).astype(F32)
    lo = (pos % 256).astype(F32)
    s4 = slopes[:, None]
    ones = jnp.ones((DIFF_HEADS, seq), F32)
    q_left = jnp.stack([-s4 * 256.0 * hi[None], -s4 * lo[None], s4 * 256.0 * ones, s4 * ones], axis=-1)
    qx = jnp.concatenate([q_left, jnp.zeros((DIFF_HEADS, seq, HEAD_SLOT - 4), F32)], axis=-1)
    qx = qx.reshape(DIFF_HEADS, nq, TQ, HEAD_SLOT).astype(BF16)
    k_cols = jnp.stack([jnp.ones((seq,), F32), jnp.ones((seq,), F32), hi, lo], axis=-1)
    kx = jnp.concatenate([k_cols, jnp.zeros((seq, HEAD_SLOT - 4), F32)], axis=-1).astype(BF16)
    loc = jnp.arange(TQ, dtype=jnp.int32)
    bd = -slopes[:, None, None] * jnp.abs(loc[:, None] - loc[None, :]).astype(F32)[None]
    return qx, kx, bd


def _diff_call(lam, dq1, dq2, dk, dv, gsub, tabs, batch, seq):
    t = dq1.shape[0]
    nq = seq // TQ
    qx, kx, bd = tabs
    sizes = (batch, DIFF_HEADS, nq)
    cur = lambda n: _unflatten(n, sizes)
    prev = lambda n: _unflatten(jnp.maximum(n - 1, 0), sizes)

    def qmap(n, *_):
        b, h, qi = cur(n)
        return (b * nq + qi, h)

    def kmap(n, *_):
        b, h, qi = cur(n)
        return (b, h)

    def vmap(n, *_):
        b, h, qi = prev(n)
        return (b, h)

    def omap(n, *_):
        b, h, qi = prev(n)
        return (b * nq + qi, h)

    grid_spec = pltpu.PrefetchScalarGridSpec(
        num_scalar_prefetch=1, grid=(math.prod(sizes) + 1,),
        in_specs=[pl.BlockSpec((TQ, HEAD_SLOT), qmap), pl.BlockSpec((TQ, HEAD_SLOT), qmap),
                  pl.BlockSpec((1, 1, TQ, HEAD_SLOT), lambda n, *_: cur(n)[1:] + (0, 0)),
                  pl.BlockSpec((seq, HEAD_SLOT), kmap),
                  pl.BlockSpec((seq, HEAD_SLOT), lambda *_: (0, 0)),
                  pl.BlockSpec((1, TQ, TQ), lambda n, *_: (cur(n)[1], 0, 0)),
                  pl.BlockSpec((seq, HEAD_SLOT), vmap),
                  pl.BlockSpec((1, HEAD_SLOT), lambda *_: (0, 0))],
        out_specs=pl.BlockSpec((TQ, HEAD_SLOT), omap),
        scratch_shapes=_two_stage_scratch(seq, TQ))
    return pl.pallas_call(
        functools.partial(_diff_kernel, sizes=sizes), grid_spec=grid_spec,
        out_shape=jax.ShapeDtypeStruct((t, DIFF_WIDTH), BF16),
        compiler_params=_cparams(1),
    )(lam, dq1, dq2, qx, dk, kx, bd, dv, gsub)


def _mla_kernel(q_ref, k_ref, v_ref, o_ref, s0_ref, m0_ref, s1_ref, m1_ref):
    n = pl.program_id(0)
    tq = q_ref.shape[0]
    tk = TK_MLA
    n_kt = k_ref.shape[0] // tk

    def stage(cur, prev):
        s_cur, m_cur = cur
        s_prev, m_prev = prev
        mx = [None, None]
        m_rows = m_prev[...]
        lsum = None
        acc = [None, None]
        for c in range(n_kt):
            rows = slice(c * tk, (c + 1) * tk)
            v_tiles = [v_ref[rows, hh * HEAD_SLOT:(hh + 1) * HEAD_SLOT] for hh in range(2)]
            lsum = _softmax_pv_tile(s_prev, m_rows, c, tq, tk, lsum, acc, v_tiles, jnp.exp2)
            for hh in range(2):
                sl = slice(hh * HEAD_SLOT, (hh + 1) * HEAD_SLOT)
                sc = lax.dot_general(q_ref[:, sl], k_ref[rows, sl], NT_DIMS, preferred_element_type=F32)
                s_cur[hh * tq:(hh + 1) * tq, rows] = sc
                mx[hh] = _running_max(mx[hh], sc)
        mx = jnp.concatenate(mx, axis=0)
        m_cur[...] = jnp.broadcast_to(jnp.max(mx, axis=-1, keepdims=True), m_cur.shape)
        l = jnp.sum(lsum, axis=-1, keepdims=True)
        o = acc[0] * (1.0 / l[0:tq]) + acc[1] * (1.0 / l[tq:2 * tq])
        o_ref[...] = o.astype(BF16)

    _two_stage(n, stage, ((s0_ref, m0_ref), (s1_ref, m1_ref)))


def _mla_call(qm, km, vm, batch, seq):
    t = qm.shape[0]
    tq = TQ_MLA
    nq = seq // tq
    sizes = (batch, MLA_HEADS // 2, nq)
    cur = lambda n: _unflatten(n, sizes)
    prev = lambda n: _unflatten(jnp.maximum(n - 1, 0), sizes)

    def qmap(n):
        b, p, qi = cur(n)
        return (b * nq + qi, p)

    def kmap(n):
        b, p, qi = cur(n)
        return (b, p)

    def vmap(n):
        b, p, qi = prev(n)
        return (b, p)

    def omap(n):
        b, p, qi = prev(n)
        return (b * nq + qi, p)

    return pl.pallas_call(
        _mla_kernel, grid=(math.prod(sizes) + 1,),
        in_specs=[pl.BlockSpec((tq, 2 * HEAD_SLOT), qmap), pl.BlockSpec((seq, 2 * HEAD_SLOT), kmap),
                  pl.BlockSpec((seq, 2 * HEAD_SLOT), vmap)],
        out_specs=pl.BlockSpec((tq, HEAD_SLOT), omap),
        out_shape=jax.ShapeDtypeStruct((t, MLA_WIDTH), BF16),
        scratch_shapes=_two_stage_scratch(seq, tq),
        compiler_params=_cparams(1, VMEM_LIMIT_MLA),
    )(qm, km, vm)


def _pool_kernel(u_ref, w_ref, scale_ref, o_ref):
    u = u_ref[...]
    seq, width = u.shape
    zpad = jnp.zeros((POOL_PAD, width), F32)
    ue = jnp.concatenate([zpad, u, zpad], axis=0)
    n = seq + 2 * POOL_PAD

    def down(a, k):
        return pltpu.roll(a, k, axis=0)

    def up(a, k):
        return pltpu.roll(a, n - k, axis=0)

    a2 = ue + down(ue, 1)
    a4 = down(a2, 1) + up(a2, 1)
    a8 = down(a4, 2) + up(a4, 2)
    a16 = down(a8, 4) + up(a8, 4)
    core = slice(POOL_PAD, POOL_PAD + seq)
    lane = lax.broadcasted_iota(jnp.int32, (seq, width), 1)
    tpos = lax.broadcasted_iota(jnp.int32, (seq, width), 0)
    grp = lane // POOL_GROUP_DIM
    win_sum = jnp.where(grp == 0, a2[core], jnp.where(grp == 1, a4[core], jnp.where(grp == 2, a8[core], a16[core])))
    half = jnp.where(grp == 0, 1, jnp.where(grp == 1, 2, jnp.where(grp == 2, 4, 8)))
    lo_i = jnp.maximum(tpos - half, 0)
    hi_i = jnp.minimum(tpos + half - 1, seq - 1)
    cnt = (hi_i - lo_i + 1).astype(F32)
    pooled = win_sum / cnt - u
    mixed = jnp.dot(pooled.astype(BF16), w_ref[...], preferred_element_type=F32)
    o_ref[...] = (mixed * scale_ref[...]).astype(BF16)


def _pool_call(pu, w_bd, scale, batch, seq):
    t = pu.shape[0]
    return pl.pallas_call(
        _pool_kernel, grid=(batch,),
        in_specs=[pl.BlockSpec((seq, POOL_WIDTH), lambda b: (b, 0)),
                  _full((POOL_WIDTH, POOL_WIDTH)), _full((1, POOL_WIDTH))],
        out_specs=pl.BlockSpec((seq, POOL_WIDTH), lambda b: (b, 0)),
        out_shape=jax.ShapeDtypeStruct((t, POOL_WIDTH), BF16),
        compiler_params=_cparams(1),
    )(pu, w_bd, scale)


def _outproj_kernel(x_ref, od_ref, om_ref, op_ref, wo_ref, g2_ref, wrh_ref, wrl_ref, br_ref,
                    xo_ref, h2_ref, ri_ref, rg_ref, cnt_ref, carry_ref):
    i = pl.program_id(0)

    @pl.when(i == 0)
    def _():
        carry_ref[...] = jnp.zeros_like(carry_ref)

    xn = (x_ref[...]
          + jnp.dot(od_ref[...], wo_ref[0:DIFF_WIDTH, :], preferred_element_type=F32)
          + jnp.dot(om_ref[...], wo_ref[DIFF_WIDTH:DIFF_WIDTH + MLA_WIDTH, :], preferred_element_type=F32)
          + jnp.dot(op_ref[...], wo_ref[DIFF_WIDTH + MLA_WIDTH:, :], preferred_element_type=F32))
    xo_ref[...] = xn
    h2 = xn * lax.rsqrt(jnp.mean(xn * xn, axis=-1, keepdims=True) + RMS_EPS) * g2_ref[...]
    n_chunk = D_MODEL // LANES
    for j in range(n_chunk):
        h2_ref[pl.ds(j, xn.shape[0], stride=n_chunk), :] = h2[:, j * LANES:(j + 1) * LANES]

    h_hi = h2.astype(BF16)
    h_lo = (h2 - h_hi.astype(F32)).astype(BF16)
    logits = (jnp.dot(h_hi, wrh_ref[...], preferred_element_type=F32)
              + jnp.dot(h_lo, wrh_ref[...], preferred_element_type=F32)
              + jnp.dot(h_hi, wrl_ref[...], preferred_element_type=F32)
              + br_ref[...])
    tm = logits.shape[0]
    lane = lax.broadcasted_iota(jnp.int32, (tm, LANES), 1)
    lane_f = lane.astype(F32)
    neg = jnp.float32(-jnp.inf)
    big = jnp.float32(1e9)

    gmask = lane < N_GROUPS
    gl = jnp.where(gmask, logits, neg)
    gmax = jnp.max(gl, axis=-1, keepdims=True)
    gsum = jnp.sum(jnp.where(gmask, jnp.exp(gl - gmax), 0.0), axis=-1, keepdims=True)
    g_top = 1.0 / gsum
    g_idx = jnp.min(jnp.where(gl == gmax, lane_f, big), axis=-1, keepdims=True)

    e_lo = N_GROUPS + EXPERTS_PER_GROUP * g_idx
    emask = (lane_f >= e_lo) & (lane_f < e_lo + EXPERTS_PER_GROUP)
    el = jnp.where(emask, logits, neg)
    emax = jnp.max(el, axis=-1, keepdims=True)
    eexp = jnp.where(emask, jnp.exp(el - emax), 0.0)
    prob = eexp / jnp.sum(eexp, axis=-1, keepdims=True)
    pm = jnp.where(emask, prob, -1.0)
    p1 = jnp.max(pm, axis=-1, keepdims=True)
    i1 = jnp.min(jnp.where(pm == p1, lane_f, big), axis=-1, keepdims=True)
    pm2 = jnp.where(lane_f == i1, -1.0, pm)
    p2 = jnp.max(pm2, axis=-1, keepdims=True)
    i2 = jnp.min(jnp.where(pm2 == p2, lane_f, big), axis=-1, keepdims=True)
    denom = p1 + p2
    gate1 = g_top * p1 / denom
    gate2 = g_top * p2 / denom

    sel1 = lane_f == i1
    sel2 = lane_f == i2
    onehot = jnp.where(sel1 | sel2, 1.0, 0.0)
    rr = lax.broadcasted_iota(jnp.int32, (tm, tm), 0)
    cc = lax.broadcasted_iota(jnp.int32, (tm, tm), 1)
    ltri = jnp.where(cc < rr, 1.0, 0.0).astype(BF16)
    prefix = jnp.dot(ltri, onehot.astype(BF16), preferred_element_type=F32) + carry_ref[...]
    rank1 = jnp.sum(jnp.where(sel1, prefix, 0.0), axis=-1, keepdims=True)
    rank2 = jnp.sum(jnp.where(sel2, prefix, 0.0), axis=-1, keepdims=True)
    carry_ref[...] = carry_ref[...] + jnp.sum(onehot, axis=0, keepdims=True)
    cnt_ref[...] = carry_ref[...]

    info = jnp.where(lane == 0, i1 - N_GROUPS,
                     jnp.where(lane == 1, i2 - N_GROUPS,
                               jnp.where(lane == 2, rank1, jnp.where(lane == 3, rank2, 0.0))))
    ri_ref[...] = info.astype(jnp.int32)
    rg_ref[...] = jnp.where(lane == 0, gate1, jnp.where(lane == 1, gate2, 0.0))


def _outproj_call(x2, od, om, op, p):
    t = x2.shape[0]
    tm = TM_OUT
    row = lambda i: (i, 0)
    return pl.pallas_call(
        _outproj_kernel, grid=(t // tm,),
        in_specs=[pl.BlockSpec((tm, D_MODEL), row), pl.BlockSpec((tm, DIFF_WIDTH), row),
                  pl.BlockSpec((tm, MLA_WIDTH), row), pl.BlockSpec((tm, POOL_WIDTH), row),
                  _full((D_MODEL, D_MODEL)), _full((1, D_MODEL)),
                  _full((D_MODEL, LANES)), _full((D_MODEL, LANES)), _full((1, LANES))],
        out_specs=[pl.BlockSpec((tm, D_MODEL), row),
                   pl.BlockSpec((tm * D_MODEL // LANES, LANES), row),
                   pl.BlockSpec((tm, LANES), row), pl.BlockSpec((tm, LANES), row),
                   _full((1, LANES))],
        out_shape=[jax.ShapeDtypeStruct((t, D_MODEL), F32),
                   jax.ShapeDtypeStruct((t * D_MODEL // LANES, LANES), F32),
                   jax.ShapeDtypeStruct((t, LANES), jnp.int32), jax.ShapeDtypeStruct((t, LANES), F32),
                   jax.ShapeDtypeStruct((1, LANES), F32)],
        scratch_shapes=[pltpu.VMEM((1, LANES), F32)],
        compiler_params=_cparams(1),
    )(x2, od, om, op, p["wo"], p["g2"], p["wrh"], p["wrl"], p["br"])


PAD_CHUNKS = (128, 64, 32, 16, 8, 4, 2, 1)


def _dispatch_kernel(pad_ref, idx_ref, h_ref, xs_hbm, zero_ref, sem, zsem):
    i = pl.program_id(0)
    n_chunk = D_MODEL // LANES
    tm = h_ref.shape[0] // n_chunk

    @pl.when(i == 0)
    def _():
        zero_ref[...] = jnp.zeros_like(zero_ref)

        def pad_copies(e, wait):
            off = pad_ref[0, e]
            cnt = pad_ref[1, e]
            for size in PAD_CHUNKS:
                take = cnt & size

                @pl.when(take != 0)
                def _():
                    cp = pltpu.make_async_copy(
                        zero_ref.at[pl.ds(0, size * n_chunk), :],
                        xs_hbm.at[pl.ds(pl.multiple_of(off * n_chunk, n_chunk), size * n_chunk), :], zsem)
                    if wait:
                        cp.wait()
                    else:
                        cp.start()
                off = off + take

        def start_body(e, c):
            pad_copies(e, False)
            return c

        def wait_body(e, c):
            pad_copies(e, True)
            return c

        lax.fori_loop(0, N_EXPERTS, start_body, 0)
        lax.fori_loop(0, N_EXPERTS, wait_body, 0)

        tail = pad_ref[0, N_EXPERTS]
        zrows = zero_ref.shape[0]

        def tail_copy(c):
            return pltpu.make_async_copy(
                zero_ref, xs_hbm.at[pl.ds(pl.multiple_of(tail * n_chunk + c * zrows, zrows), zrows), :], zsem)

        def tail_start(c, carry):
            tail_copy(c).start()
            return carry

        def tail_wait(c, carry):
            tail_copy(c).wait()
            return carry

        lax.fori_loop(0, pad_ref[1, N_EXPERTS], tail_start, 0)
        lax.fori_loop(0, pad_ref[1, N_EXPERTS], tail_wait, 0)

    for r in range(tm):
        for kk in range(2):
            dst = pl.multiple_of(idx_ref[0, 0, 2 * r + kk] * n_chunk, n_chunk)
            pltpu.make_async_copy(h_ref.at[pl.ds(r * n_chunk, n_chunk), :],
                                  xs_hbm.at[pl.ds(dst, n_chunk), :], sem).start(priority=kk)
    for kk in range(2):
        pltpu.make_async_copy(h_ref, xs_hbm.at[pl.ds(0, tm * n_chunk), :], sem).wait()


def _dispatch_call(pads, dest3, h3, n_slots):
    n_tiles = dest3.shape[0]
    tm = dest3.shape[2] // 2
    n_chunk = D_MODEL // LANES
    grid_spec = pltpu.PrefetchScalarGridSpec(
        num_scalar_prefetch=1, grid=(n_tiles,),
        in_specs=[pl.BlockSpec((1, 1, 2 * tm), lambda i, pads: (i, 0, 0), memory_space=pltpu.SMEM),
                  pl.BlockSpec((tm * n_chunk, LANES), lambda i, pads: (i, 0))],
        out_specs=pl.BlockSpec(memory_space=pl.ANY),
        scratch_shapes=[pltpu.VMEM((PAD_CHUNKS[0] * n_chunk, LANES), F32),
                        pltpu.SemaphoreType.DMA, pltpu.SemaphoreType.DMA])
    return pl.pallas_call(
        _dispatch_kernel, grid_spec=grid_spec,
        out_shape=jax.ShapeDtypeStruct((n_slots * n_chunk, LANES), F32),
        compiler_params=_cparams(1),
    )(pads, dest3, h3)


EXPERT_BLOCKS_PER_STEP = 2


def _expert_kernel(be_ref, nu_ref, xs_ref, *refs):
    i = pl.program_id(0)
    n_chunk = D_MODEL // LANES
    rows_per_block = ROUTE_BLOCK * n_chunk
    ys_ref = refs[-1]
    for b in range(EXPERT_BLOCKS_PER_STEP):
        wg_ref, wu_ref, wd_ref = refs[3 * b:3 * b + 3]
        base = b * rows_per_block
        block = i * EXPERT_BLOCKS_PER_STEP + b

        @pl.when(block < nu_ref[0])
        def _():
            xb = jnp.concatenate([xs_ref[pl.ds(base + j, ROUTE_BLOCK, stride=n_chunk), :]
                                  for j in range(n_chunk)], axis=1).astype(BF16)
            g = jnp.dot(xb, wg_ref[0, 0].astype(BF16), preferred_element_type=F32)
            u = jnp.dot(xb, wu_ref[0, 0].astype(BF16), preferred_element_type=F32)
            hmid = g * (1.0 / (1.0 + jnp.exp(-g))) * u
            y = jnp.dot(hmid.astype(BF16), wd_ref[0, 0].astype(BF16), preferred_element_type=F32)
            for j in range(n_chunk):
                ys_ref[pl.ds(base + j, ROUTE_BLOCK, stride=n_chunk), :] = y[:, j * LANES:(j + 1) * LANES]

        @pl.when(block >= nu_ref[0])
        def _():
            ys_ref[base:base + rows_per_block, :] = jnp.zeros((rows_per_block, LANES), F32)


def _expert_call(block_eid, n_used, xs3, wg, wu, wd, layer):
    per_step = EXPERT_BLOCKS_PER_STEP
    tile = (per_step * ROUTE_BLOCK * D_MODEL // LANES, LANES)
    n_steps = xs3.shape[0] // tile[0]
    row = lambda i, be, nu: (i, 0)
    weight_specs = []
    for b in range(per_step):
        wmap = lambda i, be, nu, b=b: (layer, be[i * per_step + b], 0, 0)
        weight_specs += [pl.BlockSpec((1, 1, D_MODEL, D_FF), wmap), pl.BlockSpec((1, 1, D_MODEL, D_FF), wmap),
                         pl.BlockSpec((1, 1, D_FF, D_MODEL), wmap)]
    grid_spec = pltpu.PrefetchScalarGridSpec(
        num_scalar_prefetch=2, grid=(n_steps,),
        in_specs=[pl.BlockSpec(tile, row)] + weight_specs,
        out_specs=pl.BlockSpec(tile, row))
    return pl.pallas_call(
        _expert_kernel, grid_spec=grid_spec,
        out_shape=jax.ShapeDtypeStruct(xs3.shape, F32),
        compiler_params=_cparams(1),
    )(block_eid, n_used, xs3, *([wg, wu, wd] * per_step))


def _combine_kernel(idx0_ref, idxn_ref, ys_hbm, x_ref, rg_ref, o_ref, buf0, buf1, sem):
    i = pl.program_id(0)
    n = pl.num_programs(0)
    tm = x_ref.shape[0]
    n_chunk = D_MODEL // LANES

    def issue(idx_ref, buf, sem_slot, rows):
        for r in rows:
            for kk in range(2):
                src = pl.multiple_of(idx_ref[0, 0, 2 * r + kk] * n_chunk, n_chunk)
                pltpu.make_async_copy(ys_hbm.at[pl.ds(src, n_chunk), :],
                                      buf.at[pl.ds((kk * tm + r) * n_chunk, n_chunk), :],
                                      sem_slot).start(priority=kk)

    def wait_tile(buf, sem_slot):
        pltpu.make_async_copy(ys_hbm.at[pl.ds(0, 2 * tm * n_chunk), :], buf, sem_slot).wait()

    @pl.when(i == 0)
    def _():
        issue(idx0_ref, buf0, sem.at[0], range(tm))

    def step(buf, sem_cur, buf_next, sem_next):
        issue(idxn_ref, buf_next, sem_next, range(tm))
        wait_tile(buf, sem_cur)
        rg = rg_ref[...]
        g0 = rg[:, 0:1]
        g1 = rg[:, 1:2]
        for j in range(n_chunk):
            cols = slice(j * LANES, (j + 1) * LANES)
            y0 = buf[pl.ds(j, tm, stride=n_chunk), :]
            y1 = buf[pl.ds(tm * n_chunk + j, tm, stride=n_chunk), :]
            o_ref[:, cols] = x_ref[:, cols] + g0 * y0 + g1 * y1

    @pl.when(i % 2 == 0)
    def _():
        step(buf0, sem.at[0], buf1, sem.at[1])

    @pl.when(i % 2 == 1)
    def _():
        step(buf1, sem.at[1], buf0, sem.at[0])

    @pl.when(i == n - 1)
    def _():
        @pl.when(i % 2 == 0)
        def _():
            wait_tile(buf1, sem.at[1])

        @pl.when(i % 2 == 1)
        def _():
            wait_tile(buf0, sem.at[0])


def _combine_call(dest3, ys, x2, rg):
    t = x2.shape[0]
    tm = TM_COMB
    n = t // tm
    row = lambda i: (i, 0)
    tile = (2 * tm * D_MODEL // LANES, LANES)
    return pl.pallas_call(
        _combine_kernel, grid=(n,),
        in_specs=[
            pl.BlockSpec((1, 1, 2 * tm), lambda i: (0, 0, 0), memory_space=pltpu.SMEM),
            pl.BlockSpec((1, 1, 2 * tm), lambda i: (jnp.minimum(i + 1, n - 1), 0, 0), memory_space=pltpu.SMEM),
            pl.BlockSpec(memory_space=pl.ANY),
            pl.BlockSpec((tm, D_MODEL), row), pl.BlockSpec((tm, LANES), row)],
        out_specs=pl.BlockSpec((tm, D_MODEL), row),
        out_shape=jax.ShapeDtypeStruct((t, D_MODEL), F32),
        scratch_shapes=[pltpu.VMEM(tile, F32), pltpu.VMEM(tile, F32), pltpu.SemaphoreType.DMA((2,))],
        compiler_params=_cparams(1),
    )(dest3, dest3, ys, x2, rg)


def _swap_halves(a):
    half = a.shape[-1] // 2
    return jnp.concatenate([a[..., half:], a[..., :half]], axis=-1)


def _layer_params(l, seq, w):
    p = {}
    row = lambda v: v.reshape(1, -1).astype(F32)
    w_in = w["w_in"][l]
    kr_cols = w_in[:, 1856:1888]
    p["win"] = jnp.concatenate(
        [w_in[:, 0:1536], w_in[:, 1888:2144], w_in[:, 1728:1856], w_in[:, 1536:1728],
         kr_cols, _swap_halves(kr_cols)], axis=1).astype(BF16)
    p["g1"] = row(w["norm1_g"][l])
    p["gq"] = row(jnp.tile(w["diff_q_norm_g"][l], 2) * (DIFF_QK ** -0.5))
    p["gk"] = row(jnp.tile(w["diff_k_norm_g"][l], 2))
    p["gckv"] = row(w["mla_kv_lat_norm_g"][l])
    gcq = w["mla_q_lat_norm_g"][l]
    p["gcqa"] = row(gcq[:LANES])
    p["gcqb"] = row(jnp.concatenate([gcq[LANES:], jnp.zeros((2 * LANES - MLA_Q_RANK,), F32)]))

    wuq = w["mla_w_uq"][l].reshape(MLA_Q_RANK, MLA_HEADS, MLA_NOPE + MLA_ROPE)
    rope_w = wuq[:, :, MLA_NOPE:]
    wuq = jnp.concatenate([wuq[:, :, :MLA_NOPE], rope_w, _swap_halves(rope_w)], axis=-1)
    wuq = wuq.reshape(MLA_Q_RANK, MLA_HEADS * HEAD_SLOT)
    wuq = jnp.concatenate([wuq, jnp.zeros((2 * LANES - MLA_Q_RANK, wuq.shape[1]), F32)], axis=0).astype(BF16)
    p["wuqa"] = wuq[:LANES]
    p["wuqb"] = wuq[LANES:]

    wukv = w["mla_w_ukv"][l].reshape(MLA_KV_RANK, MLA_HEADS, MLA_NOPE + MLA_V)
    zk = jnp.zeros((MLA_KV_RANK, MLA_HEADS, HEAD_SLOT - MLA_NOPE), F32)
    p["wkk"] = jnp.concatenate([wukv[:, :, :MLA_NOPE], zk], axis=-1).reshape(MLA_KV_RANK, -1).astype(BF16)
    vcols = wukv[:, :, MLA_NOPE:]
    zv = jnp.zeros_like(vcols)
    even = (jnp.arange(MLA_HEADS) % 2 == 0)[None, :, None]
    wkv = jnp.concatenate([jnp.where(even, vcols, zv), jnp.where(even, zv, vcols)], axis=-1)
    p["wkv"] = wkv.reshape(MLA_KV_RANK, -1).astype(BF16)
    p["gkn"] = row(jnp.concatenate([w["mla_k_nope_norm_g"][l], jnp.zeros((HEAD_SLOT - MLA_NOPE,), F32)]))

    inv = 1.0 / (ROPE_BASE ** (jnp.arange(0, MLA_ROPE, 2, dtype=F32) / MLA_ROPE))
    ang = jnp.arange(seq, dtype=F32)[:, None] * inv[None, :]
    cosf = jnp.concatenate([jnp.cos(ang), jnp.cos(ang)], axis=-1)
    sinf = jnp.concatenate([-jnp.sin(ang), jnp.sin(ang)], axis=-1)
    scale = (MLA_NOPE + MLA_ROPE) ** -0.5 * math.log2(math.e)
    gqr = w["mla_q_rope_norm_g"][l]
    q_head = jnp.concatenate([jnp.broadcast_to(w["mla_q_nope_norm_g"][l][None, :], (seq, MLA_NOPE)),
                              gqr[None, :] * cosf, _swap_halves(gqr)[None, :] * sinf], axis=-1) * scale
    p["qtab"] = jnp.tile(q_head, (1, MLA_HEADS))
    gkr = w["mla_k_rope_norm_g"][l]
    p["ktab"] = jnp.concatenate([jnp.zeros((seq, MLA_NOPE), F32), gkr[None, :] * cosf,
                                 _swap_halves(gkr)[None, :] * sinf], axis=-1)
    src = jnp.arange(LANES)
    dst = jnp.arange(MLA_HEADS * HEAD_SLOT)
    src_j = jnp.where(src >= MLA_NOPE, (src - MLA_NOPE) % MLA_ROPE, -1)
    dst_l = dst % HEAD_SLOT
    dst_j = jnp.where(dst_l >= MLA_NOPE, (dst_l - MLA_NOPE) % MLA_ROPE, -2)
    p["eplace"] = (src_j[:, None] == dst_j[None, :]).astype(BF16)

    pw = w["pool_w"][l]
    bd = jnp.zeros((POOL_WIDTH, POOL_WIDTH), F32)
    for g in range(POOL_GROUPS):
        s0 = g * POOL_GROUP_DIM
        bd = bd.at[s0:s0 + POOL_GROUP_DIM, s0:s0 + POOL_GROUP_DIM].set(pw[g])
    p["pool_w"] = bd.astype(BF16)
    p["pool_scale"] = row(w["pool_scale"][l])

    lam_init = 0.8 - 0.6 * math.exp(-0.3 * l)
    lv = w["diff_lambda"][l].astype(F32)
    p["lam"] = (jnp.exp(jnp.sum(lv[0] * lv[1])) - jnp.exp(jnp.sum(lv[2] * lv[3])) + lam_init).reshape(1)
    p["gsub"] = row(w["diff_sub_norm_g"][l] * (1.0 - lam_init))

    p["wo"] = w["w_out"][l].astype(BF16)
    p["g2"] = row(w["norm2_g"][l])
    wr = jnp.concatenate([w["router_group_w"][l], w["router_expert_w"][l],
                          jnp.zeros((D_MODEL, LANES - N_GROUPS - N_EXPERTS), F32)], axis=1)
    wr_hi = wr.astype(BF16)
    p["wrh"] = wr_hi
    p["wrl"] = (wr - wr_hi.astype(F32)).astype(BF16)
    p["br"] = row(jnp.concatenate([w["router_group_b"][l], w["router_expert_b"][l],
                                   jnp.zeros((LANES - N_GROUPS - N_EXPERTS,), F32)]))
    return p


def kernel(x, norm1_g, w_in, diff_q_norm_g, diff_k_norm_g, diff_lambda, diff_sub_norm_g, mla_q_lat_norm_g, mla_kv_lat_norm_g, mla_w_uq, mla_w_ukv, mla_q_nope_norm_g, mla_q_rope_norm_g, mla_k_nope_norm_g, mla_k_rope_norm_g, pool_w, pool_scale, w_out, norm2_g, router_group_w, router_group_b, router_expert_w, router_expert_b, expert_w_gate, expert_w_up, expert_w_down):
    w = dict(norm1_g=norm1_g, w_in=w_in, diff_q_norm_g=diff_q_norm_g, diff_k_norm_g=diff_k_norm_g,
             diff_lambda=diff_lambda, diff_sub_norm_g=diff_sub_norm_g, mla_q_lat_norm_g=mla_q_lat_norm_g,
             mla_kv_lat_norm_g=mla_kv_lat_norm_g, mla_w_uq=mla_w_uq, mla_w_ukv=mla_w_ukv,
             mla_q_nope_norm_g=mla_q_nope_norm_g, mla_q_rope_norm_g=mla_q_rope_norm_g,
             mla_k_nope_norm_g=mla_k_nope_norm_g, mla_k_rope_norm_g=mla_k_rope_norm_g,
             pool_w=pool_w, pool_scale=pool_scale, w_out=w_out, norm2_g=norm2_g,
             router_group_w=router_group_w, router_group_b=router_group_b,
             router_expert_w=router_expert_w, router_expert_b=router_expert_b)
    batch, seq, d = x.shape
    t = batch * seq
    n_assign = 2 * t
    n_blocks = n_assign // ROUTE_BLOCK + N_EXPERTS
    alibi = _alibi_tables(seq)

    x2 = x.reshape(t, d)
    for l in range(DEPTH):
        p = _layer_params(l, seq, w)
        dq1, dq2, dk, dv, qm, km, vm, pu = _proj_call(x2, p, seq)
        o_diff = _diff_call(p["lam"], dq1, dq2, dk, dv, p["gsub"], alibi, batch, seq)
        o_mla = _mla_call(qm, km, vm, batch, seq)
        o_pool = _pool_call(pu, p["pool_w"], p["pool_scale"], batch, seq)
        x2, h2, route_i, route_g, counts = _outproj_call(x2, o_diff, o_mla, o_pool, p)

        cnt = counts[0, N_GROUPS:N_GROUPS + N_EXPERTS].astype(jnp.int32)
        padded = (cnt + ROUTE_BLOCK - 1) // ROUTE_BLOCK * ROUTE_BLOCK
        padded_ends = jnp.cumsum(padded)
        padded_starts = padded_ends - padded
        eid = route_i[:, 0:2]
        start_of = jnp.sum(jnp.where(eid[..., None] == jnp.arange(N_EXPERTS, dtype=jnp.int32),
                                     padded_starts, 0), axis=-1)
        dest = start_of + route_i[:, 2:4]
        block_start = jnp.arange(n_blocks, dtype=jnp.int32) * ROUTE_BLOCK
        block_eid = jnp.minimum(jnp.sum(block_start[:, None] >= padded_ends[None, :], axis=1),
                                N_EXPERTS - 1).astype(jnp.int32)
        n_used = (padded_ends[-1] // ROUTE_BLOCK).astype(jnp.int32).reshape(1)
        n_tail = (n_blocks - n_used) * (ROUTE_BLOCK // PAD_CHUNKS[0])
        pads = jnp.stack([jnp.concatenate([padded_starts + cnt, padded_ends[-1:]]),
                          jnp.concatenate([padded - cnt, n_tail])]).astype(jnp.int32)
        xs = _dispatch_call(pads, dest.reshape(t // TM_DISP, 1, 2 * TM_DISP), h2, n_blocks * ROUTE_BLOCK)
        ys = _expert_call(block_eid, n_used, xs, expert_w_gate, expert_w_up, expert_w_down, l)
        x2 = _combine_call(dest.reshape(t // TM_COMB, 1, 2 * TM_COMB), ys, x2, route_g)
    return x2.reshape(batch, seq, d)
```

```python
import functools
import math

import jax
import jax.numpy as jnp
from jax import lax
from jax.experimental import pallas as pl
from jax.experimental.pallas import tpu as pltpu

F32 = jnp.float32
BF16 = jnp.bfloat16

D_MODEL = 1024
DEPTH = 2
DIFF_HEADS = 4
DIFF_QK = 64
DIFF_V = 128
DIFF_WIDTH = 512
MLA_HEADS = 4
MLA_NOPE = 64
MLA_ROPE = 32
MLA_V = 64
MLA_Q_RANK = 192
MLA_KV_RANK = 128
MLA_WIDTH = 256
ROPE_BASE = 10000.0
POOL_WIDTH = 256
POOL_GROUPS = 4
POOL_GROUP_DIM = 64
POOL_WINDOWS = (2, 4, 8, 16)
N_GROUPS = 4
EXPERTS_PER_GROUP = 8
N_EXPERTS = 32
D_FF = 256
ROUTE_BLOCK = 256
RMS_EPS = 1e-6

LANES = 128
HEAD_SLOT = 128
PROJ_WIDTH = 2176
POOL_PAD = 16
VMEM_LIMIT = 48 * 1024 * 1024
VMEM_LIMIT_MLA = 58 * 1024 * 1024

TM_PROJ = 1024
TQ = 512
TQ_MLA = 1024
TK_DIFF = 256
TK_MLA = 512
TM_OUT = 1024
TM_DISP = 1024
TM_COMB = 256

NT_DIMS = (((1,), (1,)), ((), ()))


def _cparams(n_axes, vmem_limit=VMEM_LIMIT):
    return pltpu.CompilerParams(dimension_semantics=("arbitrary",) * n_axes,
                                vmem_limit_bytes=vmem_limit)


def _full(shape):
    return pl.BlockSpec(shape, lambda *_: (0,) * len(shape))


def _proj_kernel(x_ref, g1_ref, win_ref, gq_ref, gk_ref, gckv_ref, gcqa_ref, gcqb_ref,
                 wuqa_ref, wuqb_ref, wkk_ref, wkv_ref, gkn_ref, qtab_ref, ktab_ref, eplace_ref,
                 dq1_ref, dq2_ref, dk_ref, dv_ref, qm_ref, km_ref, vm_ref, pu_ref):
    x = x_ref[...]
    xn = x * lax.rsqrt(jnp.mean(x * x, axis=-1, keepdims=True) + RMS_EPS) * g1_ref[...]
    proj = jnp.dot(xn.astype(BF16), win_ref[...], preferred_element_type=F32)

    tm = x.shape[0]
    lane = lax.broadcasted_iota(jnp.int32, (tm, LANES), 1)
    lo = lane < DIFF_QK

    def half_norm(c, g_row):
        sq = c * c
        s_lo = jnp.sum(jnp.where(lo, sq, 0.0), axis=-1, keepdims=True)
        s_hi = jnp.sum(jnp.where(lo, 0.0, sq), axis=-1, keepdims=True)
        r = jnp.where(lo, lax.rsqrt(s_lo / DIFF_QK + RMS_EPS), lax.rsqrt(s_hi / DIFF_QK + RMS_EPS))
        return c * r * g_row

    for h in range(DIFF_HEADS):
        sl = slice(h * HEAD_SLOT, (h + 1) * HEAD_SLOT)
        qn = half_norm(proj[:, sl], gq_ref[...])
        dq1_ref[:, sl] = jnp.where(lo, qn, 0.0).astype(BF16)
        dq2_ref[:, sl] = jnp.where(lo, 0.0, qn).astype(BF16)
        ksl = slice(512 + h * HEAD_SLOT, 512 + (h + 1) * HEAD_SLOT)
        dk_ref[:, sl] = half_norm(proj[:, ksl], gk_ref[...]).astype(BF16)
    dv_ref[...] = proj[:, 1024:1536].astype(BF16)
    pu_ref[...] = proj[:, 1536:1792]

    ckv = proj[:, 1792:1920]
    ckvn = ckv * lax.rsqrt(jnp.mean(ckv * ckv, axis=-1, keepdims=True) + RMS_EPS) * gckv_ref[...]
    ckvn = ckvn.astype(BF16)
    cqa = proj[:, 1920:2048]
    last = proj[:, 2048:2176]
    lsq = last * last
    ss_q = (jnp.sum(cqa * cqa, axis=-1, keepdims=True)
            + jnp.sum(jnp.where(lo, lsq, 0.0), axis=-1, keepdims=True))
    r_q = lax.rsqrt(ss_q / MLA_Q_RANK + RMS_EPS)
    q_raw = (jnp.dot((cqa * r_q * gcqa_ref[...]).astype(BF16), wuqa_ref[...], preferred_element_type=F32)
             + jnp.dot((last * r_q * gcqb_ref[...]).astype(BF16), wuqb_ref[...], preferred_element_type=F32))

    rope_lanes = (lane >= MLA_NOPE) & (lane < MLA_NOPE + MLA_ROPE)
    ss_kr = jnp.sum(jnp.where(rope_lanes, lsq, 0.0), axis=-1, keepdims=True)
    kr_terms = last * lax.rsqrt(ss_kr / MLA_ROPE + RMS_EPS) * ktab_ref[...]
    kr_placed = jnp.dot(kr_terms.astype(BF16), eplace_ref[...], preferred_element_type=F32)

    k_raw = jnp.dot(ckvn, wkk_ref[...], preferred_element_type=F32)
    vm_ref[...] = jnp.dot(ckvn, wkv_ref[...], preferred_element_type=F32).astype(BF16)
    qtab = qtab_ref[...]
    for h in range(MLA_HEADS):
        sl = slice(h * HEAD_SLOT, (h + 1) * HEAD_SLOT)
        c = q_raw[:, sl]
        sq = c * c
        s_n = jnp.sum(jnp.where(lo, sq, 0.0), axis=-1, keepdims=True)
        s_r = jnp.sum(jnp.where(rope_lanes, sq, 0.0), axis=-1, keepdims=True)
        r = jnp.where(lo, lax.rsqrt(s_n / MLA_NOPE + RMS_EPS), lax.rsqrt(s_r / MLA_ROPE + RMS_EPS))
        qm_ref[:, sl] = (c * r * qtab[:, sl]).astype(BF16)
        kc = k_raw[:, sl]
        r_k = lax.rsqrt(jnp.sum(kc * kc, axis=-1, keepdims=True) / MLA_NOPE + RMS_EPS)
        km_ref[:, sl] = (kc * r_k * gkn_ref[...] + kr_placed[:, sl]).astype(BF16)


def _proj_call(x2, p, seq):
    t = x2.shape[0]
    tm = TM_PROJ
    n_pos = seq // tm
    row = lambda i: (i, 0)
    pos = lambda i: (i % n_pos, 0)
    bf = lambda w: jax.ShapeDtypeStruct((t, w), BF16)
    in_specs = [
        pl.BlockSpec((tm, D_MODEL), row),
        _full((1, D_MODEL)), _full((D_MODEL, PROJ_WIDTH)),
        _full((1, LANES)), _full((1, LANES)), _full((1, LANES)), _full((1, LANES)), _full((1, LANES)),
        _full((LANES, 512)), _full((LANES, 512)), _full((LANES, 512)), _full((LANES, 512)),
        _full((1, LANES)),
        pl.BlockSpec((tm, 512), pos), pl.BlockSpec((tm, LANES), pos),
        _full((LANES, 512)),
    ]
    out_specs = [pl.BlockSpec((tm, 512), row)] * 7 + [pl.BlockSpec((tm, POOL_WIDTH), row)]
    out_shape = [bf(512)] * 7 + [jax.ShapeDtypeStruct((t, POOL_WIDTH), F32)]
    return pl.pallas_call(
        _proj_kernel, grid=(t // tm,), in_specs=in_specs, out_specs=out_specs, out_shape=out_shape,
        compiler_params=_cparams(1, VMEM_LIMIT_MLA),
    )(x2, p["g1"], p["win"], p["gq"], p["gk"], p["gckv"], p["gcqa"], p["gcqb"],
      p["wuqa"], p["wuqb"], p["wkk"], p["wkv"], p["gkn"], p["qtab"], p["ktab"], p["eplace"])


def _unflatten(n, sizes):
    n = jnp.minimum(n, math.prod(sizes) - 1)
    coords = []
    for size in reversed(sizes):
        coords.append(n % size)
        n = n // size
    return tuple(reversed(coords))


def _two_stage(n, stage, bufs):
    (s0, m0), (s1, m1) = bufs

    @pl.when(n == 0)
    def _():
        s1[...] = jnp.zeros_like(s1)
        m1[...] = jnp.zeros_like(m1)

    @pl.when(n % 2 == 0)
    def _():
        stage((s0, m0), (s1, m1))

    @pl.when(n % 2 == 1)
    def _():
        stage((s1, m1), (s0, m0))


def _two_stage_scratch(seq, tq):
    pair = [pltpu.VMEM((2 * tq, seq), F32), pltpu.VMEM((2 * tq, LANES), F32)]
    return pair + pair


def _softmax_pv_tile(s_prev, m_rows, c, tq, tk, lsum, acc, v_tiles, exp_fn):
    n_half = tk // LANES
    ps = [exp_fn(s_prev[:, (c * n_half + j) * LANES:(c * n_half + j + 1) * LANES] - m_rows)
          for j in range(n_half)]
    for ch in ps:
        lsum = ch if lsum is None else lsum + ch
    pb = jnp.concatenate(ps, axis=1).astype(BF16)
    for g, vt in enumerate(v_tiles):
        pv = jnp.dot(pb[g * tq:(g + 1) * tq], vt, preferred_element_type=F32)
        acc[g] = pv if acc[g] is None else acc[g] + pv
    return lsum


def _running_max(mx, sc):
    for j in range(sc.shape[1] // LANES):
        chunk = sc[:, j * LANES:(j + 1) * LANES]
        mx = chunk if mx is None else jnp.maximum(mx, chunk)
    return mx


def _diff_kernel(lam_ref, q1_ref, q2_ref, qx_ref, k_ref, kx_ref, bd_ref, v_ref, gsub_ref, o_ref,
                 s0_ref, m0_ref, s1_ref, m1_ref, *, sizes):
    n = pl.program_id(0)
    tq = q1_ref.shape[0]
    tk = TK_DIFF
    n_kt = k_ref.shape[0] // tk
    n_diag = tq // tk
    first_cur = _unflatten(n, sizes)[2] * n_diag
    first_prev = _unflatten(jnp.maximum(n - 1, 0), sizes)[2] * n_diag

    def stage(cur, prev):
        s_cur, m_cur = cur
        s_prev, m_prev = prev
        q1, q2 = q1_ref[...], q2_ref[...]
        qx_left = qx_ref[0, 0]
        qx_right = -qx_left
        bd = bd_ref[0]
        bd2 = jnp.concatenate([bd, bd], axis=0)
        mx = None
        m_rows = m_prev[...]
        lsum = None
        acc = [None, None]
        for c in range(n_kt):
            start = pl.multiple_of(((first_prev + c) % n_kt) * tk, tk)
            vt = v_ref[pl.ds(start, tk), :]
            lsum = _softmax_pv_tile(s_prev, m_rows, c, tq, tk, lsum, acc, [vt, vt], jnp.exp)

            start = pl.multiple_of(((first_cur + c) % n_kt) * tk, tk)
            if c < n_diag:
                qx = jnp.zeros_like(qx_left)
            else:
                qx = jnp.where(first_cur + c >= n_kt, qx_left, qx_right)
            qq = jnp.concatenate([jnp.concatenate([q1, qx], axis=1),
                                  jnp.concatenate([q2, qx], axis=1)], axis=0)
            kk = jnp.concatenate([k_ref[pl.ds(start, tk), :], kx_ref[pl.ds(start, tk), :]], axis=1)
            sc = lax.dot_general(qq, kk, NT_DIMS, preferred_element_type=F32)
            if c < n_diag:
                sc = sc + bd2[:, c * tk:(c + 1) * tk]
            s_cur[:, c * tk:(c + 1) * tk] = sc
            mx = _running_max(mx, sc)
        m_cur[...] = jnp.broadcast_to(jnp.max(mx, axis=-1, keepdims=True), m_cur.shape)
        l = jnp.sum(lsum, axis=-1, keepdims=True)
        o = acc[0] * (1.0 / l[0:tq]) - acc[1] * (lam_ref[0] / l[tq:2 * tq])
        r = lax.rsqrt(jnp.mean(o * o, axis=-1, keepdims=True) + RMS_EPS)
        o_ref[...] = (o * r * gsub_ref[...]).astype(BF16)

    _two_stage(n, stage, ((s0_ref, m0_ref), (s1_ref, m1_ref)))


def _alibi_tables(seq):
    nq = seq // TQ
    slopes = 2.0 ** (-8.0 * jnp.arange(1, DIFF_HEADS + 1, dtype=F32) / DIFF_HEADS)
    pos = jnp.arange(seq, dtype=jnp.int32)
    hi = (pos // 256).astype(F32)
    lo = (pos % 256).astype(F32)
    s4 = slopes[:, None]
    ones = jnp.ones((DIFF_HEADS, seq), F32)
    q_left = jnp.stack([-s4 * 256.0 * hi[None], -s4 * lo[None], s4 * 256.0 * ones, s4 * ones], axis=-1)
    qx = jnp.concatenate([q_left, jnp.zeros((DIFF_HEADS, seq, HEAD_SLOT - 4), F32)], axis=-1)
    qx = qx.reshape(DIFF_HEADS, nq, TQ, HEAD_SLOT).astype(BF16)
    k_cols = jnp.stack([jnp.ones((seq,), F32), jnp.ones((seq,), F32), hi, lo], axis=-1)
    kx = jnp.concatenate([k_cols, jnp.zeros((seq, HEAD_SLOT - 4), F32)], axis=-1).astype(BF16)
    loc = jnp.arange(TQ, dtype=jnp.int32)
    bd = -slopes[:, None, None] * jnp.abs(loc[:, None] - loc[None, :]).astype(F32)[None]
    return qx, kx, bd


def _diff_call(lam, dq1, dq2, dk, dv, gsub, tabs, batch, seq):
    t = dq1.shape[0]
    nq = seq // TQ
    qx, kx, bd = tabs
    sizes = (batch, DIFF_HEADS, nq)
    cur = lambda n: _unflatten(n, sizes)
    prev = lambda n: _unflatten(jnp.maximum(n - 1, 0), sizes)

    def qmap(n, *_):
        b, h, qi = cur(n)
        return (b * nq + qi, h)

    def kmap(n, *_):
        b, h, qi = cur(n)
        return (b, h)

    def vmap(n, *_):
        b, h, qi = prev(n)
        return (b, h)

    def omap(n, *_):
        b, h, qi = prev(n)
        return (b * nq + qi, h)

    grid_spec = pltpu.PrefetchScalarGridSpec(
        num_scalar_prefetch=1, grid=(math.prod(sizes) + 1,),
        in_specs=[pl.BlockSpec((TQ, HEAD_SLOT), qmap), pl.BlockSpec((TQ, HEAD_SLOT), qmap),
                  pl.BlockSpec((1, 1, TQ, HEAD_SLOT), lambda n, *_: cur(n)[1:] + (0, 0)),
                  pl.BlockSpec((seq, HEAD_SLOT), kmap),
                  pl.BlockSpec((seq, HEAD_SLOT), lambda *_: (0, 0)),
                  pl.BlockSpec((1, TQ, TQ), lambda n, *_: (cur(n)[1], 0, 0)),
                  pl.BlockSpec((seq, HEAD_SLOT), vmap),
                  pl.BlockSpec((1, HEAD_SLOT), lambda *_: (0, 0))],
        out_specs=pl.BlockSpec((TQ, HEAD_SLOT), omap),
        scratch_shapes=_two_stage_scratch(seq, TQ))
    return pl.pallas_call(
        functools.partial(_diff_kernel, sizes=sizes), grid_spec=grid_spec,
        out_shape=jax.ShapeDtypeStruct((t, DIFF_WIDTH), BF16),
        compiler_params=_cparams(1),
    )(lam, dq1, dq2, qx, dk, kx, bd, dv, gsub)


def _mla_kernel(q_ref, k_ref, v_ref, o_ref, s0_ref, m0_ref, s1_ref, m1_ref):
    n = pl.program_id(0)
    tq = q_ref.shape[0]
    tk = TK_MLA
    n_kt = k_ref.shape[0] // tk

    def stage(cur, prev):
        s_cur, m_cur = cur
        s_prev, m_prev = prev
        mx = [None, None]
        m_rows = m_prev[...]
        lsum = None
        acc = [None, None]
        for c in range(n_kt):
            rows = slice(c * tk, (c + 1) * tk)
            v_tiles = [v_ref[rows, hh * HEAD_SLOT:(hh + 1) * HEAD_SLOT] for hh in range(2)]
            lsum = _softmax_pv_tile(s_prev, m_rows, c, tq, tk, lsum, acc, v_tiles, jnp.exp2)
            for hh in range(2):
                sl = slice(hh * HEAD_SLOT, (hh + 1) * HEAD_SLOT)
                sc = lax.dot_general(q_ref[:, sl], k_ref[rows, sl], NT_DIMS, preferred_element_type=F32)
                s_cur[hh * tq:(hh + 1) * tq, rows] = sc
                mx[hh] = _running_max(mx[hh], sc)
        mx = jnp.concatenate(mx, axis=0)
        m_cur[...] = jnp.broadcast_to(jnp.max(mx, axis=-1, keepdims=True), m_cur.shape)
        l = jnp.sum(lsum, axis=-1, keepdims=True)
        o = acc[0] * (1.0 / l[0:tq]) + acc[1] * (1.0 / l[tq:2 * tq])
        o_ref[...] = o.astype(BF16)

    _two_stage(n, stage, ((s0_ref, m0_ref), (s1_ref, m1_ref)))


def _mla_call(qm, km, vm, batch, seq):
    t = qm.shape[0]
    tq = TQ_MLA
    nq = seq // tq
    sizes = (batch, MLA_HEADS // 2, nq)
    cur = lambda n: _unflatten(n, sizes)
    prev = lambda n: _unflatten(jnp.maximum(n - 1, 0), sizes)

    def qmap(n):
        b, p, qi = cur(n)
        return (b * nq + qi, p)

    def kmap(n):
        b, p, qi = cur(n)
        return (b, p)

    def vmap(n):
        b, p, qi = prev(n)
        return (b, p)

    def omap(n):
        b, p, qi = prev(n)
        return (b * nq + qi, p)

    return pl.pallas_call(
        _mla_kernel, grid=(math.prod(sizes) + 1,),
        in_specs=[pl.BlockSpec((tq, 2 * HEAD_SLOT), qmap), pl.BlockSpec((seq, 2 * HEAD_SLOT), kmap),
                  pl.BlockSpec((seq, 2 * HEAD_SLOT), vmap)],
        out_specs=pl.BlockSpec((tq, HEAD_SLOT), omap),
        out_shape=jax.ShapeDtypeStruct((t, MLA_WIDTH), BF16),
        scratch_shapes=_two_stage_scratch(seq, tq),
        compiler_params=_cparams(1, VMEM_LIMIT_MLA),
    )(qm, km, vm)


def _pool_kernel(u_ref, w_ref, scale_ref, o_ref):
    u = u_ref[...]
    seq, width = u.shape
    zpad = jnp.zeros((POOL_PAD, width), F32)
    ue = jnp.concatenate([zpad, u, zpad], axis=0)
    n = seq + 2 * POOL_PAD

    def down(a, k):
        return pltpu.roll(a, k, axis=0)

    def up(a, k):
        return pltpu.roll(a, n - k, axis=0)

    a2 = ue + down(ue, 1)
    a4 = down(a2, 1) + up(a2, 1)
    a8 = down(a4, 2) + up(a4, 2)
    a16 = down(a8, 4) + up(a8, 4)
    core = slice(POOL_PAD, POOL_PAD + seq)
    lane = lax.broadcasted_iota(jnp.int32, (seq, width), 1)
    tpos = lax.broadcasted_iota(jnp.int32, (seq, width), 0)
    grp = lane // POOL_GROUP_DIM
    win_sum = jnp.where(grp == 0, a2[core], jnp.where(grp == 1, a4[core], jnp.where(grp == 2, a8[core], a16[core])))
    half = jnp.where(grp == 0, 1, jnp.where(grp == 1, 2, jnp.where(grp == 2, 4, 8)))
    lo_i = jnp.maximum(tpos - half, 0)
    hi_i = jnp.minimum(tpos + half - 1, seq - 1)
    cnt = (hi_i - lo_i + 1).astype(F32)
    pooled = win_sum / cnt - u
    mixed = jnp.dot(pooled.astype(BF16), w_ref[...], preferred_element_type=F32)
    o_ref[...] = (mixed * scale_ref[...]).astype(BF16)


def _pool_call(pu, w_bd, scale, batch, seq):
    t = pu.shape[0]
    return pl.pallas_call(
        _pool_kernel, grid=(batch,),
        in_specs=[pl.BlockSpec((seq, POOL_WIDTH), lambda b: (b, 0)),
                  _full((POOL_WIDTH, POOL_WIDTH)), _full((1, POOL_WIDTH))],
        out_specs=pl.BlockSpec((seq, POOL_WIDTH), lambda b: (b, 0)),
        out_shape=jax.ShapeDtypeStruct((t, POOL_WIDTH), BF16),
        compiler_params=_cparams(1),
    )(pu, w_bd, scale)


def _outproj_kernel(x_ref, od_ref, om_ref, op_ref, wo_ref, g2_ref, wrh_ref, wrl_ref, br_ref,
                    xo_ref, h2_ref, ri_ref, rg_ref, cnt_ref, carry_ref):
    i = pl.program_id(0)

    @pl.when(i == 0)
    def _():
        carry_ref[...] = jnp.zeros_like(carry_ref)

    xn = (x_ref[...]
          + jnp.dot(od_ref[...], wo_ref[0:DIFF_WIDTH, :], preferred_element_type=F32)
          + jnp.dot(om_ref[...], wo_ref[DIFF_WIDTH:DIFF_WIDTH + MLA_WIDTH, :], preferred_element_type=F32)
          + jnp.dot(op_ref[...], wo_ref[DIFF_WIDTH + MLA_WIDTH:, :], preferred_element_type=F32))
    xo_ref[...] = xn
    h2 = xn * lax.rsqrt(jnp.mean(xn * xn, axis=-1, keepdims=True) + RMS_EPS) * g2_ref[...]
    n_chunk = D_MODEL // LANES
    for j in range(n_chunk):
        h2_ref[pl.ds(j, xn.shape[0], stride=n_chunk), :] = h2[:, j * LANES:(j + 1) * LANES]

    h_hi = h2.astype(BF16)
    h_lo = (h2 - h_hi.astype(F32)).astype(BF16)
    logits = (jnp.dot(h_hi, wrh_ref[...], preferred_element_type=F32)
              + jnp.dot(h_lo, wrh_ref[...], preferred_element_type=F32)
              + jnp.dot(h_hi, wrl_ref[...], preferred_element_type=F32)
              + br_ref[...])
    tm = logits.shape[0]
    lane = lax.broadcasted_iota(jnp.int32, (tm, LANES), 1)
    lane_f = lane.astype(F32)
    neg = jnp.float32(-jnp.inf)
    big = jnp.float32(1e9)

    gmask = lane < N_GROUPS
    gl = jnp.where(gmask, logits, neg)
    gmax = jnp.max(gl, axis=-1, keepdims=True)
    gsum = jnp.sum(jnp.where(gmask, jnp.exp(gl - gmax), 0.0), axis=-1, keepdims=True)
    g_top = 1.0 / gsum
    g_idx = jnp.min(jnp.where(gl == gmax, lane_f, big), axis=-1, keepdims=True)

    e_lo = N_GROUPS + EXPERTS_PER_GROUP * g_idx
    emask = (lane_f >= e_lo) & (lane_f < e_lo + EXPERTS_PER_GROUP)
    el = jnp.where(emask, logits, neg)
    emax = jnp.max(el, axis=-1, keepdims=True)
    eexp = jnp.where(emask, jnp.exp(el - emax), 0.0)
    prob = eexp / jnp.sum(eexp, axis=-1, keepdims=True)
    pm = jnp.where(emask, prob, -1.0)
    p1 = jnp.max(pm, axis=-1, keepdims=True)
    i1 = jnp.min(jnp.where(pm == p1, lane_f, big), axis=-1, keepdims=True)
    pm2 = jnp.where(lane_f == i1, -1.0, pm)
    p2 = jnp.max(pm2, axis=-1, keepdims=True)
    i2 = jnp.min(jnp.where(pm2 == p2, lane_f, big), axis=-1, keepdims=True)
    denom = p1 + p2
    gate1 = g_top * p1 / denom
    gate2 = g_top * p2 / denom

    sel1 = lane_f == i1
    sel2 = lane_f == i2
    onehot = jnp.where(sel1 | sel2, 1.0, 0.0)
    rr = lax.broadcasted_iota(jnp.int32, (tm, tm), 0)
    cc = lax.broadcasted_iota(jnp.int32, (tm, tm), 1)
    ltri = jnp.where(cc < rr, 1.0, 0.0).astype(BF16)
    prefix = jnp.dot(ltri, onehot.astype(BF16), preferred_element_type=F32) + carry_ref[...]
    rank1 = jnp.sum(jnp.where(sel1, prefix, 0.0), axis=-1, keepdims=True)
    rank2 = jnp.sum(jnp.where(sel2, prefix, 0.0), axis=-1, keepdims=True)
    carry_ref[...] = carry_ref[...] + jnp.sum(onehot, axis=0, keepdims=True)
    cnt_ref[...] = carry_ref[...]

    info = jnp.where(lane == 0, i1 - N_GROUPS,
                     jnp.where(lane == 1, i2 - N_GROUPS,
                               jnp.where(lane == 2, rank1, jnp.where(lane == 3, rank2, 0.0))))
    ri_ref[...] = info.astype(jnp.int32)
    rg_ref[...] = jnp.where(lane == 0, gate1, jnp.where(lane == 1, gate2, 0.0))


def _outproj_call(x2, od, om, op, p):
    t = x2.shape[0]
    tm = TM_OUT
    row = lambda i: (i, 0)
    return pl.pallas_call(
        _outproj_kernel, grid=(t // tm,),
        in_specs=[pl.BlockSpec((tm, D_MODEL), row), pl.BlockSpec((tm, DIFF_WIDTH), row),
                  pl.BlockSpec((tm, MLA_WIDTH), row), pl.BlockSpec((tm, POOL_WIDTH), row),
                  _full((D_MODEL, D_MODEL)), _full((1, D_MODEL)),
                  _full((D_MODEL, LANES)), _full((D_MODEL, LANES)), _full((1, LANES))],
        out_specs=[pl.BlockSpec((tm, D_MODEL), row),
                   pl.BlockSpec((tm * D_MODEL // LANES, LANES), row),
                   pl.BlockSpec((tm, LANES), row), pl.BlockSpec((tm, LANES), row),
                   _full((1, LANES))],
        out_shape=[jax.ShapeDtypeStruct((t, D_MODEL), F32),
                   jax.ShapeDtypeStruct((t * D_MODEL // LANES, LANES), F32),
                   jax.ShapeDtypeStruct((t, LANES), jnp.int32), jax.ShapeDtypeStruct((t, LANES), F32),
                   jax.ShapeDtypeStruct((1, LANES), F32)],
        scratch_shapes=[pltpu.VMEM((1, LANES), F32)],
        compiler_params=_cparams(1),
    )(x2, od, om, op, p["wo"], p["g2"], p["wrh"], p["wrl"], p["br"])


PAD_CHUNKS = (128, 64, 32, 16, 8, 4, 2, 1)


def _dispatch_kernel(pad_ref, idx_ref, h_ref, xs_hbm, zero_ref, sem, zsem):
    i = pl.program_id(0)
    n_chunk = D_MODEL // LANES
    tm = h_ref.shape[0] // n_chunk

    @pl.when(i == 0)
    def _():
        zero_ref[...] = jnp.zeros_like(zero_ref)

        def pad_copies(e, wait):
            off = pad_ref[0, e]
            cnt = pad_ref[1, e]
            for size in PAD_CHUNKS:
                take = cnt & size

                @pl.when(take != 0)
                def _():
                    cp = pltpu.make_async_copy(
                        zero_ref.at[pl.ds(0, size * n_chunk), :],
                        xs_hbm.at[pl.ds(pl.multiple_of(off * n_chunk, n_chunk), size * n_chunk), :], zsem)
                    if wait:
                        cp.wait()
                    else:
                        cp.start()
                off = off + take

        def start_body(e, c):
            pad_copies(e, False)
            return c

        def wait_body(e, c):
            pad_copies(e, True)
            return c

        lax.fori_loop(0, N_EXPERTS, start_body, 0)
        lax.fori_loop(0, N_EXPERTS, wait_body, 0)

        tail = pad_ref[0, N_EXPERTS]
        zrows = zero_ref.shape[0]

        def tail_copy(c):
            return pltpu.make_async_copy(
                zero_ref, xs_hbm.at[pl.ds(pl.multiple_of(tail * n_chunk + c * zrows, zrows), zrows), :], zsem)

        def tail_start(c, carry):
            tail_copy(c).start()
            return carry

        def tail_wait(c, carry):
            tail_copy(c).wait()
            return carry

        lax.fori_loop(0, pad_ref[1, N_EXPERTS], tail_start, 0)
        lax.fori_loop(0, pad_ref[1, N_EXPERTS], tail_wait, 0)

    for r in range(tm):
        for kk in range(2):
            dst = pl.multiple_of(idx_ref[0, 0, kk * tm + r] * n_chunk, n_chunk)
            pltpu.make_async_copy(h_ref.at[pl.ds(r * n_chunk, n_chunk), :],
                                  xs_hbm.at[pl.ds(dst, n_chunk), :], sem).start(priority=kk)
    for kk in range(2):
        pltpu.make_async_copy(h_ref, xs_hbm.at[pl.ds(0, tm * n_chunk), :], sem).wait()


def _dispatch_call(pads, dest3, h3, n_slots):
    n_tiles = dest3.shape[0]
    tm = dest3.shape[2] // 2
    n_chunk = D_MODEL // LANES
    grid_spec = pltpu.PrefetchScalarGridSpec(
        num_scalar_prefetch=1, grid=(n_tiles,),
        in_specs=[pl.BlockSpec((1, 1, 2 * tm), lambda i, pads: (i, 0, 0), memory_space=pltpu.SMEM),
                  pl.BlockSpec((tm * n_chunk, LANES), lambda i, pads: (i, 0))],
        out_specs=pl.BlockSpec(memory_space=pl.ANY),
        scratch_shapes=[pltpu.VMEM((PAD_CHUNKS[0] * n_chunk, LANES), F32),
                        pltpu.SemaphoreType.DMA, pltpu.SemaphoreType.DMA])
    return pl.pallas_call(
        _dispatch_kernel, grid_spec=grid_spec,
        out_shape=jax.ShapeDtypeStruct((n_slots * n_chunk, LANES), F32),
        compiler_params=_cparams(1),
    )(pads, dest3, h3)


EXPERT_BLOCKS_PER_STEP = 2


def _expert_kernel(be_ref, nu_ref, xs_ref, *refs):
    i = pl.program_id(0)
    n_chunk = D_MODEL // LANES
    rows_per_block = ROUTE_BLOCK * n_chunk
    ys_ref = refs[-1]
    for b in range(EXPERT_BLOCKS_PER_STEP):
        wg_ref, wu_ref, wd_ref = refs[3 * b:3 * b + 3]
        base = b * rows_per_block
        block = i * EXPERT_BLOCKS_PER_STEP + b

        @pl.when(block < nu_ref[0])
        def _():
            xb = jnp.concatenate([xs_ref[pl.ds(base + j, ROUTE_BLOCK, stride=n_chunk), :]
                                  for j in range(n_chunk)], axis=1).astype(BF16)
            g = jnp.dot(xb, wg_ref[0, 0].astype(BF16), preferred_element_type=F32)
            u = jnp.dot(xb, wu_ref[0, 0].astype(BF16), preferred_element_type=F32)
            hmid = g * (1.0 / (1.0 + jnp.exp(-g))) * u
            y = jnp.dot(hmid.astype(BF16), wd_ref[0, 0].astype(BF16), preferred_element_type=F32)
            for j in range(n_chunk):
                ys_ref[pl.ds(base + j, ROUTE_BLOCK, stride=n_chunk), :] = y[:, j * LANES:(j + 1) * LANES]

        @pl.when(block >= nu_ref[0])
        def _():
            ys_ref[base:base + rows_per_block, :] = jnp.zeros((rows_per_block, LANES), F32)


def _expert_call(block_eid, n_used, xs3, wg, wu, wd, layer):
    per_step = EXPERT_BLOCKS_PER_STEP
    tile = (per_step * ROUTE_BLOCK * D_MODEL // LANES, LANES)
    n_steps = xs3.shape[0] // tile[0]
    row = lambda i, be, nu: (i, 0)
    weight_specs = []
    for b in range(per_step):
        wmap = lambda i, be, nu, b=b: (layer, be[i * per_step + b], 0, 0)
        weight_specs += [pl.BlockSpec((1, 1, D_MODEL, D_FF), wmap), pl.BlockSpec((1, 1, D_MODEL, D_FF), wmap),
                         pl.BlockSpec((1, 1, D_FF, D_MODEL), wmap)]
    grid_spec = pltpu.PrefetchScalarGridSpec(
        num_scalar_prefetch=2, grid=(n_steps,),
        in_specs=[pl.BlockSpec(tile, row)] + weight_specs,
        out_specs=pl.BlockSpec(tile, row))
    return pl.pallas_call(
        _expert_kernel, grid_spec=grid_spec,
        out_shape=jax.ShapeDtypeStruct(xs3.shape, F32),
        compiler_params=_cparams(1),
    )(block_eid, n_used, xs3, *([wg, wu, wd] * per_step))


def _combine_kernel(idx0_ref, idxn_ref, ys_hbm, x_ref, rg_ref, o_ref, buf0, buf1, sem):
    i = pl.program_id(0)
    n = pl.num_programs(0)
    tm = x_ref.shape[0]
    n_chunk = D_MODEL // LANES

    def issue(idx_ref, buf, sem_slot, rows):
        for r in rows:
            for kk in range(2):
                src = pl.multiple_of(idx_ref[0, 0, kk * tm + r] * n_chunk, n_chunk)
                pltpu.make_async_copy(ys_hbm.at[pl.ds(src, n_chunk), :],
                                      buf.at[pl.ds((kk * tm + r) * n_chunk, n_chunk), :],
                                      sem_slot).start(priority=kk)

    def wait_tile(buf, sem_slot):
        pltpu.make_async_copy(ys_hbm.at[pl.ds(0, 2 * tm * n_chunk), :], buf, sem_slot).wait()

    @pl.when(i == 0)
    def _():
        issue(idx0_ref, buf0, sem.at[0], range(tm))

    def step(buf, sem_cur, buf_next, sem_next):
        issue(idxn_ref, buf_next, sem_next, range(tm))
        wait_tile(buf, sem_cur)
        rg = rg_ref[...]
        g0 = rg[:, 0:1]
        g1 = rg[:, 1:2]
        for j in range(n_chunk):
            cols = slice(j * LANES, (j + 1) * LANES)
            y0 = buf[pl.ds(j, tm, stride=n_chunk), :]
            y1 = buf[pl.ds(tm * n_chunk + j, tm, stride=n_chunk), :]
            o_ref[:, cols] = x_ref[:, cols] + g0 * y0 + g1 * y1

    @pl.when(i % 2 == 0)
    def _():
        step(buf0, sem.at[0], buf1, sem.at[1])

    @pl.when(i % 2 == 1)
    def _():
        step(buf1, sem.at[1], buf0, sem.at[0])

    @pl.when(i == n - 1)
    def _():
        @pl.when(i % 2 == 0)
        def _():
            wait_tile(buf1, sem.at[1])

        @pl.when(i % 2 == 1)
        def _():
            wait_tile(buf0, sem.at[0])


def _combine_call(dest3, ys, x2, rg):
    t = x2.shape[0]
    tm = TM_COMB
    n = t // tm
    row = lambda i: (i, 0)
    tile = (2 * tm * D_MODEL // LANES, LANES)
    return pl.pallas_call(
        _combine_kernel, grid=(n,),
        in_specs=[
            pl.BlockSpec((1, 1, 2 * tm), lambda i: (0, 0, 0), memory_space=pltpu.SMEM),
            pl.BlockSpec((1, 1, 2 * tm), lambda i: (jnp.minimum(i + 1, n - 1), 0, 0), memory_space=pltpu.SMEM),
            pl.BlockSpec(memory_space=pl.ANY),
            pl.BlockSpec((tm, D_MODEL), row), pl.BlockSpec((tm, LANES), row)],
        out_specs=pl.BlockSpec((tm, D_MODEL), row),
        out_shape=jax.ShapeDtypeStruct((t, D_MODEL), F32),
        scratch_shapes=[pltpu.VMEM(tile, F32), pltpu.VMEM(tile, F32), pltpu.SemaphoreType.DMA((2,))],
        compiler_params=_cparams(1),
    )(dest3, dest3, ys, x2, rg)


def _swap_halves(a):
    half = a.shape[-1] // 2
    return jnp.concatenate([a[..., half:], a[..., :half]], axis=-1)


def _layer_params(l, seq, w):
    p = {}
    row = lambda v: v.reshape(1, -1).astype(F32)
    w_in = w["w_in"][l]
    kr_cols = w_in[:, 1856:1888]
    p["win"] = jnp.concatenate(
        [w_in[:, 0:1536], w_in[:, 1888:2144], w_in[:, 1728:1856], w_in[:, 1536:1728],
         kr_cols, _swap_halves(kr_cols)], axis=1).astype(BF16)
    p["g1"] = row(w["norm1_g"][l])
    p["gq"] = row(jnp.tile(w["diff_q_norm_g"][l], 2) * (DIFF_QK ** -0.5))
    p["gk"] = row(jnp.tile(w["diff_k_norm_g"][l], 2))
    p["gckv"] = row(w["mla_kv_lat_norm_g"][l])
    gcq = w["mla_q_lat_norm_g"][l]
    p["gcqa"] = row(gcq[:LANES])
    p["gcqb"] = row(jnp.concatenate([gcq[LANES:], jnp.zeros((2 * LANES - MLA_Q_RANK,), F32)]))

    wuq = w["mla_w_uq"][l].reshape(MLA_Q_RANK, MLA_HEADS, MLA_NOPE + MLA_ROPE)
    rope_w = wuq[:, :, MLA_NOPE:]
    wuq = jnp.concatenate([wuq[:, :, :MLA_NOPE], rope_w, _swap_halves(rope_w)], axis=-1)
    wuq = wuq.reshape(MLA_Q_RANK, MLA_HEADS * HEAD_SLOT)
    wuq = jnp.concatenate([wuq, jnp.zeros((2 * LANES - MLA_Q_RANK, wuq.shape[1]), F32)], axis=0).astype(BF16)
    p["wuqa"] = wuq[:LANES]
    p["wuqb"] = wuq[LANES:]

    wukv = w["mla_w_ukv"][l].reshape(MLA_KV_RANK, MLA_HEADS, MLA_NOPE + MLA_V)
    zk = jnp.zeros((MLA_KV_RANK, MLA_HEADS, HEAD_SLOT - MLA_NOPE), F32)
    p["wkk"] = jnp.concatenate([wukv[:, :, :MLA_NOPE], zk], axis=-1).reshape(MLA_KV_RANK, -1).astype(BF16)
    vcols = wukv[:, :, MLA_NOPE:]
    zv = jnp.zeros_like(vcols)
    even = (jnp.arange(MLA_HEADS) % 2 == 0)[None, :, None]
    wkv = jnp.concatenate([jnp.where(even, vcols, zv), jnp.where(even, zv, vcols)], axis=-1)
    p["wkv"] = wkv.reshape(MLA_KV_RANK, -1).astype(BF16)
    p["gkn"] = row(jnp.concatenate([w["mla_k_nope_norm_g"][l], jnp.zeros((HEAD_SLOT - MLA_NOPE,), F32)]))

    inv = 1.0 / (ROPE_BASE ** (jnp.arange(0, MLA_ROPE, 2, dtype=F32) / MLA_ROPE))
    ang = jnp.arange(seq, dtype=F32)[:, None] * inv[None, :]
    cosf = jnp.concatenate([jnp.cos(ang), jnp.cos(ang)], axis=-1)
    sinf = jnp.concatenate([-jnp.sin(ang), jnp.sin(ang)], axis=-1)
    scale = (MLA_NOPE + MLA_ROPE) ** -0.5 * math.log2(math.e)
    gqr = w["mla_q_rope_norm_g"][l]
    q_head = jnp.concatenate([jnp.broadcast_to(w["mla_q_nope_norm_g"][l][None, :], (seq, MLA_NOPE)),
                              gqr[None, :] * cosf, _swap_halves(gqr)[None, :] * sinf], axis=-1) * scale
    p["qtab"] = jnp.tile(q_head, (1, MLA_HEADS))
    gkr = w["mla_k_rope_norm_g"][l]
    p["ktab"] = jnp.concatenate([jnp.zeros((seq, MLA_NOPE), F32), gkr[None, :] * cosf,
                                 _swap_halves(gkr)[None, :] * sinf], axis=-1)
    src = jnp.arange(LANES)
    dst = jnp.arange(MLA_HEADS * HEAD_SLOT)
    src_j = jnp.where(src >= MLA_NOPE, (src - MLA_NOPE) % MLA_ROPE, -1)
    dst_l = dst % HEAD_SLOT
    dst_j = jnp.where(dst_l >= MLA_NOPE, (dst_l - MLA_NOPE) % MLA_ROPE, -2)
    p["eplace"] = (src_j[:, None] == dst_j[None, :]).astype(BF16)

    pw = w["pool_w"][l]
    bd = jnp.zeros((POOL_WIDTH, POOL_WIDTH), F32)
    for g in range(POOL_GROUPS):
        s0 = g * POOL_GROUP_DIM
        bd = bd.at[s0:s0 + POOL_GROUP_DIM, s0:s0 + POOL_GROUP_DIM].set(pw[g])
    p["pool_w"] = bd.astype(BF16)
    p["pool_scale"] = row(w["pool_scale"][l])

    lam_init = 0.8 - 0.6 * math.exp(-0.3 * l)
    lv = w["diff_lambda"][l].astype(F32)
    p["lam"] = (jnp.exp(jnp.sum(lv[0] * lv[1])) - jnp.exp(jnp.sum(lv[2] * lv[3])) + lam_init).reshape(1)
    p["gsub"] = row(w["diff_sub_norm_g"][l] * (1.0 - lam_init))

    p["wo"] = w["w_out"][l].astype(BF16)
    p["g2"] = row(w["norm2_g"][l])
    wr = jnp.concatenate([w["router_group_w"][l], w["router_expert_w"][l],
                          jnp.zeros((D_MODEL, LANES - N_GROUPS - N_EXPERTS), F32)], axis=1)
    wr_hi = wr.astype(BF16)
    p["wrh"] = wr_hi
    p["wrl"] = (wr - wr_hi.astype(F32)).astype(BF16)
    p["br"] = row(jnp.concatenate([w["router_group_b"][l], w["router_expert_b"][l],
                                   jnp.zeros((LANES - N_GROUPS - N_EXPERTS,), F32)]))
    return p


def kernel(x, norm1_g, w_in, diff_q_norm_g, diff_k_norm_g, diff_lambda, diff_sub_norm_g, mla_q_lat_norm_g, mla_kv_lat_norm_g, mla_w_uq, mla_w_ukv, mla_q_nope_norm_g, mla_q_rope_norm_g, mla_k_nope_norm_g, mla_k_rope_norm_g, pool_w, pool_scale, w_out, norm2_g, router_group_w, router_group_b, router_expert_w, router_expert_b, expert_w_gate, expert_w_up, expert_w_down):
    w = dict(norm1_g=norm1_g, w_in=w_in, diff_q_norm_g=diff_q_norm_g, diff_k_norm_g=diff_k_norm_g,
             diff_lambda=diff_lambda, diff_sub_norm_g=diff_sub_norm_g, mla_q_lat_norm_g=mla_q_lat_norm_g,
             mla_kv_lat_norm_g=mla_kv_lat_norm_g, mla_w_uq=mla_w_uq, mla_w_ukv=mla_w_ukv,
             mla_q_nope_norm_g=mla_q_nope_norm_g, mla_q_rope_norm_g=mla_q_rope_norm_g,
             mla_k_nope_norm_g=mla_k_nope_norm_g, mla_k_rope_norm_g=mla_k_rope_norm_g,
             pool_w=pool_w, pool_scale=pool_scale, w_out=w_out, norm2_g=norm2_g,
             router_group_w=router_group_w, router_group_b=router_group_b,
             router_expert_w=router_expert_w, router_expert_b=router_expert_b)
    batch, seq, d = x.shape
    t = batch * seq
    n_assign = 2 * t
    n_blocks = n_assign // ROUTE_BLOCK + N_EXPERTS
    alibi = _alibi_tables(seq)

    x2 = x.reshape(t, d)
    for l in range(DEPTH):
        p = _layer_params(l, seq, w)
        dq1, dq2, dk, dv, qm, km, vm, pu = _proj_call(x2, p, seq)
        o_diff = _diff_call(p["lam"], dq1, dq2, dk, dv, p["gsub"], alibi, batch, seq)
        o_mla = _mla_call(qm, km, vm, batch, seq)
        o_pool = _pool_call(pu, p["pool_w"], p["pool_scale"], batch, seq)
        x2, h2, route_i, route_g, counts = _outproj_call(x2, o_diff, o_mla, o_pool, p)

        cnt = counts[0, N_GROUPS:N_GROUPS + N_EXPERTS].astype(jnp.int32)
        padded = (cnt + ROUTE_BLOCK - 1) // ROUTE_BLOCK * ROUTE_BLOCK
        padded_ends = jnp.cumsum(padded)
        padded_starts = padded_ends - padded
        info_t = route_i[:, 0:4].T
        start_of = jnp.sum(jnp.where(info_t[0:2, None, :] == jnp.arange(N_EXPERTS, dtype=jnp.int32)[None, :, None],
                                     padded_starts[None, :, None], 0), axis=1)
        dest_t = start_of + info_t[2:4]

        def dest_tiles(tm):
            return dest_t.reshape(2, t // tm, tm).transpose(1, 0, 2).reshape(t // tm, 1, 2 * tm)
        block_start = jnp.arange(n_blocks, dtype=jnp.int32) * ROUTE_BLOCK
        block_eid = jnp.minimum(jnp.sum(block_start[:, None] >= padded_ends[None, :], axis=1),
                                N_EXPERTS - 1).astype(jnp.int32)
        n_used = (padded_ends[-1] // ROUTE_BLOCK).astype(jnp.int32).reshape(1)
        n_tail = (n_blocks - n_used) * (ROUTE_BLOCK // PAD_CHUNKS[0])
        pads = jnp.stack([jnp.concatenate([padded_starts + cnt, padded_ends[-1:]]),
                          jnp.concatenate([padded - cnt, n_tail])]).astype(jnp.int32)
        xs = _dispatch_call(pads, dest_tiles(TM_DISP), h2, n_blocks * ROUTE_BLOCK)
        ys = _expert_call(block_eid, n_used, xs, expert_w_gate, expert_w_up, expert_w_down, l)
        x2 = _combine_call(dest_tiles(TM_COMB), ys, x2, route_g)
    return x2.reshape(batch, seq, d)
```

```python
import functools
import math

import jax
import jax.numpy as jnp
import numpy as np
from jax import lax
from jax.experimental import pallas as pl
from jax.experimental.pallas import tpu as pltpu

F32 = jnp.float32
BF16 = jnp.bfloat16

D_MODEL = 1024
DEPTH = 2
DIFF_HEADS = 4
DIFF_QK = 64
DIFF_V = 128
DIFF_WIDTH = 512
MLA_HEADS = 4
MLA_NOPE = 64
MLA_ROPE = 32
MLA_V = 64
MLA_Q_RANK = 192
MLA_KV_RANK = 128
MLA_WIDTH = 256
ROPE_BASE = 10000.0
POOL_WIDTH = 256
POOL_GROUPS = 4
POOL_GROUP_DIM = 64
POOL_WINDOWS = (2, 4, 8, 16)
N_GROUPS = 4
EXPERTS_PER_GROUP = 8
N_EXPERTS = 32
D_FF = 256
ROUTE_BLOCK = 256
RMS_EPS = 1e-6

LANES = 128
HEAD_SLOT = 128
PROJ_WIDTH = 2176
POOL_PAD = 16
VMEM_LIMIT = 48 * 1024 * 1024
VMEM_LIMIT_MLA = 58 * 1024 * 1024

TM_PROJ = 1024
TQ = 512
TQ_MLA = 1024
TK_DIFF = 256
TK_MLA = 512
TM_OUT = 1024
TM_DISP = 1024
TM_COMB = 256

NT_DIMS = (((1,), (1,)), ((), ()))


def _cparams(n_axes, vmem_limit=VMEM_LIMIT):
    return pltpu.CompilerParams(dimension_semantics=("arbitrary",) * n_axes,
                                vmem_limit_bytes=vmem_limit)


def _full(shape):
    return pl.BlockSpec(shape, lambda *_: (0,) * len(shape))


def _proj_kernel(x_ref, g1_ref, win_ref, gq_ref, gk_ref, gckv_ref, gcqa_ref, gcqb_ref,
                 wuqa_ref, wuqb_ref, wkk_ref, wkv_ref, gkn_ref, qtab_ref, ktab_ref, eplace_ref,
                 dq1_ref, dq2_ref, dk_ref, dv_ref, qm_ref, km_ref, vm_ref, pu_ref):
    x = x_ref[...]
    xn = x * lax.rsqrt(jnp.mean(x * x, axis=-1, keepdims=True) + RMS_EPS) * g1_ref[...]
    proj = jnp.dot(xn.astype(BF16), win_ref[...], preferred_element_type=F32)

    tm = x.shape[0]
    lane = lax.broadcasted_iota(jnp.int32, (tm, LANES), 1)
    lo = lane < DIFF_QK

    def half_norm(c, g_row):
        sq = c * c
        s_lo = jnp.sum(jnp.where(lo, sq, 0.0), axis=-1, keepdims=True)
        s_hi = jnp.sum(jnp.where(lo, 0.0, sq), axis=-1, keepdims=True)
        r = jnp.where(lo, lax.rsqrt(s_lo / DIFF_QK + RMS_EPS), lax.rsqrt(s_hi / DIFF_QK + RMS_EPS))
        return c * r * g_row

    for h in range(DIFF_HEADS):
        sl = slice(h * HEAD_SLOT, (h + 1) * HEAD_SLOT)
        qn = half_norm(proj[:, sl], gq_ref[...])
        dq1_ref[:, sl] = jnp.where(lo, qn, 0.0).astype(BF16)
        dq2_ref[:, sl] = jnp.where(lo, 0.0, qn).astype(BF16)
        ksl = slice(512 + h * HEAD_SLOT, 512 + (h + 1) * HEAD_SLOT)
        dk_ref[:, sl] = half_norm(proj[:, ksl], gk_ref[...]).astype(BF16)
    dv_ref[...] = proj[:, 1024:1536].astype(BF16)
    pu_ref[...] = proj[:, 1536:1792]

    ckv = proj[:, 1792:1920]
    ckvn = ckv * lax.rsqrt(jnp.mean(ckv * ckv, axis=-1, keepdims=True) + RMS_EPS) * gckv_ref[...]
    ckvn = ckvn.astype(BF16)
    cqa = proj[:, 1920:2048]
    last = proj[:, 2048:2176]
    lsq = last * last
    ss_q = (jnp.sum(cqa * cqa, axis=-1, keepdims=True)
            + jnp.sum(jnp.where(lo, lsq, 0.0), axis=-1, keepdims=True))
    r_q = lax.rsqrt(ss_q / MLA_Q_RANK + RMS_EPS)
    q_raw = (jnp.dot((cqa * r_q * gcqa_ref[...]).astype(BF16), wuqa_ref[...], preferred_element_type=F32)
             + jnp.dot((last * r_q * gcqb_ref[...]).astype(BF16), wuqb_ref[...], preferred_element_type=F32))

    rope_lanes = (lane >= MLA_NOPE) & (lane < MLA_NOPE + MLA_ROPE)
    ss_kr = jnp.sum(jnp.where(rope_lanes, lsq, 0.0), axis=-1, keepdims=True)
    kr_terms = last * lax.rsqrt(ss_kr / MLA_ROPE + RMS_EPS) * ktab_ref[...]
    kr_placed = jnp.dot(kr_terms.astype(BF16), eplace_ref[...], preferred_element_type=F32)

    k_raw = jnp.dot(ckvn, wkk_ref[...], preferred_element_type=F32)
    vm_ref[...] = jnp.dot(ckvn, wkv_ref[...], preferred_element_type=F32).astype(BF16)
    qtab = qtab_ref[...]
    for h in range(MLA_HEADS):
        sl = slice(h * HEAD_SLOT, (h + 1) * HEAD_SLOT)
        c = q_raw[:, sl]
        sq = c * c
        s_n = jnp.sum(jnp.where(lo, sq, 0.0), axis=-1, keepdims=True)
        s_r = jnp.sum(jnp.where(rope_lanes, sq, 0.0), axis=-1, keepdims=True)
        r = jnp.where(lo, lax.rsqrt(s_n / MLA_NOPE + RMS_EPS), lax.rsqrt(s_r / MLA_ROPE + RMS_EPS))
        qm_ref[:, sl] = (c * r * qtab[:, sl]).astype(BF16)
        kc = k_raw[:, sl]
        r_k = lax.rsqrt(jnp.sum(kc * kc, axis=-1, keepdims=True) / MLA_NOPE + RMS_EPS)
        km_ref[:, sl] = (kc * r_k * gkn_ref[...] + kr_placed[:, sl]).astype(BF16)


def _proj_call(x2, p, seq):
    t = x2.shape[0]
    tm = TM_PROJ
    n_pos = seq // tm
    row = lambda i: (i, 0)
    pos = lambda i: (i % n_pos, 0)
    bf = lambda w: jax.ShapeDtypeStruct((t, w), BF16)
    in_specs = [
        pl.BlockSpec((tm, D_MODEL), row),
        _full((1, D_MODEL)), _full((D_MODEL, PROJ_WIDTH)),
        _full((1, LANES)), _full((1, LANES)), _full((1, LANES)), _full((1, LANES)), _full((1, LANES)),
        _full((LANES, 512)), _full((LANES, 512)), _full((LANES, 512)), _full((LANES, 512)),
        _full((1, LANES)),
        pl.BlockSpec((tm, 512), pos), pl.BlockSpec((tm, LANES), pos),
        _full((LANES, 512)),
    ]
    out_specs = [pl.BlockSpec((tm, 512), row)] * 7 + [pl.BlockSpec((tm, POOL_WIDTH), row)]
    out_shape = [bf(512)] * 7 + [jax.ShapeDtypeStruct((t, POOL_WIDTH), F32)]
    return pl.pallas_call(
        _proj_kernel, grid=(t // tm,), in_specs=in_specs, out_specs=out_specs, out_shape=out_shape,
        compiler_params=_cparams(1, VMEM_LIMIT_MLA),
    )(x2, p["g1"], p["win"], p["gq"], p["gk"], p["gckv"], p["gcqa"], p["gcqb"],
      p["wuqa"], p["wuqb"], p["wkk"], p["wkv"], p["gkn"], p["qtab"], p["ktab"], p["eplace"])


def _unflatten(n, sizes):
    n = jnp.minimum(n, math.prod(sizes) - 1)
    coords = []
    for size in reversed(sizes):
        coords.append(n % size)
        n = n // size
    return tuple(reversed(coords))


def _two_stage(n, stage, bufs):
    (s0, m0), (s1, m1) = bufs

    @pl.when(n == 0)
    def _():
        s1[...] = jnp.zeros_like(s1)
        m1[...] = jnp.zeros_like(m1)

    @pl.when(n % 2 == 0)
    def _():
        stage((s0, m0), (s1, m1))

    @pl.when(n % 2 == 1)
    def _():
        stage((s1, m1), (s0, m0))


def _two_stage_scratch(seq, tq):
    pair = [pltpu.VMEM((2 * tq, seq), F32), pltpu.VMEM((2 * tq, LANES), F32)]
    return pair + pair


def _softmax_pv_tile(s_prev, m_rows, c, tq, tk, lsum, acc, v_tiles, exp_fn):
    n_half = tk // LANES
    ps = [exp_fn(s_prev[:, (c * n_half + j) * LANES:(c * n_half + j + 1) * LANES] - m_rows)
          for j in range(n_half)]
    for ch in ps:
        lsum = ch if lsum is None else lsum + ch
    pb = jnp.concatenate(ps, axis=1).astype(BF16)
    for g, vt in enumerate(v_tiles):
        pv = jnp.dot(pb[g * tq:(g + 1) * tq], vt, preferred_element_type=F32)
        acc[g] = pv if acc[g] is None else acc[g] + pv
    return lsum


def _running_max(mx, sc):
    for j in range(sc.shape[1] // LANES):
        chunk = sc[:, j * LANES:(j + 1) * LANES]
        mx = chunk if mx is None else jnp.maximum(mx, chunk)
    return mx


def _diff_kernel(lam_ref, q1_ref, q2_ref, qx_ref, k_ref, kx_ref, bd_ref, v_ref, gsub_ref, o_ref,
                 s0_ref, m0_ref, s1_ref, m1_ref, *, sizes):
    n = pl.program_id(0)
    tq = q1_ref.shape[0]
    tk = TK_DIFF
    n_kt = k_ref.shape[0] // tk
    n_diag = tq // tk
    first_cur = _unflatten(n, sizes)[2] * n_diag
    first_prev = _unflatten(jnp.maximum(n - 1, 0), sizes)[2] * n_diag

    def stage(cur, prev):
        s_cur, m_cur = cur
        s_prev, m_prev = prev
        q1, q2 = q1_ref[...], q2_ref[...]
        qx_left = qx_ref[0, 0]
        qx_right = -qx_left
        bd = bd_ref[0]
        bd2 = jnp.concatenate([bd, bd], axis=0)
        mx = None
        m_rows = m_prev[...]
        lsum = None
        acc = [None, None]
        for c in range(n_kt):
            start = pl.multiple_of(((first_prev + c) % n_kt) * tk, tk)
            vt = v_ref[pl.ds(start, tk), :]
            lsum = _softmax_pv_tile(s_prev, m_rows, c, tq, tk, lsum, acc, [vt, vt], jnp.exp)

            start = pl.multiple_of(((first_cur + c) % n_kt) * tk, tk)
            if c < n_diag:
                qx = jnp.zeros_like(qx_left)
            else:
                qx = jnp.where(first_cur + c >= n_kt, qx_left, qx_right)
            qq = jnp.concatenate([jnp.concatenate([q1, qx], axis=1),
                                  jnp.concatenate([q2, qx], axis=1)], axis=0)
            kk = jnp.concatenate([k_ref[pl.ds(start, tk), :], kx_ref[pl.ds(start, tk), :]], axis=1)
            sc = lax.dot_general(qq, kk, NT_DIMS, preferred_element_type=F32)
            if c < n_diag:
                sc = sc + bd2[:, c * tk:(c + 1) * tk]
            s_cur[:, c * tk:(c + 1) * tk] = sc
            mx = _running_max(mx, sc)
        m_cur[...] = jnp.broadcast_to(jnp.max(mx, axis=-1, keepdims=True), m_cur.shape)
        l = jnp.sum(lsum, axis=-1, keepdims=True)
        o = acc[0] * (1.0 / l[0:tq]) - acc[1] * (lam_ref[0] / l[tq:2 * tq])
        r = lax.rsqrt(jnp.mean(o * o, axis=-1, keepdims=True) + RMS_EPS)
        o_ref[...] = (o * r * gsub_ref[...]).astype(BF16)

    _two_stage(n, stage, ((s0_ref, m0_ref), (s1_ref, m1_ref)))


def _alibi_tables(seq):
    nq = seq // TQ
    slopes = (2.0 ** (-8.0 * np.arange(1, DIFF_HEADS + 1, dtype=np.float32) / DIFF_HEADS)).astype(np.float32)
    pos = np.arange(seq, dtype=np.int32)
    hi = (pos // 256).astype(np.float32)
    lo = (pos % 256).astype(np.float32)
    s4 = slopes[:, None]
    ones = np.ones((DIFF_HEADS, seq), np.float32)
    q_left = np.stack([-s4 * 256.0 * hi[None], -s4 * lo[None], s4 * 256.0 * ones, s4 * ones], axis=-1)
    qx = np.concatenate([q_left, np.zeros((DIFF_HEADS, seq, HEAD_SLOT - 4), np.float32)], axis=-1)
    qx = jnp.asarray(qx.reshape(DIFF_HEADS, nq, TQ, HEAD_SLOT), dtype=BF16)
    k_cols = np.stack([np.ones((seq,), np.float32), np.ones((seq,), np.float32), hi, lo], axis=-1)
    kx = jnp.asarray(np.concatenate([k_cols, np.zeros((seq, HEAD_SLOT - 4), np.float32)], axis=-1), dtype=BF16)
    loc = np.arange(TQ, dtype=np.int32)
    bd = -slopes[:, None, None] * np.abs(loc[:, None] - loc[None, :]).astype(np.float32)[None]
    return qx, kx, jnp.asarray(bd)


def _diff_call(lam, dq1, dq2, dk, dv, gsub, tabs, batch, seq):
    t = dq1.shape[0]
    nq = seq // TQ
    qx, kx, bd = tabs
    sizes = (batch, DIFF_HEADS, nq)
    cur = lambda n: _unflatten(n, sizes)
    prev = lambda n: _unflatten(jnp.maximum(n - 1, 0), sizes)

    def qmap(n, *_):
        b, h, qi = cur(n)
        return (b * nq + qi, h)

    def kmap(n, *_):
        b, h, qi = cur(n)
        return (b, h)

    def vmap(n, *_):
        b, h, qi = prev(n)
        return (b, h)

    def omap(n, *_):
        b, h, qi = prev(n)
        return (b * nq + qi, h)

    grid_spec = pltpu.PrefetchScalarGridSpec(
        num_scalar_prefetch=1, grid=(math.prod(sizes) + 1,),
        in_specs=[pl.BlockSpec((TQ, HEAD_SLOT), qmap), pl.BlockSpec((TQ, HEAD_SLOT), qmap),
                  pl.BlockSpec((1, 1, TQ, HEAD_SLOT), lambda n, *_: cur(n)[1:] + (0, 0)),
                  pl.BlockSpec((seq, HEAD_SLOT), kmap),
                  pl.BlockSpec((seq, HEAD_SLOT), lambda *_: (0, 0)),
                  pl.BlockSpec((1, TQ, TQ), lambda n, *_: (cur(n)[1], 0, 0)),
                  pl.BlockSpec((seq, HEAD_SLOT), vmap),
                  pl.BlockSpec((1, HEAD_SLOT), lambda *_: (0, 0))],
        out_specs=pl.BlockSpec((TQ, HEAD_SLOT), omap),
        scratch_shapes=_two_stage_scratch(seq, TQ))
    return pl.pallas_call(
        functools.partial(_diff_kernel, sizes=sizes), grid_spec=grid_spec,
        out_shape=jax.ShapeDtypeStruct((t, DIFF_WIDTH), BF16),
        compiler_params=_cparams(1),
    )(lam, dq1, dq2, qx, dk, kx, bd, dv, gsub)


def _mla_kernel(q_ref, k_ref, v_ref, o_ref, s0_ref, m0_ref, s1_ref, m1_ref):
    n = pl.program_id(0)
    tq = q_ref.shape[0]
    tk = TK_MLA
    n_kt = k_ref.shape[0] // tk

    def stage(cur, prev):
        s_cur, m_cur = cur
        s_prev, m_prev = prev
        mx = [None, None]
        m_rows = m_prev[...]
        lsum = None
        acc = [None, None]
        for c in range(n_kt):
            rows = slice(c * tk, (c + 1) * tk)
            v_tiles = [v_ref[rows, hh * HEAD_SLOT:(hh + 1) * HEAD_SLOT] for hh in range(2)]
            lsum = _softmax_pv_tile(s_prev, m_rows, c, tq, tk, lsum, acc, v_tiles, jnp.exp2)
            for hh in range(2):
                sl = slice(hh * HEAD_SLOT, (hh + 1) * HEAD_SLOT)
                sc = lax.dot_general(q_ref[:, sl], k_ref[rows, sl], NT_DIMS, preferred_element_type=F32)
                s_cur[hh * tq:(hh + 1) * tq, rows] = sc
                mx[hh] = _running_max(mx[hh], sc)
        mx = jnp.concatenate(mx, axis=0)
        m_cur[...] = jnp.broadcast_to(jnp.max(mx, axis=-1, keepdims=True), m_cur.shape)
        l = jnp.sum(lsum, axis=-1, keepdims=True)
        o = acc[0] * (1.0 / l[0:tq]) + acc[1] * (1.0 / l[tq:2 * tq])
        o_ref[...] = o.astype(BF16)

    _two_stage(n, stage, ((s0_ref, m0_ref), (s1_ref, m1_ref)))


def _mla_call(qm, km, vm, batch, seq):
    t = qm.shape[0]
    tq = TQ_MLA
    nq = seq // tq
    sizes = (batch, MLA_HEADS // 2, nq)
    cur = lambda n: _unflatten(n, sizes)
    prev = lambda n: _unflatten(jnp.maximum(n - 1, 0), sizes)

    def qmap(n):
        b, p, qi = cur(n)
        return (b * nq + qi, p)

    def kmap(n):
        b, p, qi = cur(n)
        return (b, p)

    def vmap(n):
        b, p, qi = prev(n)
        return (b, p)

    def omap(n):
        b, p, qi = prev(n)
        return (b * nq + qi, p)

    return pl.pallas_call(
        _mla_kernel, grid=(math.prod(sizes) + 1,),
        in_specs=[pl.BlockSpec((tq, 2 * HEAD_SLOT), qmap), pl.BlockSpec((seq, 2 * HEAD_SLOT), kmap),
                  pl.BlockSpec((seq, 2 * HEAD_SLOT), vmap)],
        out_specs=pl.BlockSpec((tq, HEAD_SLOT), omap),
        out_shape=jax.ShapeDtypeStruct((t, MLA_WIDTH), BF16),
        scratch_shapes=_two_stage_scratch(seq, tq),
        compiler_params=_cparams(1, VMEM_LIMIT_MLA),
    )(qm, km, vm)


def _pool_kernel(u_ref, w_ref, scale_ref, o_ref):
    u = u_ref[...]
    seq, width = u.shape
    zpad = jnp.zeros((POOL_PAD, width), F32)
    ue = jnp.concatenate([zpad, u, zpad], axis=0)
    n = seq + 2 * POOL_PAD

    def down(a, k):
        return pltpu.roll(a, k, axis=0)

    def up(a, k):
        return pltpu.roll(a, n - k, axis=0)

    a2 = ue + down(ue, 1)
    a4 = down(a2, 1) + up(a2, 1)
    a8 = down(a4, 2) + up(a4, 2)
    a16 = down(a8, 4) + up(a8, 4)
    core = slice(POOL_PAD, POOL_PAD + seq)
    lane = lax.broadcasted_iota(jnp.int32, (seq, width), 1)
    tpos = lax.broadcasted_iota(jnp.int32, (seq, width), 0)
    grp = lane // POOL_GROUP_DIM
    win_sum = jnp.where(grp == 0, a2[core], jnp.where(grp == 1, a4[core], jnp.where(grp == 2, a8[core], a16[core])))
    half = jnp.where(grp == 0, 1, jnp.where(grp == 1, 2, jnp.where(grp == 2, 4, 8)))
    lo_i = jnp.maximum(tpos - half, 0)
    hi_i = jnp.minimum(tpos + half - 1, seq - 1)
    cnt = (hi_i - lo_i + 1).astype(F32)
    pooled = win_sum / cnt - u
    mixed = jnp.dot(pooled.astype(BF16), w_ref[...], preferred_element_type=F32)
    o_ref[...] = (mixed * scale_ref[...]).astype(BF16)


def _pool_call(pu, w_bd, scale, batch, seq):
    t = pu.shape[0]
    return pl.pallas_call(
        _pool_kernel, grid=(batch,),
        in_specs=[pl.BlockSpec((seq, POOL_WIDTH), lambda b: (b, 0)),
                  _full((POOL_WIDTH, POOL_WIDTH)), _full((1, POOL_WIDTH))],
        out_specs=pl.BlockSpec((seq, POOL_WIDTH), lambda b: (b, 0)),
        out_shape=jax.ShapeDtypeStruct((t, POOL_WIDTH), BF16),
        compiler_params=_cparams(1),
    )(pu, w_bd, scale)


def _outproj_kernel(x_ref, od_ref, om_ref, op_ref, wo_ref, g2_ref, wrh_ref, wrl_ref, br_ref,
                    xo_ref, h2_ref, ri_ref, rg_ref, cnt_ref, carry_ref):
    i = pl.program_id(0)

    @pl.when(i == 0)
    def _():
        carry_ref[...] = jnp.zeros_like(carry_ref)

    xn = (x_ref[...]
          + jnp.dot(od_ref[...], wo_ref[0:DIFF_WIDTH, :], preferred_element_type=F32)
          + jnp.dot(om_ref[...], wo_ref[DIFF_WIDTH:DIFF_WIDTH + MLA_WIDTH, :], preferred_element_type=F32)
          + jnp.dot(op_ref[...], wo_ref[DIFF_WIDTH + MLA_WIDTH:, :], preferred_element_type=F32))
    xo_ref[...] = xn
    h2 = xn * lax.rsqrt(jnp.mean(xn * xn, axis=-1, keepdims=True) + RMS_EPS) * g2_ref[...]
    n_chunk = D_MODEL // LANES
    for j in range(n_chunk):
        h2_ref[pl.ds(j, xn.shape[0], stride=n_chunk), :] = h2[:, j * LANES:(j + 1) * LANES]

    h_hi = h2.astype(BF16)
    h_lo = (h2 - h_hi.astype(F32)).astype(BF16)
    logits = (jnp.dot(h_hi, wrh_ref[...], preferred_element_type=F32)
              + jnp.dot(h_lo, wrh_ref[...], preferred_element_type=F32)
              + jnp.dot(h_hi, wrl_ref[...], preferred_element_type=F32)
              + br_ref[...])
    tm = logits.shape[0]
    lane = lax.broadcasted_iota(jnp.int32, (tm, LANES), 1)
    lane_f = lane.astype(F32)
    neg = jnp.float32(-jnp.inf)
    big = jnp.float32(1e9)

    gmask = lane < N_GROUPS
    gl = jnp.where(gmask, logits, neg)
    gmax = jnp.max(gl, axis=-1, keepdims=True)
    gsum = jnp.sum(jnp.where(gmask, jnp.exp(gl - gmax), 0.0), axis=-1, keepdims=True)
    g_top = 1.0 / gsum
    g_idx = jnp.min(jnp.where(gl == gmax, lane_f, big), axis=-1, keepdims=True)

    e_lo = N_GROUPS + EXPERTS_PER_GROUP * g_idx
    emask = (lane_f >= e_lo) & (lane_f < e_lo + EXPERTS_PER_GROUP)
    el = jnp.where(emask, logits, neg)
    emax = jnp.max(el, axis=-1, keepdims=True)
    eexp = jnp.where(emask, jnp.exp(el - emax), 0.0)
    prob = eexp / jnp.sum(eexp, axis=-1, keepdims=True)
    pm = jnp.where(emask, prob, -1.0)
    p1 = jnp.max(pm, axis=-1, keepdims=True)
    i1 = jnp.min(jnp.where(pm == p1, lane_f, big), axis=-1, keepdims=True)
    pm2 = jnp.where(lane_f == i1, -1.0, pm)
    p2 = jnp.max(pm2, axis=-1, keepdims=True)
    i2 = jnp.min(jnp.where(pm2 == p2, lane_f, big), axis=-1, keepdims=True)
    denom = p1 + p2
    gate1 = g_top * p1 / denom
    gate2 = g_top * p2 / denom

    sel1 = lane_f == i1
    sel2 = lane_f == i2
    onehot = jnp.where(sel1 | sel2, 1.0, 0.0)
    rr = lax.broadcasted_iota(jnp.int32, (tm, tm), 0)
    cc = lax.broadcasted_iota(jnp.int32, (tm, tm), 1)
    ltri = jnp.where(cc < rr, 1.0, 0.0).astype(BF16)
    prefix = jnp.dot(ltri, onehot.astype(BF16), preferred_element_type=F32) + carry_ref[...]
    rank1 = jnp.sum(jnp.where(sel1, prefix, 0.0), axis=-1, keepdims=True)
    rank2 = jnp.sum(jnp.where(sel2, prefix, 0.0), axis=-1, keepdims=True)
    carry_ref[...] = carry_ref[...] + jnp.sum(onehot, axis=0, keepdims=True)
    cnt_ref[...] = carry_ref[...]

    info = jnp.where(lane == 0, i1 - N_GROUPS,
                     jnp.where(lane == 1, i2 - N_GROUPS,
                               jnp.where(lane == 2, rank1, jnp.where(lane == 3, rank2, 0.0))))
    ri_ref[...] = info.astype(jnp.int32)
    rg_ref[...] = jnp.where(lane == 0, gate1, jnp.where(lane == 1, gate2, 0.0))


def _outproj_call(x2, od, om, op, p):
    t = x2.shape[0]
    tm = TM_OUT
    row = lambda i: (i, 0)
    return pl.pallas_call(
        _outproj_kernel, grid=(t // tm,),
        in_specs=[pl.BlockSpec((tm, D_MODEL), row), pl.BlockSpec((tm, DIFF_WIDTH), row),
                  pl.BlockSpec((tm, MLA_WIDTH), row), pl.BlockSpec((tm, POOL_WIDTH), row),
                  _full((D_MODEL, D_MODEL)), _full((1, D_MODEL)),
                  _full((D_MODEL, LANES)), _full((D_MODEL, LANES)), _full((1, LANES))],
        out_specs=[pl.BlockSpec((tm, D_MODEL), row),
                   pl.BlockSpec((tm * D_MODEL // LANES, LANES), row),
                   pl.BlockSpec((tm, LANES), row), pl.BlockSpec((tm, LANES), row),
                   _full((1, LANES))],
        out_shape=[jax.ShapeDtypeStruct((t, D_MODEL), F32),
                   jax.ShapeDtypeStruct((t * D_MODEL // LANES, LANES), F32),
                   jax.ShapeDtypeStruct((t, LANES), jnp.int32), jax.ShapeDtypeStruct((t, LANES), F32),
                   jax.ShapeDtypeStruct((1, LANES), F32)],
        scratch_shapes=[pltpu.VMEM((1, LANES), F32)],
        compiler_params=_cparams(1),
    )(x2, od, om, op, p["wo"], p["g2"], p["wrh"], p["wrl"], p["br"])


PAD_CHUNKS = (128, 64, 32, 16, 8, 4, 2, 1)


def _dispatch_kernel(pad_ref, idx_ref, h_ref, xs_hbm, zero_ref, sem, zsem):
    i = pl.program_id(0)
    n_chunk = D_MODEL // LANES
    tm = h_ref.shape[0] // n_chunk

    @pl.when(i == 0)
    def _():
        zero_ref[...] = jnp.zeros_like(zero_ref)

        def pad_copies(e, wait):
            off = pad_ref[0, e]
            cnt = pad_ref[1, e]
            for size in PAD_CHUNKS:
                take = cnt & size

                @pl.when(take != 0)
                def _():
                    cp = pltpu.make_async_copy(
                        zero_ref.at[pl.ds(0, size * n_chunk), :],
                        xs_hbm.at[pl.ds(pl.multiple_of(off * n_chunk, n_chunk), size * n_chunk), :], zsem)
                    if wait:
                        cp.wait()
                    else:
                        cp.start()
                off = off + take

        def start_body(e, c):
            pad_copies(e, False)
            return c

        def wait_body(e, c):
            pad_copies(e, True)
            return c

        lax.fori_loop(0, N_EXPERTS, start_body, 0)
        lax.fori_loop(0, N_EXPERTS, wait_body, 0)

        tail = pad_ref[0, N_EXPERTS]
        zrows = zero_ref.shape[0]

        def tail_copy(c):
            return pltpu.make_async_copy(
                zero_ref, xs_hbm.at[pl.ds(pl.multiple_of(tail * n_chunk + c * zrows, zrows), zrows), :], zsem)

        def tail_start(c, carry):
            tail_copy(c).start()
            return carry

        def tail_wait(c, carry):
            tail_copy(c).wait()
            return carry

        lax.fori_loop(0, pad_ref[1, N_EXPERTS], tail_start, 0)
        lax.fori_loop(0, pad_ref[1, N_EXPERTS], tail_wait, 0)

    for r in range(tm):
        for kk in range(2):
            dst = pl.multiple_of(idx_ref[0, 0, kk * tm + r] * n_chunk, n_chunk)
            pltpu.make_async_copy(h_ref.at[pl.ds(r * n_chunk, n_chunk), :],
                                  xs_hbm.at[pl.ds(dst, n_chunk), :], sem).start(priority=kk)
    for kk in range(2):
        pltpu.make_async_copy(h_ref, xs_hbm.at[pl.ds(0, tm * n_chunk), :], sem).wait()


def _dispatch_call(pads, dest3, h3, n_slots):
    n_tiles = dest3.shape[0]
    tm = dest3.shape[2] // 2
    n_chunk = D_MODEL // LANES
    grid_spec = pltpu.PrefetchScalarGridSpec(
        num_scalar_prefetch=1, grid=(n_tiles,),
        in_specs=[pl.BlockSpec((1, 1, 2 * tm), lambda i, pads: (i, 0, 0), memory_space=pltpu.SMEM),
                  pl.BlockSpec((tm * n_chunk, LANES), lambda i, pads: (i, 0))],
        out_specs=pl.BlockSpec(memory_space=pl.ANY),
        scratch_shapes=[pltpu.VMEM((PAD_CHUNKS[0] * n_chunk, LANES), F32),
                        pltpu.SemaphoreType.DMA, pltpu.SemaphoreType.DMA])
    return pl.pallas_call(
        _dispatch_kernel, grid_spec=grid_spec,
        out_shape=jax.ShapeDtypeStruct((n_slots * n_chunk, LANES), F32),
        compiler_params=_cparams(1),
    )(pads, dest3, h3)


EXPERT_BLOCKS_PER_STEP = 2


def _expert_kernel(be_ref, nu_ref, xs_ref, *refs):
    i = pl.program_id(0)
    n_chunk = D_MODEL // LANES
    rows_per_block = ROUTE_BLOCK * n_chunk
    ys_ref = refs[-1]
    for b in range(EXPERT_BLOCKS_PER_STEP):
        wg_ref, wu_ref, wd_ref = refs[3 * b:3 * b + 3]
        base = b * rows_per_block
        block = i * EXPERT_BLOCKS_PER_STEP + b

        @pl.when(block < nu_ref[0])
        def _():
            xb = jnp.concatenate([xs_ref[pl.ds(base + j, ROUTE_BLOCK, stride=n_chunk), :]
                                  for j in range(n_chunk)], axis=1).astype(BF16)
            g = jnp.dot(xb, wg_ref[0, 0].astype(BF16), preferred_element_type=F32)
            u = jnp.dot(xb, wu_ref[0, 0].astype(BF16), preferred_element_type=F32)
            hmid = g * (1.0 / (1.0 + jnp.exp(-g))) * u
            y = jnp.dot(hmid.astype(BF16), wd_ref[0, 0].astype(BF16), preferred_element_type=F32)
            for j in range(n_chunk):
                ys_ref[pl.ds(base + j, ROUTE_BLOCK, stride=n_chunk), :] = y[:, j * LANES:(j + 1) * LANES]

        @pl.when(block >= nu_ref[0])
        def _():
            ys_ref[base:base + rows_per_block, :] = jnp.zeros((rows_per_block, LANES), F32)


def _expert_call(block_eid, n_used, xs3, wg, wu, wd, layer):
    per_step = EXPERT_BLOCKS_PER_STEP
    tile = (per_step * ROUTE_BLOCK * D_MODEL // LANES, LANES)
    n_steps = xs3.shape[0] // tile[0]
    row = lambda i, be, nu: (i, 0)
    weight_specs = []
    for b in range(per_step):
        wmap = lambda i, be, nu, b=b: (layer, be[i * per_step + b], 0, 0)
        weight_specs += [pl.BlockSpec((1, 1, D_MODEL, D_FF), wmap), pl.BlockSpec((1, 1, D_MODEL, D_FF), wmap),
                         pl.BlockSpec((1, 1, D_FF, D_MODEL), wmap)]
    grid_spec = pltpu.PrefetchScalarGridSpec(
        num_scalar_prefetch=2, grid=(n_steps,),
        in_specs=[pl.BlockSpec(tile, row)] + weight_specs,
        out_specs=pl.BlockSpec(tile, row))
    return pl.pallas_call(
        _expert_kernel, grid_spec=grid_spec,
        out_shape=jax.ShapeDtypeStruct(xs3.shape, F32),
        compiler_params=_cparams(1),
    )(block_eid, n_used, xs3, *([wg, wu, wd] * per_step))


def _combine_kernel(idx0_ref, idxn_ref, ys_hbm, x_ref, rg_ref, o_ref, buf0, buf1, sem):
    i = pl.program_id(0)
    n = pl.num_programs(0)
    tm = x_ref.shape[0]
    n_chunk = D_MODEL // LANES

    def issue(idx_ref, buf, sem_slot, rows):
        for r in rows:
            for kk in range(2):
                src = pl.multiple_of(idx_ref[0, 0, kk * tm + r] * n_chunk, n_chunk)
                pltpu.make_async_copy(ys_hbm.at[pl.ds(src, n_chunk), :],
                                      buf.at[pl.ds((kk * tm + r) * n_chunk, n_chunk), :],
                                      sem_slot).start(priority=kk)

    def wait_tile(buf, sem_slot):
        pltpu.make_async_copy(ys_hbm.at[pl.ds(0, 2 * tm * n_chunk), :], buf, sem_slot).wait()

    @pl.when(i == 0)
    def _():
        issue(idx0_ref, buf0, sem.at[0], range(tm))

    def step(buf, sem_cur, buf_next, sem_next):
        issue(idxn_ref, buf_next, sem_next, range(tm))
        wait_tile(buf, sem_cur)
        rg = rg_ref[...]
        g0 = rg[:, 0:1]
        g1 = rg[:, 1:2]
        for j in range(n_chunk):
            cols = slice(j * LANES, (j + 1) * LANES)
            y0 = buf[pl.ds(j, tm, stride=n_chunk), :]
            y1 = buf[pl.ds(tm * n_chunk + j, tm, stride=n_chunk), :]
            o_ref[:, cols] = x_ref[:, cols] + g0 * y0 + g1 * y1

    @pl.when(i % 2 == 0)
    def _():
        step(buf0, sem.at[0], buf1, sem.at[1])

    @pl.when(i % 2 == 1)
    def _():
        step(buf1, sem.at[1], buf0, sem.at[0])

    @pl.when(i == n - 1)
    def _():
        @pl.when(i % 2 == 0)
        def _():
            wait_tile(buf1, sem.at[1])

        @pl.when(i % 2 == 1)
        def _():
            wait_tile(buf0, sem.at[0])


def _combine_call(dest3, ys, x2, rg):
    t = x2.shape[0]
    tm = TM_COMB
    n = t // tm
    row = lambda i: (i, 0)
    tile = (2 * tm * D_MODEL // LANES, LANES)
    return pl.pallas_call(
        _combine_kernel, grid=(n,),
        in_specs=[
            pl.BlockSpec((1, 1, 2 * tm), lambda i: (0, 0, 0), memory_space=pltpu.SMEM),
            pl.BlockSpec((1, 1, 2 * tm), lambda i: (jnp.minimum(i + 1, n - 1), 0, 0), memory_space=pltpu.SMEM),
            pl.BlockSpec(memory_space=pl.ANY),
            pl.BlockSpec((tm, D_MODEL), row), pl.BlockSpec((tm, LANES), row)],
        out_specs=pl.BlockSpec((tm, D_MODEL), row),
        out_shape=jax.ShapeDtypeStruct((t, D_MODEL), F32),
        scratch_shapes=[pltpu.VMEM(tile, F32), pltpu.VMEM(tile, F32), pltpu.SemaphoreType.DMA((2,))],
        compiler_params=_cparams(1),
    )(dest3, dest3, ys, x2, rg)


def _swap_halves(a):
    half = a.shape[-1] // 2
    return jnp.concatenate([a[..., half:], a[..., :half]], axis=-1)


def _layer_params(l, seq, w):
    p = {}
    row = lambda v: v.reshape(1, -1).astype(F32)
    w_in = w["w_in"][l]
    kr_cols = w_in[:, 1856:1888]
    p["win"] = jnp.concatenate(
        [w_in[:, 0:1536], w_in[:, 1888:2144], w_in[:, 1728:1856], w_in[:, 1536:1728],
         kr_cols, _swap_halves(kr_cols)], axis=1).astype(BF16)
    p["g1"] = row(w["norm1_g"][l])
    p["gq"] = row(jnp.tile(w["diff_q_norm_g"][l], 2) * (DIFF_QK ** -0.5))
    p["gk"] = row(jnp.tile(w["diff_k_norm_g"][l], 2))
    p["gckv"] = row(w["mla_kv_lat_norm_g"][l])
    gcq = w["mla_q_lat_norm_g"][l]
    p["gcqa"] = row(gcq[:LANES])
    p["gcqb"] = row(jnp.concatenate([gcq[LANES:], jnp.zeros((2 * LANES - MLA_Q_RANK,), F32)]))

    wuq = w["mla_w_uq"][l].reshape(MLA_Q_RANK, MLA_HEADS, MLA_NOPE + MLA_ROPE)
    rope_w = wuq[:, :, MLA_NOPE:]
    wuq = jnp.concatenate([wuq[:, :, :MLA_NOPE], rope_w, _swap_halves(rope_w)], axis=-1)
    wuq = wuq.reshape(MLA_Q_RANK, MLA_HEADS * HEAD_SLOT)
    wuq = jnp.concatenate([wuq, jnp.zeros((2 * LANES - MLA_Q_RANK, wuq.shape[1]), F32)], axis=0).astype(BF16)
    p["wuqa"] = wuq[:LANES]
    p["wuqb"] = wuq[LANES:]

    wukv = w["mla_w_ukv"][l].reshape(MLA_KV_RANK, MLA_HEADS, MLA_NOPE + MLA_V)
    zk = jnp.zeros((MLA_KV_RANK, MLA_HEADS, HEAD_SLOT - MLA_NOPE), F32)
    p["wkk"] = jnp.concatenate([wukv[:, :, :MLA_NOPE], zk], axis=-1).reshape(MLA_KV_RANK, -1).astype(BF16)
    vcols = wukv[:, :, MLA_NOPE:]
    zv = jnp.zeros_like(vcols)
    even = (jnp.arange(MLA_HEADS) % 2 == 0)[None, :, None]
    wkv = jnp.concatenate([jnp.where(even, vcols, zv), jnp.where(even, zv, vcols)], axis=-1)
    p["wkv"] = wkv.reshape(MLA_KV_RANK, -1).astype(BF16)
    p["gkn"] = row(jnp.concatenate([w["mla_k_nope_norm_g"][l], jnp.zeros((HEAD_SLOT - MLA_NOPE,), F32)]))

    inv = 1.0 / (ROPE_BASE ** (jnp.arange(0, MLA_ROPE, 2, dtype=F32) / MLA_ROPE))
    ang = jnp.arange(seq, dtype=F32)[:, None] * inv[None, :]
    cosf = jnp.concatenate([jnp.cos(ang), jnp.cos(ang)], axis=-1)
    sinf = jnp.concatenate([-jnp.sin(ang), jnp.sin(ang)], axis=-1)
    scale = (MLA_NOPE + MLA_ROPE) ** -0.5 * math.log2(math.e)
    gqr = w["mla_q_rope_norm_g"][l]
    q_head = jnp.concatenate([jnp.broadcast_to(w["mla_q_nope_norm_g"][l][None, :], (seq, MLA_NOPE)),
                              gqr[None, :] * cosf, _swap_halves(gqr)[None, :] * sinf], axis=-1) * scale
    p["qtab"] = jnp.tile(q_head, (1, MLA_HEADS))
    gkr = w["mla_k_rope_norm_g"][l]
    p["ktab"] = jnp.concatenate([jnp.zeros((seq, MLA_NOPE), F32), gkr[None, :] * cosf,
                                 _swap_halves(gkr)[None, :] * sinf], axis=-1)
    src = np.arange(LANES)
    dst = np.arange(MLA_HEADS * HEAD_SLOT)
    src_j = np.where(src >= MLA_NOPE, (src - MLA_NOPE) % MLA_ROPE, -1)
    dst_l = dst % HEAD_SLOT
    dst_j = np.where(dst_l >= MLA_NOPE, (dst_l - MLA_NOPE) % MLA_ROPE, -2)
    p["eplace"] = jnp.asarray((src_j[:, None] == dst_j[None, :]).astype(np.float32), dtype=BF16)

    pw = w["pool_w"][l]
    bd = jnp.zeros((POOL_WIDTH, POOL_WIDTH), F32)
    for g in range(POOL_GROUPS):
        s0 = g * POOL_GROUP_DIM
        bd = bd.at[s0:s0 + POOL_GROUP_DIM, s0:s0 + POOL_GROUP_DIM].set(pw[g])
    p["pool_w"] = bd.astype(BF16)
    p["pool_scale"] = row(w["pool_scale"][l])

    lam_init = 0.8 - 0.6 * math.exp(-0.3 * l)
    lv = w["diff_lambda"][l].astype(F32)
    p["lam"] = (jnp.exp(jnp.sum(lv[0] * lv[1])) - jnp.exp(jnp.sum(lv[2] * lv[3])) + lam_init).reshape(1)
    p["gsub"] = row(w["diff_sub_norm_g"][l] * (1.0 - lam_init))

    p["wo"] = w["w_out"][l].astype(BF16)
    p["g2"] = row(w["norm2_g"][l])
    wr = jnp.concatenate([w["router_group_w"][l], w["router_expert_w"][l],
                          jnp.zeros((D_MODEL, LANES - N_GROUPS - N_EXPERTS), F32)], axis=1)
    wr_hi = wr.astype(BF16)
    p["wrh"] = wr_hi
    p["wrl"] = (wr - wr_hi.astype(F32)).astype(BF16)
    p["br"] = row(jnp.concatenate([w["router_group_b"][l], w["router_expert_b"][l],
                                   jnp.zeros((LANES - N_GROUPS - N_EXPERTS,), F32)]))
    return p


def kernel(x, norm1_g, w_in, diff_q_norm_g, diff_k_norm_g, diff_lambda, diff_sub_norm_g, mla_q_lat_norm_g, mla_kv_lat_norm_g, mla_w_uq, mla_w_ukv, mla_q_nope_norm_g, mla_q_rope_norm_g, mla_k_nope_norm_g, mla_k_rope_norm_g, pool_w, pool_scale, w_out, norm2_g, router_group_w, router_group_b, router_expert_w, router_expert_b, expert_w_gate, expert_w_up, expert_w_down):
    w = dict(norm1_g=norm1_g, w_in=w_in, diff_q_norm_g=diff_q_norm_g, diff_k_norm_g=diff_k_norm_g,
             diff_lambda=diff_lambda, diff_sub_norm_g=diff_sub_norm_g, mla_q_lat_norm_g=mla_q_lat_norm_g,
             mla_kv_lat_norm_g=mla_kv_lat_norm_g, mla_w_uq=mla_w_uq, mla_w_ukv=mla_w_ukv,
             mla_q_nope_norm_g=mla_q_nope_norm_g, mla_q_rope_norm_g=mla_q_rope_norm_g,
             mla_k_nope_norm_g=mla_k_nope_norm_g, mla_k_rope_norm_g=mla_k_rope_norm_g,
             pool_w=pool_w, pool_scale=pool_scale, w_out=w_out, norm2_g=norm2_g,
             router_group_w=router_group_w, router_group_b=router_group_b,
             router_expert_w=router_expert_w, router_expert_b=router_expert_b)
    batch, seq, d = x.shape
    t = batch * seq
    n_assign = 2 * t
    n_blocks = n_assign // ROUTE_BLOCK + N_EXPERTS
    alibi = _alibi_tables(seq)

    x2 = x.reshape(t, d)
    for l in range(DEPTH):
        p = _layer_params(l, seq, w)
        dq1, dq2, dk, dv, qm, km, vm, pu = _proj_call(x2, p, seq)
        o_diff = _diff_call(p["lam"], dq1, dq2, dk, dv, p["gsub"], alibi, batch, seq)
        o_mla = _mla_call(qm, km, vm, batch, seq)
        o_pool = _pool_call(pu, p["pool_w"], p["pool_scale"], batch, seq)
        x2, h2, route_i, route_g, counts = _outproj_call(x2, o_diff, o_mla, o_pool, p)

        cnt = counts[0, N_GROUPS:N_GROUPS + N_EXPERTS].astype(jnp.int32)
        padded = (cnt + ROUTE_BLOCK - 1) // ROUTE_BLOCK * ROUTE_BLOCK
        padded_ends = jnp.cumsum(padded)
        padded_starts = padded_ends - padded
        info_t = route_i[:, 0:4].T
        start_of = jnp.sum(jnp.where(info_t[0:2, None, :] == jnp.arange(N_EXPERTS, dtype=jnp.int32)[None, :, None],
                                     padded_starts[None, :, None], 0), axis=1)
        dest_t = start_of + info_t[2:4]

        def dest_tiles(tm):
            return dest_t.reshape(2, t // tm, tm).transpose(1, 0, 2).reshape(t // tm, 1, 2 * tm)
        block_start = jnp.arange(n_blocks, dtype=jnp.int32) * ROUTE_BLOCK
        block_eid = jnp.minimum(jnp.sum(block_start[:, None] >= padded_ends[None, :], axis=1),
                                N_EXPERTS - 1).astype(jnp.int32)
        n_used = (padded_ends[-1] // ROUTE_BLOCK).astype(jnp.int32).reshape(1)
        n_tail = (n_blocks - n_used) * (ROUTE_BLOCK // PAD_CHUNKS[0])
        pads = jnp.stack([jnp.concatenate([padded_starts + cnt, padded_ends[-1:]]),
                          jnp.concatenate([padded - cnt, n_tail])]).astype(jnp.int32)
        xs = _dispatch_call(pads, dest_tiles(TM_DISP), h2, n_blocks * ROUTE_BLOCK)
        ys = _expert_call(block_eid, n_used, xs, expert_w_gate, expert_w_up, expert_w_down, l)
        x2 = _combine_call(dest_tiles(TM_COMB), ys, x2, route_g)
    return x2.reshape(batch, seq, d)
```

```python
import functools
import math

import jax
import jax.numpy as jnp
import numpy as np
from jax import lax
from jax.experimental import pallas as pl
from jax.experimental.pallas import tpu as pltpu

F32 = jnp.float32
BF16 = jnp.bfloat16

D_MODEL = 1024
DEPTH = 2
DIFF_HEADS = 4
DIFF_QK = 64
DIFF_V = 128
DIFF_WIDTH = 512
MLA_HEADS = 4
MLA_NOPE = 64
MLA_ROPE = 32
MLA_V = 64
MLA_Q_RANK = 192
MLA_KV_RANK = 128
MLA_WIDTH = 256
ROPE_BASE = 10000.0
POOL_WIDTH = 256
POOL_GROUPS = 4
POOL_GROUP_DIM = 64
POOL_WINDOWS = (2, 4, 8, 16)
N_GROUPS = 4
EXPERTS_PER_GROUP = 8
N_EXPERTS = 32
D_FF = 256
ROUTE_BLOCK = 256
RMS_EPS = 1e-6
ROUTE_INFO_ROWS = 8

LANES = 128
HEAD_SLOT = 128
PROJ_WIDTH = 2176
POOL_PAD = 16
VMEM_LIMIT = 48 * 1024 * 1024
VMEM_LIMIT_MLA = 58 * 1024 * 1024

TM_PROJ = 1024
TQ = 512
TQ_MLA = 1024
TK_DIFF = 256
TK_MLA = 512
TM_OUT = 1024
TM_DISP = 1024
TM_COMB = 256

NT_DIMS = (((1,), (1,)), ((), ()))


def _cparams(n_axes, vmem_limit=VMEM_LIMIT):
    return pltpu.CompilerParams(dimension_semantics=("arbitrary",) * n_axes,
                                vmem_limit_bytes=vmem_limit)


def _full(shape):
    return pl.BlockSpec(shape, lambda *_: (0,) * len(shape))


def _proj_kernel(x_ref, g1_ref, win_ref, gq_ref, gk_ref, gckv_ref, gcqa_ref, gcqb_ref,
                 wuqa_ref, wuqb_ref, wkk_ref, wkv_ref, gkn_ref, qtab_ref, ktab_ref, eplace_ref,
                 dq1_ref, dq2_ref, dk_ref, dv_ref, qm_ref, km_ref, vm_ref, pu_ref):
    x = x_ref[...]
    xn = x * lax.rsqrt(jnp.mean(x * x, axis=-1, keepdims=True) + RMS_EPS) * g1_ref[...]
    proj = jnp.dot(xn.astype(BF16), win_ref[...], preferred_element_type=F32)

    tm = x.shape[0]
    lane = lax.broadcasted_iota(jnp.int32, (tm, LANES), 1)
    lo = lane < DIFF_QK

    def half_norm(c, g_row):
        sq = c * c
        s_lo = jnp.sum(jnp.where(lo, sq, 0.0), axis=-1, keepdims=True)
        s_hi = jnp.sum(jnp.where(lo, 0.0, sq), axis=-1, keepdims=True)
        r = jnp.where(lo, lax.rsqrt(s_lo / DIFF_QK + RMS_EPS), lax.rsqrt(s_hi / DIFF_QK + RMS_EPS))
        return c * r * g_row

    for h in range(DIFF_HEADS):
        sl = slice(h * HEAD_SLOT, (h + 1) * HEAD_SLOT)
        qn = half_norm(proj[:, sl], gq_ref[...])
        dq1_ref[:, sl] = jnp.where(lo, qn, 0.0).astype(BF16)
        dq2_ref[:, sl] = jnp.where(lo, 0.0, qn).astype(BF16)
        ksl = slice(512 + h * HEAD_SLOT, 512 + (h + 1) * HEAD_SLOT)
        dk_ref[:, sl] = half_norm(proj[:, ksl], gk_ref[...]).astype(BF16)
    dv_ref[...] = proj[:, 1024:1536].astype(BF16)
    pu_ref[...] = proj[:, 1536:1792]

    ckv = proj[:, 1792:1920]
    ckvn = ckv * lax.rsqrt(jnp.mean(ckv * ckv, axis=-1, keepdims=True) + RMS_EPS) * gckv_ref[...]
    ckvn = ckvn.astype(BF16)
    cqa = proj[:, 1920:2048]
    last = proj[:, 2048:2176]
    lsq = last * last
    ss_q = (jnp.sum(cqa * cqa, axis=-1, keepdims=True)
            + jnp.sum(jnp.where(lo, lsq, 0.0), axis=-1, keepdims=True))
    r_q = lax.rsqrt(ss_q / MLA_Q_RANK + RMS_EPS)
    q_raw = (jnp.dot((cqa * r_q * gcqa_ref[...]).astype(BF16), wuqa_ref[...], preferred_element_type=F32)
             + jnp.dot((last * r_q * gcqb_ref[...]).astype(BF16), wuqb_ref[...], preferred_element_type=F32))

    rope_lanes = (lane >= MLA_NOPE) & (lane < MLA_NOPE + MLA_ROPE)
    ss_kr = jnp.sum(jnp.where(rope_lanes, lsq, 0.0), axis=-1, keepdims=True)
    kr_terms = last * lax.rsqrt(ss_kr / MLA_ROPE + RMS_EPS) * ktab_ref[...]
    kr_placed = jnp.dot(kr_terms.astype(BF16), eplace_ref[...], preferred_element_type=F32)

    k_raw = jnp.dot(ckvn, wkk_ref[...], preferred_element_type=F32)
    vm_ref[...] = jnp.dot(ckvn, wkv_ref[...], preferred_element_type=F32).astype(BF16)
    qtab = qtab_ref[...]
    for h in range(MLA_HEADS):
        sl = slice(h * HEAD_SLOT, (h + 1) * HEAD_SLOT)
        c = q_raw[:, sl]
        sq = c * c
        s_n = jnp.sum(jnp.where(lo, sq, 0.0), axis=-1, keepdims=True)
        s_r = jnp.sum(jnp.where(rope_lanes, sq, 0.0), axis=-1, keepdims=True)
        r = jnp.where(lo, lax.rsqrt(s_n / MLA_NOPE + RMS_EPS), lax.rsqrt(s_r / MLA_ROPE + RMS_EPS))
        qm_ref[:, sl] = (c * r * qtab[:, sl]).astype(BF16)
        kc = k_raw[:, sl]
        r_k = lax.rsqrt(jnp.sum(kc * kc, axis=-1, keepdims=True) / MLA_NOPE + RMS_EPS)
        km_ref[:, sl] = (kc * r_k * gkn_ref[...] + kr_placed[:, sl]).astype(BF16)


def _proj_call(x2, p, seq):
    t = x2.shape[0]
    tm = TM_PROJ
    n_pos = seq // tm
    row = lambda i: (i, 0)
    pos = lambda i: (i % n_pos, 0)
    bf = lambda w: jax.ShapeDtypeStruct((t, w), BF16)
    in_specs = [
        pl.BlockSpec((tm, D_MODEL), row),
        _full((1, D_MODEL)), _full((D_MODEL, PROJ_WIDTH)),
        _full((1, LANES)), _full((1, LANES)), _full((1, LANES)), _full((1, LANES)), _full((1, LANES)),
        _full((LANES, 512)), _full((LANES, 512)), _full((LANES, 512)), _full((LANES, 512)),
        _full((1, LANES)),
        pl.BlockSpec((tm, 512), pos), pl.BlockSpec((tm, LANES), pos),
        _full((LANES, 512)),
    ]
    out_specs = [pl.BlockSpec((tm, 512), row)] * 7 + [pl.BlockSpec((tm, POOL_WIDTH), row)]
    out_shape = [bf(512)] * 7 + [jax.ShapeDtypeStruct((t, POOL_WIDTH), F32)]
    return pl.pallas_call(
        _proj_kernel, grid=(t // tm,), in_specs=in_specs, out_specs=out_specs, out_shape=out_shape,
        compiler_params=_cparams(1, VMEM_LIMIT_MLA),
    )(x2, p["g1"], p["win"], p["gq"], p["gk"], p["gckv"], p["gcqa"], p["gcqb"],
      p["wuqa"], p["wuqb"], p["wkk"], p["wkv"], p["gkn"], p["qtab"], p["ktab"], p["eplace"])


def _unflatten(n, sizes):
    n = jnp.minimum(n, math.prod(sizes) - 1)
    coords = []
    for size in reversed(sizes):
        coords.append(n % size)
        n = n // size
    return tuple(reversed(coords))


def _two_stage(n, stage, bufs):
    (s0, m0), (s1, m1) = bufs

    @pl.when(n == 0)
    def _():
        s1[...] = jnp.zeros_like(s1)
        m1[...] = jnp.zeros_like(m1)

    @pl.when(n % 2 == 0)
    def _():
        stage((s0, m0), (s1, m1))

    @pl.when(n % 2 == 1)
    def _():
        stage((s1, m1), (s0, m0))


def _two_stage_scratch(seq, tq):
    pair = [pltpu.VMEM((2 * tq, seq), F32), pltpu.VMEM((2 * tq, LANES), F32)]
    return pair + pair


def _softmax_pv_tile(s_prev, m_rows, c, tq, tk, lsum, acc, v_tiles, exp_fn):
    n_half = tk // LANES
    ps = [exp_fn(s_prev[:, (c * n_half + j) * LANES:(c * n_half + j + 1) * LANES] - m_rows)
          for j in range(n_half)]
    for ch in ps:
        lsum = ch if lsum is None else lsum + ch
    pb = jnp.concatenate(ps, axis=1).astype(BF16)
    for g, vt in enumerate(v_tiles):
        pv = jnp.dot(pb[g * tq:(g + 1) * tq], vt, preferred_element_type=F32)
        acc[g] = pv if acc[g] is None else acc[g] + pv
    return lsum


def _running_max(mx, sc):
    for j in range(sc.shape[1] // LANES):
        chunk = sc[:, j * LANES:(j + 1) * LANES]
        mx = chunk if mx is None else jnp.maximum(mx, chunk)
    return mx


def _diff_kernel(lam_ref, q1_ref, q2_ref, qx_ref, k_ref, kx_ref, bd_ref, v_ref, gsub_ref, o_ref,
                 s0_ref, m0_ref, s1_ref, m1_ref, *, sizes):
    n = pl.program_id(0)
    tq = q1_ref.shape[0]
    tk = TK_DIFF
    n_kt = k_ref.shape[0] // tk
    n_diag = tq // tk
    first_cur = _unflatten(n, sizes)[2] * n_diag
    first_prev = _unflatten(jnp.maximum(n - 1, 0), sizes)[2] * n_diag

    def stage(cur, prev):
        s_cur, m_cur = cur
        s_prev, m_prev = prev
        q1, q2 = q1_ref[...], q2_ref[...]
        qx_left = qx_ref[0, 0]
        qx_right = -qx_left
        bd = bd_ref[0]
        bd2 = jnp.concatenate([bd, bd], axis=0)
        mx = None
        m_rows = m_prev[...]
        lsum = None
        acc = [None, None]
        for c in range(n_kt):
            start = pl.multiple_of(((first_prev + c) % n_kt) * tk, tk)
            vt = v_ref[pl.ds(start, tk), :]
            lsum = _softmax_pv_tile(s_prev, m_rows, c, tq, tk, lsum, acc, [vt, vt], jnp.exp)

            start = pl.multiple_of(((first_cur + c) % n_kt) * tk, tk)
            if c < n_diag:
                qx = jnp.zeros_like(qx_left)
            else:
                qx = jnp.where(first_cur + c >= n_kt, qx_left, qx_right)
            qq = jnp.concatenate([jnp.concatenate([q1, qx], axis=1),
                                  jnp.concatenate([q2, qx], axis=1)], axis=0)
            kk = jnp.concatenate([k_ref[pl.ds(start, tk), :], kx_ref[pl.ds(start, tk), :]], axis=1)
            sc = lax.dot_general(qq, kk, NT_DIMS, preferred_element_type=F32)
            if c < n_diag:
                sc = sc + bd2[:, c * tk:(c + 1) * tk]
            s_cur[:, c * tk:(c + 1) * tk] = sc
            mx = _running_max(mx, sc)
        m_cur[...] = jnp.broadcast_to(jnp.max(mx, axis=-1, keepdims=True), m_cur.shape)
        l = jnp.sum(lsum, axis=-1, keepdims=True)
        o = acc[0] * (1.0 / l[0:tq]) - acc[1] * (lam_ref[0] / l[tq:2 * tq])
        r = lax.rsqrt(jnp.mean(o * o, axis=-1, keepdims=True) + RMS_EPS)
        o_ref[...] = (o * r * gsub_ref[...]).astype(BF16)

    _two_stage(n, stage, ((s0_ref, m0_ref), (s1_ref, m1_ref)))


def _alibi_tables(seq):
    nq = seq // TQ
    slopes = (2.0 ** (-8.0 * np.arange(1, DIFF_HEADS + 1, dtype=np.float32) / DIFF_HEADS)).astype(np.float32)
    pos = np.arange(seq, dtype=np.int32)
    hi = (pos // 256).astype(np.float32)
    lo = (pos % 256).astype(np.float32)
    s4 = slopes[:, None]
    ones = np.ones((DIFF_HEADS, seq), np.float32)
    q_left = np.stack([-s4 * 256.0 * hi[None], -s4 * lo[None], s4 * 256.0 * ones, s4 * ones], axis=-1)
    qx = np.concatenate([q_left, np.zeros((DIFF_HEADS, seq, HEAD_SLOT - 4), np.float32)], axis=-1)
    qx = jnp.asarray(qx.reshape(DIFF_HEADS, nq, TQ, HEAD_SLOT), dtype=BF16)
    k_cols = np.stack([np.ones((seq,), np.float32), np.ones((seq,), np.float32), hi, lo], axis=-1)
    kx = jnp.asarray(np.concatenate([k_cols, np.zeros((seq, HEAD_SLOT - 4), np.float32)], axis=-1), dtype=BF16)
    loc = np.arange(TQ, dtype=np.int32)
    bd = -slopes[:, None, None] * np.abs(loc[:, None] - loc[None, :]).astype(np.float32)[None]
    return qx, kx, jnp.asarray(bd)


def _diff_call(lam, dq1, dq2, dk, dv, gsub, tabs, batch, seq):
    t = dq1.shape[0]
    nq = seq // TQ
    qx, kx, bd = tabs
    sizes = (batch, DIFF_HEADS, nq)
    cur = lambda n: _unflatten(n, sizes)
    prev = lambda n: _unflatten(jnp.maximum(n - 1, 0), sizes)

    def qmap(n, *_):
        b, h, qi = cur(n)
        return (b * nq + qi, h)

    def kmap(n, *_):
        b, h, qi = cur(n)
        return (b, h)

    def vmap(n, *_):
        b, h, qi = prev(n)
        return (b, h)

    def omap(n, *_):
        b, h, qi = prev(n)
        return (b * nq + qi, h)

    grid_spec = pltpu.PrefetchScalarGridSpec(
        num_scalar_prefetch=1, grid=(math.prod(sizes) + 1,),
        in_specs=[pl.BlockSpec((TQ, HEAD_SLOT), qmap), pl.BlockSpec((TQ, HEAD_SLOT), qmap),
                  pl.BlockSpec((1, 1, TQ, HEAD_SLOT), lambda n, *_: cur(n)[1:] + (0, 0)),
                  pl.BlockSpec((seq, HEAD_SLOT), kmap),
                  pl.BlockSpec((seq, HEAD_SLOT), lambda *_: (0, 0)),
                  pl.BlockSpec((1, TQ, TQ), lambda n, *_: (cur(n)[1], 0, 0)),
                  pl.BlockSpec((seq, HEAD_SLOT), vmap),
                  pl.BlockSpec((1, HEAD_SLOT), lambda *_: (0, 0))],
        out_specs=pl.BlockSpec((TQ, HEAD_SLOT), omap),
        scratch_shapes=_two_stage_scratch(seq, TQ))
    return pl.pallas_call(
        functools.partial(_diff_kernel, sizes=sizes), grid_spec=grid_spec,
        out_shape=jax.ShapeDtypeStruct((t, DIFF_WIDTH), BF16),
        compiler_params=_cparams(1),
    )(lam, dq1, dq2, qx, dk, kx, bd, dv, gsub)


def _mla_kernel(q_ref, k_ref, v_ref, o_ref, s0_ref, m0_ref, s1_ref, m1_ref):
    n = pl.program_id(0)
    tq = q_ref.shape[0]
    tk = TK_MLA
    n_kt = k_ref.shape[0] // tk

    def stage(cur, prev):
        s_cur, m_cur = cur
        s_prev, m_prev = prev
        mx = [None, None]
        m_rows = m_prev[...]
        lsum = None
        acc = [None, None]
        for c in range(n_kt):
            rows = slice(c * tk, (c + 1) * tk)
            v_tiles = [v_ref[rows, hh * HEAD_SLOT:(hh + 1) * HEAD_SLOT] for hh in range(2)]
            lsum = _softmax_pv_tile(s_prev, m_rows, c, tq, tk, lsum, acc, v_tiles, jnp.exp2)
            for hh in range(2):
                sl = slice(hh * HEAD_SLOT, (hh + 1) * HEAD_SLOT)
                sc = lax.dot_general(q_ref[:, sl], k_ref[rows, sl], NT_DIMS, preferred_element_type=F32)
                s_cur[hh * tq:(hh + 1) * tq, rows] = sc
                mx[hh] = _running_max(mx[hh], sc)
        mx = jnp.concatenate(mx, axis=0)
        m_cur[...] = jnp.broadcast_to(jnp.max(mx, axis=-1, keepdims=True), m_cur.shape)
        l = jnp.sum(lsum, axis=-1, keepdims=True)
        o = acc[0] * (1.0 / l[0:tq]) + acc[1] * (1.0 / l[tq:2 * tq])
        o_ref[...] = o.astype(BF16)

    _two_stage(n, stage, ((s0_ref, m0_ref), (s1_ref, m1_ref)))


def _mla_call(qm, km, vm, batch, seq):
    t = qm.shape[0]
    tq = TQ_MLA
    nq = seq // tq
    sizes = (batch, MLA_HEADS // 2, nq)
    cur = lambda n: _unflatten(n, sizes)
    prev = lambda n: _unflatten(jnp.maximum(n - 1, 0), sizes)

    def qmap(n):
        b, p, qi = cur(n)
        return (b * nq + qi, p)

    def kmap(n):
        b, p, qi = cur(n)
        return (b, p)

    def vmap(n):
        b, p, qi = prev(n)
        return (b, p)

    def omap(n):
        b, p, qi = prev(n)
        return (b * nq + qi, p)

    return pl.pallas_call(
        _mla_kernel, grid=(math.prod(sizes) + 1,),
        in_specs=[pl.BlockSpec((tq, 2 * HEAD_SLOT), qmap), pl.BlockSpec((seq, 2 * HEAD_SLOT), kmap),
                  pl.BlockSpec((seq, 2 * HEAD_SLOT), vmap)],
        out_specs=pl.BlockSpec((tq, HEAD_SLOT), omap),
        out_shape=jax.ShapeDtypeStruct((t, MLA_WIDTH), BF16),
        scratch_shapes=_two_stage_scratch(seq, tq),
        compiler_params=_cparams(1, VMEM_LIMIT_MLA),
    )(qm, km, vm)


def _pool_kernel(u_ref, w_ref, scale_ref, o_ref):
    u = u_ref[...]
    seq, width = u.shape
    zpad = jnp.zeros((POOL_PAD, width), F32)
    ue = jnp.concatenate([zpad, u, zpad], axis=0)
    n = seq + 2 * POOL_PAD

    def down(a, k):
        return pltpu.roll(a, k, axis=0)

    def up(a, k):
        return pltpu.roll(a, n - k, axis=0)

    a2 = ue + down(ue, 1)
    a4 = down(a2, 1) + up(a2, 1)
    a8 = down(a4, 2) + up(a4, 2)
    a16 = down(a8, 4) + up(a8, 4)
    core = slice(POOL_PAD, POOL_PAD + seq)
    lane = lax.broadcasted_iota(jnp.int32, (seq, width), 1)
    tpos = lax.broadcasted_iota(jnp.int32, (seq, width), 0)
    grp = lane // POOL_GROUP_DIM
    win_sum = jnp.where(grp == 0, a2[core], jnp.where(grp == 1, a4[core], jnp.where(grp == 2, a8[core], a16[core])))
    half = jnp.where(grp == 0, 1, jnp.where(grp == 1, 2, jnp.where(grp == 2, 4, 8)))
    lo_i = jnp.maximum(tpos - half, 0)
    hi_i = jnp.minimum(tpos + half - 1, seq - 1)
    cnt = (hi_i - lo_i + 1).astype(F32)
    pooled = win_sum / cnt - u
    mixed = jnp.dot(pooled.astype(BF16), w_ref[...], preferred_element_type=F32)
    o_ref[...] = (mixed * scale_ref[...]).astype(BF16)


def _pool_call(pu, w_bd, scale, batch, seq):
    t = pu.shape[0]
    return pl.pallas_call(
        _pool_kernel, grid=(batch,),
        in_specs=[pl.BlockSpec((seq, POOL_WIDTH), lambda b: (b, 0)),
                  _full((POOL_WIDTH, POOL_WIDTH)), _full((1, POOL_WIDTH))],
        out_specs=pl.BlockSpec((seq, POOL_WIDTH), lambda b: (b, 0)),
        out_shape=jax.ShapeDtypeStruct((t, POOL_WIDTH), BF16),
        compiler_params=_cparams(1),
    )(pu, w_bd, scale)


def _outproj_kernel(x_ref, od_ref, om_ref, op_ref, wo_ref, g2_ref, wrh_ref, wrl_ref, br_ref,
                    xo_ref, h2_ref, ri_ref, rg_ref, cnt_ref, carry_ref):
    i = pl.program_id(0)

    @pl.when(i == 0)
    def _():
        carry_ref[...] = jnp.zeros_like(carry_ref)

    xn = (x_ref[...]
          + jnp.dot(od_ref[...], wo_ref[0:DIFF_WIDTH, :], preferred_element_type=F32)
          + jnp.dot(om_ref[...], wo_ref[DIFF_WIDTH:DIFF_WIDTH + MLA_WIDTH, :], preferred_element_type=F32)
          + jnp.dot(op_ref[...], wo_ref[DIFF_WIDTH + MLA_WIDTH:, :], preferred_element_type=F32))
    xo_ref[...] = xn
    h2 = xn * lax.rsqrt(jnp.mean(xn * xn, axis=-1, keepdims=True) + RMS_EPS) * g2_ref[...]
    n_chunk = D_MODEL // LANES
    for j in range(n_chunk):
        h2_ref[pl.ds(j, xn.shape[0], stride=n_chunk), :] = h2[:, j * LANES:(j + 1) * LANES]

    h_hi = h2.astype(BF16)
    h_lo = (h2 - h_hi.astype(F32)).astype(BF16)
    logits = (jnp.dot(h_hi, wrh_ref[...], preferred_element_type=F32)
              + jnp.dot(h_lo, wrh_ref[...], preferred_element_type=F32)
              + jnp.dot(h_hi, wrl_ref[...], preferred_element_type=F32)
              + br_ref[...])
    tm = logits.shape[0]
    lane = lax.broadcasted_iota(jnp.int32, (tm, LANES), 1)
    lane_f = lane.astype(F32)
    neg = jnp.float32(-jnp.inf)
    big = jnp.float32(1e9)

    gmask = lane < N_GROUPS
    gl = jnp.where(gmask, logits, neg)
    gmax = jnp.max(gl, axis=-1, keepdims=True)
    gsum = jnp.sum(jnp.where(gmask, jnp.exp(gl - gmax), 0.0), axis=-1, keepdims=True)
    g_top = 1.0 / gsum
    g_idx = jnp.min(jnp.where(gl == gmax, lane_f, big), axis=-1, keepdims=True)

    e_lo = N_GROUPS + EXPERTS_PER_GROUP * g_idx
    emask = (lane_f >= e_lo) & (lane_f < e_lo + EXPERTS_PER_GROUP)
    el = jnp.where(emask, logits, neg)
    emax = jnp.max(el, axis=-1, keepdims=True)
    eexp = jnp.where(emask, jnp.exp(el - emax), 0.0)
    prob = eexp / jnp.sum(eexp, axis=-1, keepdims=True)
    pm = jnp.where(emask, prob, -1.0)
    p1 = jnp.max(pm, axis=-1, keepdims=True)
    i1 = jnp.min(jnp.where(pm == p1, lane_f, big), axis=-1, keepdims=True)
    pm2 = jnp.where(lane_f == i1, -1.0, pm)
    p2 = jnp.max(pm2, axis=-1, keepdims=True)
    i2 = jnp.min(jnp.where(pm2 == p2, lane_f, big), axis=-1, keepdims=True)
    denom = p1 + p2
    gate1 = g_top * p1 / denom
    gate2 = g_top * p2 / denom

    sel1 = lane_f == i1
    sel2 = lane_f == i2
    onehot = jnp.where(sel1 | sel2, 1.0, 0.0)
    rr = lax.broadcasted_iota(jnp.int32, (tm, tm), 0)
    cc = lax.broadcasted_iota(jnp.int32, (tm, tm), 1)
    ltri = jnp.where(cc < rr, 1.0, 0.0).astype(BF16)
    prefix = jnp.dot(ltri, onehot.astype(BF16), preferred_element_type=F32) + carry_ref[...]
    rank1 = jnp.sum(jnp.where(sel1, prefix, 0.0), axis=-1, keepdims=True)
    rank2 = jnp.sum(jnp.where(sel2, prefix, 0.0), axis=-1, keepdims=True)
    carry_ref[...] = carry_ref[...] + jnp.sum(onehot, axis=0, keepdims=True)
    cnt_ref[...] = carry_ref[...]

    info = jnp.where(lane == 0, i1 - N_GROUPS,
                     jnp.where(lane == 1, i2 - N_GROUPS,
                               jnp.where(lane == 2, rank1, jnp.where(lane == 3, rank2, 0.0))))
    ri_ref[...] = info.T[0:ri_ref.shape[0], :].astype(jnp.int32)
    rg_ref[...] = jnp.where(lane == 0, gate1, jnp.where(lane == 1, gate2, 0.0))


def _outproj_call(x2, od, om, op, p):
    t = x2.shape[0]
    tm = TM_OUT
    row = lambda i: (i, 0)
    return pl.pallas_call(
        _outproj_kernel, grid=(t // tm,),
        in_specs=[pl.BlockSpec((tm, D_MODEL), row), pl.BlockSpec((tm, DIFF_WIDTH), row),
                  pl.BlockSpec((tm, MLA_WIDTH), row), pl.BlockSpec((tm, POOL_WIDTH), row),
                  _full((D_MODEL, D_MODEL)), _full((1, D_MODEL)),
                  _full((D_MODEL, LANES)), _full((D_MODEL, LANES)), _full((1, LANES))],
        out_specs=[pl.BlockSpec((tm, D_MODEL), row),
                   pl.BlockSpec((tm * D_MODEL // LANES, LANES), row),
                   pl.BlockSpec((ROUTE_INFO_ROWS, tm), lambda i: (0, i)), pl.BlockSpec((tm, LANES), row),
                   _full((1, LANES))],
        out_shape=[jax.ShapeDtypeStruct((t, D_MODEL), F32),
                   jax.ShapeDtypeStruct((t * D_MODEL // LANES, LANES), F32),
                   jax.ShapeDtypeStruct((ROUTE_INFO_ROWS, t), jnp.int32), jax.ShapeDtypeStruct((t, LANES), F32),
                   jax.ShapeDtypeStruct((1, LANES), F32)],
        scratch_shapes=[pltpu.VMEM((1, LANES), F32)],
        compiler_params=_cparams(1),
    )(x2, od, om, op, p["wo"], p["g2"], p["wrh"], p["wrl"], p["br"])


PAD_CHUNKS = (128, 64, 32, 16, 8, 4, 2, 1)


def _dispatch_kernel(pad_ref, idx_ref, h_ref, xs_hbm, zero_ref, sem, zsem):
    i = pl.program_id(0)
    n_chunk = D_MODEL // LANES
    tm = h_ref.shape[0] // n_chunk

    @pl.when(i == 0)
    def _():
        zero_ref[...] = jnp.zeros_like(zero_ref)

        def pad_copies(e, wait):
            off = pad_ref[0, e]
            cnt = pad_ref[1, e]
            for size in PAD_CHUNKS:
                take = cnt & size

                @pl.when(take != 0)
                def _():
                    cp = pltpu.make_async_copy(
                        zero_ref.at[pl.ds(0, size * n_chunk), :],
                        xs_hbm.at[pl.ds(pl.multiple_of(off * n_chunk, n_chunk), size * n_chunk), :], zsem)
                    if wait:
                        cp.wait()
                    else:
                        cp.start()
                off = off + take

        def start_body(e, c):
            pad_copies(e, False)
            return c

        def wait_body(e, c):
            pad_copies(e, True)
            return c

        lax.fori_loop(0, N_EXPERTS, start_body, 0)
        lax.fori_loop(0, N_EXPERTS, wait_body, 0)

        tail = pad_ref[0, N_EXPERTS]
        zrows = zero_ref.shape[0]

        def tail_copy(c):
            return pltpu.make_async_copy(
                zero_ref, xs_hbm.at[pl.ds(pl.multiple_of(tail * n_chunk + c * zrows, zrows), zrows), :], zsem)

        def tail_start(c, carry):
            tail_copy(c).start()
            return carry

        def tail_wait(c, carry):
            tail_copy(c).wait()
            return carry

        lax.fori_loop(0, pad_ref[1, N_EXPERTS], tail_start, 0)
        lax.fori_loop(0, pad_ref[1, N_EXPERTS], tail_wait, 0)

    for r in range(tm):
        for kk in range(2):
            dst = pl.multiple_of(idx_ref[0, 0, kk * tm + r] * n_chunk, n_chunk)
            pltpu.make_async_copy(h_ref.at[pl.ds(r * n_chunk, n_chunk), :],
                                  xs_hbm.at[pl.ds(dst, n_chunk), :], sem).start(priority=kk)
    for kk in range(2):
        pltpu.make_async_copy(h_ref, xs_hbm.at[pl.ds(0, tm * n_chunk), :], sem).wait()


def _dispatch_call(pads, dest3, h3, n_slots):
    n_tiles = dest3.shape[0]
    tm = dest3.shape[2] // 2
    n_chunk = D_MODEL // LANES
    grid_spec = pltpu.PrefetchScalarGridSpec(
        num_scalar_prefetch=1, grid=(n_tiles,),
        in_specs=[pl.BlockSpec((1, 1, 2 * tm), lambda i, pads: (i, 0, 0), memory_space=pltpu.SMEM),
                  pl.BlockSpec((tm * n_chunk, LANES), lambda i, pads: (i, 0))],
        out_specs=pl.BlockSpec(memory_space=pl.ANY),
        scratch_shapes=[pltpu.VMEM((PAD_CHUNKS[0] * n_chunk, LANES), F32),
                        pltpu.SemaphoreType.DMA, pltpu.SemaphoreType.DMA])
    return pl.pallas_call(
        _dispatch_kernel, grid_spec=grid_spec,
        out_shape=jax.ShapeDtypeStruct((n_slots * n_chunk, LANES), F32),
        compiler_params=_cparams(1),
    )(pads, dest3, h3)


EXPERT_BLOCKS_PER_STEP = 2


def _expert_kernel(be_ref, nu_ref, xs_ref, *refs):
    i = pl.program_id(0)
    n_chunk = D_MODEL // LANES
    rows_per_block = ROUTE_BLOCK * n_chunk
    ys_ref = refs[-1]
    for b in range(EXPERT_BLOCKS_PER_STEP):
        wg_ref, wu_ref, wd_ref = refs[3 * b:3 * b + 3]
        base = b * rows_per_block
        block = i * EXPERT_BLOCKS_PER_STEP + b

        @pl.when(block < nu_ref[0])
        def _():
            xb = jnp.concatenate([xs_ref[pl.ds(base + j, ROUTE_BLOCK, stride=n_chunk), :]
                                  for j in range(n_chunk)], axis=1).astype(BF16)
            g = jnp.dot(xb, wg_ref[0, 0].astype(BF16), preferred_element_type=F32)
            u = jnp.dot(xb, wu_ref[0, 0].astype(BF16), preferred_element_type=F32)
            hmid = g * (1.0 / (1.0 + jnp.exp(-g))) * u
            y = jnp.dot(hmid.astype(BF16), wd_ref[0, 0].astype(BF16), preferred_element_type=F32)
            for j in range(n_chunk):
                ys_ref[pl.ds(base + j, ROUTE_BLOCK, stride=n_chunk), :] = y[:, j * LANES:(j + 1) * LANES]

        @pl.when(block >= nu_ref[0])
        def _():
            ys_ref[base:base + rows_per_block, :] = jnp.zeros((rows_per_block, LANES), F32)


def _expert_call(block_eid, n_used, xs3, wg, wu, wd, layer):
    per_step = EXPERT_BLOCKS_PER_STEP
    tile = (per_step * ROUTE_BLOCK * D_MODEL // LANES, LANES)
    n_steps = xs3.shape[0] // tile[0]
    row = lambda i, be, nu: (i, 0)
    weight_specs = []
    for b in range(per_step):
        wmap = lambda i, be, nu, b=b: (layer, be[i * per_step + b], 0, 0)
        weight_specs += [pl.BlockSpec((1, 1, D_MODEL, D_FF), wmap), pl.BlockSpec((1, 1, D_MODEL, D_FF), wmap),
                         pl.BlockSpec((1, 1, D_FF, D_MODEL), wmap)]
    grid_spec = pltpu.PrefetchScalarGridSpec(
        num_scalar_prefetch=2, grid=(n_steps,),
        in_specs=[pl.BlockSpec(tile, lambda i, be, nu: (jnp.minimum(i, (nu[0] - 1) // per_step), 0))] + weight_specs,
        out_specs=pl.BlockSpec(tile, row))
    return pl.pallas_call(
        _expert_kernel, grid_spec=grid_spec,
        out_shape=jax.ShapeDtypeStruct(xs3.shape, F32),
        compiler_params=_cparams(1),
    )(block_eid, n_used, xs3, *([wg, wu, wd] * per_step))


def _combine_kernel(idx0_ref, idxn_ref, ys_hbm, x_ref, rg_ref, o_ref, buf0, buf1, sem):
    i = pl.program_id(0)
    n = pl.num_programs(0)
    tm = x_ref.shape[0]
    n_chunk = D_MODEL // LANES

    def issue(idx_ref, buf, sem_slot, rows):
        for r in rows:
            for kk in range(2):
                src = pl.multiple_of(idx_ref[0, 0, kk * tm + r] * n_chunk, n_chunk)
                pltpu.make_async_copy(ys_hbm.at[pl.ds(src, n_chunk), :],
                                      buf.at[pl.ds((kk * tm + r) * n_chunk, n_chunk), :],
                                      sem_slot).start(priority=kk)

    def wait_tile(buf, sem_slot):
        pltpu.make_async_copy(ys_hbm.at[pl.ds(0, 2 * tm * n_chunk), :], buf, sem_slot).wait()

    @pl.when(i == 0)
    def _():
        issue(idx0_ref, buf0, sem.at[0], range(tm))

    def step(buf, sem_cur, buf_next, sem_next):
        issue(idxn_ref, buf_next, sem_next, range(tm))
        wait_tile(buf, sem_cur)
        rg = rg_ref[...]
        g0 = rg[:, 0:1]
        g1 = rg[:, 1:2]
        for j in range(n_chunk):
            cols = slice(j * LANES, (j + 1) * LANES)
            y0 = buf[pl.ds(j, tm, stride=n_chunk), :]
            y1 = buf[pl.ds(tm * n_chunk + j, tm, stride=n_chunk), :]
            o_ref[:, cols] = x_ref[:, cols] + g0 * y0 + g1 * y1

    @pl.when(i % 2 == 0)
    def _():
        step(buf0, sem.at[0], buf1, sem.at[1])

    @pl.when(i % 2 == 1)
    def _():
        step(buf1, sem.at[1], buf0, sem.at[0])

    @pl.when(i == n - 1)
    def _():
        @pl.when(i % 2 == 0)
        def _():
            wait_tile(buf1, sem.at[1])

        @pl.when(i % 2 == 1)
        def _():
            wait_tile(buf0, sem.at[0])


def _combine_call(dest3, ys, x2, rg):
    t = x2.shape[0]
    tm = TM_COMB
    n = t // tm
    row = lambda i: (i, 0)
    tile = (2 * tm * D_MODEL // LANES, LANES)
    return pl.pallas_call(
        _combine_kernel, grid=(n,),
        in_specs=[
            pl.BlockSpec((1, 1, 2 * tm), lambda i: (0, 0, 0), memory_space=pltpu.SMEM),
            pl.BlockSpec((1, 1, 2 * tm), lambda i: (jnp.minimum(i + 1, n - 1), 0, 0), memory_space=pltpu.SMEM),
            pl.BlockSpec(memory_space=pl.ANY),
            pl.BlockSpec((tm, D_MODEL), row), pl.BlockSpec((tm, LANES), row)],
        out_specs=pl.BlockSpec((tm, D_MODEL), row),
        out_shape=jax.ShapeDtypeStruct((t, D_MODEL), F32),
        scratch_shapes=[pltpu.VMEM(tile, F32), pltpu.VMEM(tile, F32), pltpu.SemaphoreType.DMA((2,))],
        compiler_params=_cparams(1),
    )(dest3, dest3, ys, x2, rg)


def _swap_halves(a):
    half = a.shape[-1] // 2
    return jnp.concatenate([a[..., half:], a[..., :half]], axis=-1)


def _layer_params(l, seq, w):
    p = {}
    row = lambda v: v.reshape(1, -1).astype(F32)
    w_in = w["w_in"][l]
    kr_cols = w_in[:, 1856:1888]
    p["win"] = jnp.concatenate(
        [w_in[:, 0:1536], w_in[:, 1888:2144], w_in[:, 1728:1856], w_in[:, 1536:1728],
         kr_cols, _swap_halves(kr_cols)], axis=1).astype(BF16)
    p["g1"] = row(w["norm1_g"][l])
    p["gq"] = row(jnp.tile(w["diff_q_norm_g"][l], 2) * (DIFF_QK ** -0.5))
    p["gk"] = row(jnp.tile(w["diff_k_norm_g"][l], 2))
    p["gckv"] = row(w["mla_kv_lat_norm_g"][l])
    gcq = w["mla_q_lat_norm_g"][l]
    p["gcqa"] = row(gcq[:LANES])
    p["gcqb"] = row(jnp.concatenate([gcq[LANES:], jnp.zeros((2 * LANES - MLA_Q_RANK,), F32)]))

    wuq = w["mla_w_uq"][l].reshape(MLA_Q_RANK, MLA_HEADS, MLA_NOPE + MLA_ROPE)
    rope_w = wuq[:, :, MLA_NOPE:]
    wuq = jnp.concatenate([wuq[:, :, :MLA_NOPE], rope_w, _swap_halves(rope_w)], axis=-1)
    wuq = wuq.reshape(MLA_Q_RANK, MLA_HEADS * HEAD_SLOT)
    wuq = jnp.concatenate([wuq, jnp.zeros((2 * LANES - MLA_Q_RANK, wuq.shape[1]), F32)], axis=0).astype(BF16)
    p["wuqa"] = wuq[:LANES]
    p["wuqb"] = wuq[LANES:]

    wukv = w["mla_w_ukv"][l].reshape(MLA_KV_RANK, MLA_HEADS, MLA_NOPE + MLA_V)
    zk = jnp.zeros((MLA_KV_RANK, MLA_HEADS, HEAD_SLOT - MLA_NOPE), F32)
    p["wkk"] = jnp.concatenate([wukv[:, :, :MLA_NOPE], zk], axis=-1).reshape(MLA_KV_RANK, -1).astype(BF16)
    vcols = wukv[:, :, MLA_NOPE:]
    zv = jnp.zeros_like(vcols)
    even = (jnp.arange(MLA_HEADS) % 2 == 0)[None, :, None]
    wkv = jnp.concatenate([jnp.where(even, vcols, zv), jnp.where(even, zv, vcols)], axis=-1)
    p["wkv"] = wkv.reshape(MLA_KV_RANK, -1).astype(BF16)
    p["gkn"] = row(jnp.concatenate([w["mla_k_nope_norm_g"][l], jnp.zeros((HEAD_SLOT - MLA_NOPE,), F32)]))

    inv = 1.0 / (ROPE_BASE ** (jnp.arange(0, MLA_ROPE, 2, dtype=F32) / MLA_ROPE))
    ang = jnp.arange(seq, dtype=F32)[:, None] * inv[None, :]
    cosf = jnp.concatenate([jnp.cos(ang), jnp.cos(ang)], axis=-1)
    sinf = jnp.concatenate([-jnp.sin(ang), jnp.sin(ang)], axis=-1)
    scale = (MLA_NOPE + MLA_ROPE) ** -0.5 * math.log2(math.e)
    gqr = w["mla_q_rope_norm_g"][l]
    q_head = jnp.concatenate([jnp.broadcast_to(w["mla_q_nope_norm_g"][l][None, :], (seq, MLA_NOPE)),
                              gqr[None, :] * cosf, _swap_halves(gqr)[None, :] * sinf], axis=-1) * scale
    p["qtab"] = jnp.tile(q_head, (1, MLA_HEADS))
    gkr = w["mla_k_rope_norm_g"][l]
    p["ktab"] = jnp.concatenate([jnp.zeros((seq, MLA_NOPE), F32), gkr[None, :] * cosf,
                                 _swap_halves(gkr)[None, :] * sinf], axis=-1)
    src = np.arange(LANES)
    dst = np.arange(MLA_HEADS * HEAD_SLOT)
    src_j = np.where(src >= MLA_NOPE, (src - MLA_NOPE) % MLA_ROPE, -1)
    dst_l = dst % HEAD_SLOT
    dst_j = np.where(dst_l >= MLA_NOPE, (dst_l - MLA_NOPE) % MLA_ROPE, -2)
    p["eplace"] = jnp.asarray((src_j[:, None] == dst_j[None, :]).astype(np.float32), dtype=BF16)

    pw = w["pool_w"][l]
    bd = jnp.zeros((POOL_WIDTH, POOL_WIDTH), F32)
    for g in range(POOL_GROUPS):
        s0 = g * POOL_GROUP_DIM
        bd = bd.at[s0:s0 + POOL_GROUP_DIM, s0:s0 + POOL_GROUP_DIM].set(pw[g])
    p["pool_w"] = bd.astype(BF16)
    p["pool_scale"] = row(w["pool_scale"][l])

    lam_init = 0.8 - 0.6 * math.exp(-0.3 * l)
    lv = w["diff_lambda"][l].astype(F32)
    p["lam"] = (jnp.exp(jnp.sum(lv[0] * lv[1])) - jnp.exp(jnp.sum(lv[2] * lv[3])) + lam_init).reshape(1)
    p["gsub"] = row(w["diff_sub_norm_g"][l] * (1.0 - lam_init))

    p["wo"] = w["w_out"][l].astype(BF16)
    p["g2"] = row(w["norm2_g"][l])
    wr = jnp.concatenate([w["router_group_w"][l], w["router_expert_w"][l],
                          jnp.zeros((D_MODEL, LANES - N_GROUPS - N_EXPERTS), F32)], axis=1)
    wr_hi = wr.astype(BF16)
    p["wrh"] = wr_hi
    p["wrl"] = (wr - wr_hi.astype(F32)).astype(BF16)
    p["br"] = row(jnp.concatenate([w["router_group_b"][l], w["router_expert_b"][l],
                                   jnp.zeros((LANES - N_GROUPS - N_EXPERTS,), F32)]))
    return p


def kernel(x, norm1_g, w_in, diff_q_norm_g, diff_k_norm_g, diff_lambda, diff_sub_norm_g, mla_q_lat_norm_g, mla_kv_lat_norm_g, mla_w_uq, mla_w_ukv, mla_q_nope_norm_g, mla_q_rope_norm_g, mla_k_nope_norm_g, mla_k_rope_norm_g, pool_w, pool_scale, w_out, norm2_g, router_group_w, router_group_b, router_expert_w, router_expert_b, expert_w_gate, expert_w_up, expert_w_down):
    w = dict(norm1_g=norm1_g, w_in=w_in, diff_q_norm_g=diff_q_norm_g, diff_k_norm_g=diff_k_norm_g,
             diff_lambda=diff_lambda, diff_sub_norm_g=diff_sub_norm_g, mla_q_lat_norm_g=mla_q_lat_norm_g,
             mla_kv_lat_norm_g=mla_kv_lat_norm_g, mla_w_uq=mla_w_uq, mla_w_ukv=mla_w_ukv,
             mla_q_nope_norm_g=mla_q_nope_norm_g, mla_q_rope_norm_g=mla_q_rope_norm_g,
             mla_k_nope_norm_g=mla_k_nope_norm_g, mla_k_rope_norm_g=mla_k_rope_norm_g,
             pool_w=pool_w, pool_scale=pool_scale, w_out=w_out, norm2_g=norm2_g,
             router_group_w=router_group_w, router_group_b=router_group_b,
             router_expert_w=router_expert_w, router_expert_b=router_expert_b)
    batch, seq, d = x.shape
    t = batch * seq
    n_assign = 2 * t
    n_blocks = n_assign // ROUTE_BLOCK + N_EXPERTS
    alibi = _alibi_tables(seq)

    x2 = x.reshape(t, d)
    for l in range(DEPTH):
        p = _layer_params(l, seq, w)
        dq1, dq2, dk, dv, qm, km, vm, pu = _proj_call(x2, p, seq)
        o_diff = _diff_call(p["lam"], dq1, dq2, dk, dv, p["gsub"], alibi, batch, seq)
        o_mla = _mla_call(qm, km, vm, batch, seq)
        o_pool = _pool_call(pu, p["pool_w"], p["pool_scale"], batch, seq)
        x2, h2, route_i, route_g, counts = _outproj_call(x2, o_diff, o_mla, o_pool, p)

        cnt = counts[0, N_GROUPS:N_GROUPS + N_EXPERTS].astype(jnp.int32)
        padded = (cnt + ROUTE_BLOCK - 1) // ROUTE_BLOCK * ROUTE_BLOCK
        padded_ends = jnp.cumsum(padded)
        padded_starts = padded_ends - padded
        info_t = route_i[0:4]
        start_of = jnp.sum(jnp.where(info_t[0:2, None, :] == jnp.arange(N_EXPERTS, dtype=jnp.int32)[None, :, None],
                                     padded_starts[None, :, None], 0), axis=1)
        dest_t = start_of + info_t[2:4]

        def dest_tiles(tm):
            return dest_t.reshape(2, t // tm, tm).transpose(1, 0, 2).reshape(t // tm, 1, 2 * tm)
        block_start = jnp.arange(n_blocks, dtype=jnp.int32) * ROUTE_BLOCK
        block_eid = jnp.minimum(jnp.sum(block_start[:, None] >= padded_ends[None, :], axis=1),
                                N_EXPERTS - 1).astype(jnp.int32)
        n_used = (padded_ends[-1] // ROUTE_BLOCK).astype(jnp.int32).reshape(1)
        n_tail = (n_blocks - n_used) * (ROUTE_BLOCK // PAD_CHUNKS[0])
        pads = jnp.stack([jnp.concatenate([padded_starts + cnt, padded_ends[-1:]]),
                          jnp.concatenate([padded - cnt, n_tail])]).astype(jnp.int32)
        xs = _dispatch_call(pads, dest_tiles(TM_DISP), h2, n_blocks * ROUTE_BLOCK)
        ys = _expert_call(block_eid, n_used, xs, expert_w_gate, expert_w_up, expert_w_down, l)
        x2 = _combine_call(dest_tiles(TM_COMB), ys, x2, route_g)
    return x2.reshape(batch, seq, d)
```

```python
import functools
import math

import jax
import jax.numpy as jnp
import numpy as np
from jax import lax
from jax.experimental import pallas as pl
from jax.experimental.pallas import tpu as pltpu

F32 = jnp.float32
BF16 = jnp.bfloat16

D_MODEL = 1024
DEPTH = 2
DIFF_HEADS = 4
DIFF_QK = 64
DIFF_V = 128
DIFF_WIDTH = 512
MLA_HEADS = 4
MLA_NOPE = 64
MLA_ROPE = 32
MLA_V = 64
MLA_Q_RANK = 192
MLA_KV_RANK = 128
MLA_WIDTH = 256
ROPE_BASE = 10000.0
POOL_WIDTH = 256
POOL_GROUPS = 4
POOL_GROUP_DIM = 64
POOL_WINDOWS = (2, 4, 8, 16)
N_GROUPS = 4
EXPERTS_PER_GROUP = 8
N_EXPERTS = 32
D_FF = 256
ROUTE_BLOCK = 256
RMS_EPS = 1e-6
ROUTE_INFO_ROWS = 8

LANES = 128
HEAD_SLOT = 128
PROJ_WIDTH = 2176
POOL_PAD = 16
VMEM_LIMIT = 48 * 1024 * 1024
VMEM_LIMIT_MLA = 58 * 1024 * 1024

TM_PROJ = 1024
TQ = 512
TQ_MLA = 1024
TK_DIFF = 256
TK_MLA = 512
TM_OUT = 1024
TM_DISP = 1024
TM_COMB = 256

NT_DIMS = (((1,), (1,)), ((), ()))


def _cparams(n_axes, vmem_limit=VMEM_LIMIT):
    return pltpu.CompilerParams(dimension_semantics=("arbitrary",) * n_axes,
                                vmem_limit_bytes=vmem_limit)


def _full(shape):
    return pl.BlockSpec(shape, lambda *_: (0,) * len(shape))


def _proj_kernel(x_ref, g1_ref, win_ref, gq_ref, gk_ref, gckv_ref, gcqa_ref, gcqb_ref,
                 wuqa_ref, wuqb_ref, wkk_ref, wkv_ref, gkn_ref, qtab_ref, ktab_ref, eplace_ref,
                 dq1_ref, dq2_ref, dk_ref, dv_ref, qm_ref, km_ref, vm_ref, pu_ref):
    x = x_ref[...]
    xn = x * lax.rsqrt(jnp.mean(x * x, axis=-1, keepdims=True) + RMS_EPS) * g1_ref[...]
    proj = jnp.dot(xn.astype(BF16), win_ref[...], preferred_element_type=F32)

    tm = x.shape[0]
    lane = lax.broadcasted_iota(jnp.int32, (tm, LANES), 1)
    lo = lane < DIFF_QK

    def half_norm(c, g_row):
        sq = c * c
        s_lo = jnp.sum(jnp.where(lo, sq, 0.0), axis=-1, keepdims=True)
        s_hi = jnp.sum(jnp.where(lo, 0.0, sq), axis=-1, keepdims=True)
        r = jnp.where(lo, lax.rsqrt(s_lo / DIFF_QK + RMS_EPS), lax.rsqrt(s_hi / DIFF_QK + RMS_EPS))
        return c * r * g_row

    for h in range(DIFF_HEADS):
        sl = slice(h * HEAD_SLOT, (h + 1) * HEAD_SLOT)
        qn = half_norm(proj[:, sl], gq_ref[...])
        dq1_ref[:, sl] = jnp.where(lo, qn, 0.0).astype(BF16)
        dq2_ref[:, sl] = jnp.where(lo, 0.0, qn).astype(BF16)
        ksl = slice(512 + h * HEAD_SLOT, 512 + (h + 1) * HEAD_SLOT)
        dk_ref[:, sl] = half_norm(proj[:, ksl], gk_ref[...]).astype(BF16)
    dv_ref[...] = proj[:, 1024:1536].astype(BF16)
    pu_ref[...] = proj[:, 1536:1792]

    ckv = proj[:, 1792:1920]
    ckvn = ckv * lax.rsqrt(jnp.mean(ckv * ckv, axis=-1, keepdims=True) + RMS_EPS) * gckv_ref[...]
    ckvn = ckvn.astype(BF16)
    cqa = proj[:, 1920:2048]
    last = proj[:, 2048:2176]
    lsq = last * last
    ss_q = (jnp.sum(cqa * cqa, axis=-1, keepdims=True)
            + jnp.sum(jnp.where(lo, lsq, 0.0), axis=-1, keepdims=True))
    r_q = lax.rsqrt(ss_q / MLA_Q_RANK + RMS_EPS)
    q_raw = (jnp.dot((cqa * r_q * gcqa_ref[...]).astype(BF16), wuqa_ref[...], preferred_element_type=F32)
             + jnp.dot((last * r_q * gcqb_ref[...]).astype(BF16), wuqb_ref[...], preferred_element_type=F32))

    rope_lanes = (lane >= MLA_NOPE) & (lane < MLA_NOPE + MLA_ROPE)
    ss_kr = jnp.sum(jnp.where(rope_lanes, lsq, 0.0), axis=-1, keepdims=True)
    kr_terms = last * lax.rsqrt(ss_kr / MLA_ROPE + RMS_EPS) * ktab_ref[...]
    kr_placed = jnp.dot(kr_terms.astype(BF16), eplace_ref[...], preferred_element_type=F32)

    k_raw = jnp.dot(ckvn, wkk_ref[...], preferred_element_type=F32)
    vm_ref[...] = jnp.dot(ckvn, wkv_ref[...], preferred_element_type=F32).astype(BF16)
    qtab = qtab_ref[...]
    for h in range(MLA_HEADS):
        sl = slice(h * HEAD_SLOT, (h + 1) * HEAD_SLOT)
        c = q_raw[:, sl]
        sq = c * c
        s_n = jnp.sum(jnp.where(lo, sq, 0.0), axis=-1, keepdims=True)
        s_r = jnp.sum(jnp.where(rope_lanes, sq, 0.0), axis=-1, keepdims=True)
        r = jnp.where(lo, lax.rsqrt(s_n / MLA_NOPE + RMS_EPS), lax.rsqrt(s_r / MLA_ROPE + RMS_EPS))
        qm_ref[:, sl] = (c * r * qtab[:, sl]).astype(BF16)
        kc = k_raw[:, sl]
        r_k = lax.rsqrt(jnp.sum(kc * kc, axis=-1, keepdims=True) / MLA_NOPE + RMS_EPS)
        km_ref[:, sl] = (kc * r_k * gkn_ref[...] + kr_placed[:, sl]).astype(BF16)


def _proj_call(x2, p, seq):
    t = x2.shape[0]
    tm = TM_PROJ
    n_pos = seq // tm
    row = lambda i: (i, 0)
    pos = lambda i: (i % n_pos, 0)
    bf = lambda w: jax.ShapeDtypeStruct((t, w), BF16)
    in_specs = [
        pl.BlockSpec((tm, D_MODEL), row),
        _full((1, D_MODEL)), _full((D_MODEL, PROJ_WIDTH)),
        _full((1, LANES)), _full((1, LANES)), _full((1, LANES)), _full((1, LANES)), _full((1, LANES)),
        _full((LANES, 512)), _full((LANES, 512)), _full((LANES, 512)), _full((LANES, 512)),
        _full((1, LANES)),
        pl.BlockSpec((tm, 512), pos), pl.BlockSpec((tm, LANES), pos),
        _full((LANES, 512)),
    ]
    out_specs = [pl.BlockSpec((tm, 512), row)] * 7 + [pl.BlockSpec((tm, POOL_WIDTH), row)]
    out_shape = [bf(512)] * 7 + [jax.ShapeDtypeStruct((t, POOL_WIDTH), F32)]
    return pl.pallas_call(
        _proj_kernel, grid=(t // tm,), in_specs=in_specs, out_specs=out_specs, out_shape=out_shape,
        compiler_params=_cparams(1, VMEM_LIMIT_MLA),
    )(x2, p["g1"], p["win"], p["gq"], p["gk"], p["gckv"], p["gcqa"], p["gcqb"],
      p["wuqa"], p["wuqb"], p["wkk"], p["wkv"], p["gkn"], p["qtab"], p["ktab"], p["eplace"])


def _unflatten(n, sizes):
    n = jnp.minimum(n, math.prod(sizes) - 1)
    coords = []
    for size in reversed(sizes):
        coords.append(n % size)
        n = n // size
    return tuple(reversed(coords))


def _two_stage(n, stage, bufs):
    (s0, m0), (s1, m1) = bufs

    @pl.when(n == 0)
    def _():
        s1[...] = jnp.zeros_like(s1)
        m1[...] = jnp.zeros_like(m1)

    @pl.when(n % 2 == 0)
    def _():
        stage((s0, m0), (s1, m1))

    @pl.when(n % 2 == 1)
    def _():
        stage((s1, m1), (s0, m0))


def _two_stage_scratch(seq, tq):
    pair = [pltpu.VMEM((2 * tq, seq), F32), pltpu.VMEM((2 * tq, LANES), F32)]
    return pair + pair


def _softmax_pv_tile(s_prev, m_rows, c, tq, tk, lsum, acc, v_tiles, exp_fn):
    n_half = tk // LANES
    ps = [exp_fn(s_prev[:, (c * n_half + j) * LANES:(c * n_half + j + 1) * LANES] - m_rows)
          for j in range(n_half)]
    for ch in ps:
        lsum = ch if lsum is None else lsum + ch
    pb = jnp.concatenate(ps, axis=1).astype(BF16)
    for g, vt in enumerate(v_tiles):
        pv = jnp.dot(pb[g * tq:(g + 1) * tq], vt, preferred_element_type=F32)
        acc[g] = pv if acc[g] is None else acc[g] + pv
    return lsum


def _running_max(mx, sc):
    for j in range(sc.shape[1] // LANES):
        chunk = sc[:, j * LANES:(j + 1) * LANES]
        mx = chunk if mx is None else jnp.maximum(mx, chunk)
    return mx


def _diff_kernel(lam_ref, q1_ref, q2_ref, qx_ref, k_ref, kx_ref, bd_ref, v_ref, gsub_ref, o_ref,
                 s0_ref, m0_ref, s1_ref, m1_ref, *, sizes):
    n = pl.program_id(0)
    tq = q1_ref.shape[0]
    tk = TK_DIFF
    n_kt = k_ref.shape[0] // tk
    n_diag = tq // tk
    first_cur = _unflatten(n, sizes)[2] * n_diag
    first_prev = _unflatten(jnp.maximum(n - 1, 0), sizes)[2] * n_diag

    def stage(cur, prev):
        s_cur, m_cur = cur
        s_prev, m_prev = prev
        q1, q2 = q1_ref[...], q2_ref[...]
        qx_left = qx_ref[0, 0]
        qx_right = -qx_left
        bd = bd_ref[0]
        bd2 = jnp.concatenate([bd, bd], axis=0)
        mx = None
        m_rows = m_prev[...]
        lsum = None
        acc = [None, None]
        for c in range(n_kt):
            start = pl.multiple_of(((first_prev + c) % n_kt) * tk, tk)
            vt = v_ref[pl.ds(start, tk), :]
            lsum = _softmax_pv_tile(s_prev, m_rows, c, tq, tk, lsum, acc, [vt, vt], jnp.exp)

            start = pl.multiple_of(((first_cur + c) % n_kt) * tk, tk)
            if c < n_diag:
                qx = jnp.zeros_like(qx_left)
            else:
                qx = jnp.where(first_cur + c >= n_kt, qx_left, qx_right)
            qq = jnp.concatenate([jnp.concatenate([q1, qx], axis=1),
                                  jnp.concatenate([q2, qx], axis=1)], axis=0)
            kk = jnp.concatenate([k_ref[pl.ds(start, tk), :], kx_ref[pl.ds(start, tk), :]], axis=1)
            sc = lax.dot_general(qq, kk, NT_DIMS, preferred_element_type=F32)
            if c < n_diag:
                sc = sc + bd2[:, c * tk:(c + 1) * tk]
            s_cur[:, c * tk:(c + 1) * tk] = sc
            mx = _running_max(mx, sc)
        m_cur[...] = jnp.broadcast_to(jnp.max(mx, axis=-1, keepdims=True), m_cur.shape)
        l = jnp.sum(lsum, axis=-1, keepdims=True)
        o = acc[0] * (1.0 / l[0:tq]) - acc[1] * (lam_ref[0] / l[tq:2 * tq])
        r = lax.rsqrt(jnp.mean(o * o, axis=-1, keepdims=True) + RMS_EPS)
        o_ref[...] = (o * r * gsub_ref[...]).astype(BF16)

    _two_stage(n, stage, ((s0_ref, m0_ref), (s1_ref, m1_ref)))


def _alibi_tables(seq):
    nq = seq // TQ
    slopes = (2.0 ** (-8.0 * np.arange(1, DIFF_HEADS + 1, dtype=np.float32) / DIFF_HEADS)).astype(np.float32)
    pos = np.arange(seq, dtype=np.int32)
    hi = (pos // 256).astype(np.float32)
    lo = (pos % 256).astype(np.float32)
    s4 = slopes[:, None]
    ones = np.ones((DIFF_HEADS, seq), np.float32)
    q_left = np.stack([-s4 * 256.0 * hi[None], -s4 * lo[None], s4 * 256.0 * ones, s4 * ones], axis=-1)
    qx = np.concatenate([q_left, np.zeros((DIFF_HEADS, seq, HEAD_SLOT - 4), np.float32)], axis=-1)
    qx = jnp.asarray(qx.reshape(DIFF_HEADS, nq, TQ, HEAD_SLOT), dtype=BF16)
    k_cols = np.stack([np.ones((seq,), np.float32), np.ones((seq,), np.float32), hi, lo], axis=-1)
    kx = jnp.asarray(np.concatenate([k_cols, np.zeros((seq, HEAD_SLOT - 4), np.float32)], axis=-1), dtype=BF16)
    loc = np.arange(TQ, dtype=np.int32)
    bd = -slopes[:, None, None] * np.abs(loc[:, None] - loc[None, :]).astype(np.float32)[None]
    return qx, kx, jnp.asarray(bd)


def _diff_call(lam, dq1, dq2, dk, dv, gsub, tabs, batch, seq):
    t = dq1.shape[0]
    nq = seq // TQ
    qx, kx, bd = tabs
    sizes = (batch, DIFF_HEADS, nq)
    cur = lambda n: _unflatten(n, sizes)
    prev = lambda n: _unflatten(jnp.maximum(n - 1, 0), sizes)

    def qmap(n, *_):
        b, h, qi = cur(n)
        return (b * nq + qi, h)

    def kmap(n, *_):
        b, h, qi = cur(n)
        return (b, h)

    def vmap(n, *_):
        b, h, qi = prev(n)
        return (b, h)

    def omap(n, *_):
        b, h, qi = prev(n)
        return (b * nq + qi, h)

    grid_spec = pltpu.PrefetchScalarGridSpec(
        num_scalar_prefetch=1, grid=(math.prod(sizes) + 1,),
        in_specs=[pl.BlockSpec((TQ, HEAD_SLOT), qmap), pl.BlockSpec((TQ, HEAD_SLOT), qmap),
                  pl.BlockSpec((1, 1, TQ, HEAD_SLOT), lambda n, *_: cur(n)[1:] + (0, 0)),
                  pl.BlockSpec((seq, HEAD_SLOT), kmap),
                  pl.BlockSpec((seq, HEAD_SLOT), lambda *_: (0, 0)),
                  pl.BlockSpec((1, TQ, TQ), lambda n, *_: (cur(n)[1], 0, 0)),
                  pl.BlockSpec((seq, HEAD_SLOT), vmap),
                  pl.BlockSpec((1, HEAD_SLOT), lambda *_: (0, 0))],
        out_specs=pl.BlockSpec((TQ, HEAD_SLOT), omap),
        scratch_shapes=_two_stage_scratch(seq, TQ))
    return pl.pallas_call(
        functools.partial(_diff_kernel, sizes=sizes), grid_spec=grid_spec,
        out_shape=jax.ShapeDtypeStruct((t, DIFF_WIDTH), BF16),
        compiler_params=_cparams(1),
    )(lam, dq1, dq2, qx, dk, kx, bd, dv, gsub)


def _mla_kernel(q_ref, k_ref, v_ref, o_ref, s0_ref, m0_ref, s1_ref, m1_ref):
    n = pl.program_id(0)
    tq = q_ref.shape[0]
    tk = TK_MLA
    n_kt = k_ref.shape[0] // tk

    def stage(cur, prev):
        s_cur, m_cur = cur
        s_prev, m_prev = prev
        mx = [None, None]
        m_rows = m_prev[...]
        lsum = None
        acc = [None, None]
        for c in range(n_kt):
            rows = slice(c * tk, (c + 1) * tk)
            v_tiles = [v_ref[rows, hh * HEAD_SLOT:(hh + 1) * HEAD_SLOT] for hh in range(2)]
            lsum = _softmax_pv_tile(s_prev, m_rows, c, tq, tk, lsum, acc, v_tiles, jnp.exp2)
            for hh in range(2):
                sl = slice(hh * HEAD_SLOT, (hh + 1) * HEAD_SLOT)
                sc = lax.dot_general(q_ref[:, sl], k_ref[rows, sl], NT_DIMS, preferred_element_type=F32)
                s_cur[hh * tq:(hh + 1) * tq, rows] = sc
                mx[hh] = _running_max(mx[hh], sc)
        mx = jnp.concatenate(mx, axis=0)
        m_cur[...] = jnp.broadcast_to(jnp.max(mx, axis=-1, keepdims=True), m_cur.shape)
        l = jnp.sum(lsum, axis=-1, keepdims=True)
        o = acc[0] * (1.0 / l[0:tq]) + acc[1] * (1.0 / l[tq:2 * tq])
        o_ref[...] = o.astype(BF16)

    _two_stage(n, stage, ((s0_ref, m0_ref), (s1_ref, m1_ref)))


def _mla_call(qm, km, vm, batch, seq):
    t = qm.shape[0]
    tq = TQ_MLA
    nq = seq // tq
    sizes = (batch, MLA_HEADS // 2, nq)
    cur = lambda n: _unflatten(n, sizes)
    prev = lambda n: _unflatten(jnp.maximum(n - 1, 0), sizes)

    def qmap(n):
        b, p, qi = cur(n)
        return (b * nq + qi, p)

    def kmap(n):
        b, p, qi = cur(n)
        return (b, p)

    def vmap(n):
        b, p, qi = prev(n)
        return (b, p)

    def omap(n):
        b, p, qi = prev(n)
        return (b * nq + qi, p)

    return pl.pallas_call(
        _mla_kernel, grid=(math.prod(sizes) + 1,),
        in_specs=[pl.BlockSpec((tq, 2 * HEAD_SLOT), qmap), pl.BlockSpec((seq, 2 * HEAD_SLOT), kmap),
                  pl.BlockSpec((seq, 2 * HEAD_SLOT), vmap)],
        out_specs=pl.BlockSpec((tq, HEAD_SLOT), omap),
        out_shape=jax.ShapeDtypeStruct((t, MLA_WIDTH), BF16),
        scratch_shapes=_two_stage_scratch(seq, tq),
        compiler_params=_cparams(1, VMEM_LIMIT_MLA),
    )(qm, km, vm)


def _pool_kernel(u_ref, w_ref, scale_ref, o_ref):
    u = u_ref[...]
    seq, width = u.shape
    zpad = jnp.zeros((POOL_PAD, width), F32)
    ue = jnp.concatenate([zpad, u, zpad], axis=0)
    n = seq + 2 * POOL_PAD

    def down(a, k):
        return pltpu.roll(a, k, axis=0)

    def up(a, k):
        return pltpu.roll(a, n - k, axis=0)

    a2 = ue + down(ue, 1)
    a4 = down(a2, 1) + up(a2, 1)
    a8 = down(a4, 2) + up(a4, 2)
    a16 = down(a8, 4) + up(a8, 4)
    core = slice(POOL_PAD, POOL_PAD + seq)
    lane = lax.broadcasted_iota(jnp.int32, (seq, width), 1)
    tpos = lax.broadcasted_iota(jnp.int32, (seq, width), 0)
    grp = lane // POOL_GROUP_DIM
    win_sum = jnp.where(grp == 0, a2[core], jnp.where(grp == 1, a4[core], jnp.where(grp == 2, a8[core], a16[core])))
    half = jnp.where(grp == 0, 1, jnp.where(grp == 1, 2, jnp.where(grp == 2, 4, 8)))
    lo_i = jnp.maximum(tpos - half, 0)
    hi_i = jnp.minimum(tpos + half - 1, seq - 1)
    cnt = (hi_i - lo_i + 1).astype(F32)
    pooled = win_sum / cnt - u
    mixed = jnp.dot(pooled.astype(BF16), w_ref[...], preferred_element_type=F32)
    o_ref[...] = (mixed * scale_ref[...]).astype(BF16)


def _pool_call(pu, w_bd, scale, batch, seq):
    t = pu.shape[0]
    return pl.pallas_call(
        _pool_kernel, grid=(batch,),
        in_specs=[pl.BlockSpec((seq, POOL_WIDTH), lambda b: (b, 0)),
                  _full((POOL_WIDTH, POOL_WIDTH)), _full((1, POOL_WIDTH))],
        out_specs=pl.BlockSpec((seq, POOL_WIDTH), lambda b: (b, 0)),
        out_shape=jax.ShapeDtypeStruct((t, POOL_WIDTH), BF16),
        compiler_params=_cparams(1),
    )(pu, w_bd, scale)


def _outproj_kernel(x_ref, od_ref, om_ref, op_ref, wo_ref, g2_ref, wrh_ref, wrl_ref, br_ref,
                    xo_ref, h2_ref, ri_ref, rg_ref, cnt_ref, carry_ref):
    i = pl.program_id(0)

    @pl.when(i == 0)
    def _():
        carry_ref[...] = jnp.zeros_like(carry_ref)

    xn = (x_ref[...]
          + jnp.dot(od_ref[...], wo_ref[0:DIFF_WIDTH, :], preferred_element_type=F32)
          + jnp.dot(om_ref[...], wo_ref[DIFF_WIDTH:DIFF_WIDTH + MLA_WIDTH, :], preferred_element_type=F32)
          + jnp.dot(op_ref[...], wo_ref[DIFF_WIDTH + MLA_WIDTH:, :], preferred_element_type=F32))
    xo_ref[...] = xn
    h2 = xn * lax.rsqrt(jnp.mean(xn * xn, axis=-1, keepdims=True) + RMS_EPS) * g2_ref[...]
    n_chunk = D_MODEL // LANES
    for j in range(n_chunk):
        h2_ref[pl.ds(j, xn.shape[0], stride=n_chunk), :] = h2[:, j * LANES:(j + 1) * LANES]

    h_hi = h2.astype(BF16)
    h_lo = (h2 - h_hi.astype(F32)).astype(BF16)
    logits = (jnp.dot(h_hi, wrh_ref[...], preferred_element_type=F32)
              + jnp.dot(h_lo, wrh_ref[...], preferred_element_type=F32)
              + jnp.dot(h_hi, wrl_ref[...], preferred_element_type=F32)
              + br_ref[...])
    tm = logits.shape[0]
    lane = lax.broadcasted_iota(jnp.int32, (tm, LANES), 1)
    lane_f = lane.astype(F32)
    neg = jnp.float32(-jnp.inf)
    big = jnp.float32(1e9)

    gmask = lane < N_GROUPS
    gl = jnp.where(gmask, logits, neg)
    gmax = jnp.max(gl, axis=-1, keepdims=True)
    gsum = jnp.sum(jnp.where(gmask, jnp.exp(gl - gmax), 0.0), axis=-1, keepdims=True)
    g_top = 1.0 / gsum
    g_idx = jnp.min(jnp.where(gl == gmax, lane_f, big), axis=-1, keepdims=True)

    e_lo = N_GROUPS + EXPERTS_PER_GROUP * g_idx
    emask = (lane_f >= e_lo) & (lane_f < e_lo + EXPERTS_PER_GROUP)
    el = jnp.where(emask, logits, neg)
    emax = jnp.max(el, axis=-1, keepdims=True)
    eexp = jnp.where(emask, jnp.exp(el - emax), 0.0)
    prob = eexp / jnp.sum(eexp, axis=-1, keepdims=True)
    pm = jnp.where(emask, prob, -1.0)
    p1 = jnp.max(pm, axis=-1, keepdims=True)
    i1 = jnp.min(jnp.where(pm == p1, lane_f, big), axis=-1, keepdims=True)
    pm2 = jnp.where(lane_f == i1, -1.0, pm)
    p2 = jnp.max(pm2, axis=-1, keepdims=True)
    i2 = jnp.min(jnp.where(pm2 == p2, lane_f, big), axis=-1, keepdims=True)
    denom = p1 + p2
    gate1 = g_top * p1 / denom
    gate2 = g_top * p2 / denom

    sel1 = lane_f == i1
    sel2 = lane_f == i2
    onehot = jnp.where(sel1 | sel2, 1.0, 0.0)
    rr = lax.broadcasted_iota(jnp.int32, (tm, tm), 0)
    cc = lax.broadcasted_iota(jnp.int32, (tm, tm), 1)
    ltri = jnp.where(cc < rr, 1.0, 0.0).astype(BF16)
    prefix = jnp.dot(ltri, onehot.astype(BF16), preferred_element_type=F32) + carry_ref[...]
    rank1 = jnp.sum(jnp.where(sel1, prefix, 0.0), axis=-1, keepdims=True)
    rank2 = jnp.sum(jnp.where(sel2, prefix, 0.0), axis=-1, keepdims=True)
    carry_ref[...] = carry_ref[...] + jnp.sum(onehot, axis=0, keepdims=True)
    cnt_ref[...] = carry_ref[...]

    info = jnp.where(lane == 0, i1 - N_GROUPS,
                     jnp.where(lane == 1, i2 - N_GROUPS,
                               jnp.where(lane == 2, rank1, jnp.where(lane == 3, rank2, 0.0))))
    ri_ref[...] = info.T[0:ri_ref.shape[0], :].astype(jnp.int32)
    rg_ref[...] = jnp.where(lane == 0, gate1, jnp.where(lane == 1, gate2, 0.0))


def _outproj_call(x2, od, om, op, p):
    t = x2.shape[0]
    tm = TM_OUT
    row = lambda i: (i, 0)
    return pl.pallas_call(
        _outproj_kernel, grid=(t // tm,),
        in_specs=[pl.BlockSpec((tm, D_MODEL), row), pl.BlockSpec((tm, DIFF_WIDTH), row),
                  pl.BlockSpec((tm, MLA_WIDTH), row), pl.BlockSpec((tm, POOL_WIDTH), row),
                  _full((D_MODEL, D_MODEL)), _full((1, D_MODEL)),
                  _full((D_MODEL, LANES)), _full((D_MODEL, LANES)), _full((1, LANES))],
        out_specs=[pl.BlockSpec((tm, D_MODEL), row),
                   pl.BlockSpec((tm * D_MODEL // LANES, LANES), row),
                   pl.BlockSpec((ROUTE_INFO_ROWS, tm), lambda i: (0, i)), pl.BlockSpec((tm, LANES), row),
                   _full((1, LANES))],
        out_shape=[jax.ShapeDtypeStruct((t, D_MODEL), F32),
                   jax.ShapeDtypeStruct((t * D_MODEL // LANES, LANES), F32),
                   jax.ShapeDtypeStruct((ROUTE_INFO_ROWS, t), jnp.int32), jax.ShapeDtypeStruct((t, LANES), F32),
                   jax.ShapeDtypeStruct((1, LANES), F32)],
        scratch_shapes=[pltpu.VMEM((1, LANES), F32)],
        compiler_params=_cparams(1),
    )(x2, od, om, op, p["wo"], p["g2"], p["wrh"], p["wrl"], p["br"])


PAD_CHUNKS = (128, 64, 32, 16, 8, 4, 2, 1)


def _dispatch_kernel(pad_ref, idx_ref, h_ref, xs_hbm, zero_ref, sem, zsem):
    i = pl.program_id(0)
    n_chunk = D_MODEL // LANES
    tm = h_ref.shape[0] // n_chunk

    @pl.when(i == 0)
    def _():
        zero_ref[...] = jnp.zeros_like(zero_ref)

        def pad_copies(e, wait):
            off = pad_ref[0, e]
            cnt = pad_ref[1, e]
            for size in PAD_CHUNKS:
                take = cnt & size

                @pl.when(take != 0)
                def _():
                    cp = pltpu.make_async_copy(
                        zero_ref.at[pl.ds(0, size * n_chunk), :],
                        xs_hbm.at[pl.ds(pl.multiple_of(off * n_chunk, n_chunk), size * n_chunk), :], zsem)
                    if wait:
                        cp.wait()
                    else:
                        cp.start()
                off = off + take

        def start_body(e, c):
            pad_copies(e, False)
            return c

        def wait_body(e, c):
            pad_copies(e, True)
            return c

        lax.fori_loop(0, N_EXPERTS, start_body, 0)
        lax.fori_loop(0, N_EXPERTS, wait_body, 0)

        tail = pad_ref[0, N_EXPERTS]
        zrows = zero_ref.shape[0]

        def tail_copy(c):
            return pltpu.make_async_copy(
                zero_ref, xs_hbm.at[pl.ds(pl.multiple_of(tail * n_chunk + c * zrows, zrows), zrows), :], zsem)

        def tail_start(c, carry):
            tail_copy(c).start()
            return carry

        def tail_wait(c, carry):
            tail_copy(c).wait()
            return carry

        lax.fori_loop(0, pad_ref[1, N_EXPERTS], tail_start, 0)
        lax.fori_loop(0, pad_ref[1, N_EXPERTS], tail_wait, 0)

    for r in range(tm):
        for kk in range(2):
            dst = pl.multiple_of(idx_ref[0, 0, kk * tm + r] * n_chunk, n_chunk)
            pltpu.make_async_copy(h_ref.at[pl.ds(r * n_chunk, n_chunk), :],
                                  xs_hbm.at[pl.ds(dst, n_chunk), :], sem).start(priority=kk)
    for kk in range(2):
        pltpu.make_async_copy(h_ref, xs_hbm.at[pl.ds(0, tm * n_chunk), :], sem).wait()


def _dispatch_call(pads, dest3, h3, n_slots):
    n_tiles = dest3.shape[0]
    tm = dest3.shape[2] // 2
    n_chunk = D_MODEL // LANES
    grid_spec = pltpu.PrefetchScalarGridSpec(
        num_scalar_prefetch=1, grid=(n_tiles,),
        in_specs=[pl.BlockSpec((1, 1, 2 * tm), lambda i, pads: (i, 0, 0), memory_space=pltpu.SMEM),
                  pl.BlockSpec((tm * n_chunk, LANES), lambda i, pads: (i, 0))],
        out_specs=pl.BlockSpec(memory_space=pl.ANY),
        scratch_shapes=[pltpu.VMEM((PAD_CHUNKS[0] * n_chunk, LANES), F32),
                        pltpu.SemaphoreType.DMA, pltpu.SemaphoreType.DMA])
    return pl.pallas_call(
        _dispatch_kernel, grid_spec=grid_spec,
        out_shape=jax.ShapeDtypeStruct((n_slots * n_chunk, LANES), F32),
        compiler_params=_cparams(1),
    )(pads, dest3, h3)


EXPERT_BLOCKS_PER_STEP = 2


def _expert_kernel(be_ref, nu_ref, hold_ref, xs_ref, wga_ref, wua_ref, wda_ref, wgb_ref, wub_ref, wdb_ref, ys_ref):
    del hold_ref
    i = pl.program_id(0)
    n_chunk = D_MODEL // LANES
    rows_per_block = ROUTE_BLOCK * n_chunk
    first = i * EXPERT_BLOCKS_PER_STEP
    same_expert = be_ref[first + 1] == be_ref[first]

    def ffn(base, wg_ref, wu_ref, wd_ref):
        xb = jnp.concatenate([xs_ref[pl.ds(base + j, ROUTE_BLOCK, stride=n_chunk), :]
                              for j in range(n_chunk)], axis=1).astype(BF16)
        g = jnp.dot(xb, wg_ref[0, 0].astype(BF16), preferred_element_type=F32)
        u = jnp.dot(xb, wu_ref[0, 0].astype(BF16), preferred_element_type=F32)
        hmid = g * (1.0 / (1.0 + jnp.exp(-g))) * u
        y = jnp.dot(hmid.astype(BF16), wd_ref[0, 0].astype(BF16), preferred_element_type=F32)
        for j in range(n_chunk):
            ys_ref[pl.ds(base + j, ROUTE_BLOCK, stride=n_chunk), :] = y[:, j * LANES:(j + 1) * LANES]

    def zero(base):
        ys_ref[base:base + rows_per_block, :] = jnp.zeros((rows_per_block, LANES), F32)

    @pl.when(first < nu_ref[0])
    def _():
        ffn(0, wga_ref, wua_ref, wda_ref)

    @pl.when(first >= nu_ref[0])
    def _():
        zero(0)

    second_used = first + 1 < nu_ref[0]

    @pl.when(second_used & same_expert)
    def _():
        ffn(rows_per_block, wga_ref, wua_ref, wda_ref)

    @pl.when(second_used & jnp.logical_not(same_expert))
    def _():
        ffn(rows_per_block, wgb_ref, wub_ref, wdb_ref)

    @pl.when(jnp.logical_not(second_used))
    def _():
        zero(rows_per_block)


def _expert_call(block_eid, n_used, xs3, wg, wu, wd, layer):
    per_step = EXPERT_BLOCKS_PER_STEP
    assert per_step == 2
    tile = (per_step * ROUTE_BLOCK * D_MODEL // LANES, LANES)
    n_steps = xs3.shape[0] // tile[0]
    pairs = block_eid.reshape(n_steps, per_step)
    hold = jnp.maximum(lax.cummax(jnp.where(pairs[:, 1] != pairs[:, 0], pairs[:, 1], -1)), 0).astype(jnp.int32)
    row = lambda i, be, nu, hold: (i, 0)
    amap = lambda i, be, nu, hold: (layer, be[i * per_step], 0, 0)
    bmap = lambda i, be, nu, hold: (layer, hold[i], 0, 0)
    weight_specs = [pl.BlockSpec((1, 1, D_MODEL, D_FF), amap), pl.BlockSpec((1, 1, D_MODEL, D_FF), amap),
                    pl.BlockSpec((1, 1, D_FF, D_MODEL), amap),
                    pl.BlockSpec((1, 1, D_MODEL, D_FF), bmap), pl.BlockSpec((1, 1, D_MODEL, D_FF), bmap),
                    pl.BlockSpec((1, 1, D_FF, D_MODEL), bmap)]
    grid_spec = pltpu.PrefetchScalarGridSpec(
        num_scalar_prefetch=3, grid=(n_steps,),
        in_specs=[pl.BlockSpec(tile, lambda i, be, nu, hold: (jnp.minimum(i, (nu[0] - 1) // per_step), 0))]
        + weight_specs,
        out_specs=pl.BlockSpec(tile, row))
    return pl.pallas_call(
        _expert_kernel, grid_spec=grid_spec,
        out_shape=jax.ShapeDtypeStruct(xs3.shape, F32),
        compiler_params=_cparams(1),
    )(block_eid, n_used, hold, xs3, wg, wu, wd, wg, wu, wd)


def _combine_kernel(idx0_ref, idxn_ref, ys_hbm, x_ref, rg_ref, o_ref, buf0, buf1, sem):
    i = pl.program_id(0)
    n = pl.num_programs(0)
    tm = x_ref.shape[0]
    n_chunk = D_MODEL // LANES

    def issue(idx_ref, buf, sem_slot, rows):
        for r in rows:
            for kk in range(2):
                src = pl.multiple_of(idx_ref[0, 0, kk * tm + r] * n_chunk, n_chunk)
                pltpu.make_async_copy(ys_hbm.at[pl.ds(src, n_chunk), :],
                                      buf.at[pl.ds((kk * tm + r) * n_chunk, n_chunk), :],
                                      sem_slot).start(priority=kk)

    def wait_tile(buf, sem_slot):
        pltpu.make_async_copy(ys_hbm.at[pl.ds(0, 2 * tm * n_chunk), :], buf, sem_slot).wait()

    @pl.when(i == 0)
    def _():
        issue(idx0_ref, buf0, sem.at[0], range(tm))

    def step(buf, sem_cur, buf_next, sem_next):
        issue(idxn_ref, buf_next, sem_next, range(tm))
        wait_tile(buf, sem_cur)
        rg = rg_ref[...]
        g0 = rg[:, 0:1]
        g1 = rg[:, 1:2]
        for j in range(n_chunk):
            cols = slice(j * LANES, (j + 1) * LANES)
            y0 = buf[pl.ds(j, tm, stride=n_chunk), :]
            y1 = buf[pl.ds(tm * n_chunk + j, tm, stride=n_chunk), :]
            o_ref[:, cols] = x_ref[:, cols] + g0 * y0 + g1 * y1

    @pl.when(i % 2 == 0)
    def _():
        step(buf0, sem.at[0], buf1, sem.at[1])

    @pl.when(i % 2 == 1)
    def _():
        step(buf1, sem.at[1], buf0, sem.at[0])

    @pl.when(i == n - 1)
    def _():
        @pl.when(i % 2 == 0)
        def _():
            wait_tile(buf1, sem.at[1])

        @pl.when(i % 2 == 1)
        def _():
            wait_tile(buf0, sem.at[0])


def _combine_call(dest3, ys, x2, rg):
    t = x2.shape[0]
    tm = TM_COMB
    n = t // tm
    row = lambda i: (i, 0)
    tile = (2 * tm * D_MODEL // LANES, LANES)
    return pl.pallas_call(
        _combine_kernel, grid=(n,),
        in_specs=[
            pl.BlockSpec((1, 1, 2 * tm), lambda i: (0, 0, 0), memory_space=pltpu.SMEM),
            pl.BlockSpec((1, 1, 2 * tm), lambda i: (jnp.minimum(i + 1, n - 1), 0, 0), memory_space=pltpu.SMEM),
            pl.BlockSpec(memory_space=pl.ANY),
            pl.BlockSpec((tm, D_MODEL), row), pl.BlockSpec((tm, LANES), row)],
        out_specs=pl.BlockSpec((tm, D_MODEL), row),
        out_shape=jax.ShapeDtypeStruct((t, D_MODEL), F32),
        scratch_shapes=[pltpu.VMEM(tile, F32), pltpu.VMEM(tile, F32), pltpu.SemaphoreType.DMA((2,))],
        compiler_params=_cparams(1),
    )(dest3, dest3, ys, x2, rg)


def _swap_halves(a):
    half = a.shape[-1] // 2
    return jnp.concatenate([a[..., half:], a[..., :half]], axis=-1)


def _layer_params(l, seq, w):
    p = {}
    row = lambda v: v.reshape(1, -1).astype(F32)
    w_in = w["w_in"][l]
    kr_cols = w_in[:, 1856:1888]
    p["win"] = jnp.concatenate(
        [w_in[:, 0:1536], w_in[:, 1888:2144], w_in[:, 1728:1856], w_in[:, 1536:1728],
         kr_cols, _swap_halves(kr_cols)], axis=1).astype(BF16)
    p["g1"] = row(w["norm1_g"][l])
    p["gq"] = row(jnp.tile(w["diff_q_norm_g"][l], 2) * (DIFF_QK ** -0.5))
    p["gk"] = row(jnp.tile(w["diff_k_norm_g"][l], 2))
    p["gckv"] = row(w["mla_kv_lat_norm_g"][l])
    gcq = w["mla_q_lat_norm_g"][l]
    p["gcqa"] = row(gcq[:LANES])
    p["gcqb"] = row(jnp.concatenate([gcq[LANES:], jnp.zeros((2 * LANES - MLA_Q_RANK,), F32)]))

    wuq = w["mla_w_uq"][l].reshape(MLA_Q_RANK, MLA_HEADS, MLA_NOPE + MLA_ROPE)
    rope_w = wuq[:, :, MLA_NOPE:]
    wuq = jnp.concatenate([wuq[:, :, :MLA_NOPE], rope_w, _swap_halves(rope_w)], axis=-1)
    wuq = wuq.reshape(MLA_Q_RANK, MLA_HEADS * HEAD_SLOT)
    wuq = jnp.concatenate([wuq, jnp.zeros((2 * LANES - MLA_Q_RANK, wuq.shape[1]), F32)], axis=0).astype(BF16)
    p["wuqa"] = wuq[:LANES]
    p["wuqb"] = wuq[LANES:]

    wukv = w["mla_w_ukv"][l].reshape(MLA_KV_RANK, MLA_HEADS, MLA_NOPE + MLA_V)
    zk = jnp.zeros((MLA_KV_RANK, MLA_HEADS, HEAD_SLOT - MLA_NOPE), F32)
    p["wkk"] = jnp.concatenate([wukv[:, :, :MLA_NOPE], zk], axis=-1).reshape(MLA_KV_RANK, -1).astype(BF16)
    vcols = wukv[:, :, MLA_NOPE:]
    zv = jnp.zeros_like(vcols)
    even = (jnp.arange(MLA_HEADS) % 2 == 0)[None, :, None]
    wkv = jnp.concatenate([jnp.where(even, vcols, zv), jnp.where(even, zv, vcols)], axis=-1)
    p["wkv"] = wkv.reshape(MLA_KV_RANK, -1).astype(BF16)
    p["gkn"] = row(jnp.concatenate([w["mla_k_nope_norm_g"][l], jnp.zeros((HEAD_SLOT - MLA_NOPE,), F32)]))

    inv = 1.0 / (ROPE_BASE ** (jnp.arange(0, MLA_ROPE, 2, dtype=F32) / MLA_ROPE))
    ang = jnp.arange(seq, dtype=F32)[:, None] * inv[None, :]
    cosf = jnp.concatenate([jnp.cos(ang), jnp.cos(ang)], axis=-1)
    sinf = jnp.concatenate([-jnp.sin(ang), jnp.sin(ang)], axis=-1)
    scale = (MLA_NOPE + MLA_ROPE) ** -0.5 * math.log2(math.e)
    gqr = w["mla_q_rope_norm_g"][l]
    q_head = jnp.concatenate([jnp.broadcast_to(w["mla_q_nope_norm_g"][l][None, :], (seq, MLA_NOPE)),
                              gqr[None, :] * cosf, _swap_halves(gqr)[None, :] * sinf], axis=-1) * scale
    p["qtab"] = jnp.tile(q_head, (1, MLA_HEADS))
    gkr = w["mla_k_rope_norm_g"][l]
    p["ktab"] = jnp.concatenate([jnp.zeros((seq, MLA_NOPE), F32), gkr[None, :] * cosf,
                                 _swap_halves(gkr)[None, :] * sinf], axis=-1)
    src = np.arange(LANES)
    dst = np.arange(MLA_HEADS * HEAD_SLOT)
    src_j = np.where(src >= MLA_NOPE, (src - MLA_NOPE) % MLA_ROPE, -1)
    dst_l = dst % HEAD_SLOT
    dst_j = np.where(dst_l >= MLA_NOPE, (dst_l - MLA_NOPE) % MLA_ROPE, -2)
    p["eplace"] = jnp.asarray((src_j[:, None] == dst_j[None, :]).astype(np.float32), dtype=BF16)

    pw = w["pool_w"][l]
    bd = jnp.zeros((POOL_WIDTH, POOL_WIDTH), F32)
    for g in range(POOL_GROUPS):
        s0 = g * POOL_GROUP_DIM
        bd = bd.at[s0:s0 + POOL_GROUP_DIM, s0:s0 + POOL_GROUP_DIM].set(pw[g])
    p["pool_w"] = bd.astype(BF16)
    p["pool_scale"] = row(w["pool_scale"][l])

    lam_init = 0.8 - 0.6 * math.exp(-0.3 * l)
    lv = w["diff_lambda"][l].astype(F32)
    p["lam"] = (jnp.exp(jnp.sum(lv[0] * lv[1])) - jnp.exp(jnp.sum(lv[2] * lv[3])) + lam_init).reshape(1)
    p["gsub"] = row(w["diff_sub_norm_g"][l] * (1.0 - lam_init))

    p["wo"] = w["w_out"][l].astype(BF16)
    p["g2"] = row(w["norm2_g"][l])
    wr = jnp.concatenate([w["router_group_w"][l], w["router_expert_w"][l],
                          jnp.zeros((D_MODEL, LANES - N_GROUPS - N_EXPERTS), F32)], axis=1)
    wr_hi = wr.astype(BF16)
    p["wrh"] = wr_hi
    p["wrl"] = (wr - wr_hi.astype(F32)).astype(BF16)
    p["br"] = row(jnp.concatenate([w["router_group_b"][l], w["router_expert_b"][l],
                                   jnp.zeros((LANES - N_GROUPS - N_EXPERTS,), F32)]))
    return p


def kernel(x, norm1_g, w_in, diff_q_norm_g, diff_k_norm_g, diff_lambda, diff_sub_norm_g, mla_q_lat_norm_g, mla_kv_lat_norm_g, mla_w_uq, mla_w_ukv, mla_q_nope_norm_g, mla_q_rope_norm_g, mla_k_nope_norm_g, mla_k_rope_norm_g, pool_w, pool_scale, w_out, norm2_g, router_group_w, router_group_b, router_expert_w, router_expert_b, expert_w_gate, expert_w_up, expert_w_down):
    w = dict(norm1_g=norm1_g, w_in=w_in, diff_q_norm_g=diff_q_norm_g, diff_k_norm_g=diff_k_norm_g,
             diff_lambda=diff_lambda, diff_sub_norm_g=diff_sub_norm_g, mla_q_lat_norm_g=mla_q_lat_norm_g,
             mla_kv_lat_norm_g=mla_kv_lat_norm_g, mla_w_uq=mla_w_uq, mla_w_ukv=mla_w_ukv,
             mla_q_nope_norm_g=mla_q_nope_norm_g, mla_q_rope_norm_g=mla_q_rope_norm_g,
             mla_k_nope_norm_g=mla_k_nope_norm_g, mla_k_rope_norm_g=mla_k_rope_norm_g,
             pool_w=pool_w, pool_scale=pool_scale, w_out=w_out, norm2_g=norm2_g,
             router_group_w=router_group_w, router_group_b=router_group_b,
             router_expert_w=router_expert_w, router_expert_b=router_expert_b)
    batch, seq, d = x.shape
    t = batch * seq
    n_assign = 2 * t
    n_blocks = n_assign // ROUTE_BLOCK + N_EXPERTS
    alibi = _alibi_tables(seq)

    x2 = x.reshape(t, d)
    for l in range(DEPTH):
        p = _layer_params(l, seq, w)
        dq1, dq2, dk, dv, qm, km, vm, pu = _proj_call(x2, p, seq)
        o_diff = _diff_call(p["lam"], dq1, dq2, dk, dv, p["gsub"], alibi, batch, seq)
        o_mla = _mla_call(qm, km, vm, batch, seq)
        o_pool = _pool_call(pu, p["pool_w"], p["pool_scale"], batch, seq)
        x2, h2, route_i, route_g, counts = _outproj_call(x2, o_diff, o_mla, o_pool, p)

        cnt = counts[0, N_GROUPS:N_GROUPS + N_EXPERTS].astype(jnp.int32)
        padded = (cnt + ROUTE_BLOCK - 1) // ROUTE_BLOCK * ROUTE_BLOCK
        padded_ends = jnp.cumsum(padded)
        padded_starts = padded_ends - padded
        info_t = route_i[0:4]
        start_of = jnp.sum(jnp.where(info_t[0:2, None, :] == jnp.arange(N_EXPERTS, dtype=jnp.int32)[None, :, None],
                                     padded_starts[None, :, None], 0), axis=1)
        dest_t = start_of + info_t[2:4]

        def dest_tiles(tm):
            return dest_t.reshape(2, t // tm, tm).transpose(1, 0, 2).reshape(t // tm, 1, 2 * tm)
        block_start = jnp.arange(n_blocks, dtype=jnp.int32) * ROUTE_BLOCK
        block_eid = jnp.minimum(jnp.sum(block_start[:, None] >= padded_ends[None, :], axis=1),
                                N_EXPERTS - 1).astype(jnp.int32)
        n_used = (padded_ends[-1] // ROUTE_BLOCK).astype(jnp.int32).reshape(1)
        n_tail = (n_blocks - n_used) * (ROUTE_BLOCK // PAD_CHUNKS[0])
        pads = jnp.stack([jnp.concatenate([padded_starts + cnt, padded_ends[-1:]]),
                          jnp.concatenate([padded - cnt, n_tail])]).astype(jnp.int32)
        xs = _dispatch_call(pads, dest_tiles(TM_DISP), h2, n_blocks * ROUTE_BLOCK)
        ys = _expert_call(block_eid, n_used, xs, expert_w_gate, expert_w_up, expert_w_down, l)
        x2 = _combine_call(dest_tiles(TM_COMB), ys, x2, route_g)
    return x2.reshape(batch, seq, d)
```
